```python
import jax, jax.numpy as jnp
from jax import lax
import numpy as np

D_MODEL = 1024
BATCH = 4
SEQ = 8192
DEPTH = 1
DEC_BATCH = 8
DEC_SEQ = 16
PAST_LEN = 2048

CHUNK = 64
MIX_W = D_MODEL
POOL_W = MIX_W // 2
POOL_WINDOWS = (2, 4, 8, 16)
POOL_GROUPS = len(POOL_WINDOWS)
POOL_GW = POOL_W // POOL_GROUPS
POOL_CTX = max(POOL_WINDOWS) - 1
RWKV_W = MIX_W - POOL_W
HEAD_DIM = 64
N_HEADS = RWKV_W // HEAD_DIM
DECAY_LORA = 64
ICLR_LORA = 64
GATE_LORA = 128
RWKV_PROJ = 3 * RWKV_W + DECAY_LORA + ICLR_LORA + GATE_LORA
IN_PROJ = POOL_W + RWKV_PROJ
LNX_EPS = 64e-5
N_EXPERTS = 32
TOP_K = 4
D_EXPERT = D_MODEL
SWIGLU_LIMIT = 7.0
SWIGLU_ALPHA = 1.702
EXPERT_BLOCK = 256
PLE_DIM = 256
RMS_EPS = 1e-6

kernel_name = "hymba_pool_rwkv7_moe_stream_step"


def _rmsnorm(x, g):
    xf = x.astype(jnp.float32)
    y = xf * lax.rsqrt(jnp.mean(xf * xf, axis=-1, keepdims=True) + RMS_EPS)
    return (y * g.astype(jnp.float32)).astype(x.dtype)


def _pool_mixer(u, pool_ctx, pos0, w_pool, pool_scale):
    B, T, _ = u.shape
    ctx_u = jnp.concatenate([pool_ctx.astype(u.dtype), u], axis=1)
    new_ctx = ctx_u[:, -POOL_CTX:]
    cs = jnp.cumsum(ctx_u.astype(jnp.float32), axis=1)
    cs = jnp.concatenate([jnp.zeros((B, 1, POOL_W), jnp.float32), cs], axis=1)
    pos = pos0 + jnp.arange(T)
    means = []
    for gi, w in enumerate(POOL_WINDOWS):
        sl = slice(gi * POOL_GW, (gi + 1) * POOL_GW)
        hi = cs[:, POOL_CTX + 1:POOL_CTX + 1 + T, sl]
        lo = cs[:, POOL_CTX + 1 - w:POOL_CTX + 1 - w + T, sl]
        cnt = jnp.minimum(w, pos + 1).astype(jnp.float32)
        means.append((hi - lo) / cnt[None, :, None])
    pooled = jnp.concatenate(means, axis=-1) - u.astype(jnp.float32)
    pooled = pooled.reshape(B, T, POOL_GROUPS, POOL_GW).astype(u.dtype)
    out = jnp.einsum('btgc,gcd->btgd', pooled, w_pool).reshape(B, T, POOL_W)
    return out * pool_scale, new_ctx


def _wkv_step(S, inp):
    r, w, k, v, a_, b_ = inp
    sa = jnp.einsum('bhvk,bhk->bhv', S, a_)
    S = S * w[:, :, None, :] + sa[..., None] * b_[:, :, None, :] + v[..., None] * k[:, :, None, :]
    y = jnp.einsum('bhvk,bhk->bhv', S, r)
    return S, y


def _rwkv7_mixer(zr, shift_prev, wkv0, mu_shift, w0, w2, a0, a2, g2, k_k, k_a, r_k, lnx_g, lnx_b):
    B, T, _ = zr.shape
    prev = jnp.concatenate([shift_prev.astype(zr.dtype), zr[:, :-1]], axis=1)
    zs = zr + (prev - zr) * mu_shift
    new_shift = zr[:, -1:]
    o1, o2, o3 = RWKV_W, 2 * RWKV_W, 3 * RWKV_W
    o4, o5 = o3 + DECAY_LORA, o3 + DECAY_LORA + ICLR_LORA
    r, k, v = zs[..., :o1], zs[..., o1:o2], zs[..., o2:o3]
    wd, ad, gd = zs[..., o3:o4], zs[..., o4:o5], zs[..., o5:]
    wz = (w0 + jnp.tanh(wd) @ w2).astype(jnp.float32)
    decay = jnp.exp(-jnp.exp(-jax.nn.softplus(-wz) - 0.5))
    a = jax.nn.sigmoid(a0 + ad @ a2)
    g = jax.nn.sigmoid(gd) @ g2
    hs = (B, T, N_HEADS, HEAD_DIM)
    kk = (k * k_k).reshape(hs).astype(jnp.float32)
    kk = kk / jnp.maximum(jnp.sqrt(jnp.sum(kk * kk, axis=-1, keepdims=True)), 1e-12)
    k = k * (1 + (a - 1) * k_a)
    rf = r.reshape(hs).astype(jnp.float32)
    kf = k.reshape(hs).astype(jnp.float32)
    vf = v.reshape(hs).astype(jnp.float32)
    af = a.reshape(hs).astype(jnp.float32)
    tm = lambda t: jnp.swapaxes(t, 0, 1)
    seqs = (tm(rf), tm(decay.reshape(hs)), tm(kf), tm(vf), tm(-kk), tm(kk * af))
    S_fin, y = lax.scan(_wkv_step, wkv0.astype(jnp.float32), seqs)
    y = jnp.swapaxes(y, 0, 1)
    mu = jnp.mean(y, axis=-1, keepdims=True)
    var = jnp.mean((y - mu) ** 2, axis=-1, keepdims=True)
    yn = ((y - mu) * lax.rsqrt(var + LNX_EPS)).reshape(B, T, RWKV_W)
    yn = yn * lnx_g.astype(jnp.float32) + lnx_b.astype(jnp.float32)
    bonus = jnp.sum(rf * kf * r_k.astype(jnp.float32), axis=-1, keepdims=True) * vf
    out = (yn + bonus.reshape(B, T, RWKV_W)).astype(zr.dtype) * g
    return out, new_shift, S_fin


def _moe(x, router_w, router_b, w_gu, b_gu, w_down, b_down):
    n, d = x.shape
    logits = (x @ router_w).astype(jnp.float32) + router_b.astype(jnp.float32)
    top_val, top_idx = lax.top_k(logits, TOP_K)
    gates = jax.nn.softmax(top_val, axis=-1)
    n_asg = n * TOP_K
    flat_e = top_idx.reshape(-1).astype(jnp.int32)
    flat_tok = jnp.arange(n_asg, dtype=jnp.int32) // TOP_K
    flat_g = gates.reshape(-1)
    order = jnp.argsort(flat_e)
    e_sorted = flat_e[order]
    counts = jnp.bincount(flat_e, length=N_EXPERTS).astype(jnp.int32)
    padded = (counts + EXPERT_BLOCK - 1) // EXPERT_BLOCK * EXPERT_BLOCK
    start_sorted = jnp.cumsum(counts) - counts
    pad_end = jnp.cumsum(padded)
    start_pad = pad_end - padded
    rank = jnp.arange(n_asg, dtype=jnp.int32) - start_sorted[e_sorted]
    dest = start_pad[e_sorted] + rank
    n_blocks = -(-n_asg // EXPERT_BLOCK) + N_EXPERTS
    n_slots = n_blocks * EXPERT_BLOCK
    slot_tok = jnp.full((n_slots,), n, jnp.int32).at[dest].set(flat_tok[order])
    slot_gate = jnp.zeros((n_slots,), jnp.float32).at[dest].set(flat_g[order])
    block_start = jnp.arange(n_blocks, dtype=jnp.int32) * EXPERT_BLOCK
    block_e = jnp.minimum(jnp.searchsorted(pad_end, block_start, side='right'), N_EXPERTS - 1)
    x_pad = jnp.concatenate([x, jnp.zeros((1, d), x.dtype)], axis=0)

    def expert_block(args):
        tok, e, gt = args
        xb = x_pad[tok]
        gu = xb @ w_gu[e] + b_gu[e]
        gate = jnp.minimum(gu[:, :D_EXPERT], SWIGLU_LIMIT)
        up = jnp.clip(gu[:, D_EXPERT:], -SWIGLU_LIMIT, SWIGLU_LIMIT)
        hmid = (up + 1) * (gate * jax.nn.sigmoid(SWIGLU_ALPHA * gate))
        return (hmid @ w_down[e] + b_down[e]) * gt[:, None].astype(x.dtype)

    outs = lax.map(expert_block, (slot_tok.reshape(n_blocks, EXPERT_BLOCK), block_e,
                                  slot_gate.reshape(n_blocks, EXPERT_BLOCK)))
    y = jnp.zeros((n + 1, d), x.dtype).at[slot_tok].add(outs.reshape(n_slots, d))
    return y[:n]


def _layer(x, p, pos0, pool_ctx, shift_prev, wkv0,
           ln1_g, w_in, mu_shift, w_pool, pool_scale, w0, w2, a0, a2, g2, k_k, k_a, r_k,
           lnx_g, lnx_b, w_out, ln2_g, router_w, router_b, w_gu, b_gu, w_down, b_down,
           ln3_g, ple_gate, ple_proj):
    B, T, D = x.shape
    z = _rmsnorm(x, ln1_g) @ w_in
    pool_out, new_pool = _pool_mixer(z[..., :POOL_W], pool_ctx, pos0, w_pool, pool_scale)
    rwkv_out, new_shift, new_wkv = _rwkv7_mixer(z[..., POOL_W:], shift_prev, wkv0, mu_shift,
                                                w0, w2, a0, a2, g2, k_k, k_a, r_k, lnx_g, lnx_b)
    h = x + (jnp.concatenate([pool_out, rwkv_out], axis=-1) @ w_out).astype(x.dtype)
    h = h + _moe(_rmsnorm(h, ln2_g).reshape(B * T, D), router_w, router_b,
                 w_gu, b_gu, w_down, b_down).reshape(B, T, D)
    h = h + jax.nn.sigmoid(_rmsnorm(h, ln3_g) @ ple_gate) * (p.astype(x.dtype) @ ple_proj)
    return h, new_pool, new_shift, new_wkv


def setup_inputs(seed: int = 0) -> dict:
    key = jax.random.key(seed)
    ks = jax.random.split(key, 40)
    f32 = jnp.float32
    L, D = DEPTH, D_MODEL

    def nrm(k, shape, scale):
        return jax.random.normal(k, shape, f32) * scale

    return {
        "x_prompt": nrm(ks[0], (BATCH, SEQ, D), 1.0),
        "x_sample": nrm(ks[1], (DEC_BATCH, DEC_SEQ, D), 1.0),
        "p_prompt": nrm(ks[2], (L, BATCH, SEQ, PLE_DIM), 1.0),
        "p_sample": nrm(ks[3], (L, DEC_BATCH, DEC_SEQ, PLE_DIM), 1.0),
        "cache_pool": nrm(ks[4], (L, DEC_BATCH, POOL_CTX, POOL_W), 1.0),
        "state_shift": nrm(ks[5], (L, DEC_BATCH, 1, RWKV_PROJ), 1.0),
        "state_wkv": nrm(ks[6], (L, DEC_BATCH, N_HEADS, HEAD_DIM, HEAD_DIM), 0.3),
        "ln1_g": 1.0 + nrm(ks[7], (L, D), 0.05),
        "w_in": nrm(ks[8], (L, D, IN_PROJ), D ** -0.5),
        "mu_shift": jax.random.uniform(ks[9], (L, RWKV_PROJ), f32),
        "w_pool": nrm(ks[10], (L, POOL_GROUPS, POOL_GW, POOL_GW), POOL_GW ** -0.5),
        "pool_scale": 1.0 + nrm(ks[11], (L, POOL_W), 0.1),
        "w0": jax.random.uniform(ks[12], (L, RWKV_W), f32, -3.0, 1.0),
        "w2": nrm(ks[13], (L, DECAY_LORA, RWKV_W), 0.5 * DECAY_LORA ** -0.5),
        "a0": nrm(ks[14], (L, RWKV_W), 0.5),
        "a2": nrm(ks[15], (L, ICLR_LORA, RWKV_W), ICLR_LORA ** -0.5),
        "g2": nrm(ks[16], (L, GATE_LORA, RWKV_W), GATE_LORA ** -0.5),
        "k_k": 0.85 + nrm(ks[17], (L, RWKV_W), 0.05),
        "k_a": 1.0 + nrm(ks[18], (L, RWKV_W), 0.05),
        "r_k": nrm(ks[19], (L, N_HEADS, HEAD_DIM), 0.1),
        "lnx_g": 1.0 + nrm(ks[20], (L, RWKV_W), 0.05),
        "lnx_b": nrm(ks[21], (L, RWKV_W), 0.01),
        "w_out": nrm(ks[22], (L, MIX_W, D), MIX_W ** -0.5),
        "ln2_g": 1.0 + nrm(ks[23], (L, D), 0.05),
        "router_w": nrm(ks[24], (L, D, N_EXPERTS), D ** -0.5),
        "router_b": nrm(ks[25], (L, N_EXPERTS), 0.01),
        "w_gu": nrm(ks[26], (L, N_EXPERTS, D, 2 * D_EXPERT), D ** -0.5),
        "b_gu": nrm(ks[27], (L, N_EXPERTS, 2 * D_EXPERT), 0.01),
        "w_down": nrm(ks[28], (L, N_EXPERTS, D_EXPERT, D), D_EXPERT ** -0.5),
        "b_down": nrm(ks[29], (L, N_EXPERTS, D), 0.01),
        "ln3_g": 1.0 + nrm(ks[30], (L, D), 0.05),
        "ple_gate": nrm(ks[31], (L, D, D), D ** -0.5),
        "ple_proj": nrm(ks[32], (L, PLE_DIM, D), PLE_DIM ** -0.5),
        "final_g": 1.0 + nrm(ks[33], (D,), 0.05),
    }


def reference(x_prompt, x_sample, p_prompt, p_sample, cache_pool, state_shift, state_wkv,
              ln1_g, w_in, mu_shift, w_pool, pool_scale, w0, w2, a0, a2, g2, k_k, k_a, r_k,
              lnx_g, lnx_b, w_out, ln2_g, router_w, router_b, w_gu, b_gu, w_down, b_down,
              ln3_g, ple_gate, ple_proj, final_g):
    assert x_sample.shape[1] <= CHUNK
    B, T = x_prompt.shape[0], x_prompt.shape[1]
    hp, hs = x_prompt, x_sample
    pool_p, shift_p, wkv_p = [], [], []
    pool_s, shift_s, wkv_s = [], [], []
    for i in range(DEPTH):
        lw = (ln1_g[i], w_in[i], mu_shift[i], w_pool[i], pool_scale[i], w0[i], w2[i], a0[i], a2[i],
              g2[i], k_k[i], k_a[i], r_k[i], lnx_g[i], lnx_b[i], w_out[i], ln2_g[i], router_w[i],
              router_b[i], w_gu[i], b_gu[i], w_down[i], b_down[i], ln3_g[i], ple_gate[i], ple_proj[i])
        zero_pool = jnp.zeros((B, POOL_CTX, POOL_W), hp.dtype)
        zero_shift = jnp.zeros((B, 1, RWKV_PROJ), hp.dtype)
        zero_wkv = jnp.zeros((B, N_HEADS, HEAD_DIM, HEAD_DIM), jnp.float32)
        hp, npool, nshift, nwkv = _layer(hp, p_prompt[i], 0, zero_pool, zero_shift, zero_wkv, *lw)
        pool_p.append(npool); shift_p.append(nshift); wkv_p.append(nwkv)
        hs, npool, nshift, nwkv = _layer(hs, p_sample[i], PAST_LEN, cache_pool[i], state_shift[i],
                                         state_wkv[i], *lw)
        pool_s.append(npool); shift_s.append(nshift); wkv_s.append(nwkv)
    y_prompt = _rmsnorm(hp, final_g)
    y_sample = _rmsnorm(hs, final_g)
    return (y_prompt, y_sample,
            jnp.stack(pool_p), jnp.stack(shift_p), jnp.stack(wkv_p),
            jnp.stack(pool_s), jnp.stack(shift_s), jnp.stack(wkv_s))
```

```python
import functools

import jax
import jax.numpy as jnp
from jax import lax
from jax.experimental import pallas as pl
from jax.experimental.pallas import tpu as pltpu

F32 = jnp.float32
BF16 = jnp.bfloat16
I32 = jnp.int32

D_MODEL = 1024
POOL_W = 512
POOL_WINDOWS = (2, 4, 8, 16)
POOL_GW = 128
POOL_CTX = 15
POOL_HALO = 16
RWKV_W = 512
HEAD_DIM = 64
N_HEADS = 8
N_PAIRS = N_HEADS // 2
PAIR_W = 2 * HEAD_DIM
DECAY_LORA = 64
ICLR_LORA = 64
GATE_LORA = 128
RWKV_PROJ = 3 * RWKV_W + DECAY_LORA + ICLR_LORA + GATE_LORA
IN_PROJ = POOL_W + RWKV_PROJ
LNX_EPS = 64e-5
N_EXPERTS = 32
TOP_K = 4
D_EXPERT = 1024
SWIGLU_LIMIT = 7.0
SWIGLU_ALPHA = 1.702
PLE_DIM = 256
RMS_EPS = 1e-6
PAST_LEN = 2048

CHUNK = 64
EXPERT_BLOCK = 256
ROUTE_LANES = 128
SUBLANES = 8
EXP_M05 = 0.6065306597126334
V7X_VMEM_BYTES = 64 * 1024 * 1024

NT_DIMS = (((1,), (1,)), ((), ()))
TN_DIMS = (((0,), (0,)), ((), ()))


def _vmem_limit(est_bytes):
    return int(min(est_bytes * 3 // 2 + (4 << 20), V7X_VMEM_BYTES - (8 << 20)))


def _params(n_axes, est_bytes):
    return pltpu.CompilerParams(dimension_semantics=("arbitrary",) * n_axes,
                                vmem_limit_bytes=_vmem_limit(est_bytes))


def _dot(a, b):
    return jnp.dot(a, b, preferred_element_type=F32)


def _split2(x):
    hi = x.astype(BF16)
    lo = (x - hi.astype(F32)).astype(BF16)
    return hi, lo


def _split3(x):
    hi = x.astype(BF16)
    r1 = x - hi.astype(F32)
    mid = r1.astype(BF16)
    lo = (r1 - mid.astype(F32)).astype(BF16)
    return hi, mid, lo


def _group_sum(x, ones_bd):
    hi, lo = _split2(x)
    return _dot(hi, ones_bd) + _dot(lo, ones_bd)


def _rmsnorm_rows(x, g):
    ms = jnp.mean(x * x, axis=-1, keepdims=True)
    return (x * lax.rsqrt(ms + RMS_EPS)) * g


def _in_proj_kernel(x_ref, g_ref, w_ref, u_ref, zr_ref):
    xn = _rmsnorm_rows(x_ref[...], g_ref[...])
    z = _dot(xn.astype(BF16), w_ref[...])
    u_ref[...] = z[:, :POOL_W]
    zr_ref[...] = z[:, POOL_W:]


def _in_proj(x2d, ln1_g, w_in_bf, tm):
    n = x2d.shape[0]
    est = 2 * tm * D_MODEL * 4 + 2 * D_MODEL * IN_PROJ * 2 + 3 * tm * IN_PROJ * 4
    return pl.pallas_call(
        _in_proj_kernel,
        grid=(n // tm,),
        in_specs=[pl.BlockSpec((tm, D_MODEL), lambda i: (i, 0)),
                  pl.BlockSpec((1, D_MODEL), lambda i: (0, 0)),
                  pl.BlockSpec((D_MODEL, IN_PROJ), lambda i: (0, 0))],
        out_specs=[pl.BlockSpec((tm, POOL_W), lambda i: (i, 0)),
                   pl.BlockSpec((tm, RWKV_PROJ), lambda i: (i, 0))],
        out_shape=[jax.ShapeDtypeStruct((n, POOL_W), F32),
                   jax.ShapeDtypeStruct((n, RWKV_PROJ), F32)],
        compiler_params=_params(1, est),
        name="in_proj",
    )(x2d, ln1_g.reshape(1, D_MODEL), w_in_bf)


def _pool_kernel(u_ref, halo_ref, ctx_ref, wp_ref, ps_ref, o_ref, buf, *, pos0):
    t = pl.program_id(1)
    tt = u_ref.shape[1]
    buf[0:POOL_HALO, :] = jnp.where(t == 0, ctx_ref[0], halo_ref[0])
    buf[POOL_HALO:, :] = u_ref[0]
    pos = lax.broadcasted_iota(I32, (tt, 1), 0) + (t * tt + pos0)
    outs = []
    for gi, w in enumerate(POOL_WINDOWS):
        sl = slice(gi * POOL_GW, (gi + 1) * POOL_GW)
        cur = buf[POOL_HALO:POOL_HALO + tt, sl]
        acc = cur
        for j in range(1, w):
            acc = acc + buf[POOL_HALO - j:POOL_HALO - j + tt, sl]
        cnt = jnp.minimum(w, pos + 1).astype(F32)
        pooled = acc / cnt - cur
        outs.append(_dot(pooled.astype(BF16), wp_ref[gi]))
    o_ref[0] = (jnp.concatenate(outs, axis=-1) * ps_ref[...]).astype(o_ref.dtype)


def _pool_mixer(u, ctx16, w_pool_bf, pool_scale, pos0, tt):
    b, t, _ = u.shape
    hb = tt // POOL_HALO
    est = 2 * tt * POOL_W * 4 * 3 + (tt + POOL_HALO) * POOL_W * 4
    return pl.pallas_call(
        functools.partial(_pool_kernel, pos0=pos0),
        grid=(b, t // tt),
        in_specs=[pl.BlockSpec((1, tt, POOL_W), lambda i, j: (i, j, 0)),
                  pl.BlockSpec((1, POOL_HALO, POOL_W), lambda i, j: (i, jnp.maximum(j * hb - 1, 0), 0)),
                  pl.BlockSpec((1, POOL_HALO, POOL_W), lambda i, j: (i, 0, 0)),
                  pl.BlockSpec((len(POOL_WINDOWS), POOL_GW, POOL_GW), lambda i, j: (0, 0, 0)),
                  pl.BlockSpec((1, POOL_W), lambda i, j: (0, 0))],
        out_specs=pl.BlockSpec((1, tt, POOL_W), lambda i, j: (i, j, 0)),
        out_shape=jax.ShapeDtypeStruct((b, t, POOL_W), BF16),
        scratch_shapes=[pltpu.VMEM((tt + POOL_HALO, POOL_W), F32)],
        compiler_params=_params(2, est),
        name="pool_mixer",
    )(u, u, ctx16, w_pool_bf, pool_scale.reshape(1, POOL_W))


def _rwkv_pre_kernel(zr_ref, halo_ref, sh_ref, mu_ref, w0_ref, w2_ref, a0_ref, a2_ref, g2_ref,
                     kk_ref, ka_ref, rk_ref, ones_ref,
                     r_o, lw_o, kx_o, v_o, na_o, kb_o, g_o, bonus_o, buf):
    t = pl.program_id(1)
    tt = zr_ref.shape[1]
    buf[0:SUBLANES, :] = jnp.where(t == 0, sh_ref[0], halo_ref[0])
    zr = zr_ref[0]
    buf[SUBLANES:, :] = zr
    prev = buf[SUBLANES - 1:SUBLANES - 1 + tt, :]
    zs = zr + (prev - zr) * mu_ref[...]
    o1, o2, o3 = RWKV_W, 2 * RWKV_W, 3 * RWKV_W
    o4, o5 = o3 + DECAY_LORA, o3 + DECAY_LORA + ICLR_LORA
    r, k, v = zs[:, :o1], zs[:, o1:o2], zs[:, o2:o3]
    wd, ad, gd = zs[:, o3:o4], zs[:, o4:o5], zs[:, o5:]
    ones_bd = ones_ref[...]
    wz = w0_ref[...] + _dot(jnp.tanh(wd).astype(BF16), w2_ref[...])
    lw = -EXP_M05 * jax.nn.sigmoid(wz)
    a = jax.nn.sigmoid(a0_ref[...] + _dot(ad.astype(BF16), a2_ref[...]))
    g = _dot(jax.nn.sigmoid(gd).astype(BF16), g2_ref[...])
    kk = k * kk_ref[...]
    kk = kk / jnp.maximum(jnp.sqrt(_group_sum(kk * kk, ones_bd)), 1e-12)
    kx = k * (1.0 + (a - 1.0) * ka_ref[...])
    bonus = _group_sum(r * kx * rk_ref[...], ones_bd) * v
    r_o[0] = r
    lw_o[0] = lw
    kx_o[0] = kx
    v_o[0] = v
    na_o[0] = -kk
    kb_o[0] = kk * a
    g_o[0] = g
    bonus_o[0] = bonus


def _rwkv_pre(zr, sh8, mu, w0, w2_bf, a0, a2_bf, g2_bf, k_k, k_a, r_k, ones_bd, tt):
    b, t, _ = zr.shape
    hb = tt // SUBLANES
    row = lambda x: x.reshape(1, -1)
    cst = lambda shape: pl.BlockSpec(shape, lambda i, j: (0,) * len(shape))
    seq = pl.BlockSpec((1, tt, RWKV_W), lambda i, j: (i, j, 0))
    est = 3 * tt * RWKV_PROJ * 4 + 2 * 8 * tt * RWKV_W * 4 + 12 * tt * RWKV_W * 4
    return pl.pallas_call(
        _rwkv_pre_kernel,
        grid=(b, t // tt),
        in_specs=[pl.BlockSpec((1, tt, RWKV_PROJ), lambda i, j: (i, j, 0)),
                  pl.BlockSpec((1, SUBLANES, RWKV_PROJ), lambda i, j: (i, jnp.maximum(j * hb - 1, 0), 0)),
                  pl.BlockSpec((1, SUBLANES, RWKV_PROJ), lambda i, j: (i, 0, 0)),
                  cst((1, RWKV_PROJ)), cst((1, RWKV_W)), cst((DECAY_LORA, RWKV_W)),
                  cst((1, RWKV_W)), cst((ICLR_LORA, RWKV_W)), cst((GATE_LORA, RWKV_W)),
                  cst((1, RWKV_W)), cst((1, RWKV_W)), cst((1, RWKV_W)), cst((RWKV_W, RWKV_W))],
        out_specs=[seq] * 8,
        out_shape=[jax.ShapeDtypeStruct((b, t, RWKV_W), F32)] * 8,
        scratch_shapes=[pltpu.VMEM((tt + SUBLANES, RWKV_PROJ), F32)],
        compiler_params=_params(2, est),
        name="rwkv_pre",
    )(zr, zr, sh8, row(mu), row(w0), w2_bf, row(a0), a2_bf, g2_bf, row(k_k), row(k_a), row(r_k), ones_bd)


def _pair_blockdiag(z, left):
    return jnp.concatenate([jnp.where(left, z, 0.0), jnp.where(left, 0.0, z)], axis=0)


def _wkv_chunk(r, lw, kx, v, na, kb, s, consts):
    tri_incl_bf, strict, incl, eye_pair, left, bd_mask = consts
    c = r.shape[0]
    bf = lambda x: x.astype(BF16)
    h3 = _split3(lw)
    cum = _dot(tri_incl_bf, h3[0]) + _dot(tri_incl_bf, h3[1]) + _dot(tri_incl_bf, h3[2])
    tot = cum[c - 1:c, :]
    p_end = jnp.exp(tot - cum)
    inv_p = jnp.exp(-cum)
    a_t = na * jnp.exp(cum - lw)
    r_t = r * jnp.exp(cum)
    b_t = kb * inv_p
    k_t = kx * inv_p
    b_h = kb * p_end
    k_h = kx * p_end
    p_c = jnp.exp(tot)

    lhs = bf(jnp.concatenate([a_t, r_t], axis=0))
    rhs = bf(jnp.concatenate([_pair_blockdiag(b_t, left), _pair_blockdiag(k_t, left)], axis=0))
    sc = lax.dot_general(lhs, rhs, NT_DIMS, preferred_element_type=F32)
    a_ab = jnp.where(strict, sc[:c, :2 * c], 0.0)
    a_ak = jnp.where(strict, sc[:c, 2 * c:], 0.0)
    a_rb = jnp.where(incl, sc[c:, :2 * c], 0.0)
    a_rk = jnp.where(incl, sc[c:, 2 * c:], 0.0)

    x = a_ab
    t_inv = eye_pair + x
    n_sq = max(c.bit_length() - 2, 0)
    for _ in range(n_sq):
        x = _dot(bf(x), bf(_pair_blockdiag(x, left)))
        t_inv = t_inv + _dot(bf(t_inv), bf(_pair_blockdiag(x, left)))

    v_bd = bf(_pair_blockdiag(v, left))
    w1 = _dot(bf(a_ak), v_bd)
    tu = _dot(bf(t_inv), bf(jnp.concatenate([_pair_blockdiag(w1, left), _pair_blockdiag(a_t, left)], axis=1)))
    u_loc, a_tt = tu[:, :PAIR_W], tu[:, PAIR_W:]
    ar = _dot(bf(a_rb), bf(jnp.concatenate([_pair_blockdiag(u_loc, left), _pair_blockdiag(a_tt, left)], axis=1)))
    y_loc = ar[:, :PAIR_W] + _dot(bf(a_rk), v_bd)
    r_g = r_t + ar[:, PAIR_W:]

    phi = jnp.where(bd_mask, lax.dot_general(bf(a_tt), bf(b_h), TN_DIMS, preferred_element_type=F32), 0.0)
    d_loc = jnp.where(bd_mask, lax.dot_general(bf(jnp.concatenate([v, u_loc], axis=0)),
                                               bf(jnp.concatenate([k_h, b_h], axis=0)),
                                               TN_DIMS, preferred_element_type=F32), 0.0)

    s_hi, s_lo = _split2(s)
    y = lax.dot_general(bf(r_g), s_hi, NT_DIMS, preferred_element_type=F32) + y_loc
    phi_bf = bf(phi)
    s_new = s * p_c + (_dot(s_hi, phi_bf) + _dot(s_lo, phi_bf)) + d_loc
    return y, s_new


def _wkv_kernel(r_ref, lw_ref, kx_ref, v_ref, na_ref, kb_ref, s0_ref, y_ref, sf_ref, s_scr):
    t = pl.program_id(1)
    tt = r_ref.shape[1]
    c = min(CHUNK, tt)

    @pl.when(t == 0)
    def _():
        s_scr[...] = s0_ref[0]

    lane = lax.broadcasted_iota(I32, (c, 2 * c), 1)
    rowi = lax.broadcasted_iota(I32, (c, 2 * c), 0)
    jm = lane & (c - 1)
    strict = jm < rowi
    incl = jm <= rowi
    eye_pair = jnp.where(jm == rowi, 1.0, 0.0).astype(F32)
    left = lax.broadcasted_iota(I32, (c, PAIR_W), 1) < HEAD_DIM
    rr = lax.broadcasted_iota(I32, (c, c), 0)
    cc = lax.broadcasted_iota(I32, (c, c), 1)
    tri_incl_bf = jnp.where(cc <= rr, 1.0, 0.0).astype(BF16)
    br = lax.broadcasted_iota(I32, (PAIR_W, PAIR_W), 0) < HEAD_DIM
    bc = lax.broadcasted_iota(I32, (PAIR_W, PAIR_W), 1) < HEAD_DIM
    bd_mask = br == bc
    consts = (tri_incl_bf, strict, incl, eye_pair, left, bd_mask)

    for ci in range(tt // c):
        rows = slice(ci * c, (ci + 1) * c)
        for p in range(N_PAIRS):
            cols = slice(p * PAIR_W, (p + 1) * PAIR_W)
            y, s_new = _wkv_chunk(r_ref[0, rows, cols], lw_ref[0, rows, cols], kx_ref[0, rows, cols],
                                  v_ref[0, rows, cols], na_ref[0, rows, cols], kb_ref[0, rows, cols],
                                  s_scr[p], consts)
            y_ref[0, rows, cols] = y
            s_scr[p] = s_new

    @pl.when(t == pl.num_programs(1) - 1)
    def _():
        sf_ref[0] = s_scr[...]


def _wkv_scan(r, lw, kx, v, na, kb, s0_pair, tt):
    b, t, _ = r.shape
    assert CHUNK == HEAD_DIM and tt % CHUNK == 0
    seq = pl.BlockSpec((1, tt, RWKV_W), lambda i, j: (i, j, 0))
    st = pl.BlockSpec((1, N_PAIRS, PAIR_W, PAIR_W), lambda i, j: (i, 0, 0, 0))
    est = 2 * 7 * tt * RWKV_W * 4 + 5 * N_PAIRS * PAIR_W * PAIR_W * 4 + (8 << 20)
    return pl.pallas_call(
        _wkv_kernel,
        grid=(b, t // tt),
        in_specs=[seq] * 6 + [st],
        out_specs=[seq, st],
        out_shape=[jax.ShapeDtypeStruct((b, t, RWKV_W), F32),
                   jax.ShapeDtypeStruct((b, N_PAIRS, PAIR_W, PAIR_W), F32)],
        scratch_shapes=[pltpu.VMEM((N_PAIRS, PAIR_W, PAIR_W), F32)],
        compiler_params=_params(2, est),
        name="wkv_scan",
    )(r, lw, kx, v, na, kb, s0_pair)


def _state_to_pairs(s):
    b = s.shape[0]
    s = s.astype(F32).reshape(b, N_PAIRS, 2, HEAD_DIM, HEAD_DIM)
    out = jnp.zeros((b, N_PAIRS, PAIR_W, PAIR_W), F32)
    out = out.at[:, :, :HEAD_DIM, :HEAD_DIM].set(s[:, :, 0])
    return out.at[:, :, HEAD_DIM:, HEAD_DIM:].set(s[:, :, 1])


def _pairs_to_state(sp):
    b = sp.shape[0]
    s = jnp.stack([sp[:, :, :HEAD_DIM, :HEAD_DIM], sp[:, :, HEAD_DIM:, HEAD_DIM:]], axis=2)
    return s.reshape(b, N_HEADS, HEAD_DIM, HEAD_DIM)


def _mix_out_kernel(y_ref, bonus_ref, g_ref, po_ref, x_ref, lg_ref, lb_ref, ones_ref, wo_ref, ln2_ref,
                    rw_hi_ref, rw_lo_ref, rb_ref, cin_ref,
                    h_o, xn_o, route_o, gate_o, cnt_o, carry):
    i = pl.program_id(0)
    tm = y_ref.shape[0]

    @pl.when(i == 0)
    def _():
        carry[...] = cin_ref[...]

    ones_bd = ones_ref[...]
    y = y_ref[...]
    inv_n = 1.0 / HEAD_DIM
    mu = _group_sum(y, ones_bd) * inv_n
    dlt = y - mu
    var = _group_sum(dlt * dlt, ones_bd) * inv_n
    yn = dlt * lax.rsqrt(var + LNX_EPS) * lg_ref[...] + lb_ref[...]
    rw_out = (yn + bonus_ref[...]) * g_ref[...]
    mix = _dot(po_ref[...], wo_ref[:POOL_W, :]) + _dot(rw_out.astype(BF16), wo_ref[POOL_W:, :])
    h = x_ref[...] + mix
    h_o[...] = h
    xn = _rmsnorm_rows(h, ln2_ref[...])
    xn_o[...] = xn

    x_hi, x_lo = _split2(xn)
    logits = (_dot(x_hi, rw_hi_ref[...]) + _dot(x_hi, rw_lo_ref[...]) + _dot(x_lo, rw_hi_ref[...])
              + rb_ref[...])
    lane = lax.broadcasted_iota(I32, (tm, ROUTE_LANES), 1)
    lane_f = lane.astype(F32)
    vals, idxs, hots = [], [], []
    work = logits
    for _ in range(TOP_K):
        m = jnp.max(work, axis=-1, keepdims=True)
        idx = jnp.min(jnp.where(work == m, lane_f, float(ROUTE_LANES)), axis=-1, keepdims=True)
        hit = lane_f == idx
        vals.append(m)
        idxs.append(idx)
        hots.append(jnp.where(hit, 1.0, 0.0).astype(F32))
        work = jnp.where(hit, -jnp.inf, work)
    exps = [jnp.exp(vv - vals[0]) for vv in vals]
    den = exps[0] + exps[1] + exps[2] + exps[3]

    hot_all = hots[0] + hots[1] + hots[2] + hots[3]
    rr = lax.broadcasted_iota(I32, (tm, tm), 0)
    cc = lax.broadcasted_iota(I32, (tm, tm), 1)
    tri_strict = jnp.where(cc < rr, 1.0, 0.0).astype(BF16)
    prefix = _dot(tri_strict, hot_all.astype(BF16)) + carry[0:1, :]
    route = jnp.zeros((tm, ROUTE_LANES), F32)
    gates = jnp.zeros((tm, ROUTE_LANES), F32)
    for j in range(TOP_K):
        rank = jnp.sum(hots[j] * prefix, axis=-1, keepdims=True)
        route = jnp.where(lane == j, idxs[j], route)
        route = jnp.where(lane == TOP_K + j, rank, route)
        gates = jnp.where(lane == j, exps[j] / den, gates)
    route_o[...] = route.astype(I32)
    gate_o[...] = gates
    new_carry = carry[...] + jnp.sum(hot_all, axis=0, keepdims=True)
    carry[...] = new_carry
    cnt_o[...] = new_carry


def _mix_out(y2d, bonus2d, g2d, po2d, x2d, lnx_g, lnx_b, ones_bd, w_out_bf, ln2_g,
             rw_hi, rw_lo, rb_pad, carry_in, tm):
    n = x2d.shape[0]
    row = lambda x: x.reshape(1, -1)
    cst = lambda shape: pl.BlockSpec(shape, lambda i: (0,) * len(shape))
    half = pl.BlockSpec((tm, RWKV_W), lambda i: (i, 0))
    full = pl.BlockSpec((tm, D_MODEL), lambda i: (i, 0))
    lanes = pl.BlockSpec((tm, ROUTE_LANES), lambda i: (i, 0))
    est = (2 * (4 * tm * RWKV_W * 4 + 3 * tm * D_MODEL * 4) + 2 * (D_MODEL * D_MODEL * 2 + RWKV_W * RWKV_W * 2)
           + 8 * tm * D_MODEL * 4 + 4 * tm * tm)
    return pl.pallas_call(
        _mix_out_kernel,
        grid=(n // tm,),
        in_specs=[half, half, half, half, full,
                  cst((1, RWKV_W)), cst((1, RWKV_W)), cst((RWKV_W, RWKV_W)), cst((D_MODEL, D_MODEL)),
                  cst((1, D_MODEL)), cst((D_MODEL, ROUTE_LANES)), cst((D_MODEL, ROUTE_LANES)),
                  cst((1, ROUTE_LANES)), cst((SUBLANES, ROUTE_LANES))],
        out_specs=[full, full, lanes, lanes, cst((SUBLANES, ROUTE_LANES))],
        out_shape=[jax.ShapeDtypeStruct((n, D_MODEL), F32), jax.ShapeDtypeStruct((n, D_MODEL), F32),
                   jax.ShapeDtypeStruct((n, ROUTE_LANES), I32), jax.ShapeDtypeStruct((n, ROUTE_LANES), F32),
                   jax.ShapeDtypeStruct((SUBLANES, ROUTE_LANES), F32)],
        scratch_shapes=[pltpu.VMEM((SUBLANES, ROUTE_LANES), F32)],
        compiler_params=_params(1, est),
        name="mix_out_router",
    )(y2d, bonus2d, g2d, po2d, x2d, row(lnx_g), row(lnx_b), ones_bd, w_out_bf, row(ln2_g),
      rw_hi, rw_lo, rb_pad, carry_in)


def _row_copy(src, src_row, dst, dst_row, sem):
    return pltpu.make_async_copy(src.at[pl.ds(src_row, 1), :], dst.at[pl.ds(dst_row, 1), :], sem)


def _dispatch_kernel(dest_ref, x_hbm, xs_in, xs_out, sem, *, tm):
    del xs_in
    base = pl.program_id(0) * tm

    def issue(n, carry):
        for j in range(TOP_K):
            _row_copy(x_hbm, base + n, xs_out, dest_ref[n * TOP_K + j], sem).start()
        return carry

    lax.fori_loop(0, tm, issue, 0)
    pltpu.make_async_copy(xs_out.at[pl.ds(0, tm * TOP_K), :], xs_out.at[pl.ds(0, tm * TOP_K), :], sem).wait()


def _dispatch(dest_flat, xn2d, xs, tm):
    n = xn2d.shape[0]
    return pl.pallas_call(
        functools.partial(_dispatch_kernel, tm=tm),
        grid=(n // tm,),
        in_specs=[pl.BlockSpec((tm * TOP_K,), lambda i: (i,), memory_space=pltpu.SMEM),
                  pl.BlockSpec(memory_space=pl.ANY),
                  pl.BlockSpec(memory_space=pl.ANY)],
        out_specs=pl.BlockSpec(memory_space=pl.ANY),
        out_shape=jax.ShapeDtypeStruct(xs.shape, xs.dtype),
        scratch_shapes=[pltpu.SemaphoreType.DMA(())],
        input_output_aliases={2: 0},
        compiler_params=pltpu.CompilerParams(dimension_semantics=("arbitrary",)),
        name="dispatch",
    )(dest_flat, xn2d, xs)


def _expert_kernel(be_ref, nu_ref, xs_ref, wgu_ref, bgu_ref, wdn_ref, bdn_ref, o_ref, wgu_bf, wdn_bf):
    i = pl.program_id(0)
    used = i < nu_ref[0]
    changed = jnp.logical_or(i == 0, be_ref[i] != be_ref[jnp.maximum(i - 1, 0)])

    @pl.when(jnp.logical_and(used, changed))
    def _():
        wgu_bf[...] = wgu_ref[0].astype(BF16)
        wdn_bf[...] = wdn_ref[0].astype(BF16)

    @pl.when(used)
    def _():
        gu = _dot(xs_ref[...].astype(BF16), wgu_bf[...]) + bgu_ref[0]
        gate = jnp.minimum(gu[:, :D_EXPERT], SWIGLU_LIMIT)
        up = jnp.clip(gu[:, D_EXPERT:], -SWIGLU_LIMIT, SWIGLU_LIMIT)
        hmid = (up + 1.0) * (gate * jax.nn.sigmoid(SWIGLU_ALPHA * gate))
        o_ref[...] = _dot(hmid.astype(BF16), wdn_bf[...]) + bdn_ref[0]

    @pl.when(jnp.logical_not(used))
    def _():
        o_ref[...] = jnp.zeros_like(o_ref)


def _experts(block_e, n_used, xs, w_gu, b_gu, w_down, b_down):
    n_slots = xs.shape[0]
    nb = n_slots // EXPERT_BLOCK
    blk = EXPERT_BLOCK
    est = (2 * 2 * blk * D_MODEL * 4 + 2 * (D_MODEL * 2 * D_EXPERT + D_EXPERT * D_MODEL) * 4
           + (D_MODEL * 2 * D_EXPERT + D_EXPERT * D_MODEL) * 2 + 4 * blk * 2 * D_EXPERT * 4)
    grid_spec = pltpu.PrefetchScalarGridSpec(
        num_scalar_prefetch=2,
        grid=(nb,),
        in_specs=[pl.BlockSpec((blk, D_MODEL), lambda i, be, nu: (i, 0)),
                  pl.BlockSpec((1, D_MODEL, 2 * D_EXPERT), lambda i, be, nu: (be[i], 0, 0)),
                  pl.BlockSpec((1, 1, 2 * D_EXPERT), lambda i, be, nu: (be[i], 0, 0)),
                  pl.BlockSpec((1, D_EXPERT, D_MODEL), lambda i, be, nu: (be[i], 0, 0)),
                  pl.BlockSpec((1, 1, D_MODEL), lambda i, be, nu: (be[i], 0, 0))],
        out_specs=pl.BlockSpec((blk, D_MODEL), lambda i, be, nu: (i, 0)),
        scratch_shapes=[pltpu.VMEM((D_MODEL, 2 * D_EXPERT), BF16), pltpu.VMEM((D_EXPERT, D_MODEL), BF16)],
    )
    return pl.pallas_call(
        _expert_kernel,
        grid_spec=grid_spec,
        out_shape=jax.ShapeDtypeStruct((n_slots, D_MODEL), F32),
        compiler_params=_params(1, est),
        name="experts",
    )(block_e, n_used, xs, w_gu, b_gu.reshape(N_EXPERTS, 1, 2 * D_EXPERT), w_down,
      b_down.reshape(N_EXPERTS, 1, D_MODEL))


def _combine_kernel(dest_ref, ys_hbm, gate_ref, h_ref, p_ref, ln3_ref, pg_ref, pp_ref, fg_ref,
                    o_ref, gbuf, sem, *, tm):
    def issue(n, carry):
        for j in range(TOP_K):
            _row_copy(ys_hbm, dest_ref[n * TOP_K + j], gbuf.at[j], n, sem).start()
        return carry

    lax.fori_loop(0, tm, issue, 0)
    for j in range(TOP_K):
        pltpu.make_async_copy(gbuf.at[j], gbuf.at[j], sem).wait()

    gates = gate_ref[...]
    moe = gates[:, 0:1] * gbuf[0]
    for j in range(1, TOP_K):
        moe = moe + gates[:, j:j + 1] * gbuf[j]
    h = h_ref[...] + moe
    gate = jax.nn.sigmoid(_dot(_rmsnorm_rows(h, ln3_ref[...]).astype(BF16), pg_ref[...]))
    h = h + gate * _dot(p_ref[...].astype(BF16), pp_ref[...])
    o_ref[...] = _rmsnorm_rows(h, fg_ref[...])


def _combine(dest_flat, ys, gates, h2d, p2d, ln3_g, ple_gate_bf, ple_proj_bf, final_g, tm):
    n = h2d.shape[0]
    row = lambda x: x.reshape(1, -1)
    cst = lambda shape: pl.BlockSpec(shape, lambda i: (0,) * len(shape))
    est = (TOP_K * tm * D_MODEL * 4 + 2 * (2 * tm * D_MODEL * 4 + tm * PLE_DIM * 4 + tm * ROUTE_LANES * 4)
           + 2 * (D_MODEL * D_MODEL * 2 + PLE_DIM * D_MODEL * 2) + 6 * tm * D_MODEL * 4)
    return pl.pallas_call(
        functools.partial(_combine_kernel, tm=tm),
        grid=(n // tm,),
        in_specs=[pl.BlockSpec((tm * TOP_K,), lambda i: (i,), memory_space=pltpu.SMEM),
                  pl.BlockSpec(memory_space=pl.ANY),
                  pl.BlockSpec((tm, ROUTE_LANES), lambda i: (i, 0)),
                  pl.BlockSpec((tm, D_MODEL), lambda i: (i, 0)),
                  pl.BlockSpec((tm, PLE_DIM), lambda i: (i, 0)),
                  cst((1, D_MODEL)), cst((D_MODEL, D_MODEL)), cst((PLE_DIM, D_MODEL)), cst((1, D_MODEL))],
        out_specs=pl.BlockSpec((tm, D_MODEL), lambda i: (i, 0)),
        out_shape=jax.ShapeDtypeStruct((n, D_MODEL), F32),
        scratch_shapes=[pltpu.VMEM((TOP_K, tm, D_MODEL), F32), pltpu.SemaphoreType.DMA(())],
        compiler_params=_params(1, est),
        name="combine_ple_norm",
    )(dest_flat, ys, gates, h2d, p2d, row(ln3_g), ple_gate_bf, ple_proj_bf, row(final_g))


def _tile(n, pref):
    t = min(n, pref)
    assert n % t == 0, (n, pref)
    return t


def _front(x, pos0, pool_ctx, shift_prev, wkv0, wts, carry_in):
    b, t, _ = x.shape
    n = b * t
    assert t >= POOL_CTX and t % SUBLANES == 0
    x2d = x.reshape(n, D_MODEL)
    u2d, zr2d = _in_proj(x2d, wts["ln1_g"], wts["w_in_bf"], _tile(n, 512))
    u = u2d.reshape(b, t, POOL_W)
    zr = zr2d.reshape(b, t, RWKV_PROJ)

    ctx16 = jnp.concatenate([jnp.zeros((b, POOL_HALO - POOL_CTX, POOL_W), F32), pool_ctx.astype(F32)], axis=1)
    pool_out = _pool_mixer(u, ctx16, wts["w_pool_bf"], wts["pool_scale"], pos0, _tile(t, 512))
    new_pool = u[:, -POOL_CTX:]

    sh8 = jnp.broadcast_to(shift_prev.astype(F32), (b, SUBLANES, RWKV_PROJ))
    r, lw, kx, v, na, kb, g, bonus = _rwkv_pre(
        zr, sh8, wts["mu_shift"], wts["w0"], wts["w2_bf"], wts["a0"], wts["a2_bf"], wts["g2_bf"],
        wts["k_k"], wts["k_a"], wts["r_k"], wts["ones_bd"], _tile(t, 256))
    new_shift = zr[:, -1:]

    t_pad = -(-t // CHUNK) * CHUNK
    seqs = (r, lw, kx, v, na, kb)
    if t_pad != t:
        seqs = tuple(jnp.pad(s, ((0, 0), (0, t_pad - t), (0, 0))) for s in seqs)
    y, s_fin = _wkv_scan(*seqs, _state_to_pairs(wkv0), _tile(t_pad, 2 * CHUNK))
    y = y[:, :t]
    new_wkv = _pairs_to_state(s_fin)

    tm = _tile(n, 256)
    h, xn, route, gates, counts = _mix_out(
        y.reshape(n, RWKV_W), bonus.reshape(n, RWKV_W), g.reshape(n, RWKV_W), pool_out.reshape(n, POOL_W),
        x2d, wts["lnx_g"], wts["lnx_b"], wts["ones_bd"], wts["w_out_bf"], wts["ln2_g"],
        wts["rw_hi"], wts["rw_lo"], wts["rb_pad"], carry_in, tm)
    return dict(h=h, xn=xn, route=route, gates=gates, counts=counts,
                new_pool=new_pool, new_shift=new_shift, new_wkv=new_wkv)


def _prep_weights(i, ln1_g, w_in, mu_shift, w_pool, pool_scale, w0, w2, a0, a2, g2, k_k, k_a, r_k,
                  lnx_g, lnx_b, w_out, ln2_g, router_w, router_b, ln3_g, ple_gate, ple_proj):
    head = jnp.arange(RWKV_W) // HEAD_DIM
    ones_bd = (head[:, None] == head[None, :]).astype(BF16)
    rw_pad = jnp.pad(router_w[i].astype(F32), ((0, 0), (0, ROUTE_LANES - N_EXPERTS)))
    rw_hi = rw_pad.astype(BF16)
    rw_lo = (rw_pad - rw_hi.astype(F32)).astype(BF16)
    rb_pad = jnp.concatenate([router_b[i].astype(F32), jnp.full((ROUTE_LANES - N_EXPERTS,), -jnp.inf, F32)])
    return dict(
        ln1_g=ln1_g[i], w_in_bf=w_in[i].astype(BF16), mu_shift=mu_shift[i], w_pool_bf=w_pool[i].astype(BF16),
        pool_scale=pool_scale[i], w0=w0[i], w2_bf=w2[i].astype(BF16), a0=a0[i], a2_bf=a2[i].astype(BF16),
        g2_bf=g2[i].astype(BF16), k_k=k_k[i], k_a=k_a[i], r_k=r_k[i].reshape(RWKV_W), lnx_g=lnx_g[i],
        lnx_b=lnx_b[i], w_out_bf=w_out[i].astype(BF16), ln2_g=ln2_g[i], rw_hi=rw_hi, rw_lo=rw_lo,
        rb_pad=rb_pad.reshape(1, ROUTE_LANES), ones_bd=ones_bd, ln3_g=ln3_g[i],
        ple_gate_bf=ple_gate[i].astype(BF16), ple_proj_bf=ple_proj[i].astype(BF16))


def _layer_pair(hp, hs, pp, ps, pool_ctx, shift_prev, wkv0, wts, w_gu, b_gu, w_down, b_down, final_g):
    bp, tp, _ = hp.shape
    bs, ts, _ = hs.shape
    np_, ns = bp * tp, bs * ts
    zero_carry = jnp.zeros((SUBLANES, ROUTE_LANES), F32)
    fp = _front(hp, 0, jnp.zeros((bp, POOL_CTX, POOL_W), F32), jnp.zeros((bp, 1, RWKV_PROJ), F32),
                jnp.zeros((bp, N_HEADS, HEAD_DIM, HEAD_DIM), F32), wts, zero_carry)
    fs = _front(hs, PAST_LEN, pool_ctx, shift_prev, wkv0, wts, fp["counts"])

    counts = fs["counts"][0, :N_EXPERTS].astype(I32)
    padded = (counts + EXPERT_BLOCK - 1) // EXPERT_BLOCK * EXPERT_BLOCK
    pad_end = jnp.cumsum(padded)
    start_pad = pad_end - padded
    n_blocks = -(-(np_ + ns) * TOP_K // EXPERT_BLOCK) + N_EXPERTS
    n_slots = n_blocks * EXPERT_BLOCK
    n_used = (pad_end[-1] // EXPERT_BLOCK).astype(I32)
    block_start = jnp.arange(n_blocks, dtype=I32) * EXPERT_BLOCK
    block_e = jnp.minimum(jnp.searchsorted(pad_end, block_start, side="right"), N_EXPERTS - 1).astype(I32)
    last_e = block_e[jnp.maximum(n_used - 1, 0)]
    block_e = jnp.where(jnp.arange(n_blocks) < n_used, block_e, last_e)

    def dest_of(route):
        idx = route[:, :TOP_K]
        rank = route[:, TOP_K:2 * TOP_K]
        return (start_pad[idx] + rank).reshape(-1).astype(I32)

    dest_p, dest_s = dest_of(fp["route"]), dest_of(fs["route"])
    xs = jnp.zeros((n_slots, D_MODEL), F32)
    xs = _dispatch(dest_p, fp["xn"], xs, _tile(np_, 512))
    xs = _dispatch(dest_s, fs["xn"], xs, _tile(ns, 512))
    ys = _experts(block_e, n_used.reshape(1), xs, w_gu, b_gu, w_down, b_down)

    outs = []
    for f, dest, p in ((fp, dest_p, pp), (fs, dest_s, ps)):
        n = f["h"].shape[0]
        outs.append(_combine(dest, ys, f["gates"], f["h"], p.reshape(n, PLE_DIM).astype(F32), wts["ln3_g"],
                             wts["ple_gate_bf"], wts["ple_proj_bf"], final_g, _tile(n, 256)))
    return outs[0].reshape(hp.shape), outs[1].reshape(hs.shape), fp, fs


def kernel(x_prompt, x_sample, p_prompt, p_sample, cache_pool, state_shift, state_wkv, ln1_g, w_in, mu_shift, w_pool, pool_scale, w0, w2, a0, a2, g2, k_k, k_a, r_k, lnx_g, lnx_b, w_out, ln2_g, router_w, router_b, w_gu, b_gu, w_down, b_down, ln3_g, ple_gate, ple_proj, final_g):
    assert ln1_g.shape[0] == 1, "single-layer kernel"
    i = 0
    wts = _prep_weights(i, ln1_g, w_in, mu_shift, w_pool, pool_scale, w0, w2, a0, a2, g2, k_k, k_a, r_k,
                        lnx_g, lnx_b, w_out, ln2_g, router_w, router_b, ln3_g, ple_gate, ple_proj)
    y_p, y_s, fp, fs = _layer_pair(x_prompt, x_sample, p_prompt[i], p_sample[i], cache_pool[i], state_shift[i],
                                   state_wkv[i], wts, w_gu[i], b_gu[i], w_down[i], b_down[i], final_g)
    stack = lambda a: a[None]
    return (y_p, y_s,
            stack(fp["new_pool"]), stack(fp["new_shift"]), stack(fp["new_wkv"]),
            stack(fs["new_pool"]), stack(fs["new_shift"]), stack(fs["new_wkv"]))
```

```python
import functools

import jax
import jax.numpy as jnp
from jax import lax
from jax.experimental import pallas as pl
from jax.experimental.pallas import tpu as pltpu

F32 = jnp.float32
BF16 = jnp.bfloat16
I32 = jnp.int32

D_MODEL = 1024
POOL_W = 512
POOL_WINDOWS = (2, 4, 8, 16)
POOL_GW = 128
POOL_CTX = 15
POOL_HALO = 16
RWKV_W = 512
HEAD_DIM = 64
N_HEADS = 8
N_PAIRS = N_HEADS // 2
PAIR_W = 2 * HEAD_DIM
DECAY_LORA = 64
ICLR_LORA = 64
GATE_LORA = 128
RWKV_PROJ = 3 * RWKV_W + DECAY_LORA + ICLR_LORA + GATE_LORA
IN_PROJ = POOL_W + RWKV_PROJ
LNX_EPS = 64e-5
N_EXPERTS = 32
TOP_K = 4
D_EXPERT = 1024
SWIGLU_LIMIT = 7.0
SWIGLU_ALPHA = 1.702
PLE_DIM = 256
RMS_EPS = 1e-6
PAST_LEN = 2048

CHUNK = 64
EXPERT_BLOCK = 256
ROUTE_LANES = 128
SUBLANES = 8
EXP_M05 = 0.6065306597126334
V7X_VMEM_BYTES = 64 * 1024 * 1024

NT_DIMS = (((1,), (1,)), ((), ()))
TN_DIMS = (((0,), (0,)), ((), ()))


def _vmem_limit(est_bytes):
    return int(min(est_bytes * 3 // 2 + (4 << 20), V7X_VMEM_BYTES - (8 << 20)))


def _params(n_axes, est_bytes):
    return pltpu.CompilerParams(dimension_semantics=("arbitrary",) * n_axes,
                                vmem_limit_bytes=_vmem_limit(est_bytes))


def _dot(a, b):
    return jnp.dot(a, b, preferred_element_type=F32)


def _split2(x):
    hi = x.astype(BF16)
    lo = (x - hi.astype(F32)).astype(BF16)
    return hi, lo


def _split3(x):
    hi = x.astype(BF16)
    r1 = x - hi.astype(F32)
    mid = r1.astype(BF16)
    lo = (r1 - mid.astype(F32)).astype(BF16)
    return hi, mid, lo


def _group_sum(x, ones_bd):
    hi, lo = _split2(x)
    return _dot(hi, ones_bd) + _dot(lo, ones_bd)


def _rmsnorm_rows(x, g):
    ms = jnp.mean(x * x, axis=-1, keepdims=True)
    return (x * lax.rsqrt(ms + RMS_EPS)) * g


def _in_proj_kernel(x_ref, g_ref, w_ref, u_ref, zr_ref):
    xn = _rmsnorm_rows(x_ref[...], g_ref[...])
    z = _dot(xn.astype(BF16), w_ref[...])
    u_ref[...] = z[:, :POOL_W]
    zr_ref[...] = z[:, POOL_W:]


def _in_proj(x2d, ln1_g, w_in_bf, tm):
    n = x2d.shape[0]
    est = 2 * tm * D_MODEL * 4 + 2 * D_MODEL * IN_PROJ * 2 + 3 * tm * IN_PROJ * 4
    return pl.pallas_call(
        _in_proj_kernel,
        grid=(n // tm,),
        in_specs=[pl.BlockSpec((tm, D_MODEL), lambda i: (i, 0)),
                  pl.BlockSpec((1, D_MODEL), lambda i: (0, 0)),
                  pl.BlockSpec((D_MODEL, IN_PROJ), lambda i: (0, 0))],
        out_specs=[pl.BlockSpec((tm, POOL_W), lambda i: (i, 0)),
                   pl.BlockSpec((tm, RWKV_PROJ), lambda i: (i, 0))],
        out_shape=[jax.ShapeDtypeStruct((n, POOL_W), F32),
                   jax.ShapeDtypeStruct((n, RWKV_PROJ), F32)],
        compiler_params=_params(1, est),
        name="in_proj",
    )(x2d, ln1_g.reshape(1, D_MODEL), w_in_bf)


def _pool_kernel(u_ref, halo_ref, ctx_ref, wp_ref, ps_ref, o_ref, buf, *, pos0):
    t = pl.program_id(1)
    tt = u_ref.shape[1]
    buf[0:POOL_HALO, :] = jnp.where(t == 0, ctx_ref[0], halo_ref[0])
    buf[POOL_HALO:, :] = u_ref[0]
    pos = lax.broadcasted_iota(I32, (tt, 1), 0) + (t * tt + pos0)
    outs = []
    for gi, w in enumerate(POOL_WINDOWS):
        sl = slice(gi * POOL_GW, (gi + 1) * POOL_GW)
        cur = buf[POOL_HALO:POOL_HALO + tt, sl]
        acc = cur
        for j in range(1, w):
            acc = acc + buf[POOL_HALO - j:POOL_HALO - j + tt, sl]
        cnt = jnp.minimum(w, pos + 1).astype(F32)
        pooled = acc / cnt - cur
        outs.append(_dot(pooled.astype(BF16), wp_ref[gi]))
    o_ref[0] = (jnp.concatenate(outs, axis=-1) * ps_ref[...]).astype(o_ref.dtype)


def _pool_mixer(u, ctx16, w_pool_bf, pool_scale, pos0, tt):
    b, t, _ = u.shape
    hb = tt // POOL_HALO
    est = 2 * tt * POOL_W * 4 * 3 + (tt + POOL_HALO) * POOL_W * 4
    return pl.pallas_call(
        functools.partial(_pool_kernel, pos0=pos0),
        grid=(b, t // tt),
        in_specs=[pl.BlockSpec((1, tt, POOL_W), lambda i, j: (i, j, 0)),
                  pl.BlockSpec((1, POOL_HALO, POOL_W), lambda i, j: (i, jnp.maximum(j * hb - 1, 0), 0)),
                  pl.BlockSpec((1, POOL_HALO, POOL_W), lambda i, j: (i, 0, 0)),
                  pl.BlockSpec((len(POOL_WINDOWS), POOL_GW, POOL_GW), lambda i, j: (0, 0, 0)),
                  pl.BlockSpec((1, POOL_W), lambda i, j: (0, 0))],
        out_specs=pl.BlockSpec((1, tt, POOL_W), lambda i, j: (i, j, 0)),
        out_shape=jax.ShapeDtypeStruct((b, t, POOL_W), BF16),
        scratch_shapes=[pltpu.VMEM((tt + POOL_HALO, POOL_W), F32)],
        compiler_params=_params(2, est),
        name="pool_mixer",
    )(u, u, ctx16, w_pool_bf, pool_scale.reshape(1, POOL_W))


def _rwkv_pre_kernel(zr_ref, halo_ref, sh_ref, mu_ref, w0_ref, w2_ref, a0_ref, a2_ref, g2_ref,
                     kk_ref, ka_ref, rk_ref, ones_ref,
                     r_o, lw_o, kx_o, v_o, na_o, kb_o, g_o, bonus_o, buf):
    t = pl.program_id(1)
    tt = zr_ref.shape[1]
    buf[0:SUBLANES, :] = jnp.where(t == 0, sh_ref[0], halo_ref[0])
    zr = zr_ref[0]
    buf[SUBLANES:, :] = zr
    prev = buf[SUBLANES - 1:SUBLANES - 1 + tt, :]
    zs = zr + (prev - zr) * mu_ref[...]
    o1, o2, o3 = RWKV_W, 2 * RWKV_W, 3 * RWKV_W
    o4, o5 = o3 + DECAY_LORA, o3 + DECAY_LORA + ICLR_LORA
    r, k, v = zs[:, :o1], zs[:, o1:o2], zs[:, o2:o3]
    wd, ad, gd = zs[:, o3:o4], zs[:, o4:o5], zs[:, o5:]
    ones_bd = ones_ref[...]
    wz = w0_ref[...] + _dot(jnp.tanh(wd).astype(BF16), w2_ref[...])
    lw = -EXP_M05 * jax.nn.sigmoid(wz)
    a = jax.nn.sigmoid(a0_ref[...] + _dot(ad.astype(BF16), a2_ref[...]))
    g = _dot(jax.nn.sigmoid(gd).astype(BF16), g2_ref[...])
    kk = k * kk_ref[...]
    kk = kk / jnp.maximum(jnp.sqrt(_group_sum(kk * kk, ones_bd)), 1e-12)
    kx = k * (1.0 + (a - 1.0) * ka_ref[...])
    bonus = _group_sum(r * kx * rk_ref[...], ones_bd) * v
    r_o[0] = r
    lw_o[0] = lw
    kx_o[0] = kx
    v_o[0] = v
    na_o[0] = -kk
    kb_o[0] = kk * a
    g_o[0] = g
    bonus_o[0] = bonus


def _rwkv_pre(zr, sh8, mu, w0, w2_bf, a0, a2_bf, g2_bf, k_k, k_a, r_k, ones_bd, tt):
    b, t, _ = zr.shape
    hb = tt // SUBLANES
    row = lambda x: x.reshape(1, -1)
    cst = lambda shape: pl.BlockSpec(shape, lambda i, j: (0,) * len(shape))
    seq = pl.BlockSpec((1, tt, RWKV_W), lambda i, j: (i, j, 0))
    est = 3 * tt * RWKV_PROJ * 4 + 2 * 8 * tt * RWKV_W * 4 + 12 * tt * RWKV_W * 4
    return pl.pallas_call(
        _rwkv_pre_kernel,
        grid=(b, t // tt),
        in_specs=[pl.BlockSpec((1, tt, RWKV_PROJ), lambda i, j: (i, j, 0)),
                  pl.BlockSpec((1, SUBLANES, RWKV_PROJ), lambda i, j: (i, jnp.maximum(j * hb - 1, 0), 0)),
                  pl.BlockSpec((1, SUBLANES, RWKV_PROJ), lambda i, j: (i, 0, 0)),
                  cst((1, RWKV_PROJ)), cst((1, RWKV_W)), cst((DECAY_LORA, RWKV_W)),
                  cst((1, RWKV_W)), cst((ICLR_LORA, RWKV_W)), cst((GATE_LORA, RWKV_W)),
                  cst((1, RWKV_W)), cst((1, RWKV_W)), cst((1, RWKV_W)), cst((RWKV_W, RWKV_W))],
        out_specs=[seq] * 8,
        out_shape=[jax.ShapeDtypeStruct((b, t, RWKV_W), F32)] * 8,
        scratch_shapes=[pltpu.VMEM((tt + SUBLANES, RWKV_PROJ), F32)],
        compiler_params=_params(2, est),
        name="rwkv_pre",
    )(zr, zr, sh8, row(mu), row(w0), w2_bf, row(a0), a2_bf, g2_bf, row(k_k), row(k_a), row(r_k), ones_bd)


def _pair_blockdiag(z, left):
    return jnp.concatenate([jnp.where(left, z, 0.0), jnp.where(left, 0.0, z)], axis=0)


def _wkv_kernel(r_ref, lw_ref, kx_ref, v_ref, na_ref, kb_ref, s0_ref, y_ref, sf_ref, s_scr):
    t = pl.program_id(1)
    tt = r_ref.shape[1]
    c = CHUNK

    @pl.when(t == 0)
    def _():
        s_scr[...] = s0_ref[0]

    lane = lax.broadcasted_iota(I32, (c, 2 * c), 1)
    rowi = lax.broadcasted_iota(I32, (c, 2 * c), 0)
    jm = lane & (c - 1)
    strict = jm < rowi
    incl = jm <= rowi
    eye_pair = jnp.where(jm == rowi, 1.0, 0.0).astype(F32)
    left = lax.broadcasted_iota(I32, (c, PAIR_W), 1) < HEAD_DIM
    rr = lax.broadcasted_iota(I32, (c, c), 0)
    cc = lax.broadcasted_iota(I32, (c, c), 1)
    tri_incl_bf = jnp.where(cc <= rr, 1.0, 0.0).astype(BF16)
    br = lax.broadcasted_iota(I32, (PAIR_W, PAIR_W), 0) < HEAD_DIM
    bc = lax.broadcasted_iota(I32, (PAIR_W, PAIR_W), 1) < HEAD_DIM
    bd_mask = br == bc
    bf = lambda x: x.astype(BF16)
    bd = lambda z: _pair_blockdiag(z, left)
    n_chunks = tt // c

    cums = []
    for ci in range(n_chunks):
        h3 = _split3(lw_ref[0, ci * c:(ci + 1) * c, :])
        cums.append(_dot(tri_incl_bf, h3[0]) + _dot(tri_incl_bf, h3[1]) + _dot(tri_incl_bf, h3[2]))

    chains = []
    for ci in range(n_chunks):
        rows = slice(ci * c, (ci + 1) * c)
        for p in range(N_PAIRS):
            cols = slice(p * PAIR_W, (p + 1) * PAIR_W)
            cum = cums[ci][:, cols]
            lw, kx, kb = lw_ref[0, rows, cols], kx_ref[0, rows, cols], kb_ref[0, rows, cols]
            tot = cum[c - 1:c, :]
            p_end = jnp.exp(tot - cum)
            inv_p = jnp.exp(-cum)
            chains.append(dict(
                ci=ci, p=p, v=v_ref[0, rows, cols], p_c=jnp.exp(tot),
                a_t=na_ref[0, rows, cols] * jnp.exp(cum - lw), r_t=r_ref[0, rows, cols] * jnp.exp(cum),
                b_t=kb * inv_p, k_t=kx * inv_p, b_h=kb * p_end, k_h=kx * p_end))

    for d in chains:
        lhs = bf(jnp.concatenate([d["a_t"], d["r_t"]], axis=0))
        rhs = bf(jnp.concatenate([bd(d["b_t"]), bd(d["k_t"])], axis=0))
        sc = lax.dot_general(lhs, rhs, NT_DIMS, preferred_element_type=F32)
        d["a_ab"] = jnp.where(strict, sc[:c, :2 * c], 0.0)
        d["a_k"] = jnp.concatenate([jnp.where(strict, sc[:c, 2 * c:], 0.0),
                                    jnp.where(incl, sc[c:, 2 * c:], 0.0)], axis=0)
        d["a_rb"] = jnp.where(incl, sc[c:, :2 * c], 0.0)

    for d in chains:
        d["x"] = _dot(bf(d["a_ab"]), bf(bd(d["a_ab"])))
        d["t"] = eye_pair + d["a_ab"]
    for _ in range(c.bit_length() - 3):
        for d in chains:
            st = _dot(bf(jnp.concatenate([d["x"], d["t"]], axis=0)), bf(bd(d["x"])))
            d["x"] = st[:c]
            d["t"] = d["t"] + st[c:]
    for d in chains:
        d["t"] = d["t"] + _dot(bf(d["t"]), bf(bd(d["x"])))

    for d in chains:
        wv = _dot(bf(d["a_k"]), bf(bd(d["v"])))
        d["w1"], d["rkv"] = wv[:c], wv[c:]
    for d in chains:
        tu = _dot(bf(d["t"]), bf(jnp.concatenate([bd(d["w1"]), bd(d["a_t"])], axis=1)))
        d["u_loc"], d["a_tt"] = tu[:, :PAIR_W], tu[:, PAIR_W:]
    for d in chains:
        ar = _dot(bf(d["a_rb"]), bf(jnp.concatenate([bd(d["u_loc"]), bd(d["a_tt"])], axis=1)))
        d["y_loc"] = ar[:, :PAIR_W] + d["rkv"]
        d["r_g"] = d["r_t"] + ar[:, PAIR_W:]
    for d in chains:
        lhs = jnp.concatenate([jnp.concatenate([d["v"], jnp.zeros_like(d["v"])], axis=1),
                               jnp.concatenate([d["u_loc"], d["a_tt"]], axis=1)], axis=0)
        rhs = jnp.concatenate([d["k_h"], d["b_h"]], axis=0)
        dp = lax.dot_general(bf(lhs), bf(rhs), TN_DIMS, preferred_element_type=F32)
        d["d_loc"] = jnp.where(bd_mask, dp[:PAIR_W], 0.0)
        d["phi"] = bf(jnp.where(bd_mask, dp[PAIR_W:], 0.0))

    state = [s_scr[p] for p in range(N_PAIRS)]
    for d in chains:
        p = d["p"]
        s_hi, s_lo = _split2(state[p])
        y = lax.dot_general(bf(d["r_g"]), s_hi, NT_DIMS, preferred_element_type=F32) + d["y_loc"]
        y_ref[0, d["ci"] * c:(d["ci"] + 1) * c, p * PAIR_W:(p + 1) * PAIR_W] = y
        state[p] = state[p] * d["p_c"] + (_dot(s_hi, d["phi"]) + _dot(s_lo, d["phi"])) + d["d_loc"]
    for p in range(N_PAIRS):
        s_scr[p] = state[p]

    @pl.when(t == pl.num_programs(1) - 1)
    def _():
        sf_ref[0] = s_scr[...]


def _wkv_scan(r, lw, kx, v, na, kb, s0_pair, tt):
    b, t, _ = r.shape
    assert CHUNK == HEAD_DIM and tt % CHUNK == 0
    seq = pl.BlockSpec((1, tt, RWKV_W), lambda i, j: (i, j, 0))
    st = pl.BlockSpec((1, N_PAIRS, PAIR_W, PAIR_W), lambda i, j: (i, 0, 0, 0))
    est = 2 * 7 * tt * RWKV_W * 4 + 5 * N_PAIRS * PAIR_W * PAIR_W * 4 + (8 << 20)
    return pl.pallas_call(
        _wkv_kernel,
        grid=(b, t // tt),
        in_specs=[seq] * 6 + [st],
        out_specs=[seq, st],
        out_shape=[jax.ShapeDtypeStruct((b, t, RWKV_W), F32),
                   jax.ShapeDtypeStruct((b, N_PAIRS, PAIR_W, PAIR_W), F32)],
        scratch_shapes=[pltpu.VMEM((N_PAIRS, PAIR_W, PAIR_W), F32)],
        compiler_params=_params(2, est),
        name="wkv_scan",
    )(r, lw, kx, v, na, kb, s0_pair)


def _state_to_pairs(s):
    b = s.shape[0]
    s = s.astype(F32).reshape(b, N_PAIRS, 2, HEAD_DIM, HEAD_DIM)
    out = jnp.zeros((b, N_PAIRS, PAIR_W, PAIR_W), F32)
    out = out.at[:, :, :HEAD_DIM, :HEAD_DIM].set(s[:, :, 0])
    return out.at[:, :, HEAD_DIM:, HEAD_DIM:].set(s[:, :, 1])


def _pairs_to_state(sp):
    b = sp.shape[0]
    s = jnp.stack([sp[:, :, :HEAD_DIM, :HEAD_DIM], sp[:, :, HEAD_DIM:, HEAD_DIM:]], axis=2)
    return s.reshape(b, N_HEADS, HEAD_DIM, HEAD_DIM)


def _mix_out_kernel(y_ref, bonus_ref, g_ref, po_ref, x_ref, lg_ref, lb_ref, ones_ref, wo_ref, ln2_ref,
                    rw_hi_ref, rw_lo_ref, rb_ref, cin_ref,
                    h_o, xn_o, route_o, gate_o, cnt_o, carry):
    i = pl.program_id(0)
    tm = y_ref.shape[0]

    @pl.when(i == 0)
    def _():
        carry[...] = cin_ref[...]

    ones_bd = ones_ref[...]
    y = y_ref[...]
    inv_n = 1.0 / HEAD_DIM
    mu = _group_sum(y, ones_bd) * inv_n
    dlt = y - mu
    var = _group_sum(dlt * dlt, ones_bd) * inv_n
    yn = dlt * lax.rsqrt(var + LNX_EPS) * lg_ref[...] + lb_ref[...]
    rw_out = (yn + bonus_ref[...]) * g_ref[...]
    mix = _dot(po_ref[...], wo_ref[:POOL_W, :]) + _dot(rw_out.astype(BF16), wo_ref[POOL_W:, :])
    h = x_ref[...] + mix
    h_o[...] = h
    xn = _rmsnorm_rows(h, ln2_ref[...])
    xn_o[...] = xn

    x_hi, x_lo = _split2(xn)
    logits = (_dot(x_hi, rw_hi_ref[...]) + _dot(x_hi, rw_lo_ref[...]) + _dot(x_lo, rw_hi_ref[...])
              + rb_ref[...])
    lane = lax.broadcasted_iota(I32, (tm, ROUTE_LANES), 1)
    lane_f = lane.astype(F32)
    vals, idxs, hots = [], [], []
    work = logits
    for _ in range(TOP_K):
        m = jnp.max(work, axis=-1, keepdims=True)
        idx = jnp.min(jnp.where(work == m, lane_f, float(ROUTE_LANES)), axis=-1, keepdims=True)
        hit = lane_f == idx
        vals.append(m)
        idxs.append(idx)
        hots.append(jnp.where(hit, 1.0, 0.0).astype(F32))
        work = jnp.where(hit, -jnp.inf, work)
    exps = [jnp.exp(vv - vals[0]) for vv in vals]
    den = exps[0] + exps[1] + exps[2] + exps[3]

    hot_all = hots[0] + hots[1] + hots[2] + hots[3]
    rr = lax.broadcasted_iota(I32, (tm, tm), 0)
    cc = lax.broadcasted_iota(I32, (tm, tm), 1)
    tri_strict = jnp.where(cc < rr, 1.0, 0.0).astype(BF16)
    prefix = _dot(tri_strict, hot_all.astype(BF16)) + carry[0:1, :]
    route = jnp.zeros((tm, ROUTE_LANES), F32)
    gates = jnp.zeros((tm, ROUTE_LANES), F32)
    for j in range(TOP_K):
        rank = jnp.sum(hots[j] * prefix, axis=-1, keepdims=True)
        route = jnp.where(lane == j, idxs[j], route)
        route = jnp.where(lane == TOP_K + j, rank, route)
        gates = jnp.where(lane == j, exps[j] / den, gates)
    route_o[...] = route.astype(I32)
    gate_o[...] = gates
    new_carry = carry[...] + jnp.sum(hot_all, axis=0, keepdims=True)
    carry[...] = new_carry
    cnt_o[...] = new_carry


def _mix_out(y2d, bonus2d, g2d, po2d, x2d, lnx_g, lnx_b, ones_bd, w_out_bf, ln2_g,
             rw_hi, rw_lo, rb_pad, carry_in, tm):
    n = x2d.shape[0]
    row = lambda x: x.reshape(1, -1)
    cst = lambda shape: pl.BlockSpec(shape, lambda i: (0,) * len(shape))
    half = pl.BlockSpec((tm, RWKV_W), lambda i: (i, 0))
    full = pl.BlockSpec((tm, D_MODEL), lambda i: (i, 0))
    lanes = pl.BlockSpec((tm, ROUTE_LANES), lambda i: (i, 0))
    est = (2 * (4 * tm * RWKV_W * 4 + 3 * tm * D_MODEL * 4) + 2 * (D_MODEL * D_MODEL * 2 + RWKV_W * RWKV_W * 2)
           + 8 * tm * D_MODEL * 4 + 4 * tm * tm)
    return pl.pallas_call(
        _mix_out_kernel,
        grid=(n // tm,),
        in_specs=[half, half, half, half, full,
                  cst((1, RWKV_W)), cst((1, RWKV_W)), cst((RWKV_W, RWKV_W)), cst((D_MODEL, D_MODEL)),
                  cst((1, D_MODEL)), cst((D_MODEL, ROUTE_LANES)), cst((D_MODEL, ROUTE_LANES)),
                  cst((1, ROUTE_LANES)), cst((SUBLANES, ROUTE_LANES))],
        out_specs=[full, full, lanes, lanes, cst((SUBLANES, ROUTE_LANES))],
        out_shape=[jax.ShapeDtypeStruct((n, D_MODEL), F32), jax.ShapeDtypeStruct((n, D_MODEL), F32),
                   jax.ShapeDtypeStruct((n, ROUTE_LANES), I32), jax.ShapeDtypeStruct((n, ROUTE_LANES), F32),
                   jax.ShapeDtypeStruct((SUBLANES, ROUTE_LANES), F32)],
        scratch_shapes=[pltpu.VMEM((SUBLANES, ROUTE_LANES), F32)],
        compiler_params=_params(1, est),
        name="mix_out_router",
    )(y2d, bonus2d, g2d, po2d, x2d, row(lnx_g), row(lnx_b), ones_bd, w_out_bf, row(ln2_g),
      rw_hi, rw_lo, rb_pad, carry_in)


def _row_copy(src, src_row, dst, dst_row, sem):
    return pltpu.make_async_copy(src.at[pl.ds(src_row, 1), :], dst.at[pl.ds(dst_row, 1), :], sem)


def _dispatch_kernel(dest_ref, x_ref, xs_in, xs_out, sem, *, tm):
    del xs_in

    def issue(n, carry):
        for j in range(TOP_K):
            _row_copy(x_ref, n, xs_out, dest_ref[n * TOP_K + j], sem).start()
        return carry

    lax.fori_loop(0, tm, issue, 0)
    pltpu.make_async_copy(xs_out.at[pl.ds(0, tm * TOP_K), :], xs_out.at[pl.ds(0, tm * TOP_K), :], sem).wait()


def _dispatch(dest_flat, xn2d, xs, tm):
    n = xn2d.shape[0]
    return pl.pallas_call(
        functools.partial(_dispatch_kernel, tm=tm),
        grid=(n // tm,),
        in_specs=[pl.BlockSpec((tm * TOP_K,), lambda i: (i,), memory_space=pltpu.SMEM),
                  pl.BlockSpec((tm, D_MODEL), lambda i: (i, 0)),
                  pl.BlockSpec(memory_space=pl.ANY)],
        out_specs=pl.BlockSpec(memory_space=pl.ANY),
        out_shape=jax.ShapeDtypeStruct(xs.shape, xs.dtype),
        scratch_shapes=[pltpu.SemaphoreType.DMA(())],
        input_output_aliases={2: 0},
        compiler_params=_params(1, 2 * tm * D_MODEL * 4),
        name="dispatch",
    )(dest_flat, xn2d, xs)


def _expert_kernel(be_ref, nu_ref, xs_ref, wgu_ref, bgu_ref, wdn_ref, bdn_ref, o_ref, wgu_bf, wdn_bf):
    i = pl.program_id(0)
    used = i < nu_ref[0]
    changed = jnp.logical_or(i == 0, be_ref[i] != be_ref[jnp.maximum(i - 1, 0)])

    @pl.when(jnp.logical_and(used, changed))
    def _():
        wgu_bf[...] = wgu_ref[0].astype(BF16)
        wdn_bf[...] = wdn_ref[0].astype(BF16)

    @pl.when(used)
    def _():
        gu = _dot(xs_ref[...].astype(BF16), wgu_bf[...]) + bgu_ref[0]
        gate = jnp.minimum(gu[:, :D_EXPERT], SWIGLU_LIMIT)
        up = jnp.clip(gu[:, D_EXPERT:], -SWIGLU_LIMIT, SWIGLU_LIMIT)
        hmid = (up + 1.0) * (gate * jax.nn.sigmoid(SWIGLU_ALPHA * gate))
        o_ref[...] = _dot(hmid.astype(BF16), wdn_bf[...]) + bdn_ref[0]

    @pl.when(jnp.logical_not(used))
    def _():
        o_ref[...] = jnp.zeros_like(o_ref)


def _experts(block_e, n_used, xs, w_gu, b_gu, w_down, b_down):
    n_slots = xs.shape[0]
    nb = n_slots // EXPERT_BLOCK
    blk = EXPERT_BLOCK
    est = (2 * 2 * blk * D_MODEL * 4 + 2 * (D_MODEL * 2 * D_EXPERT + D_EXPERT * D_MODEL) * 4
           + (D_MODEL * 2 * D_EXPERT + D_EXPERT * D_MODEL) * 2 + 4 * blk * 2 * D_EXPERT * 4)
    grid_spec = pltpu.PrefetchScalarGridSpec(
        num_scalar_prefetch=2,
        grid=(nb,),
        in_specs=[pl.BlockSpec((blk, D_MODEL), lambda i, be, nu: (i, 0)),
                  pl.BlockSpec((1, D_MODEL, 2 * D_EXPERT), lambda i, be, nu: (be[i], 0, 0)),
                  pl.BlockSpec((1, 1, 2 * D_EXPERT), lambda i, be, nu: (be[i], 0, 0)),
                  pl.BlockSpec((1, D_EXPERT, D_MODEL), lambda i, be, nu: (be[i], 0, 0)),
                  pl.BlockSpec((1, 1, D_MODEL), lambda i, be, nu: (be[i], 0, 0))],
        out_specs=pl.BlockSpec((blk, D_MODEL), lambda i, be, nu: (i, 0)),
        scratch_shapes=[pltpu.VMEM((D_MODEL, 2 * D_EXPERT), BF16), pltpu.VMEM((D_EXPERT, D_MODEL), BF16)],
    )
    return pl.pallas_call(
        _expert_kernel,
        grid_spec=grid_spec,
        out_shape=jax.ShapeDtypeStruct((n_slots, D_MODEL), F32),
        compiler_params=_params(1, est),
        name="experts",
    )(block_e, n_used, xs, w_gu, b_gu.reshape(N_EXPERTS, 1, 2 * D_EXPERT), w_down,
      b_down.reshape(N_EXPERTS, 1, D_MODEL))


def _combine_kernel(dest_ref, ys_hbm, gate_ref, h_ref, p_ref, ln3_ref, pg_ref, pp_ref, fg_ref,
                    o_ref, gbuf, sem, *, tm):
    def issue(n, carry):
        for j in range(TOP_K):
            _row_copy(ys_hbm, dest_ref[n * TOP_K + j], gbuf.at[j], n, sem).start()
        return carry

    lax.fori_loop(0, tm, issue, 0)
    for j in range(TOP_K):
        pltpu.make_async_copy(gbuf.at[j], gbuf.at[j], sem).wait()

    gates = gate_ref[...]
    moe = gates[:, 0:1] * gbuf[0]
    for j in range(1, TOP_K):
        moe = moe + gates[:, j:j + 1] * gbuf[j]
    h = h_ref[...] + moe
    gate = jax.nn.sigmoid(_dot(_rmsnorm_rows(h, ln3_ref[...]).astype(BF16), pg_ref[...]))
    h = h + gate * _dot(p_ref[...].astype(BF16), pp_ref[...])
    o_ref[...] = _rmsnorm_rows(h, fg_ref[...])


def _combine(dest_flat, ys, gates, h2d, p2d, ln3_g, ple_gate_bf, ple_proj_bf, final_g, tm):
    n = h2d.shape[0]
    row = lambda x: x.reshape(1, -1)
    cst = lambda shape: pl.BlockSpec(shape, lambda i: (0,) * len(shape))
    est = (TOP_K * tm * D_MODEL * 4 + 2 * (2 * tm * D_MODEL * 4 + tm * PLE_DIM * 4 + tm * ROUTE_LANES * 4)
           + 2 * (D_MODEL * D_MODEL * 2 + PLE_DIM * D_MODEL * 2) + 6 * tm * D_MODEL * 4)
    return pl.pallas_call(
        functools.partial(_combine_kernel, tm=tm),
        grid=(n // tm,),
        in_specs=[pl.BlockSpec((tm * TOP_K,), lambda i: (i,), memory_space=pltpu.SMEM),
                  pl.BlockSpec(memory_space=pl.ANY),
                  pl.BlockSpec((tm, ROUTE_LANES), lambda i: (i, 0)),
                  pl.BlockSpec((tm, D_MODEL), lambda i: (i, 0)),
                  pl.BlockSpec((tm, PLE_DIM), lambda i: (i, 0)),
                  cst((1, D_MODEL)), cst((D_MODEL, D_MODEL)), cst((PLE_DIM, D_MODEL)), cst((1, D_MODEL))],
        out_specs=pl.BlockSpec((tm, D_MODEL), lambda i: (i, 0)),
        out_shape=jax.ShapeDtypeStruct((n, D_MODEL), F32),
        scratch_shapes=[pltpu.VMEM((TOP_K, tm, D_MODEL), F32), pltpu.SemaphoreType.DMA(())],
        compiler_params=_params(1, est),
        name="combine_ple_norm",
    )(dest_flat, ys, gates, h2d, p2d, row(ln3_g), ple_gate_bf, ple_proj_bf, row(final_g))


def _tile(n, pref):
    t = min(n, pref)
    assert n % t == 0, (n, pref)
    return t


def _front(x, pos0, pool_ctx, shift_prev, wkv0, wts, carry_in):
    b, t, _ = x.shape
    n = b * t
    assert t >= POOL_CTX and t % SUBLANES == 0
    x2d = x.reshape(n, D_MODEL)
    u2d, zr2d = _in_proj(x2d, wts["ln1_g"], wts["w_in_bf"], _tile(n, 512))
    u = u2d.reshape(b, t, POOL_W)
    zr = zr2d.reshape(b, t, RWKV_PROJ)

    ctx16 = jnp.concatenate([jnp.zeros((b, POOL_HALO - POOL_CTX, POOL_W), F32), pool_ctx.astype(F32)], axis=1)
    pool_out = _pool_mixer(u, ctx16, wts["w_pool_bf"], wts["pool_scale"], pos0, _tile(t, 512))
    new_pool = u[:, -POOL_CTX:]

    sh8 = jnp.broadcast_to(shift_prev.astype(F32), (b, SUBLANES, RWKV_PROJ))
    r, lw, kx, v, na, kb, g, bonus = _rwkv_pre(
        zr, sh8, wts["mu_shift"], wts["w0"], wts["w2_bf"], wts["a0"], wts["a2_bf"], wts["g2_bf"],
        wts["k_k"], wts["k_a"], wts["r_k"], wts["ones_bd"], _tile(t, 256))
    new_shift = zr[:, -1:]

    t_pad = -(-t // CHUNK) * CHUNK
    seqs = (r, lw, kx, v, na, kb)
    if t_pad != t:
        seqs = tuple(jnp.pad(s, ((0, 0), (0, t_pad - t), (0, 0))) for s in seqs)
    y, s_fin = _wkv_scan(*seqs, _state_to_pairs(wkv0), _tile(t_pad, 2 * CHUNK))
    y = y[:, :t]
    new_wkv = _pairs_to_state(s_fin)

    tm = _tile(n, 256)
    h, xn, route, gates, counts = _mix_out(
        y.reshape(n, RWKV_W), bonus.reshape(n, RWKV_W), g.reshape(n, RWKV_W), pool_out.reshape(n, POOL_W),
        x2d, wts["lnx_g"], wts["lnx_b"], wts["ones_bd"], wts["w_out_bf"], wts["ln2_g"],
        wts["rw_hi"], wts["rw_lo"], wts["rb_pad"], carry_in, tm)
    return dict(h=h, xn=xn, route=route, gates=gates, counts=counts,
                new_pool=new_pool, new_shift=new_shift, new_wkv=new_wkv)


def _prep_weights(i, ln1_g, w_in, mu_shift, w_pool, pool_scale, w0, w2, a0, a2, g2, k_k, k_a, r_k,
                  lnx_g, lnx_b, w_out, ln2_g, router_w, router_b, ln3_g, ple_gate, ple_proj):
    head = jnp.arange(RWKV_W) // HEAD_DIM
    ones_bd = (head[:, None] == head[None, :]).astype(BF16)
    rw_pad = jnp.pad(router_w[i].astype(F32), ((0, 0), (0, ROUTE_LANES - N_EXPERTS)))
    rw_hi = rw_pad.astype(BF16)
    rw_lo = (rw_pad - rw_hi.astype(F32)).astype(BF16)
    rb_pad = jnp.concatenate([router_b[i].astype(F32), jnp.full((ROUTE_LANES - N_EXPERTS,), -jnp.inf, F32)])
    return dict(
        ln1_g=ln1_g[i], w_in_bf=w_in[i].astype(BF16), mu_shift=mu_shift[i], w_pool_bf=w_pool[i].astype(BF16),
        pool_scale=pool_scale[i], w0=w0[i], w2_bf=w2[i].astype(BF16), a0=a0[i], a2_bf=a2[i].astype(BF16),
        g2_bf=g2[i].astype(BF16), k_k=k_k[i], k_a=k_a[i], r_k=r_k[i].reshape(RWKV_W), lnx_g=lnx_g[i],
        lnx_b=lnx_b[i], w_out_bf=w_out[i].astype(BF16), ln2_g=ln2_g[i], rw_hi=rw_hi, rw_lo=rw_lo,
        rb_pad=rb_pad.reshape(1, ROUTE_LANES), ones_bd=ones_bd, ln3_g=ln3_g[i],
        ple_gate_bf=ple_gate[i].astype(BF16), ple_proj_bf=ple_proj[i].astype(BF16))


def _layer_pair(hp, hs, pp, ps, pool_ctx, shift_prev, wkv0, wts, w_gu, b_gu, w_down, b_down, final_g):
    bp, tp, _ = hp.shape
    bs, ts, _ = hs.shape
    np_, ns = bp * tp, bs * ts
    zero_carry = jnp.zeros((SUBLANES, ROUTE_LANES), F32)
    fp = _front(hp, 0, jnp.zeros((bp, POOL_CTX, POOL_W), F32), jnp.zeros((bp, 1, RWKV_PROJ), F32),
                jnp.zeros((bp, N_HEADS, HEAD_DIM, HEAD_DIM), F32), wts, zero_carry)
    fs = _front(hs, PAST_LEN, pool_ctx, shift_prev, wkv0, wts, fp["counts"])

    counts = fs["counts"][0, :N_EXPERTS].astype(I32)
    padded = (counts + EXPERT_BLOCK - 1) // EXPERT_BLOCK * EXPERT_BLOCK
    pad_end = jnp.cumsum(padded)
    start_pad = pad_end - padded
    n_blocks = -(-(np_ + ns) * TOP_K // EXPERT_BLOCK) + N_EXPERTS
    n_slots = n_blocks * EXPERT_BLOCK
    n_used = (pad_end[-1] // EXPERT_BLOCK).astype(I32)
    block_start = jnp.arange(n_blocks, dtype=I32) * EXPERT_BLOCK
    block_e = jnp.minimum(jnp.sum((block_start[:, None] >= pad_end[None, :]).astype(I32), axis=1), N_EXPERTS - 1)
    last_e = block_e[jnp.maximum(n_used - 1, 0)]
    block_e = jnp.where(jnp.arange(n_blocks) < n_used, block_e, last_e)

    def dest_of(route):
        idx = route[:, :TOP_K]
        rank = route[:, TOP_K:2 * TOP_K]
        return (start_pad[idx] + rank).reshape(-1).astype(I32)

    dest_p, dest_s = dest_of(fp["route"]), dest_of(fs["route"])
    xs = jnp.zeros((n_slots, D_MODEL), F32)
    xs = _dispatch(dest_p, fp["xn"], xs, _tile(np_, 512))
    xs = _dispatch(dest_s, fs["xn"], xs, _tile(ns, 512))
    ys = _experts(block_e, n_used.reshape(1), xs, w_gu, b_gu, w_down, b_down)

    outs = []
    for f, dest, p in ((fp, dest_p, pp), (fs, dest_s, ps)):
        n = f["h"].shape[0]
        outs.append(_combine(dest, ys, f["gates"], f["h"], p.reshape(n, PLE_DIM).astype(F32), wts["ln3_g"],
                             wts["ple_gate_bf"], wts["ple_proj_bf"], final_g, _tile(n, 256)))
    return outs[0].reshape(hp.shape), outs[1].reshape(hs.shape), fp, fs


def kernel(x_prompt, x_sample, p_prompt, p_sample, cache_pool, state_shift, state_wkv, ln1_g, w_in, mu_shift, w_pool, pool_scale, w0, w2, a0, a2, g2, k_k, k_a, r_k, lnx_g, lnx_b, w_out, ln2_g, router_w, router_b, w_gu, b_gu, w_down, b_down, ln3_g, ple_gate, ple_proj, final_g):
    assert ln1_g.shape[0] == 1, "single-layer kernel"
    i = 0
    wts = _prep_weights(i, ln1_g, w_in, mu_shift, w_pool, pool_scale, w0, w2, a0, a2, g2, k_k, k_a, r_k,
                        lnx_g, lnx_b, w_out, ln2_g, router_w, router_b, ln3_g, ple_gate, ple_proj)
    y_p, y_s, fp, fs = _layer_pair(x_prompt, x_sample, p_prompt[i], p_sample[i], cache_pool[i], state_shift[i],
                                   state_wkv[i], wts, w_gu[i], b_gu[i], w_down[i], b_down[i], final_g)
    stack = lambda a: a[None]
    return (y_p, y_s,
            stack(fp["new_pool"]), stack(fp["new_shift"]), stack(fp["new_wkv"]),
            stack(fs["new_pool"]), stack(fs["new_shift"]), stack(fs["new_wkv"]))
```

```python
import functools

import jax
import jax.numpy as jnp
from jax import lax
from jax.experimental import pallas as pl
from jax.experimental.pallas import tpu as pltpu

F32 = jnp.float32
BF16 = jnp.bfloat16
I32 = jnp.int32

D_MODEL = 1024
POOL_W = 512
POOL_WINDOWS = (2, 4, 8, 16)
POOL_GW = 128
POOL_CTX = 15
POOL_HALO = 16
RWKV_W = 512
HEAD_DIM = 64
N_HEADS = 8
N_PAIRS = N_HEADS // 2
PAIR_W = 2 * HEAD_DIM
DECAY_LORA = 64
ICLR_LORA = 64
GATE_LORA = 128
RWKV_PROJ = 3 * RWKV_W + DECAY_LORA + ICLR_LORA + GATE_LORA
IN_PROJ = POOL_W + RWKV_PROJ
LNX_EPS = 64e-5
N_EXPERTS = 32
TOP_K = 4
D_EXPERT = 1024
SWIGLU_LIMIT = 7.0
SWIGLU_ALPHA = 1.702
PLE_DIM = 256
RMS_EPS = 1e-6
PAST_LEN = 2048

CHUNK = 64
EXPERT_BLOCK = 256
ROUTE_LANES = 128
SUBLANES = 8
EXP_M05 = 0.6065306597126334
V7X_VMEM_BYTES = 64 * 1024 * 1024

NT_DIMS = (((1,), (1,)), ((), ()))
TN_DIMS = (((0,), (0,)), ((), ()))


def _vmem_limit(est_bytes):
    return int(min(est_bytes * 3 // 2 + (4 << 20), V7X_VMEM_BYTES - (8 << 20)))


def _params(n_axes, est_bytes):
    return pltpu.CompilerParams(dimension_semantics=("arbitrary",) * n_axes,
                                vmem_limit_bytes=_vmem_limit(est_bytes))


def _dot(a, b):
    return jnp.dot(a, b, preferred_element_type=F32)


def _split2(x):
    hi = x.astype(BF16)
    lo = (x - hi.astype(F32)).astype(BF16)
    return hi, lo


def _split3(x):
    hi = x.astype(BF16)
    r1 = x - hi.astype(F32)
    mid = r1.astype(BF16)
    lo = (r1 - mid.astype(F32)).astype(BF16)
    return hi, mid, lo


def _group_sum(x, ones_bd):
    hi, lo = _split2(x)
    return _dot(hi, ones_bd) + _dot(lo, ones_bd)


def _rmsnorm_rows(x, g):
    ms = jnp.mean(x * x, axis=-1, keepdims=True)
    return (x * lax.rsqrt(ms + RMS_EPS)) * g


def _in_proj_kernel(x_ref, g_ref, w_ref, u_ref, zr_ref):
    xn = _rmsnorm_rows(x_ref[...], g_ref[...])
    z = _dot(xn.astype(BF16), w_ref[...])
    u_ref[...] = z[:, :POOL_W]
    zr_ref[...] = z[:, POOL_W:]


def _in_proj(x2d, ln1_g, w_in_bf, tm):
    n = x2d.shape[0]
    est = 2 * tm * D_MODEL * 4 + 2 * D_MODEL * IN_PROJ * 2 + 3 * tm * IN_PROJ * 4
    return pl.pallas_call(
        _in_proj_kernel,
        grid=(n // tm,),
        in_specs=[pl.BlockSpec((tm, D_MODEL), lambda i: (i, 0)),
                  pl.BlockSpec((1, D_MODEL), lambda i: (0, 0)),
                  pl.BlockSpec((D_MODEL, IN_PROJ), lambda i: (0, 0))],
        out_specs=[pl.BlockSpec((tm, POOL_W), lambda i: (i, 0)),
                   pl.BlockSpec((tm, RWKV_PROJ), lambda i: (i, 0))],
        out_shape=[jax.ShapeDtypeStruct((n, POOL_W), F32),
                   jax.ShapeDtypeStruct((n, RWKV_PROJ), F32)],
        compiler_params=_params(1, est),
        name="in_proj",
    )(x2d, ln1_g.reshape(1, D_MODEL), w_in_bf)


def _pool_kernel(u_ref, halo_ref, ctx_ref, wp_ref, ps_ref, o_ref, buf, *, pos0):
    t = pl.program_id(1)
    tt = u_ref.shape[1]
    buf[0:POOL_HALO, :] = jnp.where(t == 0, ctx_ref[0], halo_ref[0])
    buf[POOL_HALO:, :] = u_ref[0]
    pos = lax.broadcasted_iota(I32, (tt, 1), 0) + (t * tt + pos0)
    outs = []
    for gi, w in enumerate(POOL_WINDOWS):
        sl = slice(gi * POOL_GW, (gi + 1) * POOL_GW)
        cur = buf[POOL_HALO:POOL_HALO + tt, sl]
        acc = cur
        for j in range(1, w):
            acc = acc + buf[POOL_HALO - j:POOL_HALO - j + tt, sl]
        cnt = jnp.minimum(w, pos + 1).astype(F32)
        pooled = acc / cnt - cur
        outs.append(_dot(pooled.astype(BF16), wp_ref[gi]))
    o_ref[0] = (jnp.concatenate(outs, axis=-1) * ps_ref[...]).astype(o_ref.dtype)


def _pool_mixer(u, ctx16, w_pool_bf, pool_scale, pos0, tt):
    b, t, _ = u.shape
    hb = tt // POOL_HALO
    est = 2 * tt * POOL_W * 4 * 3 + (tt + POOL_HALO) * POOL_W * 4
    return pl.pallas_call(
        functools.partial(_pool_kernel, pos0=pos0),
        grid=(b, t // tt),
        in_specs=[pl.BlockSpec((1, tt, POOL_W), lambda i, j: (i, j, 0)),
                  pl.BlockSpec((1, POOL_HALO, POOL_W), lambda i, j: (i, jnp.maximum(j * hb - 1, 0), 0)),
                  pl.BlockSpec((1, POOL_HALO, POOL_W), lambda i, j: (i, 0, 0)),
                  pl.BlockSpec((len(POOL_WINDOWS), POOL_GW, POOL_GW), lambda i, j: (0, 0, 0)),
                  pl.BlockSpec((1, POOL_W), lambda i, j: (0, 0))],
        out_specs=pl.BlockSpec((1, tt, POOL_W), lambda i, j: (i, j, 0)),
        out_shape=jax.ShapeDtypeStruct((b, t, POOL_W), BF16),
        scratch_shapes=[pltpu.VMEM((tt + POOL_HALO, POOL_W), F32)],
        compiler_params=_params(2, est),
        name="pool_mixer",
    )(u, u, ctx16, w_pool_bf, pool_scale.reshape(1, POOL_W))


def _rwkv_pre_kernel(zr_ref, halo_ref, sh_ref, mu_ref, w0_ref, w2_ref, a0_ref, a2_ref, g2_ref,
                     kk_ref, ka_ref, rk_ref, ones_ref,
                     r_o, lw_o, kx_o, v_o, na_o, kb_o, g_o, bonus_o, buf):
    t = pl.program_id(1)
    tt = zr_ref.shape[1]
    buf[0:SUBLANES, :] = jnp.where(t == 0, sh_ref[0], halo_ref[0])
    zr = zr_ref[0]
    buf[SUBLANES:, :] = zr
    prev = buf[SUBLANES - 1:SUBLANES - 1 + tt, :]
    zs = zr + (prev - zr) * mu_ref[...]
    o1, o2, o3 = RWKV_W, 2 * RWKV_W, 3 * RWKV_W
    o4, o5 = o3 + DECAY_LORA, o3 + DECAY_LORA + ICLR_LORA
    r, k, v = zs[:, :o1], zs[:, o1:o2], zs[:, o2:o3]
    wd, ad, gd = zs[:, o3:o4], zs[:, o4:o5], zs[:, o5:]
    ones_bd = ones_ref[...]
    wz = w0_ref[...] + _dot(jnp.tanh(wd).astype(BF16), w2_ref[...])
    lw = -EXP_M05 * jax.nn.sigmoid(wz)
    a = jax.nn.sigmoid(a0_ref[...] + _dot(ad.astype(BF16), a2_ref[...]))
    g = _dot(jax.nn.sigmoid(gd).astype(BF16), g2_ref[...])
    kk = k * kk_ref[...]
    kk = kk / jnp.maximum(jnp.sqrt(_group_sum(kk * kk, ones_bd)), 1e-12)
    kx = k * (1.0 + (a - 1.0) * ka_ref[...])
    bonus = _group_sum(r * kx * rk_ref[...], ones_bd) * v
    r_o[0] = r
    lw_o[0] = lw
    kx_o[0] = kx
    v_o[0] = v
    na_o[0] = -kk
    kb_o[0] = kk * a
    g_o[0] = g
    bonus_o[0] = bonus


def _rwkv_pre(zr, sh8, mu, w0, w2_bf, a0, a2_bf, g2_bf, k_k, k_a, r_k, ones_bd, tt):
    b, t, _ = zr.shape
    hb = tt // SUBLANES
    row = lambda x: x.reshape(1, -1)
    cst = lambda shape: pl.BlockSpec(shape, lambda i, j: (0,) * len(shape))
    seq = pl.BlockSpec((1, tt, RWKV_W), lambda i, j: (i, j, 0))
    est = 3 * tt * RWKV_PROJ * 4 + 2 * 8 * tt * RWKV_W * 4 + 12 * tt * RWKV_W * 4
    return pl.pallas_call(
        _rwkv_pre_kernel,
        grid=(b, t // tt),
        in_specs=[pl.BlockSpec((1, tt, RWKV_PROJ), lambda i, j: (i, j, 0)),
                  pl.BlockSpec((1, SUBLANES, RWKV_PROJ), lambda i, j: (i, jnp.maximum(j * hb - 1, 0), 0)),
                  pl.BlockSpec((1, SUBLANES, RWKV_PROJ), lambda i, j: (i, 0, 0)),
                  cst((1, RWKV_PROJ)), cst((1, RWKV_W)), cst((DECAY_LORA, RWKV_W)),
                  cst((1, RWKV_W)), cst((ICLR_LORA, RWKV_W)), cst((GATE_LORA, RWKV_W)),
                  cst((1, RWKV_W)), cst((1, RWKV_W)), cst((1, RWKV_W)), cst((RWKV_W, RWKV_W))],
        out_specs=[seq] * 8,
        out_shape=[jax.ShapeDtypeStruct((b, t, RWKV_W), F32)] * 8,
        scratch_shapes=[pltpu.VMEM((tt + SUBLANES, RWKV_PROJ), F32)],
        compiler_params=_params(2, est),
        name="rwkv_pre",
    )(zr, zr, sh8, row(mu), row(w0), w2_bf, row(a0), a2_bf, g2_bf, row(k_k), row(k_a), row(r_k), ones_bd)


def _pair_blockdiag(z, left):
    return jnp.concatenate([jnp.where(left, z, 0.0), jnp.where(left, 0.0, z)], axis=0)


def _wkv_kernel(r_ref, lw_ref, kx_ref, v_ref, na_ref, kb_ref, s0_ref, y_ref, sf_ref, s_scr):
    t = pl.program_id(1)
    tt = r_ref.shape[1]
    c = CHUNK

    @pl.when(t == 0)
    def _():
        s_scr[...] = s0_ref[0]

    lane = lax.broadcasted_iota(I32, (c, 2 * c), 1)
    rowi = lax.broadcasted_iota(I32, (c, 2 * c), 0)
    jm = lane & (c - 1)
    strict = jm < rowi
    incl = jm <= rowi
    eye_pair = jnp.where(jm == rowi, 1.0, 0.0).astype(F32)
    left = lax.broadcasted_iota(I32, (c, PAIR_W), 1) < HEAD_DIM
    rr = lax.broadcasted_iota(I32, (c, c), 0)
    cc = lax.broadcasted_iota(I32, (c, c), 1)
    tri_incl_bf = jnp.where(cc <= rr, 1.0, 0.0).astype(BF16)
    br = lax.broadcasted_iota(I32, (PAIR_W, PAIR_W), 0) < HEAD_DIM
    bc = lax.broadcasted_iota(I32, (PAIR_W, PAIR_W), 1) < HEAD_DIM
    bd_mask = br == bc
    bf = lambda x: x.astype(BF16)
    bd = lambda z: _pair_blockdiag(z, left)
    n_chunks = tt // c

    cums = []
    for ci in range(n_chunks):
        h3 = _split3(lw_ref[0, ci * c:(ci + 1) * c, :])
        cums.append(_dot(tri_incl_bf, h3[0]) + _dot(tri_incl_bf, h3[1]) + _dot(tri_incl_bf, h3[2]))

    chains = []
    for ci in range(n_chunks):
        rows = slice(ci * c, (ci + 1) * c)
        for p in range(N_PAIRS):
            cols = slice(p * PAIR_W, (p + 1) * PAIR_W)
            cum = cums[ci][:, cols]
            lw, kx, kb = lw_ref[0, rows, cols], kx_ref[0, rows, cols], kb_ref[0, rows, cols]
            tot = cum[c - 1:c, :]
            p_end = jnp.exp(tot - cum)
            inv_p = jnp.exp(-cum)
            chains.append(dict(
                ci=ci, p=p, v=v_ref[0, rows, cols], p_c=jnp.exp(tot),
                a_t=na_ref[0, rows, cols] * jnp.exp(cum - lw), r_t=r_ref[0, rows, cols] * jnp.exp(cum),
                b_t=kb * inv_p, k_t=kx * inv_p, b_h=kb * p_end, k_h=kx * p_end))

    for d in chains:
        lhs = bf(jnp.concatenate([d["a_t"], d["r_t"]], axis=0))
        rhs = bf(jnp.concatenate([bd(d["b_t"]), bd(d["k_t"])], axis=0))
        sc = lax.dot_general(lhs, rhs, NT_DIMS, preferred_element_type=F32)
        d["a_ab"] = jnp.where(strict, sc[:c, :2 * c], 0.0)
        d["a_k"] = jnp.concatenate([jnp.where(strict, sc[:c, 2 * c:], 0.0),
                                    jnp.where(incl, sc[c:, 2 * c:], 0.0)], axis=0)
        d["a_rb"] = jnp.where(incl, sc[c:, :2 * c], 0.0)

    for d in chains:
        d["x"] = _dot(bf(d["a_ab"]), bf(bd(d["a_ab"])))
        d["t"] = eye_pair + d["a_ab"]
    for _ in range(c.bit_length() - 3):
        for d in chains:
            st = _dot(bf(jnp.concatenate([d["x"], d["t"]], axis=0)), bf(bd(d["x"])))
            d["x"] = st[:c]
            d["t"] = d["t"] + st[c:]
    for d in chains:
        d["t"] = d["t"] + _dot(bf(d["t"]), bf(bd(d["x"])))

    for d in chains:
        wv = _dot(bf(d["a_k"]), bf(bd(d["v"])))
        d["w1"], d["rkv"] = wv[:c], wv[c:]
    for d in chains:
        tu = _dot(bf(d["t"]), bf(jnp.concatenate([bd(d["w1"]), bd(d["a_t"])], axis=1)))
        d["u_loc"], d["a_tt"] = tu[:, :PAIR_W], tu[:, PAIR_W:]
    for d in chains:
        ar = _dot(bf(d["a_rb"]), bf(jnp.concatenate([bd(d["u_loc"]), bd(d["a_tt"])], axis=1)))
        d["y_loc"] = ar[:, :PAIR_W] + d["rkv"]
        d["r_g"] = d["r_t"] + ar[:, PAIR_W:]
    for d in chains:
        lhs = jnp.concatenate([jnp.concatenate([d["v"], jnp.zeros_like(d["v"])], axis=1),
                               jnp.concatenate([d["u_loc"], d["a_tt"]], axis=1)], axis=0)
        rhs = jnp.concatenate([d["k_h"], d["b_h"]], axis=0)
        dp = lax.dot_general(bf(lhs), bf(rhs), TN_DIMS, preferred_element_type=F32)
        d["d_loc"] = jnp.where(bd_mask, dp[:PAIR_W], 0.0)
        d["phi"] = bf(jnp.where(bd_mask, dp[PAIR_W:], 0.0))

    state = [s_scr[p] for p in range(N_PAIRS)]
    for d in chains:
        p = d["p"]
        s_hi, s_lo = _split2(state[p])
        y = lax.dot_general(bf(d["r_g"]), s_hi, NT_DIMS, preferred_element_type=F32) + d["y_loc"]
        y_ref[0, d["ci"] * c:(d["ci"] + 1) * c, p * PAIR_W:(p + 1) * PAIR_W] = y
        state[p] = state[p] * d["p_c"] + (_dot(s_hi, d["phi"]) + _dot(s_lo, d["phi"])) + d["d_loc"]
    for p in range(N_PAIRS):
        s_scr[p] = state[p]

    @pl.when(t == pl.num_programs(1) - 1)
    def _():
        sf_ref[0] = s_scr[...]


def _wkv_scan(r, lw, kx, v, na, kb, s0_pair, tt):
    b, t, _ = r.shape
    assert CHUNK == HEAD_DIM and tt % CHUNK == 0
    seq = pl.BlockSpec((1, tt, RWKV_W), lambda i, j: (i, j, 0))
    st = pl.BlockSpec((1, N_PAIRS, PAIR_W, PAIR_W), lambda i, j: (i, 0, 0, 0))
    est = 2 * 7 * tt * RWKV_W * 4 + 5 * N_PAIRS * PAIR_W * PAIR_W * 4 + (8 << 20)
    return pl.pallas_call(
        _wkv_kernel,
        grid=(b, t // tt),
        in_specs=[seq] * 6 + [st],
        out_specs=[seq, st],
        out_shape=[jax.ShapeDtypeStruct((b, t, RWKV_W), F32),
                   jax.ShapeDtypeStruct((b, N_PAIRS, PAIR_W, PAIR_W), F32)],
        scratch_shapes=[pltpu.VMEM((N_PAIRS, PAIR_W, PAIR_W), F32)],
        compiler_params=_params(2, est),
        name="wkv_scan",
    )(r, lw, kx, v, na, kb, s0_pair)


def _state_to_pairs(s):
    b = s.shape[0]
    s = s.astype(F32).reshape(b, N_PAIRS, 2, HEAD_DIM, HEAD_DIM)
    out = jnp.zeros((b, N_PAIRS, PAIR_W, PAIR_W), F32)
    out = out.at[:, :, :HEAD_DIM, :HEAD_DIM].set(s[:, :, 0])
    return out.at[:, :, HEAD_DIM:, HEAD_DIM:].set(s[:, :, 1])


def _pairs_to_state(sp):
    b = sp.shape[0]
    s = jnp.stack([sp[:, :, :HEAD_DIM, :HEAD_DIM], sp[:, :, HEAD_DIM:, HEAD_DIM:]], axis=2)
    return s.reshape(b, N_HEADS, HEAD_DIM, HEAD_DIM)


def _mix_out_kernel(y_ref, bonus_ref, g_ref, po_ref, x_ref, lg_ref, lb_ref, ones_ref, wo_ref, ln2_ref,
                    rw_hi_ref, rw_lo_ref, rb_ref, cin_ref, tail_ref,
                    h_o, xn_o, route_o, gate_o, cnt_o, carry, *, n_blocks, has_tail):
    i = pl.program_id(0)

    @pl.when(i == 0)
    def _():
        carry[...] = cin_ref[...]

    @pl.when(i < n_blocks)
    def _():
        _mix_out_block(y_ref, bonus_ref, g_ref, po_ref, x_ref, lg_ref, lb_ref, ones_ref, wo_ref, ln2_ref,
                       rw_hi_ref, rw_lo_ref, rb_ref, h_o, xn_o, route_o, gate_o, cnt_o, carry)

    if has_tail:
        @pl.when(i >= n_blocks)
        def _():
            xn_o[...] = tail_ref[...]


def _mix_out_block(y_ref, bonus_ref, g_ref, po_ref, x_ref, lg_ref, lb_ref, ones_ref, wo_ref, ln2_ref,
                   rw_hi_ref, rw_lo_ref, rb_ref, h_o, xn_o, route_o, gate_o, cnt_o, carry):
    tm = y_ref.shape[0]
    ones_bd = ones_ref[...]
    y = y_ref[...]
    inv_n = 1.0 / HEAD_DIM
    mu = _group_sum(y, ones_bd) * inv_n
    dlt = y - mu
    var = _group_sum(dlt * dlt, ones_bd) * inv_n
    yn = dlt * lax.rsqrt(var + LNX_EPS) * lg_ref[...] + lb_ref[...]
    rw_out = (yn + bonus_ref[...]) * g_ref[...]
    mix = _dot(po_ref[...], wo_ref[:POOL_W, :]) + _dot(rw_out.astype(BF16), wo_ref[POOL_W:, :])
    h = x_ref[...] + mix
    h_o[...] = h
    xn = _rmsnorm_rows(h, ln2_ref[...])
    xn_o[...] = xn

    x_hi, x_lo = _split2(xn)
    logits = (_dot(x_hi, rw_hi_ref[...]) + _dot(x_hi, rw_lo_ref[...]) + _dot(x_lo, rw_hi_ref[...])
              + rb_ref[...])
    lane = lax.broadcasted_iota(I32, (tm, ROUTE_LANES), 1)
    lane_f = lane.astype(F32)
    vals, idxs, hots = [], [], []
    work = logits
    for _ in range(TOP_K):
        m = jnp.max(work, axis=-1, keepdims=True)
        idx = jnp.min(jnp.where(work == m, lane_f, float(ROUTE_LANES)), axis=-1, keepdims=True)
        hit = lane_f == idx
        vals.append(m)
        idxs.append(idx)
        hots.append(jnp.where(hit, 1.0, 0.0).astype(F32))
        work = jnp.where(hit, -jnp.inf, work)
    exps = [jnp.exp(vv - vals[0]) for vv in vals]
    den = exps[0] + exps[1] + exps[2] + exps[3]

    hot_all = hots[0] + hots[1] + hots[2] + hots[3]
    rr = lax.broadcasted_iota(I32, (tm, tm), 0)
    cc = lax.broadcasted_iota(I32, (tm, tm), 1)
    tri_strict = jnp.where(cc < rr, 1.0, 0.0).astype(BF16)
    prefix = _dot(tri_strict, hot_all.astype(BF16)) + carry[0:1, :]
    route = jnp.zeros((tm, ROUTE_LANES), F32)
    gates = jnp.zeros((tm, ROUTE_LANES), F32)
    for j in range(TOP_K):
        rank = jnp.sum(hots[j] * prefix, axis=-1, keepdims=True)
        route = jnp.where(lane == j, idxs[j], route)
        route = jnp.where(lane == TOP_K + j, rank, route)
        gates = jnp.where(lane == j, exps[j] / den, gates)
    route_o[...] = route.astype(I32)
    gate_o[...] = gates
    new_carry = carry[...] + jnp.sum(hot_all, axis=0, keepdims=True)
    carry[...] = new_carry
    cnt_o[...] = new_carry


def _mix_out(y2d, bonus2d, g2d, po2d, x2d, lnx_g, lnx_b, ones_bd, w_out_bf, ln2_g,
             rw_hi, rw_lo, rb_pad, carry_in, tail, tm):
    n = x2d.shape[0]
    nb = n // tm
    has_tail = tail is not None
    if not has_tail:
        tail = jnp.zeros((SUBLANES, D_MODEL), F32)
    row = lambda x: x.reshape(1, -1)
    cst = lambda shape: pl.BlockSpec(shape, lambda i: (0,) * len(shape))
    blk = lambda i: jnp.minimum(i, nb - 1)
    half = pl.BlockSpec((tm, RWKV_W), lambda i: (blk(i), 0))
    full = pl.BlockSpec((tm, D_MODEL), lambda i: (blk(i), 0))
    lanes = pl.BlockSpec((tm, ROUTE_LANES), lambda i: (blk(i), 0))
    est = (2 * (4 * tm * RWKV_W * 4 + 4 * tm * D_MODEL * 4) + 2 * (D_MODEL * D_MODEL * 2 + RWKV_W * RWKV_W * 2)
           + 8 * tm * D_MODEL * 4 + 4 * tm * tm)
    return pl.pallas_call(
        functools.partial(_mix_out_kernel, n_blocks=nb, has_tail=has_tail),
        grid=(nb + int(has_tail),),
        in_specs=[half, half, half, half, full,
                  cst((1, RWKV_W)), cst((1, RWKV_W)), cst((RWKV_W, RWKV_W)), cst((D_MODEL, D_MODEL)),
                  cst((1, D_MODEL)), cst((D_MODEL, ROUTE_LANES)), cst((D_MODEL, ROUTE_LANES)),
                  cst((1, ROUTE_LANES)), cst((SUBLANES, ROUTE_LANES)), cst(tail.shape)],
        out_specs=[full, pl.BlockSpec((tm, D_MODEL), lambda i: (i, 0)), lanes, lanes, cst((SUBLANES, ROUTE_LANES))],
        out_shape=[jax.ShapeDtypeStruct((n, D_MODEL), F32),
                   jax.ShapeDtypeStruct((n + tm * int(has_tail), D_MODEL), F32),
                   jax.ShapeDtypeStruct((n, ROUTE_LANES), I32), jax.ShapeDtypeStruct((n, ROUTE_LANES), F32),
                   jax.ShapeDtypeStruct((SUBLANES, ROUTE_LANES), F32)],
        scratch_shapes=[pltpu.VMEM((SUBLANES, ROUTE_LANES), F32)],
        compiler_params=_params(1, est),
        name="mix_out_router",
    )(y2d, bonus2d, g2d, po2d, x2d, row(lnx_g), row(lnx_b), ones_bd, w_out_bf, row(ln2_g),
      rw_hi, rw_lo, rb_pad, carry_in, tail)


def _row_copy(src, src_row, dst, dst_row, sem):
    return pltpu.make_async_copy(src.at[pl.ds(src_row, 1), :], dst.at[pl.ds(dst_row, 1), :], sem)


def _expert_kernel(be_ref, nu_ref, bv_ref, src_ref, srcn_ref, dst_ref, xn_hbm, wgu_ref, bgu_ref, wdn_ref, bdn_ref,
                   y4_hbm, xbuf, ybuf, sem_g, sem_s, wgu_bf, wdn_bf, *, n_rows, n_valid_rows):
    i = pl.program_id(0)
    n_used = nu_ref[0]
    blk = xbuf.shape[1]
    slot = lax.rem(i, 2)
    other = 1 - slot

    def gather(idx_ref, s):
        def body(r, carry):
            _row_copy(xn_hbm, idx_ref[r], xbuf.at[s], r, sem_g.at[s]).start()
            return carry
        lax.fori_loop(0, blk, body, 0)

    def scatter(s, n_rows_valid):
        def body(r, carry):
            _row_copy(ybuf.at[s], r, y4_hbm, dst_ref[r], sem_s.at[s]).start()
            return carry
        lax.fori_loop(0, n_rows_valid, body, 0)

    def wait_scatter(s, n_rows_valid):
        @pl.when(n_rows_valid == blk)
        def _():
            pltpu.make_async_copy(ybuf.at[s], ybuf.at[s], sem_s.at[s]).wait()

        @pl.when(n_rows_valid < blk)
        def _():
            def body(r, carry):
                _row_copy(ybuf.at[s], 0, y4_hbm, 0, sem_s.at[s]).wait()
                return carry
            lax.fori_loop(0, n_rows_valid, body, 0)

    @pl.when(i == 0)
    def _():
        n_fill = n_rows - n_valid_rows
        if n_fill:
            ybuf[1, 0:n_fill, :] = jnp.zeros((n_fill, D_MODEL), F32)
            fills = [pltpu.make_async_copy(ybuf.at[1, pl.ds(0, n_fill), :],
                                           y4_hbm.at[pl.ds(j * n_rows + n_valid_rows, n_fill), :], sem_s.at[1])
                     for j in range(TOP_K)]
            for f in fills:
                f.start()
            for f in fills:
                f.wait()
        gather(src_ref, 0)

    @pl.when(i < n_used)
    def _():
        changed = jnp.logical_or(i == 0, be_ref[i] != be_ref[jnp.maximum(i - 1, 0)])

        @pl.when(changed)
        def _():
            wgu_bf[...] = wgu_ref[0].astype(BF16)
            wdn_bf[...] = wdn_ref[0].astype(BF16)

        @pl.when(i + 1 < n_used)
        def _():
            gather(srcn_ref, other)

        pltpu.make_async_copy(xbuf.at[slot], xbuf.at[slot], sem_g.at[slot]).wait()
        gu = _dot(xbuf[slot].astype(BF16), wgu_bf[...]) + bgu_ref[0]
        gate = jnp.minimum(gu[:, :D_EXPERT], SWIGLU_LIMIT)
        up = jnp.clip(gu[:, D_EXPERT:], -SWIGLU_LIMIT, SWIGLU_LIMIT)
        hmid = (up + 1.0) * (gate * jax.nn.sigmoid(SWIGLU_ALPHA * gate))
        ybuf[slot] = _dot(hmid.astype(BF16), wdn_bf[...]) + bdn_ref[0]
        scatter(slot, bv_ref[i])

        @pl.when(i >= 1)
        def _():
            wait_scatter(other, bv_ref[jnp.maximum(i - 1, 0)])

        @pl.when(i == n_used - 1)
        def _():
            wait_scatter(slot, bv_ref[i])


def _experts(block_e, n_used, block_valid, slot_src, slot_dst, xn_all, w_gu, b_gu, w_down, b_down,
             n_rows, n_valid_rows):
    blk = EXPERT_BLOCK
    nb = slot_src.shape[0] // blk
    est = (4 * blk * D_MODEL * 4 + 2 * (D_MODEL * 2 * D_EXPERT + D_EXPERT * D_MODEL) * 4
           + (D_MODEL * 2 * D_EXPERT + D_EXPERT * D_MODEL) * 2 + 4 * blk * 2 * D_EXPERT * 4)
    smem_blk = lambda f: pl.BlockSpec((blk,), f, memory_space=pltpu.SMEM)
    grid_spec = pltpu.PrefetchScalarGridSpec(
        num_scalar_prefetch=3,
        grid=(nb,),
        in_specs=[smem_blk(lambda i, be, nu, bv: (i,)),
                  smem_blk(lambda i, be, nu, bv: (jnp.minimum(i + 1, nb - 1),)),
                  smem_blk(lambda i, be, nu, bv: (i,)),
                  pl.BlockSpec(memory_space=pl.ANY),
                  pl.BlockSpec((1, D_MODEL, 2 * D_EXPERT), lambda i, be, nu, bv: (be[i], 0, 0)),
                  pl.BlockSpec((1, 1, 2 * D_EXPERT), lambda i, be, nu, bv: (be[i], 0, 0)),
                  pl.BlockSpec((1, D_EXPERT, D_MODEL), lambda i, be, nu, bv: (be[i], 0, 0)),
                  pl.BlockSpec((1, 1, D_MODEL), lambda i, be, nu, bv: (be[i], 0, 0))],
        out_specs=pl.BlockSpec(memory_space=pl.ANY),
        scratch_shapes=[pltpu.VMEM((2, blk, D_MODEL), F32), pltpu.VMEM((2, blk, D_MODEL), F32),
                        pltpu.SemaphoreType.DMA((2,)), pltpu.SemaphoreType.DMA((2,)),
                        pltpu.VMEM((D_MODEL, 2 * D_EXPERT), BF16), pltpu.VMEM((D_EXPERT, D_MODEL), BF16)],
    )
    return pl.pallas_call(
        functools.partial(_expert_kernel, n_rows=n_rows, n_valid_rows=n_valid_rows),
        grid_spec=grid_spec,
        out_shape=jax.ShapeDtypeStruct((TOP_K * n_rows, D_MODEL), F32),
        compiler_params=_params(1, est),
        name="experts",
    )(block_e, n_used, block_valid, slot_src, slot_src, slot_dst, xn_all, w_gu,
      b_gu.reshape(N_EXPERTS, 1, 2 * D_EXPERT), w_down, b_down.reshape(N_EXPERTS, 1, D_MODEL))


def _combine_kernel(y0_ref, y1_ref, y2_ref, y3_ref, gate_ref, h_ref, p_ref, ln3_ref, pg_ref, pp_ref, fg_ref, o_ref):
    gates = gate_ref[...]
    moe = gates[:, 0:1] * y0_ref[...]
    for j, y_ref in enumerate((y1_ref, y2_ref, y3_ref), start=1):
        moe = moe + gates[:, j:j + 1] * y_ref[...]
    h = h_ref[...] + moe
    gate = jax.nn.sigmoid(_dot(_rmsnorm_rows(h, ln3_ref[...]).astype(BF16), pg_ref[...]))
    h = h + gate * _dot(p_ref[...].astype(BF16), pp_ref[...])
    o_ref[...] = _rmsnorm_rows(h, fg_ref[...])


def _combine(y4, row0, n_rows, gates, h2d, p2d, ln3_g, ple_gate_bf, ple_proj_bf, final_g, tm):
    n = h2d.shape[0]
    assert row0 % tm == 0 and n_rows % tm == 0
    row = lambda x: x.reshape(1, -1)
    cst = lambda shape: pl.BlockSpec(shape, lambda i: (0,) * len(shape))
    ysp = lambda j: pl.BlockSpec((tm, D_MODEL), lambda i: ((j * n_rows + row0) // tm + i, 0))
    est = (2 * (6 * tm * D_MODEL * 4 + tm * PLE_DIM * 4 + tm * ROUTE_LANES * 4)
           + 2 * (D_MODEL * D_MODEL * 2 + PLE_DIM * D_MODEL * 2) + 6 * tm * D_MODEL * 4)
    return pl.pallas_call(
        _combine_kernel,
        grid=(n // tm,),
        in_specs=[ysp(0), ysp(1), ysp(2), ysp(3),
                  pl.BlockSpec((tm, ROUTE_LANES), lambda i: (i, 0)),
                  pl.BlockSpec((tm, D_MODEL), lambda i: (i, 0)),
                  pl.BlockSpec((tm, PLE_DIM), lambda i: (i, 0)),
                  cst((1, D_MODEL)), cst((D_MODEL, D_MODEL)), cst((PLE_DIM, D_MODEL)), cst((1, D_MODEL))],
        out_specs=pl.BlockSpec((tm, D_MODEL), lambda i: (i, 0)),
        out_shape=jax.ShapeDtypeStruct((n, D_MODEL), F32),
        compiler_params=_params(1, est),
        name="combine_ple_norm",
    )(y4, y4, y4, y4, gates, h2d, p2d, row(ln3_g), ple_gate_bf, ple_proj_bf, row(final_g))


def _tile(n, pref):
    t = min(n, pref)
    assert n % t == 0, (n, pref)
    return t


def _front(x, pos0, pool_ctx, shift_prev, wkv0, wts, carry_in, tail, tm_mix):
    b, t, _ = x.shape
    n = b * t
    assert t >= POOL_CTX and t % SUBLANES == 0
    x2d = x.reshape(n, D_MODEL)
    u2d, zr2d = _in_proj(x2d, wts["ln1_g"], wts["w_in_bf"], _tile(n, 512))
    u = u2d.reshape(b, t, POOL_W)
    zr = zr2d.reshape(b, t, RWKV_PROJ)

    ctx16 = jnp.concatenate([jnp.zeros((b, POOL_HALO - POOL_CTX, POOL_W), F32), pool_ctx.astype(F32)], axis=1)
    pool_out = _pool_mixer(u, ctx16, wts["w_pool_bf"], wts["pool_scale"], pos0, _tile(t, 512))
    new_pool = u[:, -POOL_CTX:]

    sh8 = jnp.broadcast_to(shift_prev.astype(F32), (b, SUBLANES, RWKV_PROJ))
    r, lw, kx, v, na, kb, g, bonus = _rwkv_pre(
        zr, sh8, wts["mu_shift"], wts["w0"], wts["w2_bf"], wts["a0"], wts["a2_bf"], wts["g2_bf"],
        wts["k_k"], wts["k_a"], wts["r_k"], wts["ones_bd"], _tile(t, 256))
    new_shift = zr[:, -1:]

    t_pad = -(-t // CHUNK) * CHUNK
    seqs = (r, lw, kx, v, na, kb)
    if t_pad != t:
        seqs = tuple(jnp.pad(s, ((0, 0), (0, t_pad - t), (0, 0))) for s in seqs)
    y, s_fin = _wkv_scan(*seqs, _state_to_pairs(wkv0), _tile(t_pad, 2 * CHUNK))
    y = y[:, :t]
    new_wkv = _pairs_to_state(s_fin)

    h, xn, route, gates, counts = _mix_out(
        y.reshape(n, RWKV_W), bonus.reshape(n, RWKV_W), g.reshape(n, RWKV_W), pool_out.reshape(n, POOL_W),
        x2d, wts["lnx_g"], wts["lnx_b"], wts["ones_bd"], wts["w_out_bf"], wts["ln2_g"],
        wts["rw_hi"], wts["rw_lo"], wts["rb_pad"], carry_in, tail, tm_mix)
    return dict(h=h, xn=xn, route=route, gates=gates, counts=counts,
                new_pool=new_pool, new_shift=new_shift, new_wkv=new_wkv)


def _prep_weights(i, ln1_g, w_in, mu_shift, w_pool, pool_scale, w0, w2, a0, a2, g2, k_k, k_a, r_k,
                  lnx_g, lnx_b, w_out, ln2_g, router_w, router_b, ln3_g, ple_gate, ple_proj):
    head = jnp.arange(RWKV_W) // HEAD_DIM
    ones_bd = (head[:, None] == head[None, :]).astype(BF16)
    rw_pad = jnp.pad(router_w[i].astype(F32), ((0, 0), (0, ROUTE_LANES - N_EXPERTS)))
    rw_hi = rw_pad.astype(BF16)
    rw_lo = (rw_pad - rw_hi.astype(F32)).astype(BF16)
    rb_pad = jnp.concatenate([router_b[i].astype(F32), jnp.full((ROUTE_LANES - N_EXPERTS,), -jnp.inf, F32)])
    return dict(
        ln1_g=ln1_g[i], w_in_bf=w_in[i].astype(BF16), mu_shift=mu_shift[i], w_pool_bf=w_pool[i].astype(BF16),
        pool_scale=pool_scale[i], w0=w0[i], w2_bf=w2[i].astype(BF16), a0=a0[i], a2_bf=a2[i].astype(BF16),
        g2_bf=g2[i].astype(BF16), k_k=k_k[i], k_a=k_a[i], r_k=r_k[i].reshape(RWKV_W), lnx_g=lnx_g[i],
        lnx_b=lnx_b[i], w_out_bf=w_out[i].astype(BF16), ln2_g=ln2_g[i], rw_hi=rw_hi, rw_lo=rw_lo,
        rb_pad=rb_pad.reshape(1, ROUTE_LANES), ones_bd=ones_bd, ln3_g=ln3_g[i],
        ple_gate_bf=ple_gate[i].astype(BF16), ple_proj_bf=ple_proj[i].astype(BF16))


def _layer_pair(hp, hs, pp, ps, pool_ctx, shift_prev, wkv0, wts, w_gu, b_gu, w_down, b_down, final_g):
    bp, tp, _ = hp.shape
    bs, ts, _ = hs.shape
    np_, ns = bp * tp, bs * ts
    tm_p = _tile(np_, 256)
    tm_s = _tile(ns, 128)
    assert ns <= tm_p and np_ % tm_s == 0
    zero_carry = jnp.zeros((SUBLANES, ROUTE_LANES), F32)
    fs = _front(hs, PAST_LEN, pool_ctx, shift_prev, wkv0, wts, zero_carry, None, tm_s)
    tail = jnp.concatenate([fs["xn"], jnp.zeros((tm_p - ns, D_MODEL), F32)], axis=0)
    fp = _front(hp, 0, jnp.zeros((bp, POOL_CTX, POOL_W), F32), jnp.zeros((bp, 1, RWKV_PROJ), F32),
                jnp.zeros((bp, N_HEADS, HEAD_DIM, HEAD_DIM), F32), wts, fs["counts"], tail, tm_p)
    xn_all = fp["xn"]
    n_valid_rows = np_ + ns
    n_rows = np_ + tm_p
    zero_row = n_valid_rows if n_valid_rows < n_rows else 0

    counts = fp["counts"][0, :N_EXPERTS].astype(I32)
    padded = (counts + EXPERT_BLOCK - 1) // EXPERT_BLOCK * EXPERT_BLOCK
    pad_end = jnp.cumsum(padded)
    start_pad = pad_end - padded
    n_blocks = -(-n_valid_rows * TOP_K // EXPERT_BLOCK) + N_EXPERTS
    n_slots = n_blocks * EXPERT_BLOCK
    n_used = (pad_end[-1] // EXPERT_BLOCK).astype(I32)
    block_start = jnp.arange(n_blocks, dtype=I32) * EXPERT_BLOCK
    block_e = jnp.minimum(jnp.sum((block_start[:, None] >= pad_end[None, :]).astype(I32), axis=1), N_EXPERTS - 1)
    last_e = block_e[jnp.maximum(n_used - 1, 0)]
    block_e = jnp.where(jnp.arange(n_blocks) < n_used, block_e, last_e)

    block_valid = jnp.clip((start_pad + counts)[block_e] - block_start, 0, EXPERT_BLOCK).astype(I32)

    def slots_and_rows(route, row0):
        n = route.shape[0]
        slot = start_pad[route[:, :TOP_K]] + route[:, TOP_K:2 * TOP_K]
        rows = (row0 + jnp.arange(n, dtype=I32))[:, None] + n_rows * jnp.arange(TOP_K, dtype=I32)[None, :]
        return slot.reshape(-1), rows.reshape(-1)

    slot_p, rows_p = slots_and_rows(fp["route"], 0)
    slot_s, rows_s = slots_and_rows(fs["route"], np_)
    slot_dst = jnp.full((n_slots,), -1, I32).at[jnp.concatenate([slot_p, slot_s])].set(
        jnp.concatenate([rows_p, rows_s]), unique_indices=True)
    slot_src = jnp.where(slot_dst < 0, zero_row, slot_dst % n_rows).astype(I32)
    slot_dst = jnp.maximum(slot_dst, 0)

    y4 = _experts(block_e, n_used.reshape(1), block_valid, slot_src, slot_dst, xn_all, w_gu, b_gu, w_down, b_down,
                  n_rows, n_valid_rows)

    out_p = _combine(y4, 0, n_rows, fp["gates"], fp["h"], pp.reshape(np_, PLE_DIM).astype(F32), wts["ln3_g"],
                     wts["ple_gate_bf"], wts["ple_proj_bf"], final_g, tm_p)
    out_s = _combine(y4, np_, n_rows, fs["gates"], fs["h"], ps.reshape(ns, PLE_DIM).astype(F32), wts["ln3_g"],
                     wts["ple_gate_bf"], wts["ple_proj_bf"], final_g, tm_s)
    return out_p.reshape(hp.shape), out_s.reshape(hs.shape), fp, fs


def kernel(x_prompt, x_sample, p_prompt, p_sample, cache_pool, state_shift, state_wkv, ln1_g, w_in, mu_shift, w_pool, pool_scale, w0, w2, a0, a2, g2, k_k, k_a, r_k, lnx_g, lnx_b, w_out, ln2_g, router_w, router_b, w_gu, b_gu, w_down, b_down, ln3_g, ple_gate, ple_proj, final_g):
    assert ln1_g.shape[0] == 1, "single-layer kernel"
    i = 0
    wts = _prep_weights(i, ln1_g, w_in, mu_shift, w_pool, pool_scale, w0, w2, a0, a2, g2, k_k, k_a, r_k,
                        lnx_g, lnx_b, w_out, ln2_g, router_w, router_b, ln3_g, ple_gate, ple_proj)
    y_p, y_s, fp, fs = _layer_pair(x_prompt, x_sample, p_prompt[i], p_sample[i], cache_pool[i], state_shift[i],
                                   state_wkv[i], wts, w_gu[i], b_gu[i], w_down[i], b_down[i], final_g)
    stack = lambda a: a[None]
    return (y_p, y_s,
            stack(fp["new_pool"]), stack(fp["new_shift"]), stack(fp["new_wkv"]),
            stack(fs["new_pool"]), stack(fs["new_shift"]), stack(fs["new_wkv"]))
```

```python
import functools

import jax
import jax.numpy as jnp
from jax import lax
from jax.experimental import pallas as pl
from jax.experimental.pallas import tpu as pltpu

F32 = jnp.float32
BF16 = jnp.bfloat16
I32 = jnp.int32

D_MODEL = 1024
POOL_W = 512
POOL_WINDOWS = (2, 4, 8, 16)
POOL_GW = 128
POOL_CTX = 15
POOL_HALO = 16
RWKV_W = 512
HEAD_DIM = 64
N_HEADS = 8
N_PAIRS = N_HEADS // 2
PAIR_W = 2 * HEAD_DIM
DECAY_LORA = 64
ICLR_LORA = 64
GATE_LORA = 128
RWKV_PROJ = 3 * RWKV_W + DECAY_LORA + ICLR_LORA + GATE_LORA
IN_PROJ = POOL_W + RWKV_PROJ
LNX_EPS = 64e-5
N_EXPERTS = 32
TOP_K = 4
D_EXPERT = 1024
SWIGLU_LIMIT = 7.0
SWIGLU_ALPHA = 1.702
PLE_DIM = 256
RMS_EPS = 1e-6
PAST_LEN = 2048

CHUNK = 64
EXPERT_BLOCK = 256
ROUTE_LANES = 128
SUBLANES = 8
GROUP_TILE = 256
EXP_M05 = 0.6065306597126334
V7X_VMEM_BYTES = 64 * 1024 * 1024

NT_DIMS = (((1,), (1,)), ((), ()))
TN_DIMS = (((0,), (0,)), ((), ()))


def _vmem_limit(est_bytes):
    return int(min(est_bytes * 3 // 2 + (4 << 20), V7X_VMEM_BYTES - (8 << 20)))


def _params(n_axes, est_bytes):
    return pltpu.CompilerParams(dimension_semantics=("arbitrary",) * n_axes,
                                vmem_limit_bytes=_vmem_limit(est_bytes))


def _dot(a, b):
    return jnp.dot(a, b, preferred_element_type=F32)


def _split2(x):
    hi = x.astype(BF16)
    lo = (x - hi.astype(F32)).astype(BF16)
    return hi, lo


def _split3(x):
    hi = x.astype(BF16)
    r1 = x - hi.astype(F32)
    mid = r1.astype(BF16)
    lo = (r1 - mid.astype(F32)).astype(BF16)
    return hi, mid, lo


def _group_sum(x, ones_bd):
    hi, lo = _split2(x)
    slabs = [_dot(hi[:, s:s + GROUP_TILE], ones_bd) + _dot(lo[:, s:s + GROUP_TILE], ones_bd)
             for s in range(0, x.shape[1], GROUP_TILE)]
    return jnp.concatenate(slabs, axis=1)


def _rmsnorm_rows(x, g):
    ms = jnp.mean(x * x, axis=-1, keepdims=True)
    return (x * lax.rsqrt(ms + RMS_EPS)) * g


def _in_proj_kernel(x_ref, g_ref, w_ref, u_ref, zr_ref):
    xn = _rmsnorm_rows(x_ref[...], g_ref[...])
    z = _dot(xn.astype(BF16), w_ref[...])
    u_ref[...] = z[:, :POOL_W]
    zr_ref[...] = z[:, POOL_W:]


def _in_proj(x2d, ln1_g, w_in_bf, tm):
    n = x2d.shape[0]
    est = 2 * tm * D_MODEL * 4 + 2 * D_MODEL * IN_PROJ * 2 + 3 * tm * IN_PROJ * 4
    return pl.pallas_call(
        _in_proj_kernel,
        grid=(n // tm,),
        in_specs=[pl.BlockSpec((tm, D_MODEL), lambda i: (i, 0)),
                  pl.BlockSpec((1, D_MODEL), lambda i: (0, 0)),
                  pl.BlockSpec((D_MODEL, IN_PROJ), lambda i: (0, 0))],
        out_specs=[pl.BlockSpec((tm, POOL_W), lambda i: (i, 0)),
                   pl.BlockSpec((tm, RWKV_PROJ), lambda i: (i, 0))],
        out_shape=[jax.ShapeDtypeStruct((n, POOL_W), F32),
                   jax.ShapeDtypeStruct((n, RWKV_PROJ), F32)],
        compiler_params=_params(1, est),
        name="in_proj",
    )(x2d, ln1_g.reshape(1, D_MODEL), w_in_bf)


def _pool_kernel(u_ref, halo_ref, ctx_ref, wp_ref, ps_ref, o_ref, buf, *, pos0):
    t = pl.program_id(1)
    tt = u_ref.shape[1]
    buf[0:POOL_HALO, :] = jnp.where(t == 0, ctx_ref[0], halo_ref[0])
    buf[POOL_HALO:, :] = u_ref[0]
    pos = lax.broadcasted_iota(I32, (tt, 1), 0) + (t * tt + pos0)
    outs = []
    for gi, w in enumerate(POOL_WINDOWS):
        sl = slice(gi * POOL_GW, (gi + 1) * POOL_GW)
        cur = buf[POOL_HALO:POOL_HALO + tt, sl]
        acc = cur
        for j in range(1, w):
            acc = acc + buf[POOL_HALO - j:POOL_HALO - j + tt, sl]
        cnt = jnp.minimum(w, pos + 1).astype(F32)
        pooled = acc / cnt - cur
        outs.append(_dot(pooled.astype(BF16), wp_ref[gi]))
    o_ref[0] = (jnp.concatenate(outs, axis=-1) * ps_ref[...]).astype(o_ref.dtype)


def _pool_mixer(u, ctx16, w_pool_bf, pool_scale, pos0, tt):
    b, t, _ = u.shape
    hb = tt // POOL_HALO
    est = 2 * tt * POOL_W * 4 * 3 + (tt + POOL_HALO) * POOL_W * 4
    return pl.pallas_call(
        functools.partial(_pool_kernel, pos0=pos0),
        grid=(b, t // tt),
        in_specs=[pl.BlockSpec((1, tt, POOL_W), lambda i, j: (i, j, 0)),
                  pl.BlockSpec((1, POOL_HALO, POOL_W), lambda i, j: (i, jnp.maximum(j * hb - 1, 0), 0)),
                  pl.BlockSpec((1, POOL_HALO, POOL_W), lambda i, j: (i, 0, 0)),
                  pl.BlockSpec((len(POOL_WINDOWS), POOL_GW, POOL_GW), lambda i, j: (0, 0, 0)),
                  pl.BlockSpec((1, POOL_W), lambda i, j: (0, 0))],
        out_specs=pl.BlockSpec((1, tt, POOL_W), lambda i, j: (i, j, 0)),
        out_shape=jax.ShapeDtypeStruct((b, t, POOL_W), BF16),
        scratch_shapes=[pltpu.VMEM((tt + POOL_HALO, POOL_W), F32)],
        compiler_params=_params(2, est),
        name="pool_mixer",
    )(u, u, ctx16, w_pool_bf, pool_scale.reshape(1, POOL_W))


def _rwkv_pre_kernel(zr_ref, halo_ref, sh_ref, mu_ref, w0_ref, w2_ref, a0_ref, a2_ref, g2_ref,
                     kk_ref, ka_ref, rk_ref, ones_ref,
                     r_o, lw_o, kx_o, v_o, na_o, kb_o, g_o, bonus_o, buf):
    t = pl.program_id(1)
    tt = zr_ref.shape[1]
    buf[0:SUBLANES, :] = jnp.where(t == 0, sh_ref[0], halo_ref[0])
    zr = zr_ref[0]
    buf[SUBLANES:, :] = zr
    prev = buf[SUBLANES - 1:SUBLANES - 1 + tt, :]
    zs = zr + (prev - zr) * mu_ref[...]
    o1, o2, o3 = RWKV_W, 2 * RWKV_W, 3 * RWKV_W
    o4, o5 = o3 + DECAY_LORA, o3 + DECAY_LORA + ICLR_LORA
    r, k, v = zs[:, :o1], zs[:, o1:o2], zs[:, o2:o3]
    wd, ad, gd = zs[:, o3:o4], zs[:, o4:o5], zs[:, o5:]
    ones_bd = ones_ref[...]
    wz = w0_ref[...] + _dot(jnp.tanh(wd).astype(BF16), w2_ref[...])
    lw = -EXP_M05 * jax.nn.sigmoid(wz)
    a = jax.nn.sigmoid(a0_ref[...] + _dot(ad.astype(BF16), a2_ref[...]))
    g = _dot(jax.nn.sigmoid(gd).astype(BF16), g2_ref[...])
    kk = k * kk_ref[...]
    kk = kk / jnp.maximum(jnp.sqrt(_group_sum(kk * kk, ones_bd)), 1e-12)
    kx = k * (1.0 + (a - 1.0) * ka_ref[...])
    bonus = _group_sum(r * kx * rk_ref[...], ones_bd) * v
    r_o[0] = r
    lw_o[0] = lw
    kx_o[0] = kx
    v_o[0] = v
    na_o[0] = -kk
    kb_o[0] = kk * a
    g_o[0] = g
    bonus_o[0] = bonus


def _rwkv_pre(zr, sh8, mu, w0, w2_bf, a0, a2_bf, g2_bf, k_k, k_a, r_k, ones_bd, tt):
    b, t, _ = zr.shape
    hb = tt // SUBLANES
    row = lambda x: x.reshape(1, -1)
    cst = lambda shape: pl.BlockSpec(shape, lambda i, j: (0,) * len(shape))
    seq = pl.BlockSpec((1, tt, RWKV_W), lambda i, j: (i, j, 0))
    est = 3 * tt * RWKV_PROJ * 4 + 2 * 8 * tt * RWKV_W * 4 + 12 * tt * RWKV_W * 4
    return pl.pallas_call(
        _rwkv_pre_kernel,
        grid=(b, t // tt),
        in_specs=[pl.BlockSpec((1, tt, RWKV_PROJ), lambda i, j: (i, j, 0)),
                  pl.BlockSpec((1, SUBLANES, RWKV_PROJ), lambda i, j: (i, jnp.maximum(j * hb - 1, 0), 0)),
                  pl.BlockSpec((1, SUBLANES, RWKV_PROJ), lambda i, j: (i, 0, 0)),
                  cst((1, RWKV_PROJ)), cst((1, RWKV_W)), cst((DECAY_LORA, RWKV_W)),
                  cst((1, RWKV_W)), cst((ICLR_LORA, RWKV_W)), cst((GATE_LORA, RWKV_W)),
                  cst((1, RWKV_W)), cst((1, RWKV_W)), cst((1, RWKV_W)), cst((GROUP_TILE, GROUP_TILE))],
        out_specs=[seq] * 8,
        out_shape=[jax.ShapeDtypeStruct((b, t, RWKV_W), F32)] * 8,
        scratch_shapes=[pltpu.VMEM((tt + SUBLANES, RWKV_PROJ), F32)],
        compiler_params=_params(2, est),
        name="rwkv_pre",
    )(zr, zr, sh8, row(mu), row(w0), w2_bf, row(a0), a2_bf, g2_bf, row(k_k), row(k_a), row(r_k), ones_bd)


def _pair_blockdiag(z, left):
    return jnp.concatenate([jnp.where(left, z, 0.0), jnp.where(left, 0.0, z)], axis=0)


def _wkv_kernel(r_ref, lw_ref, kx_ref, v_ref, na_ref, kb_ref, s0_ref, y_ref, sf_ref, s_scr):
    t = pl.program_id(1)
    tt = r_ref.shape[1]
    c = CHUNK

    @pl.when(t == 0)
    def _():
        s_scr[...] = s0_ref[0]

    lane = lax.broadcasted_iota(I32, (c, 2 * c), 1)
    rowi = lax.broadcasted_iota(I32, (c, 2 * c), 0)
    jm = lane & (c - 1)
    strict = jm < rowi
    incl = jm <= rowi
    eye_pair = jnp.where(jm == rowi, 1.0, 0.0).astype(F32)
    left = lax.broadcasted_iota(I32, (c, PAIR_W), 1) < HEAD_DIM
    rr = lax.broadcasted_iota(I32, (c, c), 0)
    cc = lax.broadcasted_iota(I32, (c, c), 1)
    tri_incl_bf = jnp.where(cc <= rr, 1.0, 0.0).astype(BF16)
    br = lax.broadcasted_iota(I32, (PAIR_W, PAIR_W), 0) < HEAD_DIM
    bc = lax.broadcasted_iota(I32, (PAIR_W, PAIR_W), 1) < HEAD_DIM
    bd_mask = br == bc
    bf = lambda x: x.astype(BF16)
    bd = lambda z: _pair_blockdiag(z, left)
    n_chunks = tt // c

    cums = []
    for ci in range(n_chunks):
        h3 = _split3(lw_ref[0, ci * c:(ci + 1) * c, :])
        cums.append(_dot(tri_incl_bf, h3[0]) + _dot(tri_incl_bf, h3[1]) + _dot(tri_incl_bf, h3[2]))

    chains = []
    for ci in range(n_chunks):
        rows = slice(ci * c, (ci + 1) * c)
        for p in range(N_PAIRS):
            cols = slice(p * PAIR_W, (p + 1) * PAIR_W)
            cum = cums[ci][:, cols]
            lw, kx, kb = lw_ref[0, rows, cols], kx_ref[0, rows, cols], kb_ref[0, rows, cols]
            tot = cum[c - 1:c, :]
            p_end = jnp.exp(tot - cum)
            inv_p = jnp.exp(-cum)
            chains.append(dict(
                ci=ci, p=p, v=v_ref[0, rows, cols], p_c=jnp.exp(tot),
                a_t=na_ref[0, rows, cols] * jnp.exp(cum - lw), r_t=r_ref[0, rows, cols] * jnp.exp(cum),
                b_t=kb * inv_p, k_t=kx * inv_p, b_h=kb * p_end, k_h=kx * p_end))

    for d in chains:
        lhs = bf(jnp.concatenate([d["a_t"], d["r_t"]], axis=0))
        rhs = bf(jnp.concatenate([bd(d["b_t"]), bd(d["k_t"])], axis=0))
        sc = lax.dot_general(lhs, rhs, NT_DIMS, preferred_element_type=F32)
        d["a_ab"] = jnp.where(strict, sc[:c, :2 * c], 0.0)
        d["a_k"] = jnp.concatenate([jnp.where(strict, sc[:c, 2 * c:], 0.0),
                                    jnp.where(incl, sc[c:, 2 * c:], 0.0)], axis=0)
        d["a_rb"] = jnp.where(incl, sc[c:, :2 * c], 0.0)

    for d in chains:
        d["x"] = _dot(bf(d["a_ab"]), bf(bd(d["a_ab"])))
        d["t"] = eye_pair + d["a_ab"]
    for _ in range(c.bit_length() - 3):
        for d in chains:
            st = _dot(bf(jnp.concatenate([d["x"], d["t"]], axis=0)), bf(bd(d["x"])))
            d["x"] = st[:c]
            d["t"] = d["t"] + st[c:]
    for d in chains:
        d["t"] = d["t"] + _dot(bf(d["t"]), bf(bd(d["x"])))

    for d in chains:
        wv = _dot(bf(d["a_k"]), bf(bd(d["v"])))
        d["w1"], d["rkv"] = wv[:c], wv[c:]
    for d in chains:
        tu = _dot(bf(d["t"]), bf(jnp.concatenate([bd(d["w1"]), bd(d["a_t"])], axis=1)))
        d["u_loc"], d["a_tt"] = tu[:, :PAIR_W], tu[:, PAIR_W:]
    for d in chains:
        ar = _dot(bf(d["a_rb"]), bf(jnp.concatenate([bd(d["u_loc"]), bd(d["a_tt"])], axis=1)))
        d["y_loc"] = ar[:, :PAIR_W] + d["rkv"]
        d["r_g"] = d["r_t"] + ar[:, PAIR_W:]
    for d in chains:
        lhs = jnp.concatenate([jnp.concatenate([d["v"], jnp.zeros_like(d["v"])], axis=1),
                               jnp.concatenate([d["u_loc"], d["a_tt"]], axis=1)], axis=0)
        rhs = jnp.concatenate([d["k_h"], d["b_h"]], axis=0)
        dp = lax.dot_general(bf(lhs), bf(rhs), TN_DIMS, preferred_element_type=F32)
        d["d_loc"] = jnp.where(bd_mask, dp[:PAIR_W], 0.0)
        d["phi"] = bf(jnp.where(bd_mask, dp[PAIR_W:], 0.0))

    state = [s_scr[p] for p in range(N_PAIRS)]
    for d in chains:
        p = d["p"]
        s_hi, s_lo = _split2(state[p])
        y = lax.dot_general(bf(d["r_g"]), s_hi, NT_DIMS, preferred_element_type=F32) + d["y_loc"]
        y_ref[0, d["ci"] * c:(d["ci"] + 1) * c, p * PAIR_W:(p + 1) * PAIR_W] = y
        state[p] = state[p] * d["p_c"] + (_dot(s_hi, d["phi"]) + _dot(s_lo, d["phi"])) + d["d_loc"]
    for p in range(N_PAIRS):
        s_scr[p] = state[p]

    @pl.when(t == pl.num_programs(1) - 1)
    def _():
        sf_ref[0] = s_scr[...]


def _wkv_scan(r, lw, kx, v, na, kb, s0_pair, tt):
    b, t, _ = r.shape
    assert CHUNK == HEAD_DIM and tt % CHUNK == 0
    seq = pl.BlockSpec((1, tt, RWKV_W), lambda i, j: (i, j, 0))
    st = pl.BlockSpec((1, N_PAIRS, PAIR_W, PAIR_W), lambda i, j: (i, 0, 0, 0))
    est = 2 * 7 * tt * RWKV_W * 4 + 5 * N_PAIRS * PAIR_W * PAIR_W * 4 + (8 << 20)
    return pl.pallas_call(
        _wkv_kernel,
        grid=(b, t // tt),
        in_specs=[seq] * 6 + [st],
        out_specs=[seq, st],
        out_shape=[jax.ShapeDtypeStruct((b, t, RWKV_W), F32),
                   jax.ShapeDtypeStruct((b, N_PAIRS, PAIR_W, PAIR_W), F32)],
        scratch_shapes=[pltpu.VMEM((N_PAIRS, PAIR_W, PAIR_W), F32)],
        compiler_params=_params(2, est),
        name="wkv_scan",
    )(r, lw, kx, v, na, kb, s0_pair)


def _state_to_pairs(s):
    b = s.shape[0]
    s = s.astype(F32).reshape(b, N_PAIRS, 2, HEAD_DIM, HEAD_DIM)
    out = jnp.zeros((b, N_PAIRS, PAIR_W, PAIR_W), F32)
    out = out.at[:, :, :HEAD_DIM, :HEAD_DIM].set(s[:, :, 0])
    return out.at[:, :, HEAD_DIM:, HEAD_DIM:].set(s[:, :, 1])


def _pairs_to_state(sp):
    b = sp.shape[0]
    s = jnp.stack([sp[:, :, :HEAD_DIM, :HEAD_DIM], sp[:, :, HEAD_DIM:, HEAD_DIM:]], axis=2)
    return s.reshape(b, N_HEADS, HEAD_DIM, HEAD_DIM)


def _mix_out_kernel(y_ref, bonus_ref, g_ref, po_ref, x_ref, lg_ref, lb_ref, ones_ref, wo_ref, ln2_ref,
                    rw_hi_ref, rw_lo_ref, rb_ref, cin_ref,
                    h_o, xn_o, route_o, gate_o, cnt_o, carry):
    i = pl.program_id(0)
    tm = y_ref.shape[0]

    @pl.when(i == 0)
    def _():
        carry[...] = cin_ref[...]

    ones_bd = ones_ref[...]
    y = y_ref[...]
    inv_n = 1.0 / HEAD_DIM
    mu = _group_sum(y, ones_bd) * inv_n
    dlt = y - mu
    var = _group_sum(dlt * dlt, ones_bd) * inv_n
    yn = dlt * lax.rsqrt(var + LNX_EPS) * lg_ref[...] + lb_ref[...]
    rw_out = (yn + bonus_ref[...]) * g_ref[...]
    mix = _dot(po_ref[...], wo_ref[:POOL_W, :]) + _dot(rw_out.astype(BF16), wo_ref[POOL_W:, :])
    h = x_ref[...] + mix
    h_o[...] = h
    xn = _rmsnorm_rows(h, ln2_ref[...])
    xn_o[...] = xn

    x_hi, x_lo = _split2(xn)
    logits = (_dot(x_hi, rw_hi_ref[...]) + _dot(x_hi, rw_lo_ref[...]) + _dot(x_lo, rw_hi_ref[...])
              + rb_ref[...])
    lane = lax.broadcasted_iota(I32, (tm, ROUTE_LANES), 1)
    lane_f = lane.astype(F32)
    vals, idxs, hots = [], [], []
    work = logits
    for _ in range(TOP_K):
        m = jnp.max(work, axis=-1, keepdims=True)
        idx = jnp.min(jnp.where(work == m, lane_f, float(ROUTE_LANES)), axis=-1, keepdims=True)
        hit = lane_f == idx
        vals.append(m)
        idxs.append(idx)
        hots.append(jnp.where(hit, 1.0, 0.0).astype(F32))
        work = jnp.where(hit, -jnp.inf, work)
    exps = [jnp.exp(vv - vals[0]) for vv in vals]
    den = exps[0] + exps[1] + exps[2] + exps[3]

    hot_all = hots[0] + hots[1] + hots[2] + hots[3]
    rr = lax.broadcasted_iota(I32, (tm, tm), 0)
    cc = lax.broadcasted_iota(I32, (tm, tm), 1)
    tri_strict = jnp.where(cc < rr, 1.0, 0.0).astype(BF16)
    prefix = _dot(tri_strict, hot_all.astype(BF16)) + carry[0:1, :]
    route = jnp.zeros((tm, ROUTE_LANES), F32)
    gates = jnp.zeros((tm, ROUTE_LANES), F32)
    for j in range(TOP_K):
        rank = jnp.sum(hots[j] * prefix, axis=-1, keepdims=True)
        route = jnp.where(lane == j, idxs[j], route)
        route = jnp.where(lane == TOP_K + j, rank, route)
        gates = jnp.where(lane == j, exps[j] / den, gates)
    route_o[...] = route.astype(I32)
    gate_o[...] = gates
    new_carry = carry[...] + jnp.sum(hot_all, axis=0, keepdims=True)
    carry[...] = new_carry
    cnt_o[...] = new_carry


def _mix_out(y2d, bonus2d, g2d, po2d, x2d, lnx_g, lnx_b, ones_bd, w_out_bf, ln2_g,
             rw_hi, rw_lo, rb_pad, carry_in, tm):
    n = x2d.shape[0]
    row = lambda x: x.reshape(1, -1)
    cst = lambda shape: pl.BlockSpec(shape, lambda i: (0,) * len(shape))
    half = pl.BlockSpec((tm, RWKV_W), lambda i: (i, 0))
    full = pl.BlockSpec((tm, D_MODEL), lambda i: (i, 0))
    lanes = pl.BlockSpec((tm, ROUTE_LANES), lambda i: (i, 0))
    est = (2 * (4 * tm * RWKV_W * 4 + 3 * tm * D_MODEL * 4) + 2 * (D_MODEL * D_MODEL * 2 + RWKV_W * RWKV_W * 2)
           + 8 * tm * D_MODEL * 4 + 4 * tm * tm)
    return pl.pallas_call(
        _mix_out_kernel,
        grid=(n // tm,),
        in_specs=[half, half, half, half, full,
                  cst((1, RWKV_W)), cst((1, RWKV_W)), cst((GROUP_TILE, GROUP_TILE)), cst((D_MODEL, D_MODEL)),
                  cst((1, D_MODEL)), cst((D_MODEL, ROUTE_LANES)), cst((D_MODEL, ROUTE_LANES)),
                  cst((1, ROUTE_LANES)), cst((SUBLANES, ROUTE_LANES))],
        out_specs=[full, full, lanes, lanes, cst((SUBLANES, ROUTE_LANES))],
        out_shape=[jax.ShapeDtypeStruct((n, D_MODEL), F32), jax.ShapeDtypeStruct((n, D_MODEL), F32),
                   jax.ShapeDtypeStruct((n, ROUTE_LANES), I32), jax.ShapeDtypeStruct((n, ROUTE_LANES), F32),
                   jax.ShapeDtypeStruct((SUBLANES, ROUTE_LANES), F32)],
        scratch_shapes=[pltpu.VMEM((SUBLANES, ROUTE_LANES), F32)],
        compiler_params=_params(1, est),
        name="mix_out_router",
    )(y2d, bonus2d, g2d, po2d, x2d, row(lnx_g), row(lnx_b), ones_bd, w_out_bf, row(ln2_g),
      rw_hi, rw_lo, rb_pad, carry_in)


def _row_copy(src, src_row, dst, dst_row, sem):
    return pltpu.make_async_copy(src.at[pl.ds(src_row, 1), :], dst.at[pl.ds(dst_row, 1), :], sem)


def _dispatch_kernel(pad_start_ref, pad_len_ref, tail_ref, dest_ref, xa_ref, xb_ref, o_hbm, zbuf, sem, sem_fill,
                     *, n_main, n_last):
    i = pl.program_id(0)
    tm = xa_ref.shape[0]
    blk = zbuf.shape[0]

    @pl.when(i == 0)
    def _():
        zbuf[...] = jnp.zeros_like(zbuf)

        def per_expert(e, total):
            def body(r, carry):
                _row_copy(zbuf, 0, o_hbm, pad_start_ref[e] + r, sem_fill).start()
                return carry
            lax.fori_loop(0, pad_len_ref[e], body, 0)
            return total + pad_len_ref[e]

        n_pad_rows = lax.fori_loop(0, N_EXPERTS, per_expert, 0)

        def wait_row(r, carry):
            _row_copy(zbuf, 0, o_hbm, 0, sem_fill).wait()
            return carry

        lax.fori_loop(0, n_pad_rows, wait_row, 0)

        def fill_block(b, carry):
            start = pl.multiple_of(tail_ref[0] + b * blk, blk)
            pltpu.make_async_copy(zbuf, o_hbm.at[pl.ds(start, blk), :], sem_fill).start()
            return carry

        def wait_block(b, carry):
            pltpu.make_async_copy(zbuf, o_hbm.at[pl.ds(0, blk), :], sem_fill).wait()
            return carry

        lax.fori_loop(0, tail_ref[1], fill_block, 0)
        lax.fori_loop(0, tail_ref[1], wait_block, 0)

    def scatter_rows(x_ref, n_rows):
        def issue(n, carry):
            for j in range(TOP_K):
                _row_copy(x_ref, n, o_hbm, dest_ref[n * TOP_K + j], sem).start(priority=j % 2)
            return carry

        lax.fori_loop(0, n_rows, issue, 0)
        pltpu.make_async_copy(o_hbm.at[pl.ds(0, n_rows * TOP_K), :], o_hbm.at[pl.ds(0, n_rows * TOP_K), :], sem).wait()

    @pl.when(i < n_main)
    def _():
        scatter_rows(xa_ref, tm)

    @pl.when(i == n_main)
    def _():
        scatter_rows(xb_ref, n_last)


def _dispatch(pad_start, pad_len, tail, dest_all, xn_a, xn_b, n_slots, tm):
    n_main = xn_a.shape[0] // tm
    n_last = xn_b.shape[0]
    assert n_last <= tm and n_slots >= tm * TOP_K
    grid_spec = pltpu.PrefetchScalarGridSpec(
        num_scalar_prefetch=3,
        grid=(n_main + 1,),
        in_specs=[pl.BlockSpec((tm * TOP_K,), lambda i, *_: (i,), memory_space=pltpu.SMEM),
                  pl.BlockSpec((tm, D_MODEL), lambda i, *_: (jnp.minimum(i, n_main - 1), 0)),
                  pl.BlockSpec((n_last, D_MODEL), lambda i, *_: (0, 0))],
        out_specs=pl.BlockSpec(memory_space=pl.ANY),
        scratch_shapes=[pltpu.VMEM((EXPERT_BLOCK, D_MODEL), F32), pltpu.SemaphoreType.DMA(()),
                        pltpu.SemaphoreType.DMA(())],
    )
    return pl.pallas_call(
        functools.partial(_dispatch_kernel, n_main=n_main, n_last=n_last),
        grid_spec=grid_spec,
        out_shape=jax.ShapeDtypeStruct((n_slots, D_MODEL), F32),
        compiler_params=_params(1, 2 * (tm + n_last) * D_MODEL * 4 + EXPERT_BLOCK * D_MODEL * 4),
        name="dispatch",
    )(pad_start, pad_len, tail, dest_all, xn_a, xn_b)


def _expert_kernel(be_ref, nu_ref, xs_ref, wgu_ref, bgu_ref, wdn_ref, bdn_ref, o_ref, wgu_bf, wdn_bf):
    i = pl.program_id(0)
    used = i < nu_ref[0]
    changed = jnp.logical_or(i == 0, be_ref[i] != be_ref[jnp.maximum(i - 1, 0)])

    @pl.when(jnp.logical_and(used, changed))
    def _():
        wgu_bf[...] = wgu_ref[0].astype(BF16)
        wdn_bf[...] = wdn_ref[0].astype(BF16)

    @pl.when(used)
    def _():
        gu = _dot(xs_ref[...].astype(BF16), wgu_bf[...]) + bgu_ref[0]
        gate = jnp.minimum(gu[:, :D_EXPERT], SWIGLU_LIMIT)
        up = jnp.clip(gu[:, D_EXPERT:], -SWIGLU_LIMIT, SWIGLU_LIMIT)
        hmid = (up + 1.0) * (gate * jax.nn.sigmoid(SWIGLU_ALPHA * gate))
        o_ref[...] = _dot(hmid.astype(BF16), wdn_bf[...]) + bdn_ref[0]

    @pl.when(jnp.logical_not(used))
    def _():
        o_ref[...] = jnp.zeros_like(o_ref)


def _experts(block_e, n_used, xs, w_gu, b_gu, w_down, b_down):
    n_slots = xs.shape[0]
    nb = n_slots // EXPERT_BLOCK
    blk = EXPERT_BLOCK
    est = (2 * 2 * blk * D_MODEL * 4 + 2 * (D_MODEL * 2 * D_EXPERT + D_EXPERT * D_MODEL) * 4
           + (D_MODEL * 2 * D_EXPERT + D_EXPERT * D_MODEL) * 2 + 4 * blk * 2 * D_EXPERT * 4)
    grid_spec = pltpu.PrefetchScalarGridSpec(
        num_scalar_prefetch=2,
        grid=(nb,),
        in_specs=[pl.BlockSpec((blk, D_MODEL), lambda i, be, nu: (i, 0)),
                  pl.BlockSpec((1, D_MODEL, 2 * D_EXPERT), lambda i, be, nu: (be[i], 0, 0)),
                  pl.BlockSpec((1, 1, 2 * D_EXPERT), lambda i, be, nu: (be[i], 0, 0)),
                  pl.BlockSpec((1, D_EXPERT, D_MODEL), lambda i, be, nu: (be[i], 0, 0)),
                  pl.BlockSpec((1, 1, D_MODEL), lambda i, be, nu: (be[i], 0, 0))],
        out_specs=pl.BlockSpec((blk, D_MODEL), lambda i, be, nu: (i, 0)),
        scratch_shapes=[pltpu.VMEM((D_MODEL, 2 * D_EXPERT), BF16), pltpu.VMEM((D_EXPERT, D_MODEL), BF16)],
    )
    return pl.pallas_call(
        _expert_kernel,
        grid_spec=grid_spec,
        out_shape=jax.ShapeDtypeStruct((n_slots, D_MODEL), F32),
        compiler_params=_params(1, est),
        name="experts",
    )(block_e, n_used, xs, w_gu, b_gu.reshape(N_EXPERTS, 1, 2 * D_EXPERT), w_down,
      b_down.reshape(N_EXPERTS, 1, D_MODEL))


def _combine_kernel(dest_ref, dest_next_ref, ys_hbm, gate_ref, h_ref, p_ref, ln3_ref, pg_ref, pp_ref, fg_ref,
                    o_ref, gbuf, sem):
    i = pl.program_id(0)
    tm = h_ref.shape[0]
    slot = lax.rem(i, 2)

    def issue(d_ref, s):
        def body(n, carry):
            for j in range(TOP_K):
                _row_copy(ys_hbm, d_ref[n * TOP_K + j], gbuf.at[s, j], n, sem.at[s]).start(priority=j % 2)
            return carry
        lax.fori_loop(0, tm, body, 0)

    @pl.when(i == 0)
    def _():
        issue(dest_ref, 0)

    @pl.when(i + 1 < pl.num_programs(0))
    def _():
        issue(dest_next_ref, 1 - slot)

    for j in range(TOP_K):
        pltpu.make_async_copy(gbuf.at[slot, j], gbuf.at[slot, j], sem.at[slot]).wait()

    gates = gate_ref[...]
    moe = gates[:, 0:1] * gbuf[slot, 0]
    for j in range(1, TOP_K):
        moe = moe + gates[:, j:j + 1] * gbuf[slot, j]
    h = h_ref[...] + moe
    gate = jax.nn.sigmoid(_dot(_rmsnorm_rows(h, ln3_ref[...]).astype(BF16), pg_ref[...]))
    h = h + gate * _dot(p_ref[...].astype(BF16), pp_ref[...])
    o_ref[...] = _rmsnorm_rows(h, fg_ref[...])


def _combine(dest_flat, ys, gates, h2d, p2d, ln3_g, ple_gate_bf, ple_proj_bf, final_g, tm):
    n = h2d.shape[0]
    nb = n // tm
    row = lambda x: x.reshape(1, -1)
    cst = lambda shape: pl.BlockSpec(shape, lambda i: (0,) * len(shape))
    est = (2 * TOP_K * tm * D_MODEL * 4 + 2 * (2 * tm * D_MODEL * 4 + tm * PLE_DIM * 4 + tm * ROUTE_LANES * 4)
           + 2 * (D_MODEL * D_MODEL * 2 + PLE_DIM * D_MODEL * 2) + 6 * tm * D_MODEL * 4)
    return pl.pallas_call(
        _combine_kernel,
        grid=(nb,),
        in_specs=[pl.BlockSpec((tm * TOP_K,), lambda i: (i,), memory_space=pltpu.SMEM),
                  pl.BlockSpec((tm * TOP_K,), lambda i: (jnp.minimum(i + 1, nb - 1),), memory_space=pltpu.SMEM),
                  pl.BlockSpec(memory_space=pl.ANY),
                  pl.BlockSpec((tm, ROUTE_LANES), lambda i: (i, 0)),
                  pl.BlockSpec((tm, D_MODEL), lambda i: (i, 0)),
                  pl.BlockSpec((tm, PLE_DIM), lambda i: (i, 0)),
                  cst((1, D_MODEL)), cst((D_MODEL, D_MODEL)), cst((PLE_DIM, D_MODEL)), cst((1, D_MODEL))],
        out_specs=pl.BlockSpec((tm, D_MODEL), lambda i: (i, 0)),
        out_shape=jax.ShapeDtypeStruct((n, D_MODEL), F32),
        scratch_shapes=[pltpu.VMEM((2, TOP_K, tm, D_MODEL), F32), pltpu.SemaphoreType.DMA((2,))],
        compiler_params=_params(1, est),
        name="combine_ple_norm",
    )(dest_flat, dest_flat, ys, gates, h2d, p2d, row(ln3_g), ple_gate_bf, ple_proj_bf, row(final_g))


def _tile(n, pref):
    t = min(n, pref)
    assert n % t == 0, (n, pref)
    return t


def _front(x, pos0, pool_ctx, shift_prev, wkv0, wts, carry_in, tm_mix):
    b, t, _ = x.shape
    n = b * t
    assert t >= POOL_CTX and t % SUBLANES == 0
    x2d = x.reshape(n, D_MODEL)
    u2d, zr2d = _in_proj(x2d, wts["ln1_g"], wts["w_in_bf"], _tile(n, 512))
    u = u2d.reshape(b, t, POOL_W)
    zr = zr2d.reshape(b, t, RWKV_PROJ)

    ctx16 = jnp.concatenate([jnp.zeros((b, POOL_HALO - POOL_CTX, POOL_W), F32), pool_ctx.astype(F32)], axis=1)
    pool_out = _pool_mixer(u, ctx16, wts["w_pool_bf"], wts["pool_scale"], pos0, _tile(t, 512))
    new_pool = u[:, -POOL_CTX:]

    sh8 = jnp.broadcast_to(shift_prev.astype(F32), (b, SUBLANES, RWKV_PROJ))
    r, lw, kx, v, na, kb, g, bonus = _rwkv_pre(
        zr, sh8, wts["mu_shift"], wts["w0"], wts["w2_bf"], wts["a0"], wts["a2_bf"], wts["g2_bf"],
        wts["k_k"], wts["k_a"], wts["r_k"], wts["ones_bd"], _tile(t, 256))
    new_shift = zr[:, -1:]

    t_pad = -(-t // CHUNK) * CHUNK
    seqs = (r, lw, kx, v, na, kb)
    if t_pad != t:
        seqs = tuple(jnp.pad(s, ((0, 0), (0, t_pad - t), (0, 0))) for s in seqs)
    y, s_fin = _wkv_scan(*seqs, _state_to_pairs(wkv0), _tile(t_pad, 4 * CHUNK))
    y = y[:, :t]
    new_wkv = _pairs_to_state(s_fin)

    h, xn, route, gates, counts = _mix_out(
        y.reshape(n, RWKV_W), bonus.reshape(n, RWKV_W), g.reshape(n, RWKV_W), pool_out.reshape(n, POOL_W),
        x2d, wts["lnx_g"], wts["lnx_b"], wts["ones_bd"], wts["w_out_bf"], wts["ln2_g"],
        wts["rw_hi"], wts["rw_lo"], wts["rb_pad"], carry_in, tm_mix)
    return dict(h=h, xn=xn, route=route, gates=gates, counts=counts,
                new_pool=new_pool, new_shift=new_shift, new_wkv=new_wkv)


def _prep_weights(i, ln1_g, w_in, mu_shift, w_pool, pool_scale, w0, w2, a0, a2, g2, k_k, k_a, r_k,
                  lnx_g, lnx_b, w_out, ln2_g, router_w, router_b, ln3_g, ple_gate, ple_proj):
    head = jnp.arange(GROUP_TILE) // HEAD_DIM
    ones_bd = (head[:, None] == head[None, :]).astype(BF16)
    rw_pad = jnp.pad(router_w[i].astype(F32), ((0, 0), (0, ROUTE_LANES - N_EXPERTS)))
    rw_hi = rw_pad.astype(BF16)
    rw_lo = (rw_pad - rw_hi.astype(F32)).astype(BF16)
    rb_pad = jnp.concatenate([router_b[i].astype(F32), jnp.full((ROUTE_LANES - N_EXPERTS,), -jnp.inf, F32)])
    return dict(
        ln1_g=ln1_g[i], w_in_bf=w_in[i].astype(BF16), mu_shift=mu_shift[i], w_pool_bf=w_pool[i].astype(BF16),
        pool_scale=pool_scale[i], w0=w0[i], w2_bf=w2[i].astype(BF16), a0=a0[i], a2_bf=a2[i].astype(BF16),
        g2_bf=g2[i].astype(BF16), k_k=k_k[i], k_a=k_a[i], r_k=r_k[i].reshape(RWKV_W), lnx_g=lnx_g[i],
        lnx_b=lnx_b[i], w_out_bf=w_out[i].astype(BF16), ln2_g=ln2_g[i], rw_hi=rw_hi, rw_lo=rw_lo,
        rb_pad=rb_pad.reshape(1, ROUTE_LANES), ones_bd=ones_bd, ln3_g=ln3_g[i],
        ple_gate_bf=ple_gate[i].astype(BF16), ple_proj_bf=ple_proj[i].astype(BF16))


def _layer_pair(hp, hs, pp, ps, pool_ctx, shift_prev, wkv0, wts, w_gu, b_gu, w_down, b_down, final_g):
    bp, tp, _ = hp.shape
    bs, ts, _ = hs.shape
    np_, ns = bp * tp, bs * ts
    tm_p = _tile(np_, 256)
    tm_s = _tile(ns, 128)
    zero_carry = jnp.zeros((SUBLANES, ROUTE_LANES), F32)
    fp = _front(hp, 0, jnp.zeros((bp, POOL_CTX, POOL_W), F32), jnp.zeros((bp, 1, RWKV_PROJ), F32),
                jnp.zeros((bp, N_HEADS, HEAD_DIM, HEAD_DIM), F32), wts, zero_carry, tm_p)
    fs = _front(hs, PAST_LEN, pool_ctx, shift_prev, wkv0, wts, fp["counts"], tm_s)

    counts = fs["counts"][0, :N_EXPERTS].astype(I32)
    padded = (counts + EXPERT_BLOCK - 1) // EXPERT_BLOCK * EXPERT_BLOCK
    pad_end = jnp.cumsum(padded)
    start_pad = pad_end - padded
    n_blocks = -(-(np_ + ns) * TOP_K // EXPERT_BLOCK) + N_EXPERTS
    n_slots = n_blocks * EXPERT_BLOCK
    n_used = (pad_end[-1] // EXPERT_BLOCK).astype(I32)
    block_start = jnp.arange(n_blocks, dtype=I32) * EXPERT_BLOCK
    block_e = jnp.minimum(jnp.sum((block_start[:, None] >= pad_end[None, :]).astype(I32), axis=1), N_EXPERTS - 1)
    last_e = block_e[jnp.maximum(n_used - 1, 0)]
    block_e = jnp.where(jnp.arange(n_blocks) < n_used, block_e, last_e)

    def dest_of(route):
        idx = route[:, :TOP_K]
        rank = route[:, TOP_K:2 * TOP_K]
        return (start_pad[idx] + rank).reshape(-1).astype(I32)

    dest_p, dest_s = dest_of(fp["route"]), dest_of(fs["route"])
    tm_d = _tile(np_, 512)
    dest_all = jnp.concatenate([dest_p, dest_s, jnp.zeros(((tm_d - ns) * TOP_K,), I32)])
    tail = jnp.stack([n_used * EXPERT_BLOCK, n_blocks - n_used]).astype(I32)
    xs = _dispatch(start_pad + counts, padded - counts, tail, dest_all, fp["xn"], fs["xn"], n_slots, tm_d)
    ys = _experts(block_e, n_used.reshape(1), xs, w_gu, b_gu, w_down, b_down)

    outs = []
    for f, dest, p, tm in ((fp, dest_p, pp, tm_p), (fs, dest_s, ps, tm_s)):
        n = f["h"].shape[0]
        outs.append(_combine(dest, ys, f["gates"], f["h"], p.reshape(n, PLE_DIM).astype(F32), wts["ln3_g"],
                             wts["ple_gate_bf"], wts["ple_proj_bf"], final_g, tm))
    return outs[0].reshape(hp.shape), outs[1].reshape(hs.shape), fp, fs


def kernel(x_prompt, x_sample, p_prompt, p_sample, cache_pool, state_shift, state_wkv, ln1_g, w_in, mu_shift, w_pool, pool_scale, w0, w2, a0, a2, g2, k_k, k_a, r_k, lnx_g, lnx_b, w_out, ln2_g, router_w, router_b, w_gu, b_gu, w_down, b_down, ln3_g, ple_gate, ple_proj, final_g):
    assert ln1_g.shape[0] == 1, "single-layer kernel"
    i = 0
    wts = _prep_weights(i, ln1_g, w_in, mu_shift, w_pool, pool_scale, w0, w2, a0, a2, g2, k_k, k_a, r_k,
                        lnx_g, lnx_b, w_out, ln2_g, router_w, router_b, ln3_g, ple_gate, ple_proj)
    y_p, y_s, fp, fs = _layer_pair(x_prompt, x_sample, p_prompt[i], p_sample[i], cache_pool[i], state_shift[i],
                                   state_wkv[i], wts, w_gu[i], b_gu[i], w_down[i], b_down[i], final_g)
    stack = lambda a: a[None]
    return (y_p, y_s,
            stack(fp["new_pool"]), stack(fp["new_shift"]), stack(fp["new_wkv"]),
            stack(fs["new_pool"]), stack(fs["new_shift"]), stack(fs["new_wkv"]))
```

```python
import functools

import jax
import jax.numpy as jnp
from jax import lax
from jax.experimental import pallas as pl
from jax.experimental.pallas import tpu as pltpu

F32 = jnp.float32
BF16 = jnp.bfloat16
I32 = jnp.int32

D_MODEL = 1024
POOL_W = 512
POOL_WINDOWS = (2, 4, 8, 16)
POOL_GW = 128
POOL_CTX = 15
POOL_HALO = 16
RWKV_W = 512
HEAD_DIM = 64
N_HEADS = 8
N_PAIRS = N_HEADS // 2
PAIR_W = 2 * HEAD_DIM
DECAY_LORA = 64
ICLR_LORA = 64
GATE_LORA = 128
RWKV_PROJ = 3 * RWKV_W + DECAY_LORA + ICLR_LORA + GATE_LORA
IN_PROJ = POOL_W + RWKV_PROJ
LNX_EPS = 64e-5
N_EXPERTS = 32
TOP_K = 4
D_EXPERT = 1024
SWIGLU_LIMIT = 7.0
SWIGLU_ALPHA = 1.702
PLE_DIM = 256
RMS_EPS = 1e-6
PAST_LEN = 2048

CHUNK = 64
EXPERT_BLOCK = 512
ROUTE_LANES = 128
SUBLANES = 8
GROUP_TILE = 256
EXP_M05 = 0.6065306597126334
V7X_VMEM_BYTES = 64 * 1024 * 1024

NT_DIMS = (((1,), (1,)), ((), ()))
TN_DIMS = (((0,), (0,)), ((), ()))


def _vmem_limit(est_bytes):
    return int(min(est_bytes * 3 // 2 + (4 << 20), V7X_VMEM_BYTES - (8 << 20)))


def _params(n_axes, est_bytes):
    return pltpu.CompilerParams(dimension_semantics=("arbitrary",) * n_axes,
                                vmem_limit_bytes=_vmem_limit(est_bytes))


def _dot(a, b):
    return jnp.dot(a, b, preferred_element_type=F32)


def _split2(x):
    hi = x.astype(BF16)
    lo = (x - hi.astype(F32)).astype(BF16)
    return hi, lo


def _split3(x):
    hi = x.astype(BF16)
    r1 = x - hi.astype(F32)
    mid = r1.astype(BF16)
    lo = (r1 - mid.astype(F32)).astype(BF16)
    return hi, mid, lo


def _group_sum(x, ones_bd):
    hi, lo = _split2(x)
    slabs = [_dot(hi[:, s:s + GROUP_TILE], ones_bd) + _dot(lo[:, s:s + GROUP_TILE], ones_bd)
             for s in range(0, x.shape[1], GROUP_TILE)]
    return jnp.concatenate(slabs, axis=1)


def _rmsnorm_rows(x, g):
    ms = jnp.mean(x * x, axis=-1, keepdims=True)
    return (x * lax.rsqrt(ms + RMS_EPS)) * g


def _in_proj_kernel(x_ref, g_ref, w_ref, u_ref, zr_ref):
    xn = _rmsnorm_rows(x_ref[...], g_ref[...])
    z = _dot(xn.astype(BF16), w_ref[...])
    u_ref[...] = z[:, :POOL_W]
    zr_ref[...] = z[:, POOL_W:]


def _in_proj(x2d, ln1_g, w_in_bf, tm):
    n = x2d.shape[0]
    est = 2 * tm * D_MODEL * 4 + 2 * D_MODEL * IN_PROJ * 2 + 3 * tm * IN_PROJ * 4
    return pl.pallas_call(
        _in_proj_kernel,
        grid=(n // tm,),
        in_specs=[pl.BlockSpec((tm, D_MODEL), lambda i: (i, 0)),
                  pl.BlockSpec((1, D_MODEL), lambda i: (0, 0)),
                  pl.BlockSpec((D_MODEL, IN_PROJ), lambda i: (0, 0))],
        out_specs=[pl.BlockSpec((tm, POOL_W), lambda i: (i, 0)),
                   pl.BlockSpec((tm, RWKV_PROJ), lambda i: (i, 0))],
        out_shape=[jax.ShapeDtypeStruct((n, POOL_W), F32),
                   jax.ShapeDtypeStruct((n, RWKV_PROJ), F32)],
        compiler_params=_params(1, est),
        name="in_proj",
    )(x2d, ln1_g.reshape(1, D_MODEL), w_in_bf)


def _pool_kernel(u_ref, halo_ref, ctx_ref, wp_ref, ps_ref, o_ref, buf, *, pos0):
    t = pl.program_id(1)
    tt = u_ref.shape[1]
    buf[0:POOL_HALO, :] = jnp.where(t == 0, ctx_ref[0], halo_ref[0])
    buf[POOL_HALO:, :] = u_ref[0]
    pos = lax.broadcasted_iota(I32, (tt, 1), 0) + (t * tt + pos0)
    outs = []
    for gi, w in enumerate(POOL_WINDOWS):
        sl = slice(gi * POOL_GW, (gi + 1) * POOL_GW)
        cur = buf[POOL_HALO:POOL_HALO + tt, sl]
        acc = cur
        for j in range(1, w):
            acc = acc + buf[POOL_HALO - j:POOL_HALO - j + tt, sl]
        cnt = jnp.minimum(w, pos + 1).astype(F32)
        pooled = acc / cnt - cur
        outs.append(_dot(pooled.astype(BF16), wp_ref[gi]))
    o_ref[0] = (jnp.concatenate(outs, axis=-1) * ps_ref[...]).astype(o_ref.dtype)


def _pool_mixer(u, ctx16, w_pool_bf, pool_scale, pos0, tt):
    b, t, _ = u.shape
    hb = tt // POOL_HALO
    est = 2 * tt * POOL_W * 4 * 3 + (tt + POOL_HALO) * POOL_W * 4
    return pl.pallas_call(
        functools.partial(_pool_kernel, pos0=pos0),
        grid=(b, t // tt),
        in_specs=[pl.BlockSpec((1, tt, POOL_W), lambda i, j: (i, j, 0)),
                  pl.BlockSpec((1, POOL_HALO, POOL_W), lambda i, j: (i, jnp.maximum(j * hb - 1, 0), 0)),
                  pl.BlockSpec((1, POOL_HALO, POOL_W), lambda i, j: (i, 0, 0)),
                  pl.BlockSpec((len(POOL_WINDOWS), POOL_GW, POOL_GW), lambda i, j: (0, 0, 0)),
                  pl.BlockSpec((1, POOL_W), lambda i, j: (0, 0))],
        out_specs=pl.BlockSpec((1, tt, POOL_W), lambda i, j: (i, j, 0)),
        out_shape=jax.ShapeDtypeStruct((b, t, POOL_W), BF16),
        scratch_shapes=[pltpu.VMEM((tt + POOL_HALO, POOL_W), F32)],
        compiler_params=_params(2, est),
        name="pool_mixer",
    )(u, u, ctx16, w_pool_bf, pool_scale.reshape(1, POOL_W))


def _rwkv_pre_kernel(zr_ref, halo_ref, sh_ref, mu_ref, w0_ref, w2_ref, a0_ref, a2_ref, g2_ref,
                     kk_ref, ka_ref, rk_ref, ones_ref,
                     r_o, lw_o, kx_o, v_o, na_o, kb_o, g_o, bonus_o, buf):
    t = pl.program_id(1)
    tt = zr_ref.shape[1]
    buf[0:SUBLANES, :] = jnp.where(t == 0, sh_ref[0], halo_ref[0])
    zr = zr_ref[0]
    buf[SUBLANES:, :] = zr
    prev = buf[SUBLANES - 1:SUBLANES - 1 + tt, :]
    zs = zr + (prev - zr) * mu_ref[...]
    o1, o2, o3 = RWKV_W, 2 * RWKV_W, 3 * RWKV_W
    o4, o5 = o3 + DECAY_LORA, o3 + DECAY_LORA + ICLR_LORA
    r, k, v = zs[:, :o1], zs[:, o1:o2], zs[:, o2:o3]
    wd, ad, gd = zs[:, o3:o4], zs[:, o4:o5], zs[:, o5:]
    ones_bd = ones_ref[...]
    wz = w0_ref[...] + _dot(jnp.tanh(wd).astype(BF16), w2_ref[...])
    lw = -EXP_M05 * jax.nn.sigmoid(wz)
    a = jax.nn.sigmoid(a0_ref[...] + _dot(ad.astype(BF16), a2_ref[...]))
    g = _dot(jax.nn.sigmoid(gd).astype(BF16), g2_ref[...])
    kk = k * kk_ref[...]
    kk = kk / jnp.maximum(jnp.sqrt(_group_sum(kk * kk, ones_bd)), 1e-12)
    kx = k * (1.0 + (a - 1.0) * ka_ref[...])
    bonus = _group_sum(r * kx * rk_ref[...], ones_bd) * v
    r_o[0] = r
    lw_o[0] = lw
    kx_o[0] = kx
    v_o[0] = v
    na_o[0] = -kk
    kb_o[0] = kk * a
    g_o[0] = g
    bonus_o[0] = bonus


def _rwkv_pre(zr, sh8, mu, w0, w2_bf, a0, a2_bf, g2_bf, k_k, k_a, r_k, ones_bd, tt):
    b, t, _ = zr.shape
    hb = tt // SUBLANES
    row = lambda x: x.reshape(1, -1)
    cst = lambda shape: pl.BlockSpec(shape, lambda i, j: (0,) * len(shape))
    seq = pl.BlockSpec((1, tt, RWKV_W), lambda i, j: (i, j, 0))
    est = 3 * tt * RWKV_PROJ * 4 + 2 * 8 * tt * RWKV_W * 4 + 12 * tt * RWKV_W * 4
    return pl.pallas_call(
        _rwkv_pre_kernel,
        grid=(b, t // tt),
        in_specs=[pl.BlockSpec((1, tt, RWKV_PROJ), lambda i, j: (i, j, 0)),
                  pl.BlockSpec((1, SUBLANES, RWKV_PROJ), lambda i, j: (i, jnp.maximum(j * hb - 1, 0), 0)),
                  pl.BlockSpec((1, SUBLANES, RWKV_PROJ), lambda i, j: (i, 0, 0)),
                  cst((1, RWKV_PROJ)), cst((1, RWKV_W)), cst((DECAY_LORA, RWKV_W)),
                  cst((1, RWKV_W)), cst((ICLR_LORA, RWKV_W)), cst((GATE_LORA, RWKV_W)),
                  cst((1, RWKV_W)), cst((1, RWKV_W)), cst((1, RWKV_W)), cst((GROUP_TILE, GROUP_TILE))],
        out_specs=[seq] * 8,
        out_shape=[jax.ShapeDtypeStruct((b, t, RWKV_W), F32)] * 8,
        scratch_shapes=[pltpu.VMEM((tt + SUBLANES, RWKV_PROJ), F32)],
        compiler_params=_params(2, est),
        name="rwkv_pre",
    )(zr, zr, sh8, row(mu), row(w0), w2_bf, row(a0), a2_bf, g2_bf, row(k_k), row(k_a), row(r_k), ones_bd)


def _pair_blockdiag(z, left):
    return jnp.concatenate([jnp.where(left, z, 0.0), jnp.where(left, 0.0, z)], axis=0)


def _wkv_kernel(r_ref, lw_ref, kx_ref, v_ref, na_ref, kb_ref, s0_ref, y_ref, sf_ref, s_scr):
    t = pl.program_id(1)
    tt = r_ref.shape[1]
    c = CHUNK

    @pl.when(t == 0)
    def _():
        s_scr[...] = s0_ref[0]

    lane = lax.broadcasted_iota(I32, (c, 2 * c), 1)
    rowi = lax.broadcasted_iota(I32, (c, 2 * c), 0)
    jm = lane & (c - 1)
    strict = jm < rowi
    incl = jm <= rowi
    eye_pair = jnp.where(jm == rowi, 1.0, 0.0).astype(F32)
    left = lax.broadcasted_iota(I32, (c, PAIR_W), 1) < HEAD_DIM
    rr = lax.broadcasted_iota(I32, (c, c), 0)
    cc = lax.broadcasted_iota(I32, (c, c), 1)
    tri_incl_bf = jnp.where(cc <= rr, 1.0, 0.0).astype(BF16)
    br = lax.broadcasted_iota(I32, (PAIR_W, PAIR_W), 0) < HEAD_DIM
    bc = lax.broadcasted_iota(I32, (PAIR_W, PAIR_W), 1) < HEAD_DIM
    bd_mask = br == bc
    bf = lambda x: x.astype(BF16)
    bd = lambda z: _pair_blockdiag(z, left)
    n_chunks = tt // c

    cums = []
    for ci in range(n_chunks):
        h3 = _split3(lw_ref[0, ci * c:(ci + 1) * c, :])
        cums.append(_dot(tri_incl_bf, h3[0]) + _dot(tri_incl_bf, h3[1]) + _dot(tri_incl_bf, h3[2]))

    chains = []
    for ci in range(n_chunks):
        rows = slice(ci * c, (ci + 1) * c)
        for p in range(N_PAIRS):
            cols = slice(p * PAIR_W, (p + 1) * PAIR_W)
            cum = cums[ci][:, cols]
            lw, kx, kb = lw_ref[0, rows, cols], kx_ref[0, rows, cols], kb_ref[0, rows, cols]
            tot = cum[c - 1:c, :]
            p_end = jnp.exp(tot - cum)
            inv_p = jnp.exp(-cum)
            chains.append(dict(
                ci=ci, p=p, v=v_ref[0, rows, cols], p_c=jnp.exp(tot),
                a_t=na_ref[0, rows, cols] * jnp.exp(cum - lw), r_t=r_ref[0, rows, cols] * jnp.exp(cum),
                b_t=kb * inv_p, k_t=kx * inv_p, b_h=kb * p_end, k_h=kx * p_end))

    for d in chains:
        lhs = bf(jnp.concatenate([d["a_t"], d["r_t"]], axis=0))
        rhs = bf(jnp.concatenate([bd(d["b_t"]), bd(d["k_t"])], axis=0))
        sc = lax.dot_general(lhs, rhs, NT_DIMS, preferred_element_type=F32)
        d["a_ab"] = jnp.where(strict, sc[:c, :2 * c], 0.0)
        d["a_k"] = jnp.concatenate([jnp.where(strict, sc[:c, 2 * c:], 0.0),
                                    jnp.where(incl, sc[c:, 2 * c:], 0.0)], axis=0)
        d["a_rb"] = jnp.where(incl, sc[c:, :2 * c], 0.0)

    for d in chains:
        d["x"] = _dot(bf(d["a_ab"]), bf(bd(d["a_ab"])))
        d["t"] = eye_pair + d["a_ab"]
    for _ in range(c.bit_length() - 3):
        for d in chains:
            st = _dot(bf(jnp.concatenate([d["x"], d["t"]], axis=0)), bf(bd(d["x"])))
            d["x"] = st[:c]
            d["t"] = d["t"] + st[c:]
    for d in chains:
        d["t"] = d["t"] + _dot(bf(d["t"]), bf(bd(d["x"])))

    for d in chains:
        wv = _dot(bf(d["a_k"]), bf(bd(d["v"])))
        d["w1"], d["rkv"] = wv[:c], wv[c:]
    for d in chains:
        tu = _dot(bf(d["t"]), bf(jnp.concatenate([bd(d["w1"]), bd(d["a_t"])], axis=1)))
        d["u_loc"], d["a_tt"] = tu[:, :PAIR_W], tu[:, PAIR_W:]
    for d in chains:
        ar = _dot(bf(d["a_rb"]), bf(jnp.concatenate([bd(d["u_loc"]), bd(d["a_tt"])], axis=1)))
        d["y_loc"] = ar[:, :PAIR_W] + d["rkv"]
        d["r_g"] = d["r_t"] + ar[:, PAIR_W:]
    for d in chains:
        lhs = jnp.concatenate([jnp.concatenate([d["v"], jnp.zeros_like(d["v"])], axis=1),
                               jnp.concatenate([d["u_loc"], d["a_tt"]], axis=1)], axis=0)
        rhs = jnp.concatenate([d["k_h"], d["b_h"]], axis=0)
        dp = lax.dot_general(bf(lhs), bf(rhs), TN_DIMS, preferred_element_type=F32)
        d["d_loc"] = jnp.where(bd_mask, dp[:PAIR_W], 0.0)
        d["phi"] = bf(jnp.where(bd_mask, dp[PAIR_W:], 0.0))

    state = [s_scr[p] for p in range(N_PAIRS)]
    for d in chains:
        p = d["p"]
        s_hi, s_lo = _split2(state[p])
        y = lax.dot_general(bf(d["r_g"]), s_hi, NT_DIMS, preferred_element_type=F32) + d["y_loc"]
        y_ref[0, d["ci"] * c:(d["ci"] + 1) * c, p * PAIR_W:(p + 1) * PAIR_W] = y
        state[p] = state[p] * d["p_c"] + (_dot(s_hi, d["phi"]) + _dot(s_lo, d["phi"])) + d["d_loc"]
    for p in range(N_PAIRS):
        s_scr[p] = state[p]

    @pl.when(t == pl.num_programs(1) - 1)
    def _():
        sf_ref[0] = s_scr[...]


def _wkv_scan(r, lw, kx, v, na, kb, s0_pair, tt):
    b, t, _ = r.shape
    assert CHUNK == HEAD_DIM and tt % CHUNK == 0
    seq = pl.BlockSpec((1, tt, RWKV_W), lambda i, j: (i, j, 0))
    st = pl.BlockSpec((1, N_PAIRS, PAIR_W, PAIR_W), lambda i, j: (i, 0, 0, 0))
    est = 2 * 7 * tt * RWKV_W * 4 + 5 * N_PAIRS * PAIR_W * PAIR_W * 4 + (8 << 20)
    return pl.pallas_call(
        _wkv_kernel,
        grid=(b, t // tt),
        in_specs=[seq] * 6 + [st],
        out_specs=[seq, st],
        out_shape=[jax.ShapeDtypeStruct((b, t, RWKV_W), F32),
                   jax.ShapeDtypeStruct((b, N_PAIRS, PAIR_W, PAIR_W), F32)],
        scratch_shapes=[pltpu.VMEM((N_PAIRS, PAIR_W, PAIR_W), F32)],
        compiler_params=_params(2, est),
        name="wkv_scan",
    )(r, lw, kx, v, na, kb, s0_pair)


def _state_to_pairs(s):
    b = s.shape[0]
    s = s.astype(F32).reshape(b, N_PAIRS, 2, HEAD_DIM, HEAD_DIM)
    z = jnp.zeros((b, N_PAIRS, HEAD_DIM, HEAD_DIM), F32)
    return jnp.concatenate([jnp.concatenate([s[:, :, 0], z], axis=-1),
                            jnp.concatenate([z, s[:, :, 1]], axis=-1)], axis=-2)


def _pairs_to_state(sp):
    b = sp.shape[0]
    s = jnp.stack([sp[:, :, :HEAD_DIM, :HEAD_DIM], sp[:, :, HEAD_DIM:, HEAD_DIM:]], axis=2)
    return s.reshape(b, N_HEADS, HEAD_DIM, HEAD_DIM)


def _mix_out_kernel(y_ref, bonus_ref, g_ref, po_ref, x_ref, lg_ref, lb_ref, ones_ref, wo_ref, ln2_ref,
                    rw_hi_ref, rw_lo_ref, rb_ref, cin_ref,
                    h_o, xn_o, route_o, gate_o, cnt_o, carry):
    i = pl.program_id(0)
    tm = y_ref.shape[0]

    @pl.when(i == 0)
    def _():
        carry[...] = cin_ref[...]

    ones_bd = ones_ref[...]
    y = y_ref[...]
    inv_n = 1.0 / HEAD_DIM
    mu = _group_sum(y, ones_bd) * inv_n
    dlt = y - mu
    var = _group_sum(dlt * dlt, ones_bd) * inv_n
    yn = dlt * lax.rsqrt(var + LNX_EPS) * lg_ref[...] + lb_ref[...]
    rw_out = (yn + bonus_ref[...]) * g_ref[...]
    mix = _dot(po_ref[...], wo_ref[:POOL_W, :]) + _dot(rw_out.astype(BF16), wo_ref[POOL_W:, :])
    h = x_ref[...] + mix
    h_o[...] = h
    xn = _rmsnorm_rows(h, ln2_ref[...])
    xn_o[...] = xn

    x_hi, x_lo = _split2(xn)
    logits = (_dot(x_hi, rw_hi_ref[...]) + _dot(x_hi, rw_lo_ref[...]) + _dot(x_lo, rw_hi_ref[...])
              + rb_ref[...])
    lane = lax.broadcasted_iota(I32, (tm, ROUTE_LANES), 1)
    lane_f = lane.astype(F32)
    vals, idxs, hots = [], [], []
    work = logits
    for _ in range(TOP_K):
        m = jnp.max(work, axis=-1, keepdims=True)
        idx = jnp.min(jnp.where(work == m, lane_f, float(ROUTE_LANES)), axis=-1, keepdims=True)
        hit = lane_f == idx
        vals.append(m)
        idxs.append(idx)
        hots.append(jnp.where(hit, 1.0, 0.0).astype(F32))
        work = jnp.where(hit, -jnp.inf, work)
    exps = [jnp.exp(vv - vals[0]) for vv in vals]
    den = exps[0] + exps[1] + exps[2] + exps[3]

    hot_all = hots[0] + hots[1] + hots[2] + hots[3]
    rr = lax.broadcasted_iota(I32, (tm, tm), 0)
    cc = lax.broadcasted_iota(I32, (tm, tm), 1)
    tri_strict = jnp.where(cc < rr, 1.0, 0.0).astype(BF16)
    prefix = _dot(tri_strict, hot_all.astype(BF16)) + carry[0:1, :]
    route = jnp.zeros((tm, ROUTE_LANES), F32)
    gates = jnp.zeros((tm, ROUTE_LANES), F32)
    for j in range(TOP_K):
        rank = jnp.sum(hots[j] * prefix, axis=-1, keepdims=True)
        route = jnp.where(lane == j, idxs[j], route)
        route = jnp.where(lane == TOP_K + j, rank, route)
        gates = jnp.where(lane == j, exps[j] / den, gates)
    route_o[...] = route.astype(I32)
    gate_o[...] = gates
    new_carry = carry[...] + jnp.sum(hot_all, axis=0, keepdims=True)
    carry[...] = new_carry
    cnt_o[...] = new_carry


def _mix_out(y2d, bonus2d, g2d, po2d, x2d, lnx_g, lnx_b, ones_bd, w_out_bf, ln2_g,
             rw_hi, rw_lo, rb_pad, carry_in, tm):
    n = x2d.shape[0]
    row = lambda x: x.reshape(1, -1)
    cst = lambda shape: pl.BlockSpec(shape, lambda i: (0,) * len(shape))
    half = pl.BlockSpec((tm, RWKV_W), lambda i: (i, 0))
    full = pl.BlockSpec((tm, D_MODEL), lambda i: (i, 0))
    lanes = pl.BlockSpec((tm, ROUTE_LANES), lambda i: (i, 0))
    est = (2 * (4 * tm * RWKV_W * 4 + 3 * tm * D_MODEL * 4) + 2 * (D_MODEL * D_MODEL * 2 + RWKV_W * RWKV_W * 2)
           + 8 * tm * D_MODEL * 4 + 4 * tm * tm)
    return pl.pallas_call(
        _mix_out_kernel,
        grid=(n // tm,),
        in_specs=[half, half, half, half, full,
                  cst((1, RWKV_W)), cst((1, RWKV_W)), cst((GROUP_TILE, GROUP_TILE)), cst((D_MODEL, D_MODEL)),
                  cst((1, D_MODEL)), cst((D_MODEL, ROUTE_LANES)), cst((D_MODEL, ROUTE_LANES)),
                  cst((1, ROUTE_LANES)), cst((SUBLANES, ROUTE_LANES))],
        out_specs=[full, full, lanes, lanes, cst((SUBLANES, ROUTE_LANES))],
        out_shape=[jax.ShapeDtypeStruct((n, D_MODEL), F32), jax.ShapeDtypeStruct((n, D_MODEL), F32),
                   jax.ShapeDtypeStruct((n, ROUTE_LANES), I32), jax.ShapeDtypeStruct((n, ROUTE_LANES), F32),
                   jax.ShapeDtypeStruct((SUBLANES, ROUTE_LANES), F32)],
        scratch_shapes=[pltpu.VMEM((SUBLANES, ROUTE_LANES), F32)],
        compiler_params=_params(1, est),
        name="mix_out_router",
    )(y2d, bonus2d, g2d, po2d, x2d, row(lnx_g), row(lnx_b), ones_bd, w_out_bf, row(ln2_g),
      rw_hi, rw_lo, rb_pad, carry_in)


def _row_copy(src, src_row, dst, dst_row, sem):
    return pltpu.make_async_copy(src.at[pl.ds(src_row, 1), :], dst.at[pl.ds(dst_row, 1), :], sem)


def _dispatch_kernel(pad_start_ref, pad_len_ref, tail_ref, dest_ref, xa_ref, xb_ref, o_hbm, zbuf, sem, sem_fill,
                     *, n_main, n_last):
    i = pl.program_id(0)
    tm = xa_ref.shape[0]
    blk = zbuf.shape[0]

    @pl.when(i == 0)
    def _():
        zbuf[...] = jnp.zeros_like(zbuf)

        def per_expert(e, totals):
            start, n = pad_start_ref[e], pad_len_ref[e]
            n_head = jnp.minimum((-start) & (SUBLANES - 1), n)
            n_groups = (n - n_head) // SUBLANES
            first_group = pl.multiple_of(start + n_head, SUBLANES)

            def row(r, carry):
                _row_copy(zbuf, 0, o_hbm, start + r, sem_fill).start()
                return carry

            def group(g, carry):
                dst = pl.multiple_of(first_group + g * SUBLANES, SUBLANES)
                pltpu.make_async_copy(zbuf.at[pl.ds(0, SUBLANES), :], o_hbm.at[pl.ds(dst, SUBLANES), :],
                                      sem_fill).start()
                return carry

            lax.fori_loop(0, n_head, row, 0)
            lax.fori_loop(0, n_groups, group, 0)
            return totals[0] + n_head, totals[1] + n_groups

        n_rows_filled, n_groups_filled = lax.fori_loop(0, N_EXPERTS, per_expert, (0, 0))

        def wait_row(r, carry):
            _row_copy(zbuf, 0, o_hbm, 0, sem_fill).wait()
            return carry

        def wait_group(g, carry):
            pltpu.make_async_copy(zbuf.at[pl.ds(0, SUBLANES), :], o_hbm.at[pl.ds(0, SUBLANES), :], sem_fill).wait()
            return carry

        lax.fori_loop(0, n_rows_filled, wait_row, 0)
        lax.fori_loop(0, n_groups_filled, wait_group, 0)

        def fill_block(b, carry):
            start = pl.multiple_of(tail_ref[0] + b * blk, blk)
            pltpu.make_async_copy(zbuf, o_hbm.at[pl.ds(start, blk), :], sem_fill).start()
            return carry

        def wait_block(b, carry):
            pltpu.make_async_copy(zbuf, o_hbm.at[pl.ds(0, blk), :], sem_fill).wait()
            return carry

        lax.fori_loop(0, tail_ref[1], fill_block, 0)
        lax.fori_loop(0, tail_ref[1], wait_block, 0)

    def scatter_rows(x_ref, n_rows):
        def issue(n, carry):
            for j in range(TOP_K):
                _row_copy(x_ref, n, o_hbm, dest_ref[n * TOP_K + j], sem).start(priority=j % 2)
            return carry

        lax.fori_loop(0, n_rows, issue, 0)
        pltpu.make_async_copy(o_hbm.at[pl.ds(0, n_rows * TOP_K), :], o_hbm.at[pl.ds(0, n_rows * TOP_K), :], sem).wait()

    @pl.when(i < n_main)
    def _():
        scatter_rows(xa_ref, tm)

    @pl.when(i == n_main)
    def _():
        scatter_rows(xb_ref, n_last)


def _dispatch(pad_start, pad_len, tail, dest_all, xn_a, xn_b, n_slots, tm):
    n_main = xn_a.shape[0] // tm
    n_last = xn_b.shape[0]
    assert n_last <= tm and n_slots >= tm * TOP_K
    grid_spec = pltpu.PrefetchScalarGridSpec(
        num_scalar_prefetch=3,
        grid=(n_main + 1,),
        in_specs=[pl.BlockSpec((tm * TOP_K,), lambda i, *_: (i,), memory_space=pltpu.SMEM),
                  pl.BlockSpec((tm, D_MODEL), lambda i, *_: (jnp.minimum(i, n_main - 1), 0)),
                  pl.BlockSpec((n_last, D_MODEL), lambda i, *_: (0, 0))],
        out_specs=pl.BlockSpec(memory_space=pl.ANY),
        scratch_shapes=[pltpu.VMEM((EXPERT_BLOCK, D_MODEL), F32), pltpu.SemaphoreType.DMA(()),
                        pltpu.SemaphoreType.DMA(())],
    )
    return pl.pallas_call(
        functools.partial(_dispatch_kernel, n_main=n_main, n_last=n_last),
        grid_spec=grid_spec,
        out_shape=jax.ShapeDtypeStruct((n_slots, D_MODEL), F32),
        compiler_params=_params(1, 2 * (tm + n_last) * D_MODEL * 4 + EXPERT_BLOCK * D_MODEL * 4),
        name="dispatch",
    )(pad_start, pad_len, tail, dest_all, xn_a, xn_b)


def _expert_kernel(be_ref, nu_ref, xs_ref, wgu_ref, bgu_ref, wdn_ref, bdn_ref, o_ref, wgu_bf, wdn_bf):
    i = pl.program_id(0)
    used = i < nu_ref[0]
    changed = jnp.logical_or(i == 0, be_ref[i] != be_ref[jnp.maximum(i - 1, 0)])

    @pl.when(jnp.logical_and(used, changed))
    def _():
        wgu_bf[...] = wgu_ref[0].astype(BF16)
        wdn_bf[...] = wdn_ref[0].astype(BF16)

    @pl.when(used)
    def _():
        gu = _dot(xs_ref[...].astype(BF16), wgu_bf[...]) + bgu_ref[0]
        gate = jnp.minimum(gu[:, :D_EXPERT], SWIGLU_LIMIT)
        up = jnp.clip(gu[:, D_EXPERT:], -SWIGLU_LIMIT, SWIGLU_LIMIT)
        hmid = (up + 1.0) * (gate * jax.nn.sigmoid(SWIGLU_ALPHA * gate))
        o_ref[...] = _dot(hmid.astype(BF16), wdn_bf[...]) + bdn_ref[0]

    @pl.when(jnp.logical_not(used))
    def _():
        o_ref[...] = jnp.zeros_like(o_ref)


def _experts(block_e, n_used, xs, w_gu, b_gu, w_down, b_down):
    n_slots = xs.shape[0]
    nb = n_slots // EXPERT_BLOCK
    blk = EXPERT_BLOCK
    est = (2 * 2 * blk * D_MODEL * 4 + 2 * (D_MODEL * 2 * D_EXPERT + D_EXPERT * D_MODEL) * 4
           + (D_MODEL * 2 * D_EXPERT + D_EXPERT * D_MODEL) * 2 + 4 * blk * 2 * D_EXPERT * 4)
    grid_spec = pltpu.PrefetchScalarGridSpec(
        num_scalar_prefetch=2,
        grid=(nb,),
        in_specs=[pl.BlockSpec((blk, D_MODEL), lambda i, be, nu: (i, 0)),
                  pl.BlockSpec((1, D_MODEL, 2 * D_EXPERT), lambda i, be, nu: (be[i], 0, 0)),
                  pl.BlockSpec((1, 1, 2 * D_EXPERT), lambda i, be, nu: (be[i], 0, 0)),
                  pl.BlockSpec((1, D_EXPERT, D_MODEL), lambda i, be, nu: (be[i], 0, 0)),
                  pl.BlockSpec((1, 1, D_MODEL), lambda i, be, nu: (be[i], 0, 0))],
        out_specs=pl.BlockSpec((blk, D_MODEL), lambda i, be, nu: (i, 0)),
        scratch_shapes=[pltpu.VMEM((D_MODEL, 2 * D_EXPERT), BF16), pltpu.VMEM((D_EXPERT, D_MODEL), BF16)],
    )
    return pl.pallas_call(
        _expert_kernel,
        grid_spec=grid_spec,
        out_shape=jax.ShapeDtypeStruct((n_slots, D_MODEL), F32),
        compiler_params=_params(1, est),
        name="experts",
    )(block_e, n_used, xs, w_gu, b_gu.reshape(N_EXPERTS, 1, 2 * D_EXPERT), w_down,
      b_down.reshape(N_EXPERTS, 1, D_MODEL))


def _combine_kernel(dest_ref, dest_next_ref, ys_hbm, gate_ref, h_ref, p_ref, ln3_ref, pg_ref, pp_ref, fg_ref,
                    o_ref, gbuf, sem):
    i = pl.program_id(0)
    tm = h_ref.shape[0]
    slot = lax.rem(i, 2)

    def issue(d_ref, s):
        def body(n, carry):
            for j in range(TOP_K):
                _row_copy(ys_hbm, d_ref[n * TOP_K + j], gbuf.at[s, j], n, sem.at[s]).start(priority=j % 2)
            return carry
        lax.fori_loop(0, tm, body, 0)

    @pl.when(i == 0)
    def _():
        issue(dest_ref, 0)

    @pl.when(i + 1 < pl.num_programs(0))
    def _():
        issue(dest_next_ref, 1 - slot)

    for j in range(TOP_K):
        pltpu.make_async_copy(gbuf.at[slot, j], gbuf.at[slot, j], sem.at[slot]).wait()

    gates = gate_ref[...]
    moe = gates[:, 0:1] * gbuf[slot, 0]
    for j in range(1, TOP_K):
        moe = moe + gates[:, j:j + 1] * gbuf[slot, j]
    h = h_ref[...] + moe
    gate = jax.nn.sigmoid(_dot(_rmsnorm_rows(h, ln3_ref[...]).astype(BF16), pg_ref[...]))
    h = h + gate * _dot(p_ref[...].astype(BF16), pp_ref[...])
    o_ref[...] = _rmsnorm_rows(h, fg_ref[...])


def _combine(dest_flat, ys, gates, h2d, p2d, ln3_g, ple_gate_bf, ple_proj_bf, final_g, tm):
    n = h2d.shape[0]
    nb = n // tm
    row = lambda x: x.reshape(1, -1)
    cst = lambda shape: pl.BlockSpec(shape, lambda i: (0,) * len(shape))
    est = (2 * TOP_K * tm * D_MODEL * 4 + 2 * (2 * tm * D_MODEL * 4 + tm * PLE_DIM * 4 + tm * ROUTE_LANES * 4)
           + 2 * (D_MODEL * D_MODEL * 2 + PLE_DIM * D_MODEL * 2) + 6 * tm * D_MODEL * 4)
    return pl.pallas_call(
        _combine_kernel,
        grid=(nb,),
        in_specs=[pl.BlockSpec((tm * TOP_K,), lambda i: (i,), memory_space=pltpu.SMEM),
                  pl.BlockSpec((tm * TOP_K,), lambda i: (jnp.minimum(i + 1, nb - 1),), memory_space=pltpu.SMEM),
                  pl.BlockSpec(memory_space=pl.ANY),
                  pl.BlockSpec((tm, ROUTE_LANES), lambda i: (i, 0)),
                  pl.BlockSpec((tm, D_MODEL), lambda i: (i, 0)),
                  pl.BlockSpec((tm, PLE_DIM), lambda i: (i, 0)),
                  cst((1, D_MODEL)), cst((D_MODEL, D_MODEL)), cst((PLE_DIM, D_MODEL)), cst((1, D_MODEL))],
        out_specs=pl.BlockSpec((tm, D_MODEL), lambda i: (i, 0)),
        out_shape=jax.ShapeDtypeStruct((n, D_MODEL), F32),
        scratch_shapes=[pltpu.VMEM((2, TOP_K, tm, D_MODEL), F32), pltpu.SemaphoreType.DMA((2,))],
        compiler_params=_params(1, est),
        name="combine_ple_norm",
    )(dest_flat, dest_flat, ys, gates, h2d, p2d, row(ln3_g), ple_gate_bf, ple_proj_bf, row(final_g))


def _tile(n, pref):
    t = min(n, pref)
    assert n % t == 0, (n, pref)
    return t


def _front(x, pos0, pool_ctx, shift_prev, wkv0, wts, carry_in, tm_mix):
    b, t, _ = x.shape
    n = b * t
    assert t >= POOL_CTX and t % SUBLANES == 0
    x2d = x.reshape(n, D_MODEL)
    u2d, zr2d = _in_proj(x2d, wts["ln1_g"], wts["w_in_bf"], _tile(n, 512))
    u = u2d.reshape(b, t, POOL_W)
    zr = zr2d.reshape(b, t, RWKV_PROJ)

    ctx16 = jnp.concatenate([jnp.zeros((b, POOL_HALO - POOL_CTX, POOL_W), F32), pool_ctx.astype(F32)], axis=1)
    pool_out = _pool_mixer(u, ctx16, wts["w_pool_bf"], wts["pool_scale"], pos0, _tile(t, 512))
    new_pool = u[:, -POOL_CTX:]

    sh8 = jnp.broadcast_to(shift_prev.astype(F32), (b, SUBLANES, RWKV_PROJ))
    r, lw, kx, v, na, kb, g, bonus = _rwkv_pre(
        zr, sh8, wts["mu_shift"], wts["w0"], wts["w2_bf"], wts["a0"], wts["a2_bf"], wts["g2_bf"],
        wts["k_k"], wts["k_a"], wts["r_k"], wts["ones_bd"], _tile(t, 256))
    new_shift = zr[:, -1:]

    t_pad = -(-t // CHUNK) * CHUNK
    seqs = (r, lw, kx, v, na, kb)
    if t_pad != t:
        seqs = tuple(jnp.pad(s, ((0, 0), (0, t_pad - t), (0, 0))) for s in seqs)
    y, s_fin = _wkv_scan(*seqs, _state_to_pairs(wkv0), _tile(t_pad, 8 * CHUNK))
    y = y[:, :t]
    new_wkv = _pairs_to_state(s_fin)

    h, xn, route, gates, counts = _mix_out(
        y.reshape(n, RWKV_W), bonus.reshape(n, RWKV_W), g.reshape(n, RWKV_W), pool_out.reshape(n, POOL_W),
        x2d, wts["lnx_g"], wts["lnx_b"], wts["ones_bd"], wts["w_out_bf"], wts["ln2_g"],
        wts["rw_hi"], wts["rw_lo"], wts["rb_pad"], carry_in, tm_mix)
    return dict(h=h, xn=xn, route=route, gates=gates, counts=counts,
                new_pool=new_pool, new_shift=new_shift, new_wkv=new_wkv)


def _prep_weights(i, ln1_g, w_in, mu_shift, w_pool, pool_scale, w0, w2, a0, a2, g2, k_k, k_a, r_k,
                  lnx_g, lnx_b, w_out, ln2_g, router_w, router_b, ln3_g, ple_gate, ple_proj):
    head = jnp.arange(GROUP_TILE) // HEAD_DIM
    ones_bd = (head[:, None] == head[None, :]).astype(BF16)
    rw_pad = jnp.pad(router_w[i].astype(F32), ((0, 0), (0, ROUTE_LANES - N_EXPERTS)))
    rw_hi = rw_pad.astype(BF16)
    rw_lo = (rw_pad - rw_hi.astype(F32)).astype(BF16)
    rb_pad = jnp.concatenate([router_b[i].astype(F32), jnp.full((ROUTE_LANES - N_EXPERTS,), -jnp.inf, F32)])
    return dict(
        ln1_g=ln1_g[i], w_in_bf=w_in[i].astype(BF16), mu_shift=mu_shift[i], w_pool_bf=w_pool[i].astype(BF16),
        pool_scale=pool_scale[i], w0=w0[i], w2_bf=w2[i].astype(BF16), a0=a0[i], a2_bf=a2[i].astype(BF16),
        g2_bf=g2[i].astype(BF16), k_k=k_k[i], k_a=k_a[i], r_k=r_k[i].reshape(RWKV_W), lnx_g=lnx_g[i],
        lnx_b=lnx_b[i], w_out_bf=w_out[i].astype(BF16), ln2_g=ln2_g[i], rw_hi=rw_hi, rw_lo=rw_lo,
        rb_pad=rb_pad.reshape(1, ROUTE_LANES), ones_bd=ones_bd, ln3_g=ln3_g[i],
        ple_gate_bf=ple_gate[i].astype(BF16), ple_proj_bf=ple_proj[i].astype(BF16))


def _layer_pair(hp, hs, pp, ps, pool_ctx, shift_prev, wkv0, wts, w_gu, b_gu, w_down, b_down, final_g):
    bp, tp, _ = hp.shape
    bs, ts, _ = hs.shape
    np_, ns = bp * tp, bs * ts
    tm_p = _tile(np_, 256)
    tm_s = _tile(ns, 128)
    zero_carry = jnp.zeros((SUBLANES, ROUTE_LANES), F32)
    fp = _front(hp, 0, jnp.zeros((bp, POOL_CTX, POOL_W), F32), jnp.zeros((bp, 1, RWKV_PROJ), F32),
                jnp.zeros((bp, N_HEADS, HEAD_DIM, HEAD_DIM), F32), wts, zero_carry, tm_p)
    fs = _front(hs, PAST_LEN, pool_ctx, shift_prev, wkv0, wts, fp["counts"], tm_s)

    counts = fs["counts"][0, :N_EXPERTS].astype(I32)
    padded = (counts + EXPERT_BLOCK - 1) // EXPERT_BLOCK * EXPERT_BLOCK
    pad_end = jnp.cumsum(padded)
    start_pad = pad_end - padded
    n_blocks = -(-(np_ + ns) * TOP_K // EXPERT_BLOCK) + N_EXPERTS
    n_slots = n_blocks * EXPERT_BLOCK
    n_used = (pad_end[-1] // EXPERT_BLOCK).astype(I32)
    block_start = jnp.arange(n_blocks, dtype=I32) * EXPERT_BLOCK
    block_e = jnp.minimum(jnp.sum((block_start[:, None] >= pad_end[None, :]).astype(I32), axis=1), N_EXPERTS - 1)
    last_e = block_e[jnp.maximum(n_used - 1, 0)]
    block_e = jnp.where(jnp.arange(n_blocks) < n_used, block_e, last_e)

    def dest_of(route):
        idx = route[:, :TOP_K]
        rank = route[:, TOP_K:2 * TOP_K]
        start = jnp.sum(jnp.where(idx[..., None] == jnp.arange(N_EXPERTS, dtype=I32), start_pad, 0), axis=-1)
        return (start + rank).reshape(-1).astype(I32)

    dest_p, dest_s = dest_of(fp["route"]), dest_of(fs["route"])
    tm_d = _tile(np_, 512)
    dest_all = jnp.concatenate([dest_p, dest_s, jnp.zeros(((tm_d - ns) * TOP_K,), I32)])
    tail = jnp.stack([n_used * EXPERT_BLOCK, n_blocks - n_used]).astype(I32)
    xs = _dispatch(start_pad + counts, padded - counts, tail, dest_all, fp["xn"], fs["xn"], n_slots, tm_d)
    ys = _experts(block_e, n_used.reshape(1), xs, w_gu, b_gu, w_down, b_down)

    outs = []
    for f, dest, p, tm in ((fp, dest_p, pp, tm_p), (fs, dest_s, ps, tm_s)):
        n = f["h"].shape[0]
        outs.append(_combine(dest, ys, f["gates"], f["h"], p.reshape(n, PLE_DIM).astype(F32), wts["ln3_g"],
                             wts["ple_gate_bf"], wts["ple_proj_bf"], final_g, tm))
    return outs[0].reshape(hp.shape), outs[1].reshape(hs.shape), fp, fs


def kernel(x_prompt, x_sample, p_prompt, p_sample, cache_pool, state_shift, state_wkv, ln1_g, w_in, mu_shift, w_pool, pool_scale, w0, w2, a0, a2, g2, k_k, k_a, r_k, lnx_g, lnx_b, w_out, ln2_g, router_w, router_b, w_gu, b_gu, w_down, b_down, ln3_g, ple_gate, ple_proj, final_g):
    assert ln1_g.shape[0] == 1, "single-layer kernel"
    i = 0
    wts = _prep_weights(i, ln1_g, w_in, mu_shift, w_pool, pool_scale, w0, w2, a0, a2, g2, k_k, k_a, r_k,
                        lnx_g, lnx_b, w_out, ln2_g, router_w, router_b, ln3_g, ple_gate, ple_proj)
    y_p, y_s, fp, fs = _layer_pair(x_prompt, x_sample, p_prompt[i], p_sample[i], cache_pool[i], state_shift[i],
                                   state_wkv[i], wts, w_gu[i], b_gu[i], w_down[i], b_down[i], final_g)
    stack = lambda a: a[None]
    return (y_p, y_s,
            stack(fp["new_pool"]), stack(fp["new_shift"]), stack(fp["new_wkv"]),
            stack(fs["new_pool"]), stack(fs["new_shift"]), stack(fs["new_wkv"]))
```

```python
import functools

import jax
import jax.numpy as jnp
from jax import lax
from jax.experimental import pallas as pl
from jax.experimental.pallas import tpu as pltpu

F32 = jnp.float32
BF16 = jnp.bfloat16
I32 = jnp.int32

D_MODEL = 1024
POOL_W = 512
POOL_WINDOWS = (2, 4, 8, 16)
POOL_GW = 128
POOL_CTX = 15
POOL_HALO = 16
RWKV_W = 512
HEAD_DIM = 64
N_HEADS = 8
N_PAIRS = N_HEADS // 2
PAIR_W = 2 * HEAD_DIM
DECAY_LORA = 64
ICLR_LORA = 64
GATE_LORA = 128
RWKV_PROJ = 3 * RWKV_W + DECAY_LORA + ICLR_LORA + GATE_LORA
IN_PROJ = POOL_W + RWKV_PROJ
LNX_EPS = 64e-5
N_EXPERTS = 32
TOP_K = 4
D_EXPERT = 1024
SWIGLU_LIMIT = 7.0
SWIGLU_ALPHA = 1.702
PLE_DIM = 256
RMS_EPS = 1e-6
PAST_LEN = 2048

CHUNK = 64
EXPERT_BLOCK = 512
ROUTE_LANES = 128
SUBLANES = 8
LANES = 128
assert D_MODEL == SUBLANES * LANES
ISSUE_UNROLL = 4
GROUP_TILE = 256
EXP_M05 = 0.6065306597126334
V7X_VMEM_BYTES = 64 * 1024 * 1024

NT_DIMS = (((1,), (1,)), ((), ()))
TN_DIMS = (((0,), (0,)), ((), ()))


def _vmem_limit(est_bytes):
    return int(min(est_bytes * 3 // 2 + (4 << 20), V7X_VMEM_BYTES - (8 << 20)))


def _params(n_axes, est_bytes):
    return pltpu.CompilerParams(dimension_semantics=("arbitrary",) * n_axes,
                                vmem_limit_bytes=_vmem_limit(est_bytes))


def _dot(a, b):
    return jnp.dot(a, b, preferred_element_type=F32)


def _split2(x):
    hi = x.astype(BF16)
    lo = (x - hi.astype(F32)).astype(BF16)
    return hi, lo


def _split3(x):
    hi = x.astype(BF16)
    r1 = x - hi.astype(F32)
    mid = r1.astype(BF16)
    lo = (r1 - mid.astype(F32)).astype(BF16)
    return hi, mid, lo


def _group_sum(x, ones_bd):
    hi, lo = _split2(x)
    slabs = [_dot(hi[:, s:s + GROUP_TILE], ones_bd) + _dot(lo[:, s:s + GROUP_TILE], ones_bd)
             for s in range(0, x.shape[1], GROUP_TILE)]
    return jnp.concatenate(slabs, axis=1)


def _store_row_tiles(ref, value):
    n = value.shape[0]
    for c in range(SUBLANES):
        ref[pl.ds(c, n, stride=SUBLANES), :] = value[:, c * LANES:(c + 1) * LANES]


def _load_row_tiles(ref, n):
    return jnp.concatenate([ref[pl.ds(c, n, stride=SUBLANES), :] for c in range(SUBLANES)], axis=1)


def _rmsnorm_rows(x, g):
    ms = jnp.mean(x * x, axis=-1, keepdims=True)
    return (x * lax.rsqrt(ms + RMS_EPS)) * g


def _in_proj_kernel(x_ref, g_ref, w_ref, u_ref, zr_ref):
    xn = _rmsnorm_rows(x_ref[...], g_ref[...])
    z = _dot(xn.astype(BF16), w_ref[...])
    u_ref[...] = z[:, :POOL_W]
    zr_ref[...] = z[:, POOL_W:]


def _in_proj(x2d, ln1_g, w_in_bf, tm):
    n = x2d.shape[0]
    est = 2 * tm * D_MODEL * 4 + 2 * D_MODEL * IN_PROJ * 2 + 3 * tm * IN_PROJ * 4
    return pl.pallas_call(
        _in_proj_kernel,
        grid=(n // tm,),
        in_specs=[pl.BlockSpec((tm, D_MODEL), lambda i: (i, 0)),
                  pl.BlockSpec((1, D_MODEL), lambda i: (0, 0)),
                  pl.BlockSpec((D_MODEL, IN_PROJ), lambda i: (0, 0))],
        out_specs=[pl.BlockSpec((tm, POOL_W), lambda i: (i, 0)),
                   pl.BlockSpec((tm, RWKV_PROJ), lambda i: (i, 0))],
        out_shape=[jax.ShapeDtypeStruct((n, POOL_W), F32),
                   jax.ShapeDtypeStruct((n, RWKV_PROJ), F32)],
        compiler_params=_params(1, est),
        name="in_proj",
    )(x2d, ln1_g.reshape(1, D_MODEL), w_in_bf)


def _pool_kernel(u_ref, halo_ref, ctx_ref, wp_ref, ps_ref, o_ref, buf, *, pos0):
    t = pl.program_id(1)
    tt = u_ref.shape[1]
    buf[0:POOL_HALO, :] = jnp.where(t == 0, ctx_ref[0], halo_ref[0])
    buf[POOL_HALO:, :] = u_ref[0]
    pos = lax.broadcasted_iota(I32, (tt, 1), 0) + (t * tt + pos0)
    outs = []
    for gi, w in enumerate(POOL_WINDOWS):
        sl = slice(gi * POOL_GW, (gi + 1) * POOL_GW)
        cur = buf[POOL_HALO:POOL_HALO + tt, sl]
        acc = cur
        for j in range(1, w):
            acc = acc + buf[POOL_HALO - j:POOL_HALO - j + tt, sl]
        cnt = jnp.minimum(w, pos + 1).astype(F32)
        pooled = acc / cnt - cur
        outs.append(_dot(pooled.astype(BF16), wp_ref[gi]))
    o_ref[0] = (jnp.concatenate(outs, axis=-1) * ps_ref[...]).astype(o_ref.dtype)


def _pool_mixer(u, ctx16, w_pool_bf, pool_scale, pos0, tt):
    b, t, _ = u.shape
    hb = tt // POOL_HALO
    est = 2 * tt * POOL_W * 4 * 3 + (tt + POOL_HALO) * POOL_W * 4
    return pl.pallas_call(
        functools.partial(_pool_kernel, pos0=pos0),
        grid=(b, t // tt),
        in_specs=[pl.BlockSpec((1, tt, POOL_W), lambda i, j: (i, j, 0)),
                  pl.BlockSpec((1, POOL_HALO, POOL_W), lambda i, j: (i, jnp.maximum(j * hb - 1, 0), 0)),
                  pl.BlockSpec((1, POOL_HALO, POOL_W), lambda i, j: (i, 0, 0)),
                  pl.BlockSpec((len(POOL_WINDOWS), POOL_GW, POOL_GW), lambda i, j: (0, 0, 0)),
                  pl.BlockSpec((1, POOL_W), lambda i, j: (0, 0))],
        out_specs=pl.BlockSpec((1, tt, POOL_W), lambda i, j: (i, j, 0)),
        out_shape=jax.ShapeDtypeStruct((b, t, POOL_W), BF16),
        scratch_shapes=[pltpu.VMEM((tt + POOL_HALO, POOL_W), F32)],
        compiler_params=_params(2, est),
        name="pool_mixer",
    )(u, u, ctx16, w_pool_bf, pool_scale.reshape(1, POOL_W))


def _rwkv_pre_kernel(zr_ref, halo_ref, sh_ref, mu_ref, w0_ref, w2_ref, a0_ref, a2_ref, g2_ref,
                     kk_ref, ka_ref, rk_ref, ones_ref,
                     r_o, lw_o, kx_o, v_o, na_o, kb_o, g_o, bonus_o, buf):
    t = pl.program_id(1)
    tt = zr_ref.shape[1]
    buf[0:SUBLANES, :] = jnp.where(t == 0, sh_ref[0], halo_ref[0])
    zr = zr_ref[0]
    buf[SUBLANES:, :] = zr
    prev = buf[SUBLANES - 1:SUBLANES - 1 + tt, :]
    zs = zr + (prev - zr) * mu_ref[...]
    o1, o2, o3 = RWKV_W, 2 * RWKV_W, 3 * RWKV_W
    o4, o5 = o3 + DECAY_LORA, o3 + DECAY_LORA + ICLR_LORA
    r, k, v = zs[:, :o1], zs[:, o1:o2], zs[:, o2:o3]
    wd, ad, gd = zs[:, o3:o4], zs[:, o4:o5], zs[:, o5:]
    ones_bd = ones_ref[...]
    wz = w0_ref[...] + _dot(jnp.tanh(wd).astype(BF16), w2_ref[...])
    lw = -EXP_M05 * jax.nn.sigmoid(wz)
    a = jax.nn.sigmoid(a0_ref[...] + _dot(ad.astype(BF16), a2_ref[...]))
    g = _dot(jax.nn.sigmoid(gd).astype(BF16), g2_ref[...])
    kk = k * kk_ref[...]
    kk = kk / jnp.maximum(jnp.sqrt(_group_sum(kk * kk, ones_bd)), 1e-12)
    kx = k * (1.0 + (a - 1.0) * ka_ref[...])
    bonus = _group_sum(r * kx * rk_ref[...], ones_bd) * v
    r_o[0] = r
    lw_o[0] = lw
    kx_o[0] = kx
    v_o[0] = v
    na_o[0] = -kk
    kb_o[0] = kk * a
    g_o[0] = g
    bonus_o[0] = bonus


def _rwkv_pre(zr, sh8, mu, w0, w2_bf, a0, a2_bf, g2_bf, k_k, k_a, r_k, ones_bd, tt):
    b, t, _ = zr.shape
    hb = tt // SUBLANES
    row = lambda x: x.reshape(1, -1)
    cst = lambda shape: pl.BlockSpec(shape, lambda i, j: (0,) * len(shape))
    seq = pl.BlockSpec((1, tt, RWKV_W), lambda i, j: (i, j, 0))
    est = 3 * tt * RWKV_PROJ * 4 + 2 * 8 * tt * RWKV_W * 4 + 12 * tt * RWKV_W * 4
    return pl.pallas_call(
        _rwkv_pre_kernel,
        grid=(b, t // tt),
        in_specs=[pl.BlockSpec((1, tt, RWKV_PROJ), lambda i, j: (i, j, 0)),
                  pl.BlockSpec((1, SUBLANES, RWKV_PROJ), lambda i, j: (i, jnp.maximum(j * hb - 1, 0), 0)),
                  pl.BlockSpec((1, SUBLANES, RWKV_PROJ), lambda i, j: (i, 0, 0)),
                  cst((1, RWKV_PROJ)), cst((1, RWKV_W)), cst((DECAY_LORA, RWKV_W)),
                  cst((1, RWKV_W)), cst((ICLR_LORA, RWKV_W)), cst((GATE_LORA, RWKV_W)),
                  cst((1, RWKV_W)), cst((1, RWKV_W)), cst((1, RWKV_W)), cst((GROUP_TILE, GROUP_TILE))],
        out_specs=[seq] * 8,
        out_shape=[jax.ShapeDtypeStruct((b, t, RWKV_W), F32)] * 8,
        scratch_shapes=[pltpu.VMEM((tt + SUBLANES, RWKV_PROJ), F32)],
        compiler_params=_params(2, est),
        name="rwkv_pre",
    )(zr, zr, sh8, row(mu), row(w0), w2_bf, row(a0), a2_bf, g2_bf, row(k_k), row(k_a), row(r_k), ones_bd)


def _pair_blockdiag(z, left):
    return jnp.concatenate([jnp.where(left, z, 0.0), jnp.where(left, 0.0, z)], axis=0)


def _wkv_kernel(r_ref, lw_ref, kx_ref, v_ref, na_ref, kb_ref, s0_ref, y_ref, sf_ref, s_scr):
    t = pl.program_id(1)
    tt = r_ref.shape[1]
    c = CHUNK

    @pl.when(t == 0)
    def _():
        s_scr[...] = s0_ref[0]

    lane = lax.broadcasted_iota(I32, (c, 2 * c), 1)
    rowi = lax.broadcasted_iota(I32, (c, 2 * c), 0)
    jm = lane & (c - 1)
    strict = jm < rowi
    incl = jm <= rowi
    eye_pair = jnp.where(jm == rowi, 1.0, 0.0).astype(F32)
    left = lax.broadcasted_iota(I32, (c, PAIR_W), 1) < HEAD_DIM
    rr = lax.broadcasted_iota(I32, (c, c), 0)
    cc = lax.broadcasted_iota(I32, (c, c), 1)
    tri_incl_bf = jnp.where(cc <= rr, 1.0, 0.0).astype(BF16)
    br = lax.broadcasted_iota(I32, (PAIR_W, PAIR_W), 0) < HEAD_DIM
    bc = lax.broadcasted_iota(I32, (PAIR_W, PAIR_W), 1) < HEAD_DIM
    bd_mask = br == bc
    bf = lambda x: x.astype(BF16)
    bd = lambda z: _pair_blockdiag(z, left)
    n_chunks = tt // c

    cums = []
    for ci in range(n_chunks):
        h3 = _split3(lw_ref[0, ci * c:(ci + 1) * c, :])
        cums.append(_dot(tri_incl_bf, h3[0]) + _dot(tri_incl_bf, h3[1]) + _dot(tri_incl_bf, h3[2]))

    chains = []
    for ci in range(n_chunks):
        rows = slice(ci * c, (ci + 1) * c)
        for p in range(N_PAIRS):
            cols = slice(p * PAIR_W, (p + 1) * PAIR_W)
            cum = cums[ci][:, cols]
            lw, kx, kb = lw_ref[0, rows, cols], kx_ref[0, rows, cols], kb_ref[0, rows, cols]
            tot = cum[c - 1:c, :]
            p_end = jnp.exp(tot - cum)
            inv_p = jnp.exp(-cum)
            chains.append(dict(
                ci=ci, p=p, v=v_ref[0, rows, cols], p_c=jnp.exp(tot),
                a_t=na_ref[0, rows, cols] * jnp.exp(cum - lw), r_t=r_ref[0, rows, cols] * jnp.exp(cum),
                b_t=kb * inv_p, k_t=kx * inv_p, b_h=kb * p_end, k_h=kx * p_end))

    for d in chains:
        lhs = bf(jnp.concatenate([d["a_t"], d["r_t"]], axis=0))
        rhs = bf(jnp.concatenate([bd(d["b_t"]), bd(d["k_t"])], axis=0))
        sc = lax.dot_general(lhs, rhs, NT_DIMS, preferred_element_type=F32)
        d["a_ab"] = jnp.where(strict, sc[:c, :2 * c], 0.0)
        d["a_k"] = jnp.concatenate([jnp.where(strict, sc[:c, 2 * c:], 0.0),
                                    jnp.where(incl, sc[c:, 2 * c:], 0.0)], axis=0)
        d["a_rb"] = jnp.where(incl, sc[c:, :2 * c], 0.0)

    for d in chains:
        d["x"] = _dot(bf(d["a_ab"]), bf(bd(d["a_ab"])))
        d["t"] = eye_pair + d["a_ab"]
    for _ in range(c.bit_length() - 3):
        for d in chains:
            st = _dot(bf(jnp.concatenate([d["x"], d["t"]], axis=0)), bf(bd(d["x"])))
            d["x"] = st[:c]
            d["t"] = d["t"] + st[c:]
    for d in chains:
        d["t"] = d["t"] + _dot(bf(d["t"]), bf(bd(d["x"])))

    for d in chains:
        wv = _dot(bf(d["a_k"]), bf(bd(d["v"])))
        d["w1"], d["rkv"] = wv[:c], wv[c:]
    for d in chains:
        tu = _dot(bf(d["t"]), bf(jnp.concatenate([bd(d["w1"]), bd(d["a_t"])], axis=1)))
        d["u_loc"], d["a_tt"] = tu[:, :PAIR_W], tu[:, PAIR_W:]
    for d in chains:
        ar = _dot(bf(d["a_rb"]), bf(jnp.concatenate([bd(d["u_loc"]), bd(d["a_tt"])], axis=1)))
        d["y_loc"] = ar[:, :PAIR_W] + d["rkv"]
        d["r_g"] = d["r_t"] + ar[:, PAIR_W:]
    for d in chains:
        lhs = jnp.concatenate([jnp.concatenate([d["v"], jnp.zeros_like(d["v"])], axis=1),
                               jnp.concatenate([d["u_loc"], d["a_tt"]], axis=1)], axis=0)
        rhs = jnp.concatenate([d["k_h"], d["b_h"]], axis=0)
        dp = lax.dot_general(bf(lhs), bf(rhs), TN_DIMS, preferred_element_type=F32)
        d["d_loc"] = jnp.where(bd_mask, dp[:PAIR_W], 0.0)
        d["phi"] = bf(jnp.where(bd_mask, dp[PAIR_W:], 0.0))

    state = [s_scr[p] for p in range(N_PAIRS)]
    for d in chains:
        p = d["p"]
        s_hi, s_lo = _split2(state[p])
        y = lax.dot_general(bf(d["r_g"]), s_hi, NT_DIMS, preferred_element_type=F32) + d["y_loc"]
        y_ref[0, d["ci"] * c:(d["ci"] + 1) * c, p * PAIR_W:(p + 1) * PAIR_W] = y
        state[p] = state[p] * d["p_c"] + (_dot(s_hi, d["phi"]) + _dot(s_lo, d["phi"])) + d["d_loc"]
    for p in range(N_PAIRS):
        s_scr[p] = state[p]

    @pl.when(t == pl.num_programs(1) - 1)
    def _():
        sf_ref[0] = s_scr[...]


def _wkv_scan(r, lw, kx, v, na, kb, s0_pair, tt):
    b, t, _ = r.shape
    assert CHUNK == HEAD_DIM and tt % CHUNK == 0
    seq = pl.BlockSpec((1, tt, RWKV_W), lambda i, j: (i, j, 0))
    st = pl.BlockSpec((1, N_PAIRS, PAIR_W, PAIR_W), lambda i, j: (i, 0, 0, 0))
    est = 2 * 7 * tt * RWKV_W * 4 + 5 * N_PAIRS * PAIR_W * PAIR_W * 4 + (8 << 20)
    return pl.pallas_call(
        _wkv_kernel,
        grid=(b, t // tt),
        in_specs=[seq] * 6 + [st],
        out_specs=[seq, st],
        out_shape=[jax.ShapeDtypeStruct((b, t, RWKV_W), F32),
                   jax.ShapeDtypeStruct((b, N_PAIRS, PAIR_W, PAIR_W), F32)],
        scratch_shapes=[pltpu.VMEM((N_PAIRS, PAIR_W, PAIR_W), F32)],
        compiler_params=_params(2, est),
        name="wkv_scan",
    )(r, lw, kx, v, na, kb, s0_pair)


def _state_to_pairs(s):
    b = s.shape[0]
    s = s.astype(F32).reshape(b, N_PAIRS, 2, HEAD_DIM, HEAD_DIM)
    z = jnp.zeros((b, N_PAIRS, HEAD_DIM, HEAD_DIM), F32)
    return jnp.concatenate([jnp.concatenate([s[:, :, 0], z], axis=-1),
                            jnp.concatenate([z, s[:, :, 1]], axis=-1)], axis=-2)


def _pairs_to_state(sp):
    b = sp.shape[0]
    s = jnp.stack([sp[:, :, :HEAD_DIM, :HEAD_DIM], sp[:, :, HEAD_DIM:, HEAD_DIM:]], axis=2)
    return s.reshape(b, N_HEADS, HEAD_DIM, HEAD_DIM)


def _mix_out_kernel(y_ref, bonus_ref, g_ref, po_ref, x_ref, lg_ref, lb_ref, ones_ref, wo_ref, ln2_ref,
                    rw_hi_ref, rw_lo_ref, rb_ref, cin_ref,
                    h_o, xn_o, route_o, gate_o, cnt_o, carry):
    i = pl.program_id(0)
    tm = y_ref.shape[0]

    @pl.when(i == 0)
    def _():
        carry[...] = cin_ref[...]

    ones_bd = ones_ref[...]
    y = y_ref[...]
    inv_n = 1.0 / HEAD_DIM
    mu = _group_sum(y, ones_bd) * inv_n
    dlt = y - mu
    var = _group_sum(dlt * dlt, ones_bd) * inv_n
    yn = dlt * lax.rsqrt(var + LNX_EPS) * lg_ref[...] + lb_ref[...]
    rw_out = (yn + bonus_ref[...]) * g_ref[...]
    mix = _dot(po_ref[...], wo_ref[:POOL_W, :]) + _dot(rw_out.astype(BF16), wo_ref[POOL_W:, :])
    h = x_ref[...] + mix
    h_o[...] = h
    xn = _rmsnorm_rows(h, ln2_ref[...])
    _store_row_tiles(xn_o, xn)

    x_hi, x_lo = _split2(xn)
    logits = (_dot(x_hi, rw_hi_ref[...]) + _dot(x_hi, rw_lo_ref[...]) + _dot(x_lo, rw_hi_ref[...])
              + rb_ref[...])
    lane = lax.broadcasted_iota(I32, (tm, ROUTE_LANES), 1)
    lane_f = lane.astype(F32)
    vals, idxs, hots = [], [], []
    work = logits
    for _ in range(TOP_K):
        m = jnp.max(work, axis=-1, keepdims=True)
        idx = jnp.min(jnp.where(work == m, lane_f, float(ROUTE_LANES)), axis=-1, keepdims=True)
        hit = lane_f == idx
        vals.append(m)
        idxs.append(idx)
        hots.append(jnp.where(hit, 1.0, 0.0).astype(F32))
        work = jnp.where(hit, -jnp.inf, work)
    exps = [jnp.exp(vv - vals[0]) for vv in vals]
    den = exps[0] + exps[1] + exps[2] + exps[3]

    hot_all = hots[0] + hots[1] + hots[2] + hots[3]
    rr = lax.broadcasted_iota(I32, (tm, tm), 0)
    cc = lax.broadcasted_iota(I32, (tm, tm), 1)
    tri_strict = jnp.where(cc < rr, 1.0, 0.0).astype(BF16)
    prefix = _dot(tri_strict, hot_all.astype(BF16)) + carry[0:1, :]
    route = jnp.zeros((tm, ROUTE_LANES), F32)
    gates = jnp.zeros((tm, ROUTE_LANES), F32)
    for j in range(TOP_K):
        rank = jnp.sum(hots[j] * prefix, axis=-1, keepdims=True)
        route = jnp.where(lane == j, idxs[j], route)
        route = jnp.where(lane == TOP_K + j, rank, route)
        gates = jnp.where(lane == j, exps[j] / den, gates)
    route_o[...] = route.astype(I32)
    gate_o[...] = gates
    new_carry = carry[...] + jnp.sum(hot_all, axis=0, keepdims=True)
    carry[...] = new_carry
    cnt_o[...] = new_carry


def _mix_out(y2d, bonus2d, g2d, po2d, x2d, lnx_g, lnx_b, ones_bd, w_out_bf, ln2_g,
             rw_hi, rw_lo, rb_pad, carry_in, tm):
    n = x2d.shape[0]
    row = lambda x: x.reshape(1, -1)
    cst = lambda shape: pl.BlockSpec(shape, lambda i: (0,) * len(shape))
    half = pl.BlockSpec((tm, RWKV_W), lambda i: (i, 0))
    full = pl.BlockSpec((tm, D_MODEL), lambda i: (i, 0))
    lanes = pl.BlockSpec((tm, ROUTE_LANES), lambda i: (i, 0))
    est = (2 * (4 * tm * RWKV_W * 4 + 3 * tm * D_MODEL * 4) + 2 * (D_MODEL * D_MODEL * 2 + RWKV_W * RWKV_W * 2)
           + 8 * tm * D_MODEL * 4 + 4 * tm * tm)
    return pl.pallas_call(
        _mix_out_kernel,
        grid=(n // tm,),
        in_specs=[half, half, half, half, full,
                  cst((1, RWKV_W)), cst((1, RWKV_W)), cst((GROUP_TILE, GROUP_TILE)), cst((D_MODEL, D_MODEL)),
                  cst((1, D_MODEL)), cst((D_MODEL, ROUTE_LANES)), cst((D_MODEL, ROUTE_LANES)),
                  cst((1, ROUTE_LANES)), cst((SUBLANES, ROUTE_LANES))],
        out_specs=[full, pl.BlockSpec((tm * SUBLANES, LANES), lambda i: (i, 0)), lanes, lanes,
                   cst((SUBLANES, ROUTE_LANES))],
        out_shape=[jax.ShapeDtypeStruct((n, D_MODEL), F32), jax.ShapeDtypeStruct((n * SUBLANES, LANES), F32),
                   jax.ShapeDtypeStruct((n, ROUTE_LANES), I32), jax.ShapeDtypeStruct((n, ROUTE_LANES), F32),
                   jax.ShapeDtypeStruct((SUBLANES, ROUTE_LANES), F32)],
        scratch_shapes=[pltpu.VMEM((SUBLANES, ROUTE_LANES), F32)],
        compiler_params=_params(1, est),
        name="mix_out_router",
    )(y2d, bonus2d, g2d, po2d, x2d, row(lnx_g), row(lnx_b), ones_bd, w_out_bf, row(ln2_g),
      rw_hi, rw_lo, rb_pad, carry_in)


def _tiles(ref, first_row, n_slots):
    if not isinstance(first_row, int):
        first_row = pl.multiple_of(first_row, SUBLANES)
    return ref.at[pl.ds(first_row, n_slots * SUBLANES), :]


def _dispatch_kernel(pad_start_ref, pad_len_ref, tail_ref, dest_ref, xa_ref, xb_ref, o_hbm, zbuf, sem, sem_fill,
                     *, n_main, n_last):
    i = pl.program_id(0)
    tm = xa_ref.shape[0] // SUBLANES
    blk = zbuf.shape[0] // SUBLANES

    @pl.when(i == 0)
    def _():
        zbuf[...] = jnp.zeros_like(zbuf)

        def per_expert(e, totals):
            start, n = pad_start_ref[e], pad_len_ref[e]
            n_groups = lax.shift_right_logical(n, 3)
            n_single = n - n_groups * SUBLANES

            def group(g, carry):
                pltpu.make_async_copy(_tiles(zbuf, 0, SUBLANES),
                                      _tiles(o_hbm, (start + g * SUBLANES) * SUBLANES, SUBLANES), sem_fill).start()
                return carry

            def single(r, carry):
                pltpu.make_async_copy(_tiles(zbuf, 0, 1),
                                      _tiles(o_hbm, (start + n_groups * SUBLANES + r) * SUBLANES, 1), sem_fill).start()
                return carry

            lax.fori_loop(0, n_groups, group, 0)
            lax.fori_loop(0, n_single, single, 0)
            return totals[0] + n_groups, totals[1] + n_single

        n_groups_filled, n_single_filled = lax.fori_loop(0, N_EXPERTS, per_expert, (0, 0))

        def wait_group(g, carry):
            pltpu.make_async_copy(_tiles(zbuf, 0, SUBLANES), _tiles(o_hbm, 0, SUBLANES), sem_fill).wait()
            return carry

        def wait_single(r, carry):
            pltpu.make_async_copy(_tiles(zbuf, 0, 1), _tiles(o_hbm, 0, 1), sem_fill).wait()
            return carry

        lax.fori_loop(0, n_groups_filled, wait_group, 0)
        lax.fori_loop(0, n_single_filled, wait_single, 0)

        def fill_block(b, carry):
            pltpu.make_async_copy(zbuf, _tiles(o_hbm, (tail_ref[0] + b * blk) * SUBLANES, blk), sem_fill).start()
            return carry

        def wait_block(b, carry):
            pltpu.make_async_copy(zbuf, _tiles(o_hbm, 0, blk), sem_fill).wait()
            return carry

        lax.fori_loop(0, tail_ref[1], fill_block, 0)
        lax.fori_loop(0, tail_ref[1], wait_block, 0)

    def scatter_rows(x_ref, n_rows):
        def issue(n, carry):
            src = _tiles(x_ref, n * SUBLANES, 1)
            for j in range(TOP_K):
                pltpu.make_async_copy(src, _tiles(o_hbm, dest_ref[n * TOP_K + j], 1), sem).start(priority=j % 2)
            return carry

        lax.fori_loop(0, n_rows, issue, 0, unroll=ISSUE_UNROLL)
        pltpu.make_async_copy(_tiles(o_hbm, 0, n_rows * TOP_K), _tiles(o_hbm, 0, n_rows * TOP_K), sem).wait()

    @pl.when(i < n_main)
    def _():
        scatter_rows(xa_ref, tm)

    @pl.when(i == n_main)
    def _():
        scatter_rows(xb_ref, n_last)


def _dispatch(pad_start, pad_len, tail, dest_rows, xn_a, xn_b, n_slots, tm):
    n_main = xn_a.shape[0] // (tm * SUBLANES)
    n_last = xn_b.shape[0] // SUBLANES
    assert n_last <= tm and n_slots >= tm * TOP_K
    grid_spec = pltpu.PrefetchScalarGridSpec(
        num_scalar_prefetch=3,
        grid=(n_main + 1,),
        in_specs=[pl.BlockSpec((tm * TOP_K,), lambda i, *_: (i,), memory_space=pltpu.SMEM),
                  pl.BlockSpec((tm * SUBLANES, LANES), lambda i, *_: (jnp.minimum(i, n_main - 1), 0)),
                  pl.BlockSpec((n_last * SUBLANES, LANES), lambda i, *_: (0, 0))],
        out_specs=pl.BlockSpec(memory_space=pl.ANY),
        scratch_shapes=[pltpu.VMEM((EXPERT_BLOCK * SUBLANES, LANES), F32), pltpu.SemaphoreType.DMA(()),
                        pltpu.SemaphoreType.DMA(())],
    )
    return pl.pallas_call(
        functools.partial(_dispatch_kernel, n_main=n_main, n_last=n_last),
        grid_spec=grid_spec,
        out_shape=jax.ShapeDtypeStruct((n_slots * SUBLANES, LANES), F32),
        compiler_params=_params(1, 2 * (tm + n_last) * D_MODEL * 4 + EXPERT_BLOCK * D_MODEL * 4),
        name="dispatch",
    )(pad_start, pad_len, tail, dest_rows, xn_a, xn_b)


def _expert_kernel(be_ref, nu_ref, xs_ref, wgu_ref, bgu_ref, wdn_ref, bdn_ref, o_ref, wgu_bf, wdn_bf):
    i = pl.program_id(0)
    used = i < nu_ref[0]
    changed = jnp.logical_or(i == 0, be_ref[i] != be_ref[jnp.maximum(i - 1, 0)])

    @pl.when(jnp.logical_and(used, changed))
    def _():
        wgu_bf[...] = wgu_ref[0].astype(BF16)
        wdn_bf[...] = wdn_ref[0].astype(BF16)

    @pl.when(used)
    def _():
        x = _load_row_tiles(xs_ref, xs_ref.shape[0] // SUBLANES)
        gu = _dot(x.astype(BF16), wgu_bf[...]) + bgu_ref[0]
        gate = jnp.minimum(gu[:, :D_EXPERT], SWIGLU_LIMIT)
        up = jnp.clip(gu[:, D_EXPERT:], -SWIGLU_LIMIT, SWIGLU_LIMIT)
        hmid = (up + 1.0) * (gate * jax.nn.sigmoid(SWIGLU_ALPHA * gate))
        _store_row_tiles(o_ref, _dot(hmid.astype(BF16), wdn_bf[...]) + bdn_ref[0])

    @pl.when(jnp.logical_not(used))
    def _():
        o_ref[...] = jnp.zeros_like(o_ref)


def _experts(block_e, n_used, xs, w_gu, b_gu, w_down, b_down):
    n_slots = xs.shape[0] // SUBLANES
    nb = n_slots // EXPERT_BLOCK
    blk = EXPERT_BLOCK
    slots = pl.BlockSpec((blk * SUBLANES, LANES), lambda i, be, nu: (i, 0))
    est = (2 * 2 * blk * D_MODEL * 4 + 2 * (D_MODEL * 2 * D_EXPERT + D_EXPERT * D_MODEL) * 4
           + (D_MODEL * 2 * D_EXPERT + D_EXPERT * D_MODEL) * 2 + 4 * blk * 2 * D_EXPERT * 4)
    grid_spec = pltpu.PrefetchScalarGridSpec(
        num_scalar_prefetch=2,
        grid=(nb,),
        in_specs=[slots,
                  pl.BlockSpec((1, D_MODEL, 2 * D_EXPERT), lambda i, be, nu: (be[i], 0, 0)),
                  pl.BlockSpec((1, 1, 2 * D_EXPERT), lambda i, be, nu: (be[i], 0, 0)),
                  pl.BlockSpec((1, D_EXPERT, D_MODEL), lambda i, be, nu: (be[i], 0, 0)),
                  pl.BlockSpec((1, 1, D_MODEL), lambda i, be, nu: (be[i], 0, 0))],
        out_specs=slots,
        scratch_shapes=[pltpu.VMEM((D_MODEL, 2 * D_EXPERT), BF16), pltpu.VMEM((D_EXPERT, D_MODEL), BF16)],
    )
    return pl.pallas_call(
        _expert_kernel,
        grid_spec=grid_spec,
        out_shape=jax.ShapeDtypeStruct((n_slots * SUBLANES, LANES), F32),
        compiler_params=_params(1, est),
        name="experts",
    )(block_e, n_used, xs, w_gu, b_gu.reshape(N_EXPERTS, 1, 2 * D_EXPERT), w_down,
      b_down.reshape(N_EXPERTS, 1, D_MODEL))


def _combine_kernel(dest_ref, dest_next_ref, ys_hbm, gate_ref, h_ref, p_ref, ln3_ref, pg_ref, pp_ref, fg_ref,
                    o_ref, gbuf, sem):
    i = pl.program_id(0)
    tm = h_ref.shape[0]
    slot = lax.rem(i, 2)

    def issue(d_ref, s):
        def body(n, carry):
            for j in range(TOP_K):
                pltpu.make_async_copy(_tiles(ys_hbm, d_ref[n * TOP_K + j], 1),
                                      _tiles(gbuf.at[s, j], n * SUBLANES, 1), sem.at[s]).start(priority=j % 2)
            return carry
        lax.fori_loop(0, tm, body, 0, unroll=ISSUE_UNROLL)

    @pl.when(i == 0)
    def _():
        issue(dest_ref, 0)

    @pl.when(i + 1 < pl.num_programs(0))
    def _():
        issue(dest_next_ref, 1 - slot)

    for j in range(TOP_K):
        pltpu.make_async_copy(gbuf.at[slot, j], gbuf.at[slot, j], sem.at[slot]).wait()

    gates = gate_ref[...]
    slabs = []
    for c in range(SUBLANES):
        acc = gates[:, 0:1] * gbuf[slot, 0, pl.ds(c, tm, stride=SUBLANES), :]
        for j in range(1, TOP_K):
            acc = acc + gates[:, j:j + 1] * gbuf[slot, j, pl.ds(c, tm, stride=SUBLANES), :]
        slabs.append(acc)
    h = h_ref[...] + jnp.concatenate(slabs, axis=1)
    gate = jax.nn.sigmoid(_dot(_rmsnorm_rows(h, ln3_ref[...]).astype(BF16), pg_ref[...]))
    h = h + gate * _dot(p_ref[...].astype(BF16), pp_ref[...])
    o_ref[...] = _rmsnorm_rows(h, fg_ref[...])


def _combine(dest_flat, ys, gates, h2d, p2d, ln3_g, ple_gate_bf, ple_proj_bf, final_g, tm):
    n = h2d.shape[0]
    nb = n // tm
    row = lambda x: x.reshape(1, -1)
    cst = lambda shape: pl.BlockSpec(shape, lambda i: (0,) * len(shape))
    est = (2 * TOP_K * tm * D_MODEL * 4 + 2 * (2 * tm * D_MODEL * 4 + tm * PLE_DIM * 4 + tm * ROUTE_LANES * 4)
           + 2 * (D_MODEL * D_MODEL * 2 + PLE_DIM * D_MODEL * 2) + 6 * tm * D_MODEL * 4)
    return pl.pallas_call(
        _combine_kernel,
        grid=(nb,),
        in_specs=[pl.BlockSpec((tm * TOP_K,), lambda i: (i,), memory_space=pltpu.SMEM),
                  pl.BlockSpec((tm * TOP_K,), lambda i: (jnp.minimum(i + 1, nb - 1),), memory_space=pltpu.SMEM),
                  pl.BlockSpec(memory_space=pl.ANY),
                  pl.BlockSpec((tm, ROUTE_LANES), lambda i: (i, 0)),
                  pl.BlockSpec((tm, D_MODEL), lambda i: (i, 0)),
                  pl.BlockSpec((tm, PLE_DIM), lambda i: (i, 0)),
                  cst((1, D_MODEL)), cst((D_MODEL, D_MODEL)), cst((PLE_DIM, D_MODEL)), cst((1, D_MODEL))],
        out_specs=pl.BlockSpec((tm, D_MODEL), lambda i: (i, 0)),
        out_shape=jax.ShapeDtypeStruct((n, D_MODEL), F32),
        scratch_shapes=[pltpu.VMEM((2, TOP_K, tm * SUBLANES, LANES), F32), pltpu.SemaphoreType.DMA((2,))],
        compiler_params=_params(1, est),
        name="combine_ple_norm",
    )(dest_flat, dest_flat, ys, gates, h2d, p2d, row(ln3_g), ple_gate_bf, ple_proj_bf, row(final_g))


def _tile(n, pref):
    t = min(n, pref)
    assert n % t == 0, (n, pref)
    return t


def _front(x, pos0, pool_ctx, shift_prev, wkv0, wts, carry_in, tm_mix):
    b, t, _ = x.shape
    n = b * t
    assert t >= POOL_CTX and t % SUBLANES == 0
    x2d = x.reshape(n, D_MODEL)
    u2d, zr2d = _in_proj(x2d, wts["ln1_g"], wts["w_in_bf"], _tile(n, 512))
    u = u2d.reshape(b, t, POOL_W)
    zr = zr2d.reshape(b, t, RWKV_PROJ)

    ctx16 = jnp.concatenate([jnp.zeros((b, POOL_HALO - POOL_CTX, POOL_W), F32), pool_ctx.astype(F32)], axis=1)
    pool_out = _pool_mixer(u, ctx16, wts["w_pool_bf"], wts["pool_scale"], pos0, _tile(t, 512))
    new_pool = u[:, -POOL_CTX:]

    sh8 = jnp.broadcast_to(shift_prev.astype(F32), (b, SUBLANES, RWKV_PROJ))
    r, lw, kx, v, na, kb, g, bonus = _rwkv_pre(
        zr, sh8, wts["mu_shift"], wts["w0"], wts["w2_bf"], wts["a0"], wts["a2_bf"], wts["g2_bf"],
        wts["k_k"], wts["k_a"], wts["r_k"], wts["ones_bd"], _tile(t, 256))
    new_shift = zr[:, -1:]

    t_pad = -(-t // CHUNK) * CHUNK
    seqs = (r, lw, kx, v, na, kb)
    if t_pad != t:
        seqs = tuple(jnp.pad(s, ((0, 0), (0, t_pad - t), (0, 0))) for s in seqs)
    y, s_fin = _wkv_scan(*seqs, _state_to_pairs(wkv0), _tile(t_pad, 8 * CHUNK))
    y = y[:, :t]
    new_wkv = _pairs_to_state(s_fin)

    h, xn, route, gates, counts = _mix_out(
        y.reshape(n, RWKV_W), bonus.reshape(n, RWKV_W), g.reshape(n, RWKV_W), pool_out.reshape(n, POOL_W),
        x2d, wts["lnx_g"], wts["lnx_b"], wts["ones_bd"], wts["w_out_bf"], wts["ln2_g"],
        wts["rw_hi"], wts["rw_lo"], wts["rb_pad"], carry_in, tm_mix)
    return dict(h=h, xn=xn, route=route, gates=gates, counts=counts,
                new_pool=new_pool, new_shift=new_shift, new_wkv=new_wkv)


def _prep_weights(i, ln1_g, w_in, mu_shift, w_pool, pool_scale, w0, w2, a0, a2, g2, k_k, k_a, r_k,
                  lnx_g, lnx_b, w_out, ln2_g, router_w, router_b, ln3_g, ple_gate, ple_proj):
    head = jnp.arange(GROUP_TILE) // HEAD_DIM
    ones_bd = (head[:, None] == head[None, :]).astype(BF16)
    rw_pad = jnp.pad(router_w[i].astype(F32), ((0, 0), (0, ROUTE_LANES - N_EXPERTS)))
    rw_hi = rw_pad.astype(BF16)
    rw_lo = (rw_pad - rw_hi.astype(F32)).astype(BF16)
    rb_pad = jnp.concatenate([router_b[i].astype(F32), jnp.full((ROUTE_LANES - N_EXPERTS,), -jnp.inf, F32)])
    return dict(
        ln1_g=ln1_g[i], w_in_bf=w_in[i].astype(BF16), mu_shift=mu_shift[i], w_pool_bf=w_pool[i].astype(BF16),
        pool_scale=pool_scale[i], w0=w0[i], w2_bf=w2[i].astype(BF16), a0=a0[i], a2_bf=a2[i].astype(BF16),
        g2_bf=g2[i].astype(BF16), k_k=k_k[i], k_a=k_a[i], r_k=r_k[i].reshape(RWKV_W), lnx_g=lnx_g[i],
        lnx_b=lnx_b[i], w_out_bf=w_out[i].astype(BF16), ln2_g=ln2_g[i], rw_hi=rw_hi, rw_lo=rw_lo,
        rb_pad=rb_pad.reshape(1, ROUTE_LANES), ones_bd=ones_bd, ln3_g=ln3_g[i],
        ple_gate_bf=ple_gate[i].astype(BF16), ple_proj_bf=ple_proj[i].astype(BF16))


def _layer_pair(hp, hs, pp, ps, pool_ctx, shift_prev, wkv0, wts, w_gu, b_gu, w_down, b_down, final_g):
    bp, tp, _ = hp.shape
    bs, ts, _ = hs.shape
    np_, ns = bp * tp, bs * ts
    tm_p = _tile(np_, 256)
    tm_s = _tile(ns, 128)
    zero_carry = jnp.zeros((SUBLANES, ROUTE_LANES), F32)
    fp = _front(hp, 0, jnp.zeros((bp, POOL_CTX, POOL_W), F32), jnp.zeros((bp, 1, RWKV_PROJ), F32),
                jnp.zeros((bp, N_HEADS, HEAD_DIM, HEAD_DIM), F32), wts, zero_carry, tm_p)
    fs = _front(hs, PAST_LEN, pool_ctx, shift_prev, wkv0, wts, fp["counts"], tm_s)

    counts = fs["counts"][0, :N_EXPERTS].astype(I32)
    padded = (counts + EXPERT_BLOCK - 1) // EXPERT_BLOCK * EXPERT_BLOCK
    pad_end = jnp.cumsum(padded)
    start_pad = pad_end - padded
    n_blocks = -(-(np_ + ns) * TOP_K // EXPERT_BLOCK) + N_EXPERTS
    n_slots = n_blocks * EXPERT_BLOCK
    n_used = (pad_end[-1] // EXPERT_BLOCK).astype(I32)
    block_start = jnp.arange(n_blocks, dtype=I32) * EXPERT_BLOCK
    block_e = jnp.minimum(jnp.sum((block_start[:, None] >= pad_end[None, :]).astype(I32), axis=1), N_EXPERTS - 1)
    last_e = block_e[jnp.maximum(n_used - 1, 0)]
    block_e = jnp.where(jnp.arange(n_blocks) < n_used, block_e, last_e)

    def dest_of(route):
        idx = route[:, :TOP_K]
        rank = route[:, TOP_K:2 * TOP_K]
        start = jnp.sum(jnp.where(idx[..., None] == jnp.arange(N_EXPERTS, dtype=I32), start_pad, 0), axis=-1)
        return ((start + rank) * SUBLANES).reshape(-1).astype(I32)

    dest_p, dest_s = dest_of(fp["route"]), dest_of(fs["route"])
    tm_d = _tile(np_, 512)
    dest_all = jnp.concatenate([dest_p, dest_s, jnp.zeros(((tm_d - ns) * TOP_K,), I32)])
    tail = jnp.stack([n_used * EXPERT_BLOCK, n_blocks - n_used]).astype(I32)
    xs = _dispatch(start_pad + counts, padded - counts, tail, dest_all, fp["xn"], fs["xn"], n_slots, tm_d)
    ys = _experts(block_e, n_used.reshape(1), xs, w_gu, b_gu, w_down, b_down)

    outs = []
    for f, dest, p, tm in ((fp, dest_p, pp, tm_p), (fs, dest_s, ps, tm_s)):
        n = f["h"].shape[0]
        outs.append(_combine(dest, ys, f["gates"], f["h"], p.reshape(n, PLE_DIM).astype(F32), wts["ln3_g"],
                             wts["ple_gate_bf"], wts["ple_proj_bf"], final_g, tm))
    return outs[0].reshape(hp.shape), outs[1].reshape(hs.shape), fp, fs


def kernel(x_prompt, x_sample, p_prompt, p_sample, cache_pool, state_shift, state_wkv, ln1_g, w_in, mu_shift, w_pool, pool_scale, w0, w2, a0, a2, g2, k_k, k_a, r_k, lnx_g, lnx_b, w_out, ln2_g, router_w, router_b, w_gu, b_gu, w_down, b_down, ln3_g, ple_gate, ple_proj, final_g):
    assert ln1_g.shape[0] == 1, "single-layer kernel"
    i = 0
    wts = _prep_weights(i, ln1_g, w_in, mu_shift, w_pool, pool_scale, w0, w2, a0, a2, g2, k_k, k_a, r_k,
                        lnx_g, lnx_b, w_out, ln2_g, router_w, router_b, ln3_g, ple_gate, ple_proj)
    y_p, y_s, fp, fs = _layer_pair(x_prompt, x_sample, p_prompt[i], p_sample[i], cache_pool[i], state_shift[i],
                                   state_wkv[i], wts, w_gu[i], b_gu[i], w_down[i], b_down[i], final_g)
    stack = lambda a: a[None]
    return (y_p, y_s,
            stack(fp["new_pool"]), stack(fp["new_shift"]), stack(fp["new_wkv"]),
            stack(fs["new_pool"]), stack(fs["new_shift"]), stack(fs["new_wkv"]))
```

```python
import functools

import jax
import jax.numpy as jnp
from jax import lax
from jax.experimental import pallas as pl
from jax.experimental.pallas import tpu as pltpu

F32 = jnp.float32
BF16 = jnp.bfloat16
I32 = jnp.int32

D_MODEL = 1024
POOL_W = 512
POOL_WINDOWS = (2, 4, 8, 16)
POOL_GW = 128
POOL_CTX = 15
POOL_HALO = 16
RWKV_W = 512
HEAD_DIM = 64
N_HEADS = 8
N_PAIRS = N_HEADS // 2
PAIR_W = 2 * HEAD_DIM
DECAY_LORA = 64
ICLR_LORA = 64
GATE_LORA = 128
RWKV_PROJ = 3 * RWKV_W + DECAY_LORA + ICLR_LORA + GATE_LORA
IN_PROJ = POOL_W + RWKV_PROJ
LNX_EPS = 64e-5
N_EXPERTS = 32
TOP_K = 4
D_EXPERT = 1024
SWIGLU_LIMIT = 7.0
SWIGLU_ALPHA = 1.702
PLE_DIM = 256
RMS_EPS = 1e-6
PAST_LEN = 2048

CHUNK = 64
EXPERT_BLOCK = 512
ROUTE_LANES = 128
SUBLANES = 8
LANES = 128
assert D_MODEL == SUBLANES * LANES
PREP_ROWS = 256
ISSUE_UNROLL = 8
GROUP_TILE = 256
EXP_M05 = 0.6065306597126334
V7X_VMEM_BYTES = 64 * 1024 * 1024

NT_DIMS = (((1,), (1,)), ((), ()))
TN_DIMS = (((0,), (0,)), ((), ()))


def _vmem_limit(est_bytes):
    return int(min(est_bytes * 3 // 2 + (4 << 20), V7X_VMEM_BYTES - (8 << 20)))


def _params(n_axes, est_bytes):
    return pltpu.CompilerParams(dimension_semantics=("arbitrary",) * n_axes,
                                vmem_limit_bytes=_vmem_limit(est_bytes))


def _dot(a, b):
    return jnp.dot(a, b, preferred_element_type=F32)


def _split2(x):
    hi = x.astype(BF16)
    lo = (x - hi.astype(F32)).astype(BF16)
    return hi, lo


def _split3(x):
    hi = x.astype(BF16)
    r1 = x - hi.astype(F32)
    mid = r1.astype(BF16)
    lo = (r1 - mid.astype(F32)).astype(BF16)
    return hi, mid, lo


def _group_sum(x, ones_bd):
    hi, lo = _split2(x)
    slabs = [_dot(hi[:, s:s + GROUP_TILE], ones_bd) + _dot(lo[:, s:s + GROUP_TILE], ones_bd)
             for s in range(0, x.shape[1], GROUP_TILE)]
    return jnp.concatenate(slabs, axis=1)


def _store_row_tiles(ref, value):
    n = value.shape[0]
    for c in range(SUBLANES):
        ref[pl.ds(c, n, stride=SUBLANES), :] = value[:, c * LANES:(c + 1) * LANES]


def _load_row_tiles(ref, n):
    return jnp.concatenate([ref[pl.ds(c, n, stride=SUBLANES), :] for c in range(SUBLANES)], axis=1)


def _rmsnorm_rows(x, g):
    ms = jnp.mean(x * x, axis=-1, keepdims=True)
    return (x * lax.rsqrt(ms + RMS_EPS)) * g


def _in_proj_kernel(x_ref, g_ref, w_ref, u_ref, zr_ref):
    xn = _rmsnorm_rows(x_ref[...], g_ref[...])
    z = _dot(xn.astype(BF16), w_ref[...])
    u_ref[...] = z[:, :POOL_W]
    zr_ref[...] = z[:, POOL_W:]


def _in_proj(x2d, ln1_g, w_in_bf, tm):
    n = x2d.shape[0]
    est = 2 * tm * D_MODEL * 4 + 2 * D_MODEL * IN_PROJ * 2 + 3 * tm * IN_PROJ * 4
    return pl.pallas_call(
        _in_proj_kernel,
        grid=(n // tm,),
        in_specs=[pl.BlockSpec((tm, D_MODEL), lambda i: (i, 0)),
                  pl.BlockSpec((1, D_MODEL), lambda i: (0, 0)),
                  pl.BlockSpec((D_MODEL, IN_PROJ), lambda i: (0, 0))],
        out_specs=[pl.BlockSpec((tm, POOL_W), lambda i: (i, 0)),
                   pl.BlockSpec((tm, RWKV_PROJ), lambda i: (i, 0))],
        out_shape=[jax.ShapeDtypeStruct((n, POOL_W), F32),
                   jax.ShapeDtypeStruct((n, RWKV_PROJ), F32)],
        compiler_params=_params(1, est),
        name="in_proj",
    )(x2d, ln1_g.reshape(1, D_MODEL), w_in_bf)


def _pool_kernel(u_ref, halo_ref, ctx_ref, wp_ref, ps_ref, o_ref, buf, *, pos0):
    t = pl.program_id(1)
    tt = u_ref.shape[1]
    buf[0:POOL_HALO, :] = jnp.where(t == 0, ctx_ref[0], halo_ref[0])
    buf[POOL_HALO:, :] = u_ref[0]
    pos = lax.broadcasted_iota(I32, (tt, 1), 0) + (t * tt + pos0)
    outs = []
    for gi, w in enumerate(POOL_WINDOWS):
        sl = slice(gi * POOL_GW, (gi + 1) * POOL_GW)
        cur = buf[POOL_HALO:POOL_HALO + tt, sl]
        acc = cur
        for j in range(1, w):
            acc = acc + buf[POOL_HALO - j:POOL_HALO - j + tt, sl]
        cnt = jnp.minimum(w, pos + 1).astype(F32)
        pooled = acc / cnt - cur
        outs.append(_dot(pooled.astype(BF16), wp_ref[gi]))
    o_ref[0] = (jnp.concatenate(outs, axis=-1) * ps_ref[...]).astype(o_ref.dtype)


def _pool_mixer(u, ctx16, w_pool_bf, pool_scale, pos0, tt):
    b, t, _ = u.shape
    hb = tt // POOL_HALO
    est = 2 * tt * POOL_W * 4 * 3 + (tt + POOL_HALO) * POOL_W * 4
    return pl.pallas_call(
        functools.partial(_pool_kernel, pos0=pos0),
        grid=(b, t // tt),
        in_specs=[pl.BlockSpec((1, tt, POOL_W), lambda i, j: (i, j, 0)),
                  pl.BlockSpec((1, POOL_HALO, POOL_W), lambda i, j: (i, jnp.maximum(j * hb - 1, 0), 0)),
                  pl.BlockSpec((1, POOL_HALO, POOL_W), lambda i, j: (i, 0, 0)),
                  pl.BlockSpec((len(POOL_WINDOWS), POOL_GW, POOL_GW), lambda i, j: (0, 0, 0)),
                  pl.BlockSpec((1, POOL_W), lambda i, j: (0, 0))],
        out_specs=pl.BlockSpec((1, tt, POOL_W), lambda i, j: (i, j, 0)),
        out_shape=jax.ShapeDtypeStruct((b, t, POOL_W), BF16),
        scratch_shapes=[pltpu.VMEM((tt + POOL_HALO, POOL_W), F32)],
        compiler_params=_params(2, est),
        name="pool_mixer",
    )(u, u, ctx16, w_pool_bf, pool_scale.reshape(1, POOL_W))


def _rwkv_prepare(zr, prev, mu, w0, w2, a0, a2, g2, k_k, k_a, r_k, ones_bd):
    zs = zr + (prev - zr) * mu
    o1, o2, o3 = RWKV_W, 2 * RWKV_W, 3 * RWKV_W
    o4, o5 = o3 + DECAY_LORA, o3 + DECAY_LORA + ICLR_LORA
    r, k, v = zs[:, :o1], zs[:, o1:o2], zs[:, o2:o3]
    wd, ad, gd = zs[:, o3:o4], zs[:, o4:o5], zs[:, o5:]
    wz = w0 + _dot(jnp.tanh(wd).astype(BF16), w2)
    lw = -EXP_M05 * jax.nn.sigmoid(wz)
    a = jax.nn.sigmoid(a0 + _dot(ad.astype(BF16), a2))
    g = _dot(jax.nn.sigmoid(gd).astype(BF16), g2)
    kk = k * k_k
    kk = kk / jnp.maximum(jnp.sqrt(_group_sum(kk * kk, ones_bd)), 1e-12)
    kx = k * (1.0 + (a - 1.0) * k_a)
    bonus = _group_sum(r * kx * r_k, ones_bd) * v
    return r, lw, kx, v, -kk, kk * a, g, bonus


def _pair_blockdiag(z, left):
    return jnp.concatenate([jnp.where(left, z, 0.0), jnp.where(left, 0.0, z)], axis=0)


def _wkv_kernel(zr_ref, halo_ref, sh_ref, mu_ref, w0_ref, w2_ref, a0_ref, a2_ref, g2_ref, kk_ref, ka_ref, rk_ref,
                ones_ref, lg_ref, lb_ref, s0_ref, o_ref, sf_ref,
                buf, r_ref, lw_ref, kx_ref, v_ref, na_ref, kb_ref, g_ref, bonus_ref, y_ref, s_scr, *, t_valid):
    t = pl.program_id(1)
    tt = zr_ref.shape[1]
    c = CHUNK
    rows_per_pass = min(tt, PREP_ROWS)

    @pl.when(t == 0)
    def _():
        s_scr[...] = s0_ref[0]

    buf[0:SUBLANES, :] = jnp.where(t == 0, sh_ref[0], halo_ref[0])
    buf[SUBLANES:, :] = zr_ref[0]
    ones_bd = ones_ref[...]
    for r0 in range(0, tt, rows_per_pass):
        rows = slice(r0, r0 + rows_per_pass)
        vals = _rwkv_prepare(zr_ref[0, rows, :], buf[SUBLANES - 1 + r0:SUBLANES - 1 + r0 + rows_per_pass, :],
                             mu_ref[...], w0_ref[...], w2_ref[...], a0_ref[...], a2_ref[...], g2_ref[...],
                             kk_ref[...], ka_ref[...], rk_ref[...], ones_bd)
        if t_valid is not None:
            live = lax.broadcasted_iota(I32, (rows_per_pass, 1), 0) + (t * tt + r0) < t_valid
            vals = tuple(jnp.where(live, x, 0.0) for x in vals)
        for ref, x in zip((r_ref, lw_ref, kx_ref, v_ref, na_ref, kb_ref, g_ref, bonus_ref), vals):
            ref[rows, :] = x

    lane = lax.broadcasted_iota(I32, (c, 2 * c), 1)
    rowi = lax.broadcasted_iota(I32, (c, 2 * c), 0)
    jm = lane & (c - 1)
    strict = jm < rowi
    incl = jm <= rowi
    eye_pair = jnp.where(jm == rowi, 1.0, 0.0).astype(F32)
    left = lax.broadcasted_iota(I32, (c, PAIR_W), 1) < HEAD_DIM
    rr = lax.broadcasted_iota(I32, (c, c), 0)
    cc = lax.broadcasted_iota(I32, (c, c), 1)
    tri_incl_bf = jnp.where(cc <= rr, 1.0, 0.0).astype(BF16)
    br = lax.broadcasted_iota(I32, (PAIR_W, PAIR_W), 0) < HEAD_DIM
    bc = lax.broadcasted_iota(I32, (PAIR_W, PAIR_W), 1) < HEAD_DIM
    bd_mask = br == bc
    bf = lambda x: x.astype(BF16)
    bd = lambda z: _pair_blockdiag(z, left)
    n_chunks = tt // c

    cums = []
    for ci in range(n_chunks):
        h3 = _split3(lw_ref[ci * c:(ci + 1) * c, :])
        cums.append(_dot(tri_incl_bf, h3[0]) + _dot(tri_incl_bf, h3[1]) + _dot(tri_incl_bf, h3[2]))

    chains = []
    for ci in range(n_chunks):
        rows = slice(ci * c, (ci + 1) * c)
        for p in range(N_PAIRS):
            cols = slice(p * PAIR_W, (p + 1) * PAIR_W)
            cum = cums[ci][:, cols]
            lw, kx, kb = lw_ref[rows, cols], kx_ref[rows, cols], kb_ref[rows, cols]
            tot = cum[c - 1:c, :]
            p_end = jnp.exp(tot - cum)
            inv_p = jnp.exp(-cum)
            chains.append(dict(
                ci=ci, p=p, v=v_ref[rows, cols], p_c=jnp.exp(tot),
                a_t=na_ref[rows, cols] * jnp.exp(cum - lw), r_t=r_ref[rows, cols] * jnp.exp(cum),
                b_t=kb * inv_p, k_t=kx * inv_p, b_h=kb * p_end, k_h=kx * p_end))

    for d in chains:
        lhs = bf(jnp.concatenate([d["a_t"], d["r_t"]], axis=0))
        rhs = bf(jnp.concatenate([bd(d["b_t"]), bd(d["k_t"])], axis=0))
        sc = lax.dot_general(lhs, rhs, NT_DIMS, preferred_element_type=F32)
        d["a_ab"] = jnp.where(strict, sc[:c, :2 * c], 0.0)
        d["a_k"] = jnp.concatenate([jnp.where(strict, sc[:c, 2 * c:], 0.0),
                                    jnp.where(incl, sc[c:, 2 * c:], 0.0)], axis=0)
        d["a_rb"] = jnp.where(incl, sc[c:, :2 * c], 0.0)

    for d in chains:
        d["x"] = _dot(bf(d["a_ab"]), bf(bd(d["a_ab"])))
        d["t"] = eye_pair + d["a_ab"]
    for _ in range(c.bit_length() - 3):
        for d in chains:
            st = _dot(bf(jnp.concatenate([d["x"], d["t"]], axis=0)), bf(bd(d["x"])))
            d["x"] = st[:c]
            d["t"] = d["t"] + st[c:]
    for d in chains:
        d["t"] = d["t"] + _dot(bf(d["t"]), bf(bd(d["x"])))

    for d in chains:
        wv = _dot(bf(d["a_k"]), bf(bd(d["v"])))
        d["w1"], d["rkv"] = wv[:c], wv[c:]
    for d in chains:
        tu = _dot(bf(d["t"]), bf(jnp.concatenate([bd(d["w1"]), bd(d["a_t"])], axis=1)))
        d["u_loc"], d["a_tt"] = tu[:, :PAIR_W], tu[:, PAIR_W:]
    for d in chains:
        ar = _dot(bf(d["a_rb"]), bf(jnp.concatenate([bd(d["u_loc"]), bd(d["a_tt"])], axis=1)))
        d["y_loc"] = ar[:, :PAIR_W] + d["rkv"]
        d["r_g"] = d["r_t"] + ar[:, PAIR_W:]
    for d in chains:
        lhs = jnp.concatenate([jnp.concatenate([d["v"], jnp.zeros_like(d["v"])], axis=1),
                               jnp.concatenate([d["u_loc"], d["a_tt"]], axis=1)], axis=0)
        rhs = jnp.concatenate([d["k_h"], d["b_h"]], axis=0)
        dp = lax.dot_general(bf(lhs), bf(rhs), TN_DIMS, preferred_element_type=F32)
        d["d_loc"] = jnp.where(bd_mask, dp[:PAIR_W], 0.0)
        d["phi"] = bf(jnp.where(bd_mask, dp[PAIR_W:], 0.0))

    state = [s_scr[p] for p in range(N_PAIRS)]
    for d in chains:
        p = d["p"]
        s_hi, s_lo = _split2(state[p])
        y = lax.dot_general(bf(d["r_g"]), s_hi, NT_DIMS, preferred_element_type=F32) + d["y_loc"]
        y_ref[d["ci"] * c:(d["ci"] + 1) * c, p * PAIR_W:(p + 1) * PAIR_W] = y
        state[p] = state[p] * d["p_c"] + (_dot(s_hi, d["phi"]) + _dot(s_lo, d["phi"])) + d["d_loc"]
    for p in range(N_PAIRS):
        s_scr[p] = state[p]

    @pl.when(t == pl.num_programs(1) - 1)
    def _():
        sf_ref[0] = s_scr[...]

    inv_n = 1.0 / HEAD_DIM
    for r0 in range(0, tt, rows_per_pass):
        rows = slice(r0, r0 + rows_per_pass)
        y = y_ref[rows, :]
        mu = _group_sum(y, ones_bd) * inv_n
        dlt = y - mu
        var = _group_sum(dlt * dlt, ones_bd) * inv_n
        yn = dlt * lax.rsqrt(var + LNX_EPS) * lg_ref[...] + lb_ref[...]
        o_ref[0, rows, :] = ((yn + bonus_ref[rows, :]) * g_ref[rows, :]).astype(o_ref.dtype)


def _rwkv_mixer(zr, sh8, s0_pair, wts, tt, t_valid):
    b, t, _ = zr.shape
    assert CHUNK == HEAD_DIM and tt % CHUNK == 0
    hb = tt // SUBLANES
    row = lambda x: x.reshape(1, -1)
    cst = lambda shape: pl.BlockSpec(shape, lambda i, j: (0,) * len(shape))
    seq = pl.BlockSpec((1, tt, RWKV_W), lambda i, j: (i, j, 0))
    st = pl.BlockSpec((1, N_PAIRS, PAIR_W, PAIR_W), lambda i, j: (i, 0, 0, 0))
    est = 3 * tt * RWKV_PROJ * 4 + 10 * tt * RWKV_W * 4 + 5 * N_PAIRS * PAIR_W * PAIR_W * 4 + (8 << 20)
    return pl.pallas_call(
        functools.partial(_wkv_kernel, t_valid=t_valid),
        grid=(b, t // tt),
        in_specs=[pl.BlockSpec((1, tt, RWKV_PROJ), lambda i, j: (i, j, 0)),
                  pl.BlockSpec((1, SUBLANES, RWKV_PROJ), lambda i, j: (i, jnp.maximum(j * hb - 1, 0), 0)),
                  pl.BlockSpec((1, SUBLANES, RWKV_PROJ), lambda i, j: (i, 0, 0)),
                  cst((1, RWKV_PROJ)), cst((1, RWKV_W)), cst((DECAY_LORA, RWKV_W)),
                  cst((1, RWKV_W)), cst((ICLR_LORA, RWKV_W)), cst((GATE_LORA, RWKV_W)),
                  cst((1, RWKV_W)), cst((1, RWKV_W)), cst((1, RWKV_W)), cst((GROUP_TILE, GROUP_TILE)),
                  cst((1, RWKV_W)), cst((1, RWKV_W)), st],
        out_specs=[seq, st],
        out_shape=[jax.ShapeDtypeStruct((b, t, RWKV_W), BF16),
                   jax.ShapeDtypeStruct((b, N_PAIRS, PAIR_W, PAIR_W), F32)],
        scratch_shapes=([pltpu.VMEM((tt + SUBLANES, RWKV_PROJ), F32)] + [pltpu.VMEM((tt, RWKV_W), F32)] * 9
                        + [pltpu.VMEM((N_PAIRS, PAIR_W, PAIR_W), F32)]),
        compiler_params=_params(2, est),
        name="rwkv_mixer",
    )(zr, zr, sh8, row(wts["mu_shift"]), row(wts["w0"]), wts["w2_bf"], row(wts["a0"]), wts["a2_bf"], wts["g2_bf"],
      row(wts["k_k"]), row(wts["k_a"]), row(wts["r_k"]), wts["ones_bd"], row(wts["lnx_g"]), row(wts["lnx_b"]),
      s0_pair)


def _state_to_pairs(s):
    b = s.shape[0]
    s = s.astype(F32).reshape(b, N_PAIRS, 2, HEAD_DIM, HEAD_DIM)
    z = jnp.zeros((b, N_PAIRS, HEAD_DIM, HEAD_DIM), F32)
    return jnp.concatenate([jnp.concatenate([s[:, :, 0], z], axis=-1),
                            jnp.concatenate([z, s[:, :, 1]], axis=-1)], axis=-2)


def _pairs_to_state(sp):
    b = sp.shape[0]
    s = jnp.stack([sp[:, :, :HEAD_DIM, :HEAD_DIM], sp[:, :, HEAD_DIM:, HEAD_DIM:]], axis=2)
    return s.reshape(b, N_HEADS, HEAD_DIM, HEAD_DIM)


def _mix_out_kernel(rw_ref, po_ref, x_ref, wo_ref, ln2_ref, rw_hi_ref, rw_lo_ref, rb_ref, cin_ref,
                    h_o, xn_o, route_o, gate_o, cnt_o, carry):
    i = pl.program_id(0)
    tm = x_ref.shape[0]

    @pl.when(i == 0)
    def _():
        carry[...] = cin_ref[...]

    mix = _dot(po_ref[...], wo_ref[:POOL_W, :]) + _dot(rw_ref[...], wo_ref[POOL_W:, :])
    h = x_ref[...] + mix
    h_o[...] = h
    xn = _rmsnorm_rows(h, ln2_ref[...])
    _store_row_tiles(xn_o, xn)

    x_hi, x_lo = _split2(xn)
    logits = (_dot(x_hi, rw_hi_ref[...]) + _dot(x_hi, rw_lo_ref[...]) + _dot(x_lo, rw_hi_ref[...])
              + rb_ref[...])
    lane = lax.broadcasted_iota(I32, (tm, ROUTE_LANES), 1)
    lane_f = lane.astype(F32)
    vals, idxs, hots = [], [], []
    work = logits
    for _ in range(TOP_K):
        m = jnp.max(work, axis=-1, keepdims=True)
        idx = jnp.min(jnp.where(work == m, lane_f, float(ROUTE_LANES)), axis=-1, keepdims=True)
        hit = lane_f == idx
        vals.append(m)
        idxs.append(idx)
        hots.append(jnp.where(hit, 1.0, 0.0).astype(F32))
        work = jnp.where(hit, -jnp.inf, work)
    exps = [jnp.exp(vv - vals[0]) for vv in vals]
    den = exps[0] + exps[1] + exps[2] + exps[3]

    hot_all = hots[0] + hots[1] + hots[2] + hots[3]
    rr = lax.broadcasted_iota(I32, (tm, tm), 0)
    cc = lax.broadcasted_iota(I32, (tm, tm), 1)
    tri_strict = jnp.where(cc < rr, 1.0, 0.0).astype(BF16)
    prefix = _dot(tri_strict, hot_all.astype(BF16)) + carry[0:1, :]
    route = jnp.zeros((tm, ROUTE_LANES), F32)
    gates = jnp.zeros((tm, ROUTE_LANES), F32)
    for j in range(TOP_K):
        rank = jnp.sum(hots[j] * prefix, axis=-1, keepdims=True)
        route = jnp.where(lane == j, idxs[j], route)
        route = jnp.where(lane == TOP_K + j, rank, route)
        gates = jnp.where(lane == j, exps[j] / den, gates)
    route_o[...] = route.astype(I32)
    gate_o[...] = gates
    new_carry = carry[...] + jnp.sum(hot_all, axis=0, keepdims=True)
    carry[...] = new_carry
    cnt_o[...] = new_carry


def _mix_out(rw2d, po2d, x2d, w_out_bf, ln2_g, rw_hi, rw_lo, rb_pad, carry_in, tm):
    n = x2d.shape[0]
    row = lambda x: x.reshape(1, -1)
    cst = lambda shape: pl.BlockSpec(shape, lambda i: (0,) * len(shape))
    half = pl.BlockSpec((tm, RWKV_W), lambda i: (i, 0))
    full = pl.BlockSpec((tm, D_MODEL), lambda i: (i, 0))
    lanes = pl.BlockSpec((tm, ROUTE_LANES), lambda i: (i, 0))
    est = (2 * (2 * tm * RWKV_W * 2 + 3 * tm * D_MODEL * 4) + 2 * D_MODEL * D_MODEL * 2
           + 8 * tm * D_MODEL * 4 + 4 * tm * tm)
    return pl.pallas_call(
        _mix_out_kernel,
        grid=(n // tm,),
        in_specs=[half, half, full, cst((D_MODEL, D_MODEL)),
                  cst((1, D_MODEL)), cst((D_MODEL, ROUTE_LANES)), cst((D_MODEL, ROUTE_LANES)),
                  cst((1, ROUTE_LANES)), cst((SUBLANES, ROUTE_LANES))],
        out_specs=[full, pl.BlockSpec((tm * SUBLANES, LANES), lambda i: (i, 0)), lanes, lanes,
                   cst((SUBLANES, ROUTE_LANES))],
        out_shape=[jax.ShapeDtypeStruct((n, D_MODEL), F32), jax.ShapeDtypeStruct((n * SUBLANES, LANES), F32),
                   jax.ShapeDtypeStruct((n, ROUTE_LANES), I32), jax.ShapeDtypeStruct((n, ROUTE_LANES), F32),
                   jax.ShapeDtypeStruct((SUBLANES, ROUTE_LANES), F32)],
        scratch_shapes=[pltpu.VMEM((SUBLANES, ROUTE_LANES), F32)],
        compiler_params=_params(1, est),
        name="mix_out_router",
    )(rw2d, po2d, x2d, w_out_bf, row(ln2_g), rw_hi, rw_lo, rb_pad, carry_in)


def _tiles(ref, first_row, n_slots):
    if not isinstance(first_row, int):
        first_row = pl.multiple_of(first_row, SUBLANES)
    return ref.at[pl.ds(first_row, n_slots * SUBLANES), :]


def _dispatch_kernel(pad_start_ref, pad_len_ref, tail_ref, dest_ref, xa_ref, xb_ref, o_hbm, zbuf, sem, sem_fill,
                     *, n_main, n_last):
    i = pl.program_id(0)
    tm = xa_ref.shape[0] // SUBLANES
    blk = zbuf.shape[0] // SUBLANES

    @pl.when(i == 0)
    def _():
        zbuf[...] = jnp.zeros_like(zbuf)

        def per_expert(e, totals):
            start, n = pad_start_ref[e], pad_len_ref[e]
            n_groups = lax.shift_right_logical(n, 3)
            n_single = n - n_groups * SUBLANES

            def group(g, carry):
                pltpu.make_async_copy(_tiles(zbuf, 0, SUBLANES),
                                      _tiles(o_hbm, (start + g * SUBLANES) * SUBLANES, SUBLANES), sem_fill).start()
                return carry

            def single(r, carry):
                pltpu.make_async_copy(_tiles(zbuf, 0, 1),
                                      _tiles(o_hbm, (start + n_groups * SUBLANES + r) * SUBLANES, 1), sem_fill).start()
                return carry

            lax.fori_loop(0, n_groups, group, 0)
            lax.fori_loop(0, n_single, single, 0)
            return totals[0] + n_groups, totals[1] + n_single

        n_groups_filled, n_single_filled = lax.fori_loop(0, N_EXPERTS, per_expert, (0, 0))

        def wait_group(g, carry):
            pltpu.make_async_copy(_tiles(zbuf, 0, SUBLANES), _tiles(o_hbm, 0, SUBLANES), sem_fill).wait()
            return carry

        def wait_single(r, carry):
            pltpu.make_async_copy(_tiles(zbuf, 0, 1), _tiles(o_hbm, 0, 1), sem_fill).wait()
            return carry

        lax.fori_loop(0, n_groups_filled, wait_group, 0)
        lax.fori_loop(0, n_single_filled, wait_single, 0)

        def fill_block(b, carry):
            pltpu.make_async_copy(zbuf, _tiles(o_hbm, (tail_ref[0] + b * blk) * SUBLANES, blk), sem_fill).start()
            return carry

        def wait_block(b, carry):
            pltpu.make_async_copy(zbuf, _tiles(o_hbm, 0, blk), sem_fill).wait()
            return carry

        lax.fori_loop(0, tail_ref[1], fill_block, 0)
        lax.fori_loop(0, tail_ref[1], wait_block, 0)

    def scatter_rows(x_ref, n_rows):
        def issue(n, carry):
            src = _tiles(x_ref, n * SUBLANES, 1)
            for j in range(TOP_K):
                pltpu.make_async_copy(src, _tiles(o_hbm, dest_ref[n * TOP_K + j], 1), sem).start(priority=j % 2)
            return carry

        lax.fori_loop(0, n_rows, issue, 0, unroll=ISSUE_UNROLL)
        pltpu.make_async_copy(_tiles(o_hbm, 0, n_rows * TOP_K), _tiles(o_hbm, 0, n_rows * TOP_K), sem).wait()

    @pl.when(i < n_main)
    def _():
        scatter_rows(xa_ref, tm)

    @pl.when(i == n_main)
    def _():
        scatter_rows(xb_ref, n_last)


def _dispatch(pad_start, pad_len, tail, dest_rows, xn_a, xn_b, n_slots, tm):
    n_main = xn_a.shape[0] // (tm * SUBLANES)
    n_last = xn_b.shape[0] // SUBLANES
    assert n_last <= tm and n_slots >= tm * TOP_K
    grid_spec = pltpu.PrefetchScalarGridSpec(
        num_scalar_prefetch=3,
        grid=(n_main + 1,),
        in_specs=[pl.BlockSpec((tm * TOP_K,), lambda i, *_: (i,), memory_space=pltpu.SMEM),
                  pl.BlockSpec((tm * SUBLANES, LANES), lambda i, *_: (jnp.minimum(i, n_main - 1), 0)),
                  pl.BlockSpec((n_last * SUBLANES, LANES), lambda i, *_: (0, 0))],
        out_specs=pl.BlockSpec(memory_space=pl.ANY),
        scratch_shapes=[pltpu.VMEM((EXPERT_BLOCK * SUBLANES, LANES), F32), pltpu.SemaphoreType.DMA(()),
                        pltpu.SemaphoreType.DMA(())],
    )
    return pl.pallas_call(
        functools.partial(_dispatch_kernel, n_main=n_main, n_last=n_last),
        grid_spec=grid_spec,
        out_shape=jax.ShapeDtypeStruct((n_slots * SUBLANES, LANES), F32),
        compiler_params=_params(1, 2 * (tm + n_last) * D_MODEL * 4 + EXPERT_BLOCK * D_MODEL * 4),
        name="dispatch",
    )(pad_start, pad_len, tail, dest_rows, xn_a, xn_b)


def _expert_kernel(be_ref, nu_ref, xs_ref, wgu_ref, bgu_ref, wdn_ref, bdn_ref, o_ref, wgu_bf, wdn_bf):
    i = pl.program_id(0)
    used = i < nu_ref[0]
    changed = jnp.logical_or(i == 0, be_ref[i] != be_ref[jnp.maximum(i - 1, 0)])

    @pl.when(jnp.logical_and(used, changed))
    def _():
        wgu_bf[...] = wgu_ref[0].astype(BF16)
        wdn_bf[...] = wdn_ref[0].astype(BF16)

    @pl.when(used)
    def _():
        x = _load_row_tiles(xs_ref, xs_ref.shape[0] // SUBLANES)
        gu = _dot(x.astype(BF16), wgu_bf[...]) + bgu_ref[0]
        gate = jnp.minimum(gu[:, :D_EXPERT], SWIGLU_LIMIT)
        up = jnp.clip(gu[:, D_EXPERT:], -SWIGLU_LIMIT, SWIGLU_LIMIT)
        hmid = (up + 1.0) * (gate * jax.nn.sigmoid(SWIGLU_ALPHA * gate))
        _store_row_tiles(o_ref, _dot(hmid.astype(BF16), wdn_bf[...]) + bdn_ref[0])

    @pl.when(jnp.logical_not(used))
    def _():
        o_ref[...] = jnp.zeros_like(o_ref)


def _experts(block_e, n_used, xs, w_gu, b_gu, w_down, b_down):
    n_slots = xs.shape[0] // SUBLANES
    nb = n_slots // EXPERT_BLOCK
    blk = EXPERT_BLOCK
    slots = pl.BlockSpec((blk * SUBLANES, LANES), lambda i, be, nu: (i, 0))
    est = (2 * 2 * blk * D_MODEL * 4 + 2 * (D_MODEL * 2 * D_EXPERT + D_EXPERT * D_MODEL) * 4
           + (D_MODEL * 2 * D_EXPERT + D_EXPERT * D_MODEL) * 2 + 4 * blk * 2 * D_EXPERT * 4)
    grid_spec = pltpu.PrefetchScalarGridSpec(
        num_scalar_prefetch=2,
        grid=(nb,),
        in_specs=[slots,
                  pl.BlockSpec((1, D_MODEL, 2 * D_EXPERT), lambda i, be, nu: (be[i], 0, 0)),
                  pl.BlockSpec((1, 1, 2 * D_EXPERT), lambda i, be, nu: (be[i], 0, 0)),
                  pl.BlockSpec((1, D_EXPERT, D_MODEL), lambda i, be, nu: (be[i], 0, 0)),
                  pl.BlockSpec((1, 1, D_MODEL), lambda i, be, nu: (be[i], 0, 0))],
        out_specs=slots,
        scratch_shapes=[pltpu.VMEM((D_MODEL, 2 * D_EXPERT), BF16), pltpu.VMEM((D_EXPERT, D_MODEL), BF16)],
    )
    return pl.pallas_call(
        _expert_kernel,
        grid_spec=grid_spec,
        out_shape=jax.ShapeDtypeStruct((n_slots * SUBLANES, LANES), F32),
        compiler_params=_params(1, est),
        name="experts",
    )(block_e, n_used, xs, w_gu, b_gu.reshape(N_EXPERTS, 1, 2 * D_EXPERT), w_down,
      b_down.reshape(N_EXPERTS, 1, D_MODEL))


def _combine_kernel(dest_ref, dest_next_ref, ys_hbm, gate_ref, h_ref, p_ref, ln3_ref, pg_ref, pp_ref, fg_ref,
                    o_ref, gbuf, sem):
    i = pl.program_id(0)
    tm = h_ref.shape[0]
    slot = lax.rem(i, 2)

    def issue(d_ref, s):
        def body(n, carry):
            for j in range(TOP_K):
                pltpu.make_async_copy(_tiles(ys_hbm, d_ref[n * TOP_K + j], 1),
                                      _tiles(gbuf.at[s, j], n * SUBLANES, 1), sem.at[s]).start(priority=j % 2)
            return carry
        lax.fori_loop(0, tm, body, 0, unroll=ISSUE_UNROLL)

    @pl.when(i == 0)
    def _():
        issue(dest_ref, 0)

    @pl.when(i + 1 < pl.num_programs(0))
    def _():
        issue(dest_next_ref, 1 - slot)

    for j in range(TOP_K):
        pltpu.make_async_copy(gbuf.at[slot, j], gbuf.at[slot, j], sem.at[slot]).wait()

    gates = gate_ref[...]
    slabs = []
    for c in range(SUBLANES):
        acc = gates[:, 0:1] * gbuf[slot, 0, pl.ds(c, tm, stride=SUBLANES), :]
        for j in range(1, TOP_K):
            acc = acc + gates[:, j:j + 1] * gbuf[slot, j, pl.ds(c, tm, stride=SUBLANES), :]
        slabs.append(acc)
    h = h_ref[...] + jnp.concatenate(slabs, axis=1)
    gate = jax.nn.sigmoid(_dot(_rmsnorm_rows(h, ln3_ref[...]).astype(BF16), pg_ref[...]))
    h = h + gate * _dot(p_ref[...].astype(BF16), pp_ref[...])
    o_ref[...] = _rmsnorm_rows(h, fg_ref[...])


def _combine(dest_flat, ys, gates, h2d, p2d, ln3_g, ple_gate_bf, ple_proj_bf, final_g, tm):
    n = h2d.shape[0]
    nb = n // tm
    row = lambda x: x.reshape(1, -1)
    cst = lambda shape: pl.BlockSpec(shape, lambda i: (0,) * len(shape))
    est = (2 * TOP_K * tm * D_MODEL * 4 + 2 * (2 * tm * D_MODEL * 4 + tm * PLE_DIM * 4 + tm * ROUTE_LANES * 4)
           + 2 * (D_MODEL * D_MODEL * 2 + PLE_DIM * D_MODEL * 2) + 6 * tm * D_MODEL * 4)
    return pl.pallas_call(
        _combine_kernel,
        grid=(nb,),
        in_specs=[pl.BlockSpec((tm * TOP_K,), lambda i: (i,), memory_space=pltpu.SMEM),
                  pl.BlockSpec((tm * TOP_K,), lambda i: (jnp.minimum(i + 1, nb - 1),), memory_space=pltpu.SMEM),
                  pl.BlockSpec(memory_space=pl.ANY),
                  pl.BlockSpec((tm, ROUTE_LANES), lambda i: (i, 0)),
                  pl.BlockSpec((tm, D_MODEL), lambda i: (i, 0)),
                  pl.BlockSpec((tm, PLE_DIM), lambda i: (i, 0)),
                  cst((1, D_MODEL)), cst((D_MODEL, D_MODEL)), cst((PLE_DIM, D_MODEL)), cst((1, D_MODEL))],
        out_specs=pl.BlockSpec((tm, D_MODEL), lambda i: (i, 0)),
        out_shape=jax.ShapeDtypeStruct((n, D_MODEL), F32),
        scratch_shapes=[pltpu.VMEM((2, TOP_K, tm * SUBLANES, LANES), F32), pltpu.SemaphoreType.DMA((2,))],
        compiler_params=_params(1, est),
        name="combine_ple_norm",
    )(dest_flat, dest_flat, ys, gates, h2d, p2d, row(ln3_g), ple_gate_bf, ple_proj_bf, row(final_g))


def _tile(n, pref):
    t = min(n, pref)
    assert n % t == 0, (n, pref)
    return t


def _front(x, pos0, pool_ctx, shift_prev, wkv0, wts, carry_in, tm_mix):
    b, t, _ = x.shape
    n = b * t
    assert t >= POOL_CTX and t % SUBLANES == 0
    x2d = x.reshape(n, D_MODEL)
    u2d, zr2d = _in_proj(x2d, wts["ln1_g"], wts["w_in_bf"], _tile(n, 512))
    u = u2d.reshape(b, t, POOL_W)
    zr = zr2d.reshape(b, t, RWKV_PROJ)

    ctx16 = jnp.concatenate([jnp.zeros((b, POOL_HALO - POOL_CTX, POOL_W), F32), pool_ctx.astype(F32)], axis=1)
    pool_out = _pool_mixer(u, ctx16, wts["w_pool_bf"], wts["pool_scale"], pos0, _tile(t, 512))
    new_pool = u[:, -POOL_CTX:]

    sh8 = jnp.broadcast_to(shift_prev.astype(F32), (b, SUBLANES, RWKV_PROJ))
    new_shift = zr[:, -1:]

    t_pad = -(-t // CHUNK) * CHUNK
    zr_pad = zr if t_pad == t else jnp.pad(zr, ((0, 0), (0, t_pad - t), (0, 0)))
    rw_out, s_fin = _rwkv_mixer(zr_pad, sh8, _state_to_pairs(wkv0), wts, _tile(t_pad, 8 * CHUNK),
                                None if t_pad == t else t)
    rw_out = rw_out[:, :t]
    new_wkv = _pairs_to_state(s_fin)

    h, xn, route, gates, counts = _mix_out(
        rw_out.reshape(n, RWKV_W), pool_out.reshape(n, POOL_W), x2d, wts["w_out_bf"], wts["ln2_g"],
        wts["rw_hi"], wts["rw_lo"], wts["rb_pad"], carry_in, tm_mix)
    return dict(h=h, xn=xn, route=route, gates=gates, counts=counts,
                new_pool=new_pool, new_shift=new_shift, new_wkv=new_wkv)


def _prep_weights(i, ln1_g, w_in, mu_shift, w_pool, pool_scale, w0, w2, a0, a2, g2, k_k, k_a, r_k,
                  lnx_g, lnx_b, w_out, ln2_g, router_w, router_b, ln3_g, ple_gate, ple_proj):
    head = jnp.arange(GROUP_TILE) // HEAD_DIM
    ones_bd = (head[:, None] == head[None, :]).astype(BF16)
    rw_pad = jnp.pad(router_w[i].astype(F32), ((0, 0), (0, ROUTE_LANES - N_EXPERTS)))
    rw_hi = rw_pad.astype(BF16)
    rw_lo = (rw_pad - rw_hi.astype(F32)).astype(BF16)
    rb_pad = jnp.concatenate([router_b[i].astype(F32), jnp.full((ROUTE_LANES - N_EXPERTS,), -jnp.inf, F32)])
    return dict(
        ln1_g=ln1_g[i], w_in_bf=w_in[i].astype(BF16), mu_shift=mu_shift[i], w_pool_bf=w_pool[i].astype(BF16),
        pool_scale=pool_scale[i], w0=w0[i], w2_bf=w2[i].astype(BF16), a0=a0[i], a2_bf=a2[i].astype(BF16),
        g2_bf=g2[i].astype(BF16), k_k=k_k[i], k_a=k_a[i], r_k=r_k[i].reshape(RWKV_W), lnx_g=lnx_g[i],
        lnx_b=lnx_b[i], w_out_bf=w_out[i].astype(BF16), ln2_g=ln2_g[i], rw_hi=rw_hi, rw_lo=rw_lo,
        rb_pad=rb_pad.reshape(1, ROUTE_LANES), ones_bd=ones_bd, ln3_g=ln3_g[i],
        ple_gate_bf=ple_gate[i].astype(BF16), ple_proj_bf=ple_proj[i].astype(BF16))


def _layer_pair(hp, hs, pp, ps, pool_ctx, shift_prev, wkv0, wts, w_gu, b_gu, w_down, b_down, final_g):
    bp, tp, _ = hp.shape
    bs, ts, _ = hs.shape
    np_, ns = bp * tp, bs * ts
    tm_p = _tile(np_, 256)
    tm_s = _tile(ns, 128)
    zero_carry = jnp.zeros((SUBLANES, ROUTE_LANES), F32)
    fp = _front(hp, 0, jnp.zeros((bp, POOL_CTX, POOL_W), F32), jnp.zeros((bp, 1, RWKV_PROJ), F32),
                jnp.zeros((bp, N_HEADS, HEAD_DIM, HEAD_DIM), F32), wts, zero_carry, tm_p)
    fs = _front(hs, PAST_LEN, pool_ctx, shift_prev, wkv0, wts, fp["counts"], tm_s)

    counts = fs["counts"][0, :N_EXPERTS].astype(I32)
    padded = (counts + EXPERT_BLOCK - 1) // EXPERT_BLOCK * EXPERT_BLOCK
    pad_end = jnp.cumsum(padded)
    start_pad = pad_end - padded
    n_blocks = -(-(np_ + ns) * TOP_K // EXPERT_BLOCK) + N_EXPERTS
    n_slots = n_blocks * EXPERT_BLOCK
    n_used = (pad_end[-1] // EXPERT_BLOCK).astype(I32)
    block_start = jnp.arange(n_blocks, dtype=I32) * EXPERT_BLOCK
    block_e = jnp.minimum(jnp.sum((block_start[:, None] >= pad_end[None, :]).astype(I32), axis=1), N_EXPERTS - 1)
    last_e = block_e[jnp.maximum(n_used - 1, 0)]
    block_e = jnp.where(jnp.arange(n_blocks) < n_used, block_e, last_e)

    def dest_of(route):
        idx = route[:, :TOP_K]
        rank = route[:, TOP_K:2 * TOP_K]
        start = jnp.sum(jnp.where(idx[..., None] == jnp.arange(N_EXPERTS, dtype=I32), start_pad, 0), axis=-1)
        return ((start + rank) * SUBLANES).reshape(-1).astype(I32)

    dest_p, dest_s = dest_of(fp["route"]), dest_of(fs["route"])
    tm_d = _tile(np_, 512)
    dest_all = jnp.concatenate([dest_p, dest_s, jnp.zeros(((tm_d - ns) * TOP_K,), I32)])
    tail = jnp.stack([n_used * EXPERT_BLOCK, n_blocks - n_used]).astype(I32)
    xs = _dispatch(start_pad + counts, padded - counts, tail, dest_all, fp["xn"], fs["xn"], n_slots, tm_d)
    ys = _experts(block_e, n_used.reshape(1), xs, w_gu, b_gu, w_down, b_down)

    outs = []
    for f, dest, p, tm in ((fp, dest_p, pp, tm_p), (fs, dest_s, ps, tm_s)):
        n = f["h"].shape[0]
        outs.append(_combine(dest, ys, f["gates"], f["h"], p.reshape(n, PLE_DIM).astype(F32), wts["ln3_g"],
                             wts["ple_gate_bf"], wts["ple_proj_bf"], final_g, tm))
    return outs[0].reshape(hp.shape), outs[1].reshape(hs.shape), fp, fs


def kernel(x_prompt, x_sample, p_prompt, p_sample, cache_pool, state_shift, state_wkv, ln1_g, w_in, mu_shift, w_pool, pool_scale, w0, w2, a0, a2, g2, k_k, k_a, r_k, lnx_g, lnx_b, w_out, ln2_g, router_w, router_b, w_gu, b_gu, w_down, b_down, ln3_g, ple_gate, ple_proj, final_g):
    assert ln1_g.shape[0] == 1, "single-layer kernel"
    i = 0
    wts = _prep_weights(i, ln1_g, w_in, mu_shift, w_pool, pool_scale, w0, w2, a0, a2, g2, k_k, k_a, r_k,
                        lnx_g, lnx_b, w_out, ln2_g, router_w, router_b, ln3_g, ple_gate, ple_proj)
    y_p, y_s, fp, fs = _layer_pair(x_prompt, x_sample, p_prompt[i], p_sample[i], cache_pool[i], state_shift[i],
                                   state_wkv[i], wts, w_gu[i], b_gu[i], w_down[i], b_down[i], final_g)
    stack = lambda a: a[None]
    return (y_p, y_s,
            stack(fp["new_pool"]), stack(fp["new_shift"]), stack(fp["new_wkv"]),
            stack(fs["new_pool"]), stack(fs["new_shift"]), stack(fs["new_wkv"]))
```

```python
import functools

import jax
import jax.numpy as jnp
from jax import lax
from jax.experimental import pallas as pl
from jax.experimental.pallas import tpu as pltpu

F32 = jnp.float32
BF16 = jnp.bfloat16
I32 = jnp.int32

D_MODEL = 1024
POOL_W = 512
POOL_WINDOWS = (2, 4, 8, 16)
POOL_GW = 128
POOL_CTX = 15
POOL_HALO = 16
RWKV_W = 512
HEAD_DIM = 64
N_HEADS = 8
N_PAIRS = N_HEADS // 2
PAIR_W = 2 * HEAD_DIM
DECAY_LORA = 64
ICLR_LORA = 64
GATE_LORA = 128
RWKV_PROJ = 3 * RWKV_W + DECAY_LORA + ICLR_LORA + GATE_LORA
IN_PROJ = POOL_W + RWKV_PROJ
LNX_EPS = 64e-5
N_EXPERTS = 32
TOP_K = 4
D_EXPERT = 1024
SWIGLU_LIMIT = 7.0
SWIGLU_ALPHA = 1.702
PLE_DIM = 256
RMS_EPS = 1e-6
PAST_LEN = 2048

CHUNK = 64
EXPERT_BLOCK = 512
ROUTE_LANES = 128
SUBLANES = 8
LANES = 128
assert D_MODEL == SUBLANES * LANES
PREP_ROWS = 256
ISSUE_UNROLL = 8
GROUP_TILE = 256
EXP_M05 = 0.6065306597126334
V7X_VMEM_BYTES = 64 * 1024 * 1024

NT_DIMS = (((1,), (1,)), ((), ()))
TN_DIMS = (((0,), (0,)), ((), ()))


def _vmem_limit(est_bytes):
    return int(min(est_bytes * 3 // 2 + (4 << 20), V7X_VMEM_BYTES - (8 << 20)))


def _params(n_axes, est_bytes):
    return pltpu.CompilerParams(dimension_semantics=("arbitrary",) * n_axes,
                                vmem_limit_bytes=_vmem_limit(est_bytes))


def _dot(a, b):
    return jnp.dot(a, b, preferred_element_type=F32)


def _split2(x):
    hi = x.astype(BF16)
    lo = (x - hi.astype(F32)).astype(BF16)
    return hi, lo


def _split3(x):
    hi = x.astype(BF16)
    r1 = x - hi.astype(F32)
    mid = r1.astype(BF16)
    lo = (r1 - mid.astype(F32)).astype(BF16)
    return hi, mid, lo


def _group_sum(x, ones_bd):
    hi, lo = _split2(x)
    slabs = [_dot(hi[:, s:s + GROUP_TILE], ones_bd) + _dot(lo[:, s:s + GROUP_TILE], ones_bd)
             for s in range(0, x.shape[1], GROUP_TILE)]
    return jnp.concatenate(slabs, axis=1)


def _store_row_tiles(ref, value):
    n = value.shape[0]
    for c in range(SUBLANES):
        ref[pl.ds(c, n, stride=SUBLANES), :] = value[:, c * LANES:(c + 1) * LANES]


def _load_row_tiles(ref, n):
    return jnp.concatenate([ref[pl.ds(c, n, stride=SUBLANES), :] for c in range(SUBLANES)], axis=1)


def _rmsnorm_rows(x, g):
    ms = jnp.mean(x * x, axis=-1, keepdims=True)
    return (x * lax.rsqrt(ms + RMS_EPS)) * g


def _in_proj_kernel(x_ref, g_ref, w_ref, u_ref, zr_ref):
    xn = _rmsnorm_rows(x_ref[...], g_ref[...])
    z = _dot(xn.astype(BF16), w_ref[...])
    u_ref[...] = z[:, :POOL_W]
    zr_ref[...] = z[:, POOL_W:]


def _in_proj(x2d, ln1_g, w_in_bf, tm):
    n = x2d.shape[0]
    est = 2 * tm * D_MODEL * 4 + 2 * D_MODEL * IN_PROJ * 2 + 3 * tm * IN_PROJ * 4
    return pl.pallas_call(
        _in_proj_kernel,
        grid=(n // tm,),
        in_specs=[pl.BlockSpec((tm, D_MODEL), lambda i: (i, 0)),
                  pl.BlockSpec((1, D_MODEL), lambda i: (0, 0)),
                  pl.BlockSpec((D_MODEL, IN_PROJ), lambda i: (0, 0))],
        out_specs=[pl.BlockSpec((tm, POOL_W), lambda i: (i, 0)),
                   pl.BlockSpec((tm, RWKV_PROJ), lambda i: (i, 0))],
        out_shape=[jax.ShapeDtypeStruct((n, POOL_W), F32),
                   jax.ShapeDtypeStruct((n, RWKV_PROJ), F32)],
        compiler_params=_params(1, est),
        name="in_proj",
    )(x2d, ln1_g.reshape(1, D_MODEL), w_in_bf)


def _pool_kernel(u_ref, halo_ref, ctx_ref, wp_ref, ps_ref, o_ref, buf, *, pos0):
    t = pl.program_id(1)
    tt = u_ref.shape[1]
    buf[0:POOL_HALO, :] = jnp.where(t == 0, ctx_ref[0], halo_ref[0])
    buf[POOL_HALO:, :] = u_ref[0]
    pos = lax.broadcasted_iota(I32, (tt, 1), 0) + (t * tt + pos0)
    outs = []
    for gi, w in enumerate(POOL_WINDOWS):
        sl = slice(gi * POOL_GW, (gi + 1) * POOL_GW)
        cur = buf[POOL_HALO:POOL_HALO + tt, sl]
        acc = cur
        for j in range(1, w):
            acc = acc + buf[POOL_HALO - j:POOL_HALO - j + tt, sl]
        cnt = jnp.minimum(w, pos + 1).astype(F32)
        pooled = acc / cnt - cur
        outs.append(_dot(pooled.astype(BF16), wp_ref[gi]))
    o_ref[0] = (jnp.concatenate(outs, axis=-1) * ps_ref[...]).astype(o_ref.dtype)


def _pool_mixer(u, ctx16, w_pool_bf, pool_scale, pos0, tt):
    b, t, _ = u.shape
    hb = tt // POOL_HALO
    est = 2 * tt * POOL_W * 4 * 3 + (tt + POOL_HALO) * POOL_W * 4
    return pl.pallas_call(
        functools.partial(_pool_kernel, pos0=pos0),
        grid=(b, t // tt),
        in_specs=[pl.BlockSpec((1, tt, POOL_W), lambda i, j: (i, j, 0)),
                  pl.BlockSpec((1, POOL_HALO, POOL_W), lambda i, j: (i, jnp.maximum(j * hb - 1, 0), 0)),
                  pl.BlockSpec((1, POOL_HALO, POOL_W), lambda i, j: (i, 0, 0)),
                  pl.BlockSpec((len(POOL_WINDOWS), POOL_GW, POOL_GW), lambda i, j: (0, 0, 0)),
                  pl.BlockSpec((1, POOL_W), lambda i, j: (0, 0))],
        out_specs=pl.BlockSpec((1, tt, POOL_W), lambda i, j: (i, j, 0)),
        out_shape=jax.ShapeDtypeStruct((b, t, POOL_W), BF16),
        scratch_shapes=[pltpu.VMEM((tt + POOL_HALO, POOL_W), F32)],
        compiler_params=_params(2, est),
        name="pool_mixer",
    )(u, u, ctx16, w_pool_bf, pool_scale.reshape(1, POOL_W))


def _rwkv_prepare(zr, prev, mu, w0, w2, a0, a2, g2, k_k, k_a, r_k, ones_bd):
    zs = zr + (prev - zr) * mu
    o1, o2, o3 = RWKV_W, 2 * RWKV_W, 3 * RWKV_W
    o4, o5 = o3 + DECAY_LORA, o3 + DECAY_LORA + ICLR_LORA
    r, k, v = zs[:, :o1], zs[:, o1:o2], zs[:, o2:o3]
    wd, ad, gd = zs[:, o3:o4], zs[:, o4:o5], zs[:, o5:]
    wz = w0 + _dot(jnp.tanh(wd).astype(BF16), w2)
    lw = -EXP_M05 * jax.nn.sigmoid(wz)
    a = jax.nn.sigmoid(a0 + _dot(ad.astype(BF16), a2))
    g = _dot(jax.nn.sigmoid(gd).astype(BF16), g2)
    kk = k * k_k
    kk = kk / jnp.maximum(jnp.sqrt(_group_sum(kk * kk, ones_bd)), 1e-12)
    kx = k * (1.0 + (a - 1.0) * k_a)
    bonus = _group_sum(r * kx * r_k, ones_bd) * v
    return r, lw, kx, v, -kk, kk * a, g, bonus


def _pair_blockdiag(z, left):
    return jnp.concatenate([jnp.where(left, z, 0.0), jnp.where(left, 0.0, z)], axis=0)


def _wkv_kernel(zr_ref, halo_ref, sh_ref, mu_ref, w0_ref, w2_ref, a0_ref, a2_ref, g2_ref, kk_ref, ka_ref, rk_ref,
                ones_ref, lg_ref, lb_ref, s0_ref, o_ref, sf_ref,
                buf, r_ref, lw_ref, kx_ref, v_ref, na_ref, kb_ref, g_ref, bonus_ref, y_ref, s_scr, *, t_valid):
    t = pl.program_id(1)
    tt = zr_ref.shape[1]
    c = CHUNK
    rows_per_pass = min(tt, PREP_ROWS)

    @pl.when(t == 0)
    def _():
        s_scr[...] = s0_ref[0]

    buf[0:SUBLANES, :] = jnp.where(t == 0, sh_ref[0], halo_ref[0])
    buf[SUBLANES:, :] = zr_ref[0]
    ones_bd = ones_ref[...]
    for r0 in range(0, tt, rows_per_pass):
        rows = slice(r0, r0 + rows_per_pass)
        vals = _rwkv_prepare(zr_ref[0, rows, :], buf[SUBLANES - 1 + r0:SUBLANES - 1 + r0 + rows_per_pass, :],
                             mu_ref[...], w0_ref[...], w2_ref[...], a0_ref[...], a2_ref[...], g2_ref[...],
                             kk_ref[...], ka_ref[...], rk_ref[...], ones_bd)
        if t_valid is not None:
            live = lax.broadcasted_iota(I32, (rows_per_pass, 1), 0) + (t * tt + r0) < t_valid
            vals = tuple(jnp.where(live, x, 0.0) for x in vals)
        for ref, x in zip((r_ref, lw_ref, kx_ref, v_ref, na_ref, kb_ref, g_ref, bonus_ref), vals):
            ref[rows, :] = x

    lane = lax.broadcasted_iota(I32, (c, 2 * c), 1)
    rowi = lax.broadcasted_iota(I32, (c, 2 * c), 0)
    jm = lane & (c - 1)
    strict = jm < rowi
    incl = jm <= rowi
    eye_pair = jnp.where(jm == rowi, 1.0, 0.0).astype(F32)
    left = lax.broadcasted_iota(I32, (c, PAIR_W), 1) < HEAD_DIM
    rr = lax.broadcasted_iota(I32, (c, c), 0)
    cc = lax.broadcasted_iota(I32, (c, c), 1)
    tri_incl_bf = jnp.where(cc <= rr, 1.0, 0.0).astype(BF16)
    br = lax.broadcasted_iota(I32, (PAIR_W, PAIR_W), 0) < HEAD_DIM
    bc = lax.broadcasted_iota(I32, (PAIR_W, PAIR_W), 1) < HEAD_DIM
    bd_mask = br == bc
    bf = lambda x: x.astype(BF16)
    bd = lambda z: _pair_blockdiag(z, left)
    n_chunks = tt // c

    cums = []
    for ci in range(n_chunks):
        h3 = _split3(lw_ref[ci * c:(ci + 1) * c, :])
        cums.append(_dot(tri_incl_bf, h3[0]) + _dot(tri_incl_bf, h3[1]) + _dot(tri_incl_bf, h3[2]))

    chains = []
    for ci in range(n_chunks):
        rows = slice(ci * c, (ci + 1) * c)
        for p in range(N_PAIRS):
            cols = slice(p * PAIR_W, (p + 1) * PAIR_W)
            cum = cums[ci][:, cols]
            lw, kx, kb = lw_ref[rows, cols], kx_ref[rows, cols], kb_ref[rows, cols]
            tot = cum[c - 1:c, :]
            p_end = jnp.exp(tot - cum)
            inv_p = jnp.exp(-cum)
            chains.append(dict(
                ci=ci, p=p, v=v_ref[rows, cols], p_c=jnp.exp(tot),
                a_t=na_ref[rows, cols] * jnp.exp(cum - lw), r_t=r_ref[rows, cols] * jnp.exp(cum),
                b_t=kb * inv_p, k_t=kx * inv_p, b_h=kb * p_end, k_h=kx * p_end))

    for d in chains:
        lhs = bf(jnp.concatenate([d["a_t"], d["r_t"]], axis=0))
        rhs = bf(jnp.concatenate([bd(d["b_t"]), bd(d["k_t"])], axis=0))
        sc = lax.dot_general(lhs, rhs, NT_DIMS, preferred_element_type=F32)
        d["a_ab"] = jnp.where(strict, sc[:c, :2 * c], 0.0)
        d["a_k"] = jnp.concatenate([jnp.where(strict, sc[:c, 2 * c:], 0.0),
                                    jnp.where(incl, sc[c:, 2 * c:], 0.0)], axis=0)
        d["a_rb"] = jnp.where(incl, sc[c:, :2 * c], 0.0)

    for d in chains:
        d["x"] = _dot(bf(d["a_ab"]), bf(bd(d["a_ab"])))
        d["t"] = eye_pair + d["a_ab"]
    for _ in range(c.bit_length() - 3):
        for d in chains:
            st = _dot(bf(jnp.concatenate([d["x"], d["t"]], axis=0)), bf(bd(d["x"])))
            d["x"] = st[:c]
            d["t"] = d["t"] + st[c:]
    for d in chains:
        d["t"] = d["t"] + _dot(bf(d["t"]), bf(bd(d["x"])))

    for d in chains:
        wv = _dot(bf(d["a_k"]), bf(bd(d["v"])))
        d["w1"], d["rkv"] = wv[:c], wv[c:]
    for d in chains:
        tu = _dot(bf(d["t"]), bf(jnp.concatenate([bd(d["w1"]), bd(d["a_t"])], axis=1)))
        d["u_loc"], d["a_tt"] = tu[:, :PAIR_W], tu[:, PAIR_W:]
    for d in chains:
        ar = _dot(bf(d["a_rb"]), bf(jnp.concatenate([bd(d["u_loc"]), bd(d["a_tt"])], axis=1)))
        d["y_loc"] = ar[:, :PAIR_W] + d["rkv"]
        d["r_g"] = d["r_t"] + ar[:, PAIR_W:]
    for d in chains:
        lhs = jnp.concatenate([jnp.concatenate([d["v"], jnp.zeros_like(d["v"])], axis=1),
                               jnp.concatenate([d["u_loc"], d["a_tt"]], axis=1)], axis=0)
        rhs = jnp.concatenate([d["k_h"], d["b_h"]], axis=0)
        dp = lax.dot_general(bf(lhs), bf(rhs), TN_DIMS, preferred_element_type=F32)
        d["d_loc"] = jnp.where(bd_mask, dp[:PAIR_W], 0.0)
        d["phi"] = bf(jnp.where(bd_mask, dp[PAIR_W:], 0.0))

    state = [s_scr[p] for p in range(N_PAIRS)]
    for d in chains:
        p = d["p"]
        s_hi, s_lo = _split2(state[p])
        y = lax.dot_general(bf(d["r_g"]), s_hi, NT_DIMS, preferred_element_type=F32) + d["y_loc"]
        y_ref[d["ci"] * c:(d["ci"] + 1) * c, p * PAIR_W:(p + 1) * PAIR_W] = y
        state[p] = state[p] * d["p_c"] + (_dot(s_hi, d["phi"]) + _dot(s_lo, d["phi"])) + d["d_loc"]
    for p in range(N_PAIRS):
        s_scr[p] = state[p]

    @pl.when(t == pl.num_programs(1) - 1)
    def _():
        sf_ref[0] = s_scr[...]

    inv_n = 1.0 / HEAD_DIM
    for r0 in range(0, tt, rows_per_pass):
        rows = slice(r0, r0 + rows_per_pass)
        y = y_ref[rows, :]
        mu = _group_sum(y, ones_bd) * inv_n
        dlt = y - mu
        var = _group_sum(dlt * dlt, ones_bd) * inv_n
        yn = dlt * lax.rsqrt(var + LNX_EPS) * lg_ref[...] + lb_ref[...]
        o_ref[0, rows, :] = ((yn + bonus_ref[rows, :]) * g_ref[rows, :]).astype(o_ref.dtype)


def _rwkv_mixer(zr, sh8, s0_pair, wts, tt, t_valid):
    b, t, _ = zr.shape
    assert CHUNK == HEAD_DIM and tt % CHUNK == 0
    hb = tt // SUBLANES
    row = lambda x: x.reshape(1, -1)
    cst = lambda shape: pl.BlockSpec(shape, lambda i, j: (0,) * len(shape))
    seq = pl.BlockSpec((1, tt, RWKV_W), lambda i, j: (i, j, 0))
    st = pl.BlockSpec((1, N_PAIRS, PAIR_W, PAIR_W), lambda i, j: (i, 0, 0, 0))
    est = 3 * tt * RWKV_PROJ * 4 + 10 * tt * RWKV_W * 4 + 5 * N_PAIRS * PAIR_W * PAIR_W * 4 + (8 << 20)
    return pl.pallas_call(
        functools.partial(_wkv_kernel, t_valid=t_valid),
        grid=(b, t // tt),
        in_specs=[pl.BlockSpec((1, tt, RWKV_PROJ), lambda i, j: (i, j, 0)),
                  pl.BlockSpec((1, SUBLANES, RWKV_PROJ), lambda i, j: (i, jnp.maximum(j * hb - 1, 0), 0)),
                  pl.BlockSpec((1, SUBLANES, RWKV_PROJ), lambda i, j: (i, 0, 0)),
                  cst((1, RWKV_PROJ)), cst((1, RWKV_W)), cst((DECAY_LORA, RWKV_W)),
                  cst((1, RWKV_W)), cst((ICLR_LORA, RWKV_W)), cst((GATE_LORA, RWKV_W)),
                  cst((1, RWKV_W)), cst((1, RWKV_W)), cst((1, RWKV_W)), cst((GROUP_TILE, GROUP_TILE)),
                  cst((1, RWKV_W)), cst((1, RWKV_W)), st],
        out_specs=[seq, st],
        out_shape=[jax.ShapeDtypeStruct((b, t, RWKV_W), BF16),
                   jax.ShapeDtypeStruct((b, N_PAIRS, PAIR_W, PAIR_W), F32)],
        scratch_shapes=([pltpu.VMEM((tt + SUBLANES, RWKV_PROJ), F32)] + [pltpu.VMEM((tt, RWKV_W), F32)] * 9
                        + [pltpu.VMEM((N_PAIRS, PAIR_W, PAIR_W), F32)]),
        compiler_params=_params(2, est),
        name="rwkv_mixer",
    )(zr, zr, sh8, row(wts["mu_shift"]), row(wts["w0"]), wts["w2_bf"], row(wts["a0"]), wts["a2_bf"], wts["g2_bf"],
      row(wts["k_k"]), row(wts["k_a"]), row(wts["r_k"]), wts["ones_bd"], row(wts["lnx_g"]), row(wts["lnx_b"]),
      s0_pair)


def _state_to_pairs(s):
    b = s.shape[0]
    s = s.astype(F32).reshape(b, N_PAIRS, 2, HEAD_DIM, HEAD_DIM)
    z = jnp.zeros((b, N_PAIRS, HEAD_DIM, HEAD_DIM), F32)
    return jnp.concatenate([jnp.concatenate([s[:, :, 0], z], axis=-1),
                            jnp.concatenate([z, s[:, :, 1]], axis=-1)], axis=-2)


def _pairs_to_state(sp):
    b = sp.shape[0]
    s = jnp.stack([sp[:, :, :HEAD_DIM, :HEAD_DIM], sp[:, :, HEAD_DIM:, HEAD_DIM:]], axis=2)
    return s.reshape(b, N_HEADS, HEAD_DIM, HEAD_DIM)


def _mix_out_kernel(rw_ref, po_ref, x_ref, wo_ref, ln2_ref, rw_hi_ref, rw_lo_ref, rb_ref, cin_ref,
                    h_o, xn_o, route_o, gate_o, cnt_o, carry):
    i = pl.program_id(0)
    tm = x_ref.shape[0]

    @pl.when(i == 0)
    def _():
        carry[...] = cin_ref[...]

    mix = _dot(po_ref[...], wo_ref[:POOL_W, :]) + _dot(rw_ref[...], wo_ref[POOL_W:, :])
    h = x_ref[...] + mix
    h_o[...] = h
    xn = _rmsnorm_rows(h, ln2_ref[...])
    _store_row_tiles(xn_o, xn)

    x_hi, x_lo = _split2(xn)
    logits = (_dot(x_hi, rw_hi_ref[...]) + _dot(x_hi, rw_lo_ref[...]) + _dot(x_lo, rw_hi_ref[...])
              + rb_ref[...])
    lane = lax.broadcasted_iota(I32, (tm, ROUTE_LANES), 1)
    lane_f = lane.astype(F32)
    vals, idxs, hots = [], [], []
    work = logits
    for _ in range(TOP_K):
        m = jnp.max(work, axis=-1, keepdims=True)
        idx = jnp.min(jnp.where(work == m, lane_f, float(ROUTE_LANES)), axis=-1, keepdims=True)
        hit = lane_f == idx
        vals.append(m)
        idxs.append(idx)
        hots.append(jnp.where(hit, 1.0, 0.0).astype(F32))
        work = jnp.where(hit, -jnp.inf, work)
    exps = [jnp.exp(vv - vals[0]) for vv in vals]
    den = exps[0] + exps[1] + exps[2] + exps[3]

    hot_all = hots[0] + hots[1] + hots[2] + hots[3]
    rr = lax.broadcasted_iota(I32, (tm, tm), 0)
    cc = lax.broadcasted_iota(I32, (tm, tm), 1)
    tri_strict = jnp.where(cc < rr, 1.0, 0.0).astype(BF16)
    prefix = _dot(tri_strict, hot_all.astype(BF16)) + carry[0:1, :]
    route = jnp.zeros((tm, ROUTE_LANES), F32)
    gates = jnp.zeros((tm, ROUTE_LANES), F32)
    for j in range(TOP_K):
        rank = jnp.sum(hots[j] * prefix, axis=-1, keepdims=True)
        route = jnp.where(lane == j, idxs[j], route)
        route = jnp.where(lane == TOP_K + j, rank, route)
        gates = jnp.where(lane == j, exps[j] / den, gates)
    route_o[...] = route.astype(I32)
    gate_o[...] = gates
    new_carry = carry[...] + jnp.sum(hot_all, axis=0, keepdims=True)
    carry[...] = new_carry
    cnt_o[...] = new_carry


def _mix_out(rw2d, po2d, x2d, w_out_bf, ln2_g, rw_hi, rw_lo, rb_pad, carry_in, tm):
    n = x2d.shape[0]
    row = lambda x: x.reshape(1, -1)
    cst = lambda shape: pl.BlockSpec(shape, lambda i: (0,) * len(shape))
    half = pl.BlockSpec((tm, RWKV_W), lambda i: (i, 0))
    full = pl.BlockSpec((tm, D_MODEL), lambda i: (i, 0))
    lanes = pl.BlockSpec((tm, ROUTE_LANES), lambda i: (i, 0))
    est = (2 * (2 * tm * RWKV_W * 2 + 3 * tm * D_MODEL * 4) + 2 * D_MODEL * D_MODEL * 2
           + 8 * tm * D_MODEL * 4 + 4 * tm * tm)
    return pl.pallas_call(
        _mix_out_kernel,
        grid=(n // tm,),
        in_specs=[half, half, full, cst((D_MODEL, D_MODEL)),
                  cst((1, D_MODEL)), cst((D_MODEL, ROUTE_LANES)), cst((D_MODEL, ROUTE_LANES)),
                  cst((1, ROUTE_LANES)), cst((SUBLANES, ROUTE_LANES))],
        out_specs=[full, pl.BlockSpec((tm * SUBLANES, LANES), lambda i: (i, 0)), lanes, lanes,
                   cst((SUBLANES, ROUTE_LANES))],
        out_shape=[jax.ShapeDtypeStruct((n, D_MODEL), F32), jax.ShapeDtypeStruct((n * SUBLANES, LANES), F32),
                   jax.ShapeDtypeStruct((n, ROUTE_LANES), I32), jax.ShapeDtypeStruct((n, ROUTE_LANES), F32),
                   jax.ShapeDtypeStruct((SUBLANES, ROUTE_LANES), F32)],
        scratch_shapes=[pltpu.VMEM((SUBLANES, ROUTE_LANES), F32)],
        compiler_params=_params(1, est),
        name="mix_out_router",
    )(rw2d, po2d, x2d, w_out_bf, row(ln2_g), rw_hi, rw_lo, rb_pad, carry_in)


def _tiles(ref, first_row, n_slots):
    if not isinstance(first_row, int):
        first_row = pl.multiple_of(first_row, SUBLANES)
    return ref.at[pl.ds(first_row, n_slots * SUBLANES), :]


def _dispatch_kernel(pad_start_ref, pad_len_ref, tail_ref, dest_ref, xa_ref, xb_ref, o_hbm, zbuf, sem, sem_fill,
                     *, n_main, n_last):
    i = pl.program_id(0)
    tm = xa_ref.shape[0] // SUBLANES
    blk = zbuf.shape[0] // SUBLANES

    @pl.when(i == 0)
    def _():
        zbuf[...] = jnp.zeros_like(zbuf)

        def per_expert(e, totals):
            start, n = pad_start_ref[e], pad_len_ref[e]
            n_groups = lax.shift_right_logical(n, 3)
            n_single = n - n_groups * SUBLANES

            def group(g, carry):
                pltpu.make_async_copy(_tiles(zbuf, 0, SUBLANES),
                                      _tiles(o_hbm, (start + g * SUBLANES) * SUBLANES, SUBLANES), sem_fill).start()
                return carry

            def single(r, carry):
                pltpu.make_async_copy(_tiles(zbuf, 0, 1),
                                      _tiles(o_hbm, (start + n_groups * SUBLANES + r) * SUBLANES, 1), sem_fill).start()
                return carry

            lax.fori_loop(0, n_groups, group, 0)
            lax.fori_loop(0, n_single, single, 0)
            return totals[0] + n_groups, totals[1] + n_single

        n_groups_filled, n_single_filled = lax.fori_loop(0, N_EXPERTS, per_expert, (0, 0))

        def wait_group(g, carry):
            pltpu.make_async_copy(_tiles(zbuf, 0, SUBLANES), _tiles(o_hbm, 0, SUBLANES), sem_fill).wait()
            return carry

        def wait_single(r, carry):
            pltpu.make_async_copy(_tiles(zbuf, 0, 1), _tiles(o_hbm, 0, 1), sem_fill).wait()
            return carry

        lax.fori_loop(0, n_groups_filled, wait_group, 0)
        lax.fori_loop(0, n_single_filled, wait_single, 0)

        def fill_block(b, carry):
            pltpu.make_async_copy(zbuf, _tiles(o_hbm, (tail_ref[0] + b * blk) * SUBLANES, blk), sem_fill).start()
            return carry

        def wait_block(b, carry):
            pltpu.make_async_copy(zbuf, _tiles(o_hbm, 0, blk), sem_fill).wait()
            return carry

        lax.fori_loop(0, tail_ref[1], fill_block, 0)
        lax.fori_loop(0, tail_ref[1], wait_block, 0)

    def scatter_rows(x_ref, n_rows):
        def issue(n, carry):
            src = _tiles(x_ref, n * SUBLANES, 1)
            for j in range(TOP_K):
                pltpu.make_async_copy(src, _tiles(o_hbm, dest_ref[n * TOP_K + j], 1), sem).start(priority=j % 2)
            return carry

        lax.fori_loop(0, n_rows, issue, 0, unroll=ISSUE_UNROLL)
        pltpu.make_async_copy(_tiles(o_hbm, 0, n_rows * TOP_K), _tiles(o_hbm, 0, n_rows * TOP_K), sem).wait()

    @pl.when(i < n_main)
    def _():
        scatter_rows(xa_ref, tm)

    @pl.when(i == n_main)
    def _():
        scatter_rows(xb_ref, n_last)


def _dispatch(pad_start, pad_len, tail, dest_rows, xn_a, xn_b, n_slots, tm):
    n_main = xn_a.shape[0] // (tm * SUBLANES)
    n_last = xn_b.shape[0] // SUBLANES
    assert n_last <= tm and n_slots >= tm * TOP_K
    grid_spec = pltpu.PrefetchScalarGridSpec(
        num_scalar_prefetch=3,
        grid=(n_main + 1,),
        in_specs=[pl.BlockSpec((tm * TOP_K,), lambda i, *_: (i,), memory_space=pltpu.SMEM),
                  pl.BlockSpec((tm * SUBLANES, LANES), lambda i, *_: (jnp.minimum(i, n_main - 1), 0)),
                  pl.BlockSpec((n_last * SUBLANES, LANES), lambda i, *_: (0, 0))],
        out_specs=pl.BlockSpec(memory_space=pl.ANY),
        scratch_shapes=[pltpu.VMEM((EXPERT_BLOCK * SUBLANES, LANES), F32), pltpu.SemaphoreType.DMA(()),
                        pltpu.SemaphoreType.DMA(())],
    )
    return pl.pallas_call(
        functools.partial(_dispatch_kernel, n_main=n_main, n_last=n_last),
        grid_spec=grid_spec,
        out_shape=jax.ShapeDtypeStruct((n_slots * SUBLANES, LANES), F32),
        compiler_params=_params(1, 2 * (tm + n_last) * D_MODEL * 4 + EXPERT_BLOCK * D_MODEL * 4),
        name="dispatch",
    )(pad_start, pad_len, tail, dest_rows, xn_a, xn_b)


def _expert_kernel(be_ref, nu_ref, xs_ref, wgu_ref, bgu_ref, wdn_ref, bdn_ref, o_ref, wgu_bf, wdn_bf):
    i = pl.program_id(0)
    used = i < nu_ref[0]
    changed = jnp.logical_or(i == 0, be_ref[i] != be_ref[jnp.maximum(i - 1, 0)])

    @pl.when(jnp.logical_and(used, changed))
    def _():
        wgu_bf[...] = wgu_ref[0].astype(BF16)
        wdn_bf[...] = wdn_ref[0].astype(BF16)

    @pl.when(used)
    def _():
        x = _load_row_tiles(xs_ref, xs_ref.shape[0] // SUBLANES)
        gu = _dot(x.astype(BF16), wgu_bf[...]) + bgu_ref[0]
        gate = jnp.minimum(gu[:, :D_EXPERT], SWIGLU_LIMIT)
        up = jnp.clip(gu[:, D_EXPERT:], -SWIGLU_LIMIT, SWIGLU_LIMIT)
        hmid = (up + 1.0) * (gate * jax.nn.sigmoid(SWIGLU_ALPHA * gate))
        _store_row_tiles(o_ref, _dot(hmid.astype(BF16), wdn_bf[...]) + bdn_ref[0])

    @pl.when(jnp.logical_not(used))
    def _():
        o_ref[...] = jnp.zeros_like(o_ref)


def _experts(block_e, n_used, xs, w_gu, b_gu, w_down, b_down):
    n_slots = xs.shape[0] // SUBLANES
    nb = n_slots // EXPERT_BLOCK
    blk = EXPERT_BLOCK
    slots = pl.BlockSpec((blk * SUBLANES, LANES), lambda i, be, nu: (i, 0))
    est = (2 * 2 * blk * D_MODEL * 4 + 2 * (D_MODEL * 2 * D_EXPERT + D_EXPERT * D_MODEL) * 4
           + (D_MODEL * 2 * D_EXPERT + D_EXPERT * D_MODEL) * 2 + 4 * blk * 2 * D_EXPERT * 4)
    grid_spec = pltpu.PrefetchScalarGridSpec(
        num_scalar_prefetch=2,
        grid=(nb,),
        in_specs=[slots,
                  pl.BlockSpec((1, D_MODEL, 2 * D_EXPERT), lambda i, be, nu: (be[i], 0, 0)),
                  pl.BlockSpec((1, 1, 2 * D_EXPERT), lambda i, be, nu: (be[i], 0, 0)),
                  pl.BlockSpec((1, D_EXPERT, D_MODEL), lambda i, be, nu: (be[i], 0, 0)),
                  pl.BlockSpec((1, 1, D_MODEL), lambda i, be, nu: (be[i], 0, 0))],
        out_specs=slots,
        scratch_shapes=[pltpu.VMEM((D_MODEL, 2 * D_EXPERT), BF16), pltpu.VMEM((D_EXPERT, D_MODEL), BF16)],
    )
    return pl.pallas_call(
        _expert_kernel,
        grid_spec=grid_spec,
        out_shape=jax.ShapeDtypeStruct((n_slots * SUBLANES, LANES), F32),
        compiler_params=_params(1, est),
        name="experts",
    )(block_e, n_used, xs, w_gu, b_gu.reshape(N_EXPERTS, 1, 2 * D_EXPERT), w_down,
      b_down.reshape(N_EXPERTS, 1, D_MODEL))


def _combine_kernel(dest_ref, dest_next_ref, ys_hbm, gate_ref, h_ref, p_ref, ln3_ref, pg_ref, pp_ref, fg_ref,
                    o_ref, gbuf, sem):
    i = pl.program_id(0)
    tm = h_ref.shape[0]
    slot = lax.rem(i, 2)

    def issue(d_ref, s):
        def body(n, carry):
            for j in range(TOP_K):
                pltpu.make_async_copy(_tiles(ys_hbm, d_ref[n * TOP_K + j], 1),
                                      _tiles(gbuf.at[s, j], n * SUBLANES, 1), sem.at[s]).start(priority=j % 2)
            return carry
        lax.fori_loop(0, tm, body, 0, unroll=ISSUE_UNROLL)

    @pl.when(i == 0)
    def _():
        issue(dest_ref, 0)

    @pl.when(i + 1 < pl.num_programs(0))
    def _():
        issue(dest_next_ref, 1 - slot)

    for j in range(TOP_K):
        pltpu.make_async_copy(gbuf.at[slot, j], gbuf.at[slot, j], sem.at[slot]).wait()

    gates = gate_ref[...]
    slabs = []
    for c in range(SUBLANES):
        acc = gates[:, 0:1] * gbuf[slot, 0, pl.ds(c, tm, stride=SUBLANES), :]
        for j in range(1, TOP_K):
            acc = acc + gates[:, j:j + 1] * gbuf[slot, j, pl.ds(c, tm, stride=SUBLANES), :]
        slabs.append(acc)
    h = h_ref[...] + jnp.concatenate(slabs, axis=1)
    gate = jax.nn.sigmoid(_dot(_rmsnorm_rows(h, ln3_ref[...]).astype(BF16), pg_ref[...]))
    h = h + gate * _dot(p_ref[...].astype(BF16), pp_ref[...])
    o_ref[...] = _rmsnorm_rows(h, fg_ref[...])


def _combine(dest_flat, ys, gates, h2d, p2d, ln3_g, ple_gate_bf, ple_proj_bf, final_g, tm):
    n = h2d.shape[0]
    nb = n // tm
    row = lambda x: x.reshape(1, -1)
    cst = lambda shape: pl.BlockSpec(shape, lambda i: (0,) * len(shape))
    est = (2 * TOP_K * tm * D_MODEL * 4 + 2 * (2 * tm * D_MODEL * 4 + tm * PLE_DIM * 4 + tm * ROUTE_LANES * 4)
           + 2 * (D_MODEL * D_MODEL * 2 + PLE_DIM * D_MODEL * 2) + 6 * tm * D_MODEL * 4)
    return pl.pallas_call(
        _combine_kernel,
        grid=(nb,),
        in_specs=[pl.BlockSpec((tm * TOP_K,), lambda i: (i,), memory_space=pltpu.SMEM),
                  pl.BlockSpec((tm * TOP_K,), lambda i: (jnp.minimum(i + 1, nb - 1),), memory_space=pltpu.SMEM),
                  pl.BlockSpec(memory_space=pl.ANY),
                  pl.BlockSpec((tm, ROUTE_LANES), lambda i: (i, 0)),
                  pl.BlockSpec((tm, D_MODEL), lambda i: (i, 0)),
                  pl.BlockSpec((tm, PLE_DIM), lambda i: (i, 0)),
                  cst((1, D_MODEL)), cst((D_MODEL, D_MODEL)), cst((PLE_DIM, D_MODEL)), cst((1, D_MODEL))],
        out_specs=pl.BlockSpec((tm, D_MODEL), lambda i: (i, 0)),
        out_shape=jax.ShapeDtypeStruct((n, D_MODEL), F32),
        scratch_shapes=[pltpu.VMEM((2, TOP_K, tm * SUBLANES, LANES), F32), pltpu.SemaphoreType.DMA((2,))],
        compiler_params=_params(1, est),
        name="combine_ple_norm",
    )(dest_flat, dest_flat, ys, gates, h2d, p2d, row(ln3_g), ple_gate_bf, ple_proj_bf, row(final_g))


def _tile(n, pref):
    t = min(n, pref)
    assert n % t == 0, (n, pref)
    return t


def _front(x, pos0, pool_ctx, shift_prev, wkv0, wts, carry_in, tm_mix):
    b, t, _ = x.shape
    n = b * t
    assert t >= POOL_CTX and t % SUBLANES == 0
    x2d = x.reshape(n, D_MODEL)
    u2d, zr2d = _in_proj(x2d, wts["ln1_g"], wts["w_in_bf"], _tile(n, 512))
    u = u2d.reshape(b, t, POOL_W)
    zr = zr2d.reshape(b, t, RWKV_PROJ)

    ctx16 = jnp.concatenate([jnp.zeros((b, POOL_HALO - POOL_CTX, POOL_W), F32), pool_ctx.astype(F32)], axis=1)
    pool_out = _pool_mixer(u, ctx16, wts["w_pool_bf"], wts["pool_scale"], pos0, _tile(t, 512))
    new_pool = u[:, -POOL_CTX:]

    sh8 = jnp.broadcast_to(shift_prev.astype(F32), (b, SUBLANES, RWKV_PROJ))
    new_shift = zr[:, -1:]

    t_pad = -(-t // CHUNK) * CHUNK
    zr_pad = zr if t_pad == t else jnp.pad(zr, ((0, 0), (0, t_pad - t), (0, 0)))
    rw_out, s_fin = _rwkv_mixer(zr_pad, sh8, _state_to_pairs(wkv0), wts, _tile(t_pad, 8 * CHUNK),
                                None if t_pad == t else t)
    rw_out = rw_out[:, :t]
    new_wkv = _pairs_to_state(s_fin)

    h, xn, route, gates, counts = _mix_out(
        rw_out.reshape(n, RWKV_W), pool_out.reshape(n, POOL_W), x2d, wts["w_out_bf"], wts["ln2_g"],
        wts["rw_hi"], wts["rw_lo"], wts["rb_pad"], carry_in, tm_mix)
    return dict(h=h, xn=xn, route=route, gates=gates, counts=counts,
                new_pool=new_pool, new_shift=new_shift, new_wkv=new_wkv)


def _prep_weights(i, ln1_g, w_in, mu_shift, w_pool, pool_scale, w0, w2, a0, a2, g2, k_k, k_a, r_k,
                  lnx_g, lnx_b, w_out, ln2_g, router_w, router_b, ln3_g, ple_gate, ple_proj):
    head = jnp.arange(GROUP_TILE) // HEAD_DIM
    ones_bd = (head[:, None] == head[None, :]).astype(BF16)
    rw_pad = jnp.pad(router_w[i].astype(F32), ((0, 0), (0, ROUTE_LANES - N_EXPERTS)))
    rw_hi = rw_pad.astype(BF16)
    rw_lo = (rw_pad - rw_hi.astype(F32)).astype(BF16)
    rb_pad = jnp.concatenate([router_b[i].astype(F32), jnp.full((ROUTE_LANES - N_EXPERTS,), -jnp.inf, F32)])
    return dict(
        ln1_g=ln1_g[i], w_in_bf=w_in[i].astype(BF16), mu_shift=mu_shift[i], w_pool_bf=w_pool[i].astype(BF16),
        pool_scale=pool_scale[i], w0=w0[i], w2_bf=w2[i].astype(BF16), a0=a0[i], a2_bf=a2[i].astype(BF16),
        g2_bf=g2[i].astype(BF16), k_k=k_k[i], k_a=k_a[i], r_k=r_k[i].reshape(RWKV_W), lnx_g=lnx_g[i],
        lnx_b=lnx_b[i], w_out_bf=w_out[i].astype(BF16), ln2_g=ln2_g[i], rw_hi=rw_hi, rw_lo=rw_lo,
        rb_pad=rb_pad.reshape(1, ROUTE_LANES), ones_bd=ones_bd, ln3_g=ln3_g[i],
        ple_gate_bf=ple_gate[i].astype(BF16), ple_proj_bf=ple_proj[i].astype(BF16))


def _layer_pair(hp, hs, pp, ps, pool_ctx, shift_prev, wkv0, wts, w_gu, b_gu, w_down, b_down, final_g):
    bp, tp, _ = hp.shape
    bs, ts, _ = hs.shape
    np_, ns = bp * tp, bs * ts
    tm_p = _tile(np_, 256)
    tm_s = _tile(ns, 128)
    zero_carry = jnp.zeros((SUBLANES, ROUTE_LANES), F32)
    fp = _front(hp, 0, jnp.zeros((bp, POOL_CTX, POOL_W), F32), jnp.zeros((bp, 1, RWKV_PROJ), F32),
                jnp.zeros((bp, N_HEADS, HEAD_DIM, HEAD_DIM), F32), wts, zero_carry, _tile(np_, 512))
    fs = _front(hs, PAST_LEN, pool_ctx, shift_prev, wkv0, wts, fp["counts"], tm_s)

    counts = fs["counts"][0, :N_EXPERTS].astype(I32)
    padded = (counts + EXPERT_BLOCK - 1) // EXPERT_BLOCK * EXPERT_BLOCK
    pad_end = jnp.cumsum(padded)
    start_pad = pad_end - padded
    n_blocks = -(-(np_ + ns) * TOP_K // EXPERT_BLOCK) + N_EXPERTS
    n_slots = n_blocks * EXPERT_BLOCK
    n_used = (pad_end[-1] // EXPERT_BLOCK).astype(I32)
    block_start = jnp.arange(n_blocks, dtype=I32) * EXPERT_BLOCK
    block_e = jnp.minimum(jnp.sum((block_start[:, None] >= pad_end[None, :]).astype(I32), axis=1), N_EXPERTS - 1)
    last_e = block_e[jnp.maximum(n_used - 1, 0)]
    block_e = jnp.where(jnp.arange(n_blocks) < n_used, block_e, last_e)

    def dest_of(route):
        idx = route[:, :TOP_K]
        rank = route[:, TOP_K:2 * TOP_K]
        start = jnp.sum(jnp.where(idx[..., None] == jnp.arange(N_EXPERTS, dtype=I32), start_pad, 0), axis=-1)
        return ((start + rank) * SUBLANES).reshape(-1).astype(I32)

    dest_p, dest_s = dest_of(fp["route"]), dest_of(fs["route"])
    tm_d = _tile(np_, 1024)
    dest_all = jnp.concatenate([dest_p, dest_s, jnp.zeros(((tm_d - ns) * TOP_K,), I32)])
    tail = jnp.stack([n_used * EXPERT_BLOCK, n_blocks - n_used]).astype(I32)
    xs = _dispatch(start_pad + counts, padded - counts, tail, dest_all, fp["xn"], fs["xn"], n_slots, tm_d)
    ys = _experts(block_e, n_used.reshape(1), xs, w_gu, b_gu, w_down, b_down)

    outs = []
    for f, dest, p, tm in ((fp, dest_p, pp, tm_p), (fs, dest_s, ps, tm_s)):
        n = f["h"].shape[0]
        outs.append(_combine(dest, ys, f["gates"], f["h"], p.reshape(n, PLE_DIM).astype(F32), wts["ln3_g"],
                             wts["ple_gate_bf"], wts["ple_proj_bf"], final_g, tm))
    return outs[0].reshape(hp.shape), outs[1].reshape(hs.shape), fp, fs


def kernel(x_prompt, x_sample, p_prompt, p_sample, cache_pool, state_shift, state_wkv, ln1_g, w_in, mu_shift, w_pool, pool_scale, w0, w2, a0, a2, g2, k_k, k_a, r_k, lnx_g, lnx_b, w_out, ln2_g, router_w, router_b, w_gu, b_gu, w_down, b_down, ln3_g, ple_gate, ple_proj, final_g):
    assert ln1_g.shape[0] == 1, "single-layer kernel"
    i = 0
    wts = _prep_weights(i, ln1_g, w_in, mu_shift, w_pool, pool_scale, w0, w2, a0, a2, g2, k_k, k_a, r_k,
                        lnx_g, lnx_b, w_out, ln2_g, router_w, router_b, ln3_g, ple_gate, ple_proj)
    y_p, y_s, fp, fs = _layer_pair(x_prompt, x_sample, p_prompt[i], p_sample[i], cache_pool[i], state_shift[i],
                                   state_wkv[i], wts, w_gu[i], b_gu[i], w_down[i], b_down[i], final_g)
    stack = lambda a: a[None]
    return (y_p, y_s,
            stack(fp["new_pool"]), stack(fp["new_shift"]), stack(fp["new_wkv"]),
            stack(fs["new_pool"]), stack(fs["new_shift"]), stack(fs["new_wkv"]))
```

```python
import functools

import jax
import jax.numpy as jnp
from jax import lax
from jax.experimental import pallas as pl
from jax.experimental.pallas import tpu as pltpu

F32 = jnp.float32
BF16 = jnp.bfloat16
I32 = jnp.int32

D_MODEL = 1024
POOL_W = 512
POOL_WINDOWS = (2, 4, 8, 16)
POOL_GW = 128
POOL_CTX = 15
POOL_HALO = 16
RWKV_W = 512
HEAD_DIM = 64
N_HEADS = 8
N_PAIRS = N_HEADS // 2
PAIR_W = 2 * HEAD_DIM
DECAY_LORA = 64
ICLR_LORA = 64
GATE_LORA = 128
RWKV_PROJ = 3 * RWKV_W + DECAY_LORA + ICLR_LORA + GATE_LORA
IN_PROJ = POOL_W + RWKV_PROJ
LNX_EPS = 64e-5
N_EXPERTS = 32
TOP_K = 4
D_EXPERT = 1024
SWIGLU_LIMIT = 7.0
SWIGLU_ALPHA = 1.702
PLE_DIM = 256
RMS_EPS = 1e-6
PAST_LEN = 2048

CHUNK = 64
EXPERT_BLOCK = 512
ROUTE_LANES = 128
SUBLANES = 8
LANES = 128
assert D_MODEL == SUBLANES * LANES
PREP_ROWS = 256
ISSUE_UNROLL = 8
GROUP_TILE = 256
EXP_M05 = 0.6065306597126334
V7X_VMEM_BYTES = 64 * 1024 * 1024

NT_DIMS = (((1,), (1,)), ((), ()))
TN_DIMS = (((0,), (0,)), ((), ()))


def _vmem_limit(est_bytes):
    return int(min(est_bytes * 3 // 2 + (4 << 20), V7X_VMEM_BYTES - (8 << 20)))


def _params(n_axes, est_bytes):
    return pltpu.CompilerParams(dimension_semantics=("arbitrary",) * n_axes,
                                vmem_limit_bytes=_vmem_limit(est_bytes))


def _dot(a, b):
    return jnp.dot(a, b, preferred_element_type=F32)


def _split2(x):
    hi = x.astype(BF16)
    lo = (x - hi.astype(F32)).astype(BF16)
    return hi, lo


def _split3(x):
    hi = x.astype(BF16)
    r1 = x - hi.astype(F32)
    mid = r1.astype(BF16)
    lo = (r1 - mid.astype(F32)).astype(BF16)
    return hi, mid, lo


def _group_sum(x, ones_bd):
    xb = x.astype(BF16)
    slabs = [_dot(xb[:, s:s + GROUP_TILE], ones_bd) for s in range(0, x.shape[1], GROUP_TILE)]
    return jnp.concatenate(slabs, axis=1)


def _store_row_tiles(ref, value):
    n = value.shape[0]
    for c in range(SUBLANES):
        ref[pl.ds(c, n, stride=SUBLANES), :] = value[:, c * LANES:(c + 1) * LANES]


def _load_row_tiles(ref, n):
    return jnp.concatenate([ref[pl.ds(c, n, stride=SUBLANES), :] for c in range(SUBLANES)], axis=1)


def _rmsnorm_rows(x, g):
    ms = jnp.mean(x * x, axis=-1, keepdims=True)
    return (x * lax.rsqrt(ms + RMS_EPS)) * g


def _in_proj_pool_kernel(x_ref, ctx_ref, g_ref, w_ref, wp_ref, ps_ref, zr_ref, po_ref, last_ref, buf, *, pos0):
    t = pl.program_id(1)
    bb, tt, _ = x_ref.shape
    xn = _rmsnorm_rows(x_ref[...].reshape(bb * tt, D_MODEL), g_ref[...])
    z = _dot(xn.astype(BF16), w_ref[...])
    zr_ref[...] = z[:, POOL_W:].reshape(bb, tt, RWKV_PROJ)

    @pl.when(t == 0)
    def _():
        buf[:, 0:POOL_HALO, :] = ctx_ref[...]

    buf[:, POOL_HALO:, :] = z[:, :POOL_W].reshape(bb, tt, POOL_W)
    pos = lax.broadcasted_iota(I32, (1, tt, 1), 1) + (t * tt + pos0)
    outs = []
    for gi, w in enumerate(POOL_WINDOWS):
        sl = slice(gi * POOL_GW, (gi + 1) * POOL_GW)
        cur = buf[:, POOL_HALO:POOL_HALO + tt, sl]
        acc = cur
        for j in range(1, w):
            acc = acc + buf[:, POOL_HALO - j:POOL_HALO - j + tt, sl]
        cnt = jnp.minimum(w, pos + 1).astype(F32)
        pooled = (acc / cnt - cur).reshape(bb * tt, POOL_GW)
        outs.append(_dot(pooled.astype(BF16), wp_ref[gi]))
    po = jnp.concatenate(outs, axis=-1) * ps_ref[...]
    po_ref[...] = po.reshape(bb, tt, POOL_W).astype(po_ref.dtype)
    last = buf[:, tt:tt + POOL_HALO, :]
    last_ref[...] = last
    buf[:, 0:POOL_HALO, :] = last


def _in_proj_pool(x, ctx16, ln1_g, w_in_bf, w_pool_bf, pool_scale, pos0, bb, tt):
    b, t, _ = x.shape
    cst = lambda shape: pl.BlockSpec(shape, lambda i, j: (0,) * len(shape))
    seq = lambda w: pl.BlockSpec((bb, tt, w), lambda i, j: (i, j, 0))
    halo = pl.BlockSpec((bb, POOL_HALO, POOL_W), lambda i, j: (i, 0, 0))
    est = (2 * bb * tt * (D_MODEL + RWKV_PROJ) * 4 + 2 * D_MODEL * IN_PROJ * 2 + 3 * bb * tt * IN_PROJ * 4
           + bb * (tt + POOL_HALO) * POOL_W * 4)
    return pl.pallas_call(
        functools.partial(_in_proj_pool_kernel, pos0=pos0),
        grid=(b // bb, t // tt),
        in_specs=[seq(D_MODEL), halo, cst((1, D_MODEL)), cst((D_MODEL, IN_PROJ)),
                  cst((len(POOL_WINDOWS), POOL_GW, POOL_GW)), cst((1, POOL_W))],
        out_specs=[seq(RWKV_PROJ), seq(POOL_W), halo],
        out_shape=[jax.ShapeDtypeStruct((b, t, RWKV_PROJ), F32), jax.ShapeDtypeStruct((b, t, POOL_W), BF16),
                   jax.ShapeDtypeStruct((b, POOL_HALO, POOL_W), F32)],
        scratch_shapes=[pltpu.VMEM((bb, tt + POOL_HALO, POOL_W), F32)],
        compiler_params=_params(2, est),
        name="in_proj_pool",
    )(x, ctx16, ln1_g.reshape(1, D_MODEL), w_in_bf, w_pool_bf, pool_scale.reshape(1, POOL_W))


def _rwkv_prepare(zr, prev, mu, w0, w2, a0, a2, g2, k_k, k_a, r_k, ones_bd):
    zs = zr + (prev - zr) * mu
    o1, o2, o3 = RWKV_W, 2 * RWKV_W, 3 * RWKV_W
    o4, o5 = o3 + DECAY_LORA, o3 + DECAY_LORA + ICLR_LORA
    r, k, v = zs[:, :o1], zs[:, o1:o2], zs[:, o2:o3]
    wd, ad, gd = zs[:, o3:o4], zs[:, o4:o5], zs[:, o5:]
    wz = w0 + _dot(jnp.tanh(wd).astype(BF16), w2)
    lw = -EXP_M05 * jax.nn.sigmoid(wz)
    a = jax.nn.sigmoid(a0 + _dot(ad.astype(BF16), a2))
    g = _dot(jax.nn.sigmoid(gd).astype(BF16), g2)
    kk = k * k_k
    kk = kk / jnp.maximum(jnp.sqrt(_group_sum(kk * kk, ones_bd)), 1e-12)
    kx = k * (1.0 + (a - 1.0) * k_a)
    bonus = _group_sum(r * kx * r_k, ones_bd) * v
    return r, lw, kx, v, -kk, kk * a, g, bonus


def _pair_blockdiag(z, left):
    return jnp.concatenate([jnp.where(left, z, 0.0), jnp.where(left, 0.0, z)], axis=0)


def _wkv_kernel(zr_ref, halo_ref, sh_ref, mu_ref, w0_ref, w2_ref, a0_ref, a2_ref, g2_ref, kk_ref, ka_ref, rk_ref,
                ones_ref, lg_ref, lb_ref, s0_ref, o_ref, sf_ref,
                buf, r_ref, lw_ref, kx_ref, v_ref, na_ref, kb_ref, g_ref, bonus_ref, y_ref, s_scr, *, t_valid):
    t = pl.program_id(1)
    tt = zr_ref.shape[1]
    c = CHUNK
    rows_per_pass = min(tt, PREP_ROWS)

    @pl.when(t == 0)
    def _():
        s_scr[...] = s0_ref[0]

    buf[0:SUBLANES, :] = jnp.where(t == 0, sh_ref[0], halo_ref[0])
    buf[SUBLANES:, :] = zr_ref[0]
    ones_bd = ones_ref[...]
    for r0 in range(0, tt, rows_per_pass):
        rows = slice(r0, r0 + rows_per_pass)
        vals = _rwkv_prepare(zr_ref[0, rows, :], buf[SUBLANES - 1 + r0:SUBLANES - 1 + r0 + rows_per_pass, :],
                             mu_ref[...], w0_ref[...], w2_ref[...], a0_ref[...], a2_ref[...], g2_ref[...],
                             kk_ref[...], ka_ref[...], rk_ref[...], ones_bd)
        if t_valid is not None:
            live = lax.broadcasted_iota(I32, (rows_per_pass, 1), 0) + (t * tt + r0) < t_valid
            vals = tuple(jnp.where(live, x, 0.0) for x in vals)
        for ref, x in zip((r_ref, lw_ref, kx_ref, v_ref, na_ref, kb_ref, g_ref, bonus_ref), vals):
            ref[rows, :] = x

    lane = lax.broadcasted_iota(I32, (c, 2 * c), 1)
    rowi = lax.broadcasted_iota(I32, (c, 2 * c), 0)
    jm = lane & (c - 1)
    strict = jm < rowi
    incl = jm <= rowi
    eye_pair = jnp.where(jm == rowi, 1.0, 0.0).astype(F32)
    left = lax.broadcasted_iota(I32, (c, PAIR_W), 1) < HEAD_DIM
    rr = lax.broadcasted_iota(I32, (c, c), 0)
    cc = lax.broadcasted_iota(I32, (c, c), 1)
    tri_incl_bf = jnp.where(cc <= rr, 1.0, 0.0).astype(BF16)
    br = lax.broadcasted_iota(I32, (PAIR_W, PAIR_W), 0) < HEAD_DIM
    bc = lax.broadcasted_iota(I32, (PAIR_W, PAIR_W), 1) < HEAD_DIM
    bd_mask = br == bc
    bf = lambda x: x.astype(BF16)
    bd = lambda z: _pair_blockdiag(z, left)
    n_chunks = tt // c

    cums = []
    for ci in range(n_chunks):
        h3 = _split3(lw_ref[ci * c:(ci + 1) * c, :])
        cums.append(_dot(tri_incl_bf, h3[0]) + _dot(tri_incl_bf, h3[1]) + _dot(tri_incl_bf, h3[2]))

    chains = []
    for ci in range(n_chunks):
        rows = slice(ci * c, (ci + 1) * c)
        for p in range(N_PAIRS):
            cols = slice(p * PAIR_W, (p + 1) * PAIR_W)
            cum = cums[ci][:, cols]
            lw, kx, kb = lw_ref[rows, cols], kx_ref[rows, cols], kb_ref[rows, cols]
            tot = cum[c - 1:c, :]
            p_end = jnp.exp(tot - cum)
            inv_p = jnp.exp(-cum)
            chains.append(dict(
                ci=ci, p=p, v=v_ref[rows, cols], p_c=jnp.exp(tot),
                a_t=na_ref[rows, cols] * jnp.exp(cum - lw), r_t=r_ref[rows, cols] * jnp.exp(cum),
                b_t=kb * inv_p, k_t=kx * inv_p, b_h=kb * p_end, k_h=kx * p_end))

    for d in chains:
        lhs = bf(jnp.concatenate([d["a_t"], d["r_t"]], axis=0))
        rhs = bf(jnp.concatenate([bd(d["b_t"]), bd(d["k_t"])], axis=0))
        sc = lax.dot_general(lhs, rhs, NT_DIMS, preferred_element_type=F32)
        d["a_ab"] = jnp.where(strict, sc[:c, :2 * c], 0.0)
        d["a_k"] = jnp.concatenate([jnp.where(strict, sc[:c, 2 * c:], 0.0),
                                    jnp.where(incl, sc[c:, 2 * c:], 0.0)], axis=0)
        d["a_rb"] = jnp.where(incl, sc[c:, :2 * c], 0.0)

    for d in chains:
        d["x"] = _dot(bf(d["a_ab"]), bf(bd(d["a_ab"])))
        d["t"] = eye_pair + d["a_ab"]
    for _ in range(c.bit_length() - 3):
        for d in chains:
            st = _dot(bf(jnp.concatenate([d["x"], d["t"]], axis=0)), bf(bd(d["x"])))
            d["x"] = st[:c]
            d["t"] = d["t"] + st[c:]
    for d in chains:
        d["t"] = d["t"] + _dot(bf(d["t"]), bf(bd(d["x"])))

    for d in chains:
        wv = _dot(bf(d["a_k"]), bf(bd(d["v"])))
        d["w1"], d["rkv"] = wv[:c], wv[c:]
    for d in chains:
        tu = _dot(bf(d["t"]), bf(jnp.concatenate([bd(d["w1"]), bd(d["a_t"])], axis=1)))
        d["u_loc"], d["a_tt"] = tu[:, :PAIR_W], tu[:, PAIR_W:]
    for d in chains:
        ar = _dot(bf(d["a_rb"]), bf(jnp.concatenate([bd(d["u_loc"]), bd(d["a_tt"])], axis=1)))
        d["y_loc"] = ar[:, :PAIR_W] + d["rkv"]
        d["r_g"] = d["r_t"] + ar[:, PAIR_W:]
    for d in chains:
        lhs = jnp.concatenate([jnp.concatenate([d["v"], jnp.zeros_like(d["v"])], axis=1),
                               jnp.concatenate([d["u_loc"], d["a_tt"]], axis=1)], axis=0)
        rhs = jnp.concatenate([d["k_h"], d["b_h"]], axis=0)
        dp = lax.dot_general(bf(lhs), bf(rhs), TN_DIMS, preferred_element_type=F32)
        d["d_loc"] = jnp.where(bd_mask, dp[:PAIR_W], 0.0)
        d["phi"] = bf(jnp.where(bd_mask, dp[PAIR_W:], 0.0))

    state = [s_scr[p] for p in range(N_PAIRS)]
    for d in chains:
        p = d["p"]
        s_hi, s_lo = _split2(state[p])
        y = lax.dot_general(bf(d["r_g"]), s_hi, NT_DIMS, preferred_element_type=F32) + d["y_loc"]
        y_ref[d["ci"] * c:(d["ci"] + 1) * c, p * PAIR_W:(p + 1) * PAIR_W] = y
        state[p] = state[p] * d["p_c"] + (_dot(s_hi, d["phi"]) + _dot(s_lo, d["phi"])) + d["d_loc"]
    for p in range(N_PAIRS):
        s_scr[p] = state[p]

    @pl.when(t == pl.num_programs(1) - 1)
    def _():
        sf_ref[0] = s_scr[...]

    inv_n = 1.0 / HEAD_DIM
    for r0 in range(0, tt, rows_per_pass):
        rows = slice(r0, r0 + rows_per_pass)
        y = y_ref[rows, :]
        mu = _group_sum(y, ones_bd) * inv_n
        dlt = y - mu
        var = _group_sum(dlt * dlt, ones_bd) * inv_n
        yn = dlt * lax.rsqrt(var + LNX_EPS) * lg_ref[...] + lb_ref[...]
        o_ref[0, rows, :] = ((yn + bonus_ref[rows, :]) * g_ref[rows, :]).astype(o_ref.dtype)


def _rwkv_mixer(zr, sh8, s0_pair, wts, tt, t_valid):
    b, t, _ = zr.shape
    assert CHUNK == HEAD_DIM and tt % CHUNK == 0
    hb = tt // SUBLANES
    row = lambda x: x.reshape(1, -1)
    cst = lambda shape: pl.BlockSpec(shape, lambda i, j: (0,) * len(shape))
    seq = pl.BlockSpec((1, tt, RWKV_W), lambda i, j: (i, j, 0))
    st = pl.BlockSpec((1, N_PAIRS, PAIR_W, PAIR_W), lambda i, j: (i, 0, 0, 0))
    est = 3 * tt * RWKV_PROJ * 4 + 10 * tt * RWKV_W * 4 + 5 * N_PAIRS * PAIR_W * PAIR_W * 4 + (8 << 20)
    return pl.pallas_call(
        functools.partial(_wkv_kernel, t_valid=t_valid),
        grid=(b, t // tt),
        in_specs=[pl.BlockSpec((1, tt, RWKV_PROJ), lambda i, j: (i, j, 0)),
                  pl.BlockSpec((1, SUBLANES, RWKV_PROJ), lambda i, j: (i, jnp.maximum(j * hb - 1, 0), 0)),
                  pl.BlockSpec((1, SUBLANES, RWKV_PROJ), lambda i, j: (i, 0, 0)),
                  cst((1, RWKV_PROJ)), cst((1, RWKV_W)), cst((DECAY_LORA, RWKV_W)),
                  cst((1, RWKV_W)), cst((ICLR_LORA, RWKV_W)), cst((GATE_LORA, RWKV_W)),
                  cst((1, RWKV_W)), cst((1, RWKV_W)), cst((1, RWKV_W)), cst((GROUP_TILE, GROUP_TILE)),
                  cst((1, RWKV_W)), cst((1, RWKV_W)), st],
        out_specs=[seq, st],
        out_shape=[jax.ShapeDtypeStruct((b, t, RWKV_W), BF16),
                   jax.ShapeDtypeStruct((b, N_PAIRS, PAIR_W, PAIR_W), F32)],
        scratch_shapes=([pltpu.VMEM((tt + SUBLANES, RWKV_PROJ), F32)] + [pltpu.VMEM((tt, RWKV_W), F32)] * 9
                        + [pltpu.VMEM((N_PAIRS, PAIR_W, PAIR_W), F32)]),
        compiler_params=_params(2, est),
        name="rwkv_mixer",
    )(zr, zr, sh8, row(wts["mu_shift"]), row(wts["w0"]), wts["w2_bf"], row(wts["a0"]), wts["a2_bf"], wts["g2_bf"],
      row(wts["k_k"]), row(wts["k_a"]), row(wts["r_k"]), wts["ones_bd"], row(wts["lnx_g"]), row(wts["lnx_b"]),
      s0_pair)


def _state_to_pairs(s):
    b = s.shape[0]
    s = s.astype(F32).reshape(b, N_PAIRS, 2, HEAD_DIM, HEAD_DIM)
    z = jnp.zeros((b, N_PAIRS, HEAD_DIM, HEAD_DIM), F32)
    return jnp.concatenate([jnp.concatenate([s[:, :, 0], z], axis=-1),
                            jnp.concatenate([z, s[:, :, 1]], axis=-1)], axis=-2)


def _pairs_to_state(sp):
    b = sp.shape[0]
    s = jnp.stack([sp[:, :, :HEAD_DIM, :HEAD_DIM], sp[:, :, HEAD_DIM:, HEAD_DIM:]], axis=2)
    return s.reshape(b, N_HEADS, HEAD_DIM, HEAD_DIM)


def _mix_out_kernel(rw_ref, po_ref, x_ref, wo_ref, ln2_ref, rw_hi_ref, rw_lo_ref, rb_ref, cin_ref,
                    h_o, xn_o, route_o, gate_o, cnt_o, carry):
    i = pl.program_id(0)
    tm = x_ref.shape[0]

    @pl.when(i == 0)
    def _():
        carry[...] = cin_ref[...]

    mix = _dot(po_ref[...], wo_ref[:POOL_W, :]) + _dot(rw_ref[...], wo_ref[POOL_W:, :])
    h = x_ref[...] + mix
    h_o[...] = h
    xn = _rmsnorm_rows(h, ln2_ref[...])
    _store_row_tiles(xn_o, xn)

    x_hi, x_lo = _split2(xn)
    logits = (_dot(x_hi, rw_hi_ref[...]) + _dot(x_hi, rw_lo_ref[...]) + _dot(x_lo, rw_hi_ref[...])
              + rb_ref[...])
    lane = lax.broadcasted_iota(I32, (tm, ROUTE_LANES), 1)
    lane_f = lane.astype(F32)
    vals, idxs, hots = [], [], []
    work = logits
    for _ in range(TOP_K):
        m = jnp.max(work, axis=-1, keepdims=True)
        idx = jnp.min(jnp.where(work == m, lane_f, float(ROUTE_LANES)), axis=-1, keepdims=True)
        hit = lane_f == idx
        vals.append(m)
        idxs.append(idx)
        hots.append(jnp.where(hit, 1.0, 0.0).astype(F32))
        work = jnp.where(hit, -jnp.inf, work)
    exps = [jnp.exp(vv - vals[0]) for vv in vals]
    den = exps[0] + exps[1] + exps[2] + exps[3]

    hot_all = hots[0] + hots[1] + hots[2] + hots[3]
    rr = lax.broadcasted_iota(I32, (tm, tm), 0)
    cc = lax.broadcasted_iota(I32, (tm, tm), 1)
    tri_strict = jnp.where(cc < rr, 1.0, 0.0).astype(BF16)
    prefix = _dot(tri_strict, hot_all.astype(BF16)) + carry[0:1, :]
    route = jnp.zeros((tm, ROUTE_LANES), F32)
    gates = jnp.zeros((tm, ROUTE_LANES), F32)
    for j in range(TOP_K):
        rank = jnp.sum(hots[j] * prefix, axis=-1, keepdims=True)
        route = jnp.where(lane == j, idxs[j], route)
        route = jnp.where(lane == TOP_K + j, rank, route)
        gates = jnp.where(lane == j, exps[j] / den, gates)
    route_o[...] = route.astype(I32)
    gate_o[...] = gates
    new_carry = carry[...] + jnp.sum(hot_all, axis=0, keepdims=True)
    carry[...] = new_carry
    cnt_o[...] = new_carry


def _mix_out(rw2d, po2d, x2d, w_out_bf, ln2_g, rw_hi, rw_lo, rb_pad, carry_in, tm):
    n = x2d.shape[0]
    row = lambda x: x.reshape(1, -1)
    cst = lambda shape: pl.BlockSpec(shape, lambda i: (0,) * len(shape))
    half = pl.BlockSpec((tm, RWKV_W), lambda i: (i, 0))
    full = pl.BlockSpec((tm, D_MODEL), lambda i: (i, 0))
    lanes = pl.BlockSpec((tm, ROUTE_LANES), lambda i: (i, 0))
    est = (2 * (2 * tm * RWKV_W * 2 + 3 * tm * D_MODEL * 4) + 2 * D_MODEL * D_MODEL * 2
           + 8 * tm * D_MODEL * 4 + 4 * tm * tm)
    return pl.pallas_call(
        _mix_out_kernel,
        grid=(n // tm,),
        in_specs=[half, half, full, cst((D_MODEL, D_MODEL)),
                  cst((1, D_MODEL)), cst((D_MODEL, ROUTE_LANES)), cst((D_MODEL, ROUTE_LANES)),
                  cst((1, ROUTE_LANES)), cst((SUBLANES, ROUTE_LANES))],
        out_specs=[full, pl.BlockSpec((tm * SUBLANES, LANES), lambda i: (i, 0)), lanes, lanes,
                   cst((SUBLANES, ROUTE_LANES))],
        out_shape=[jax.ShapeDtypeStruct((n, D_MODEL), F32), jax.ShapeDtypeStruct((n * SUBLANES, LANES), F32),
                   jax.ShapeDtypeStruct((n, ROUTE_LANES), I32), jax.ShapeDtypeStruct((n, ROUTE_LANES), F32),
                   jax.ShapeDtypeStruct((SUBLANES, ROUTE_LANES), F32)],
        scratch_shapes=[pltpu.VMEM((SUBLANES, ROUTE_LANES), F32)],
        compiler_params=_params(1, est),
        name="mix_out_router",
    )(rw2d, po2d, x2d, w_out_bf, row(ln2_g), rw_hi, rw_lo, rb_pad, carry_in)


def _tiles(ref, first_row, n_slots):
    if not isinstance(first_row, int):
        first_row = pl.multiple_of(first_row, SUBLANES)
    return ref.at[pl.ds(first_row, n_slots * SUBLANES), :]


def _dispatch_kernel(pad_start_ref, pad_len_ref, tail_ref, dest_ref, xa_ref, xb_ref, o_hbm, zbuf, sem, sem_fill,
                     *, n_main, n_last):
    i = pl.program_id(0)
    tm = xa_ref.shape[0] // SUBLANES
    blk = zbuf.shape[0] // SUBLANES

    @pl.when(i == 0)
    def _():
        zbuf[...] = jnp.zeros_like(zbuf)

        def per_expert(e, totals):
            start, n = pad_start_ref[e], pad_len_ref[e]
            n_groups = lax.shift_right_logical(n, 3)
            n_single = n - n_groups * SUBLANES

            def group(g, carry):
                pltpu.make_async_copy(_tiles(zbuf, 0, SUBLANES),
                                      _tiles(o_hbm, (start + g * SUBLANES) * SUBLANES, SUBLANES), sem_fill).start()
                return carry

            def single(r, carry):
                pltpu.make_async_copy(_tiles(zbuf, 0, 1),
                                      _tiles(o_hbm, (start + n_groups * SUBLANES + r) * SUBLANES, 1), sem_fill).start()
                return carry

            lax.fori_loop(0, n_groups, group, 0)
            lax.fori_loop(0, n_single, single, 0)
            return totals[0] + n_groups, totals[1] + n_single

        n_groups_filled, n_single_filled = lax.fori_loop(0, N_EXPERTS, per_expert, (0, 0))

        def wait_group(g, carry):
            pltpu.make_async_copy(_tiles(zbuf, 0, SUBLANES), _tiles(o_hbm, 0, SUBLANES), sem_fill).wait()
            return carry

        def wait_single(r, carry):
            pltpu.make_async_copy(_tiles(zbuf, 0, 1), _tiles(o_hbm, 0, 1), sem_fill).wait()
            return carry

        lax.fori_loop(0, n_groups_filled, wait_group, 0)
        lax.fori_loop(0, n_single_filled, wait_single, 0)

        def fill_block(b, carry):
            pltpu.make_async_copy(zbuf, _tiles(o_hbm, (tail_ref[0] + b * blk) * SUBLANES, blk), sem_fill).start()
            return carry

        def wait_block(b, carry):
            pltpu.make_async_copy(zbuf, _tiles(o_hbm, 0, blk), sem_fill).wait()
            return carry

        lax.fori_loop(0, tail_ref[1], fill_block, 0)
        lax.fori_loop(0, tail_ref[1], wait_block, 0)

    def scatter_rows(x_ref, n_rows):
        def issue(n, carry):
            src = _tiles(x_ref, n * SUBLANES, 1)
            for j in range(TOP_K):
                pltpu.make_async_copy(src, _tiles(o_hbm, dest_ref[n * TOP_K + j], 1), sem).start(priority=j % 2)
            return carry

        lax.fori_loop(0, n_rows, issue, 0, unroll=ISSUE_UNROLL)
        pltpu.make_async_copy(_tiles(o_hbm, 0, n_rows * TOP_K), _tiles(o_hbm, 0, n_rows * TOP_K), sem).wait()

    @pl.when(i < n_main)
    def _():
        scatter_rows(xa_ref, tm)

    @pl.when(i == n_main)
    def _():
        scatter_rows(xb_ref, n_last)


def _dispatch(pad_start, pad_len, tail, dest_rows, xn_a, xn_b, n_slots, tm):
    n_main = xn_a.shape[0] // (tm * SUBLANES)
    n_last = xn_b.shape[0] // SUBLANES
    assert n_last <= tm and n_slots >= tm * TOP_K
    grid_spec = pltpu.PrefetchScalarGridSpec(
        num_scalar_prefetch=3,
        grid=(n_main + 1,),
        in_specs=[pl.BlockSpec((tm * TOP_K,), lambda i, *_: (i,), memory_space=pltpu.SMEM),
                  pl.BlockSpec((tm * SUBLANES, LANES), lambda i, *_: (jnp.minimum(i, n_main - 1), 0)),
                  pl.BlockSpec((n_last * SUBLANES, LANES), lambda i, *_: (0, 0))],
        out_specs=pl.BlockSpec(memory_space=pl.ANY),
        scratch_shapes=[pltpu.VMEM((EXPERT_BLOCK * SUBLANES, LANES), F32), pltpu.SemaphoreType.DMA(()),
                        pltpu.SemaphoreType.DMA(())],
    )
    return pl.pallas_call(
        functools.partial(_dispatch_kernel, n_main=n_main, n_last=n_last),
        grid_spec=grid_spec,
        out_shape=jax.ShapeDtypeStruct((n_slots * SUBLANES, LANES), F32),
        compiler_params=_params(1, 2 * (tm + n_last) * D_MODEL * 4 + EXPERT_BLOCK * D_MODEL * 4),
        name="dispatch",
    )(pad_start, pad_len, tail, dest_rows, xn_a, xn_b)


def _expert_kernel(be_ref, nu_ref, xs_ref, wgu_ref, bgu_ref, wdn_ref, bdn_ref, o_ref, wgu_bf, wdn_bf):
    i = pl.program_id(0)
    used = i < nu_ref[0]
    changed = jnp.logical_or(i == 0, be_ref[i] != be_ref[jnp.maximum(i - 1, 0)])

    @pl.when(jnp.logical_and(used, changed))
    def _():
        wgu_bf[...] = wgu_ref[0].astype(BF16)
        wdn_bf[...] = wdn_ref[0].astype(BF16)

    @pl.when(used)
    def _():
        x = _load_row_tiles(xs_ref, xs_ref.shape[0] // SUBLANES)
        gu = _dot(x.astype(BF16), wgu_bf[...]) + bgu_ref[0]
        gate = jnp.minimum(gu[:, :D_EXPERT], SWIGLU_LIMIT)
        up = jnp.clip(gu[:, D_EXPERT:], -SWIGLU_LIMIT, SWIGLU_LIMIT)
        hmid = (up + 1.0) * (gate * jax.nn.sigmoid(SWIGLU_ALPHA * gate))
        _store_row_tiles(o_ref, _dot(hmid.astype(BF16), wdn_bf[...]) + bdn_ref[0])

    @pl.when(jnp.logical_not(used))
    def _():
        o_ref[...] = jnp.zeros_like(o_ref)


def _experts(block_e, n_used, xs, w_gu, b_gu, w_down, b_down):
    n_slots = xs.shape[0] // SUBLANES
    nb = n_slots // EXPERT_BLOCK
    blk = EXPERT_BLOCK
    slots = pl.BlockSpec((blk * SUBLANES, LANES), lambda i, be, nu: (i, 0))
    est = (2 * 2 * blk * D_MODEL * 4 + 2 * (D_MODEL * 2 * D_EXPERT + D_EXPERT * D_MODEL) * 4
           + (D_MODEL * 2 * D_EXPERT + D_EXPERT * D_MODEL) * 2 + 4 * blk * 2 * D_EXPERT * 4)
    grid_spec = pltpu.PrefetchScalarGridSpec(
        num_scalar_prefetch=2,
        grid=(nb,),
        in_specs=[slots,
                  pl.BlockSpec((1, D_MODEL, 2 * D_EXPERT), lambda i, be, nu: (be[i], 0, 0)),
                  pl.BlockSpec((1, 1, 2 * D_EXPERT), lambda i, be, nu: (be[i], 0, 0)),
                  pl.BlockSpec((1, D_EXPERT, D_MODEL), lambda i, be, nu: (be[i], 0, 0)),
                  pl.BlockSpec((1, 1, D_MODEL), lambda i, be, nu: (be[i], 0, 0))],
        out_specs=slots,
        scratch_shapes=[pltpu.VMEM((D_MODEL, 2 * D_EXPERT), BF16), pltpu.VMEM((D_EXPERT, D_MODEL), BF16)],
    )
    return pl.pallas_call(
        _expert_kernel,
        grid_spec=grid_spec,
        out_shape=jax.ShapeDtypeStruct((n_slots * SUBLANES, LANES), F32),
        compiler_params=_params(1, est),
        name="experts",
    )(block_e, n_used, xs, w_gu, b_gu.reshape(N_EXPERTS, 1, 2 * D_EXPERT), w_down,
      b_down.reshape(N_EXPERTS, 1, D_MODEL))


def _combine_kernel(dest_ref, dest_next_ref, ys_hbm, gate_ref, h_ref, p_ref, ln3_ref, pg_ref, pp_ref, fg_ref,
                    o_ref, gbuf, sem):
    i = pl.program_id(0)
    tm = h_ref.shape[0]
    slot = lax.rem(i, 2)

    def issue(d_ref, s):
        def body(n, carry):
            for j in range(TOP_K):
                pltpu.make_async_copy(_tiles(ys_hbm, d_ref[n * TOP_K + j], 1),
                                      _tiles(gbuf.at[s, j], n * SUBLANES, 1), sem.at[s]).start(priority=j % 2)
            return carry
        lax.fori_loop(0, tm, body, 0, unroll=ISSUE_UNROLL)

    @pl.when(i == 0)
    def _():
        issue(dest_ref, 0)

    @pl.when(i + 1 < pl.num_programs(0))
    def _():
        issue(dest_next_ref, 1 - slot)

    for j in range(TOP_K):
        pltpu.make_async_copy(gbuf.at[slot, j], gbuf.at[slot, j], sem.at[slot]).wait()

    gates = gate_ref[...]
    slabs = []
    for c in range(SUBLANES):
        acc = gates[:, 0:1] * gbuf[slot, 0, pl.ds(c, tm, stride=SUBLANES), :]
        for j in range(1, TOP_K):
            acc = acc + gates[:, j:j + 1] * gbuf[slot, j, pl.ds(c, tm, stride=SUBLANES), :]
        slabs.append(acc)
    h = h_ref[...] + jnp.concatenate(slabs, axis=1)
    gate = jax.nn.sigmoid(_dot(_rmsnorm_rows(h, ln3_ref[...]).astype(BF16), pg_ref[...]))
    h = h + gate * _dot(p_ref[...].astype(BF16), pp_ref[...])
    o_ref[...] = _rmsnorm_rows(h, fg_ref[...])


def _combine(dest_flat, ys, gates, h2d, p2d, ln3_g, ple_gate_bf, ple_proj_bf, final_g, tm):
    n = h2d.shape[0]
    nb = n // tm
    row = lambda x: x.reshape(1, -1)
    cst = lambda shape: pl.BlockSpec(shape, lambda i: (0,) * len(shape))
    est = (2 * TOP_K * tm * D_MODEL * 4 + 2 * (2 * tm * D_MODEL * 4 + tm * PLE_DIM * 4 + tm * ROUTE_LANES * 4)
           + 2 * (D_MODEL * D_MODEL * 2 + PLE_DIM * D_MODEL * 2) + 6 * tm * D_MODEL * 4)
    return pl.pallas_call(
        _combine_kernel,
        grid=(nb,),
        in_specs=[pl.BlockSpec((tm * TOP_K,), lambda i: (i,), memory_space=pltpu.SMEM),
                  pl.BlockSpec((tm * TOP_K,), lambda i: (jnp.minimum(i + 1, nb - 1),), memory_space=pltpu.SMEM),
                  pl.BlockSpec(memory_space=pl.ANY),
                  pl.BlockSpec((tm, ROUTE_LANES), lambda i: (i, 0)),
                  pl.BlockSpec((tm, D_MODEL), lambda i: (i, 0)),
                  pl.BlockSpec((tm, PLE_DIM), lambda i: (i, 0)),
                  cst((1, D_MODEL)), cst((D_MODEL, D_MODEL)), cst((PLE_DIM, D_MODEL)), cst((1, D_MODEL))],
        out_specs=pl.BlockSpec((tm, D_MODEL), lambda i: (i, 0)),
        out_shape=jax.ShapeDtypeStruct((n, D_MODEL), F32),
        scratch_shapes=[pltpu.VMEM((2, TOP_K, tm * SUBLANES, LANES), F32), pltpu.SemaphoreType.DMA((2,))],
        compiler_params=_params(1, est),
        name="combine_ple_norm",
    )(dest_flat, dest_flat, ys, gates, h2d, p2d, row(ln3_g), ple_gate_bf, ple_proj_bf, row(final_g))


def _tile(n, pref):
    t = min(n, pref)
    assert n % t == 0, (n, pref)
    return t


def _front(x, pos0, pool_ctx, shift_prev, wkv0, wts, carry_in, tm_mix):
    b, t, _ = x.shape
    n = b * t
    assert t >= POOL_CTX and t % SUBLANES == 0
    x2d = x.reshape(n, D_MODEL)
    ctx16 = jnp.concatenate([jnp.zeros((b, POOL_HALO - POOL_CTX, POOL_W), F32), pool_ctx.astype(F32)], axis=1)
    tt = _tile(t, 512)
    bb = _tile(b, max(1, 128 // tt))
    zr, pool_out, last = _in_proj_pool(x, ctx16, wts["ln1_g"], wts["w_in_bf"], wts["w_pool_bf"],
                                       wts["pool_scale"], pos0, bb, tt)
    new_pool = last[:, POOL_HALO - POOL_CTX:]

    sh8 = jnp.broadcast_to(shift_prev.astype(F32), (b, SUBLANES, RWKV_PROJ))
    new_shift = zr[:, -1:]

    t_pad = -(-t // CHUNK) * CHUNK
    zr_pad = zr if t_pad == t else jnp.pad(zr, ((0, 0), (0, t_pad - t), (0, 0)))
    rw_out, s_fin = _rwkv_mixer(zr_pad, sh8, _state_to_pairs(wkv0), wts, _tile(t_pad, 8 * CHUNK),
                                None if t_pad == t else t)
    rw_out = rw_out[:, :t]
    new_wkv = _pairs_to_state(s_fin)

    h, xn, route, gates, counts = _mix_out(
        rw_out.reshape(n, RWKV_W), pool_out.reshape(n, POOL_W), x2d, wts["w_out_bf"], wts["ln2_g"],
        wts["rw_hi"], wts["rw_lo"], wts["rb_pad"], carry_in, tm_mix)
    return dict(h=h, xn=xn, route=route, gates=gates, counts=counts,
                new_pool=new_pool, new_shift=new_shift, new_wkv=new_wkv)


def _prep_weights(i, ln1_g, w_in, mu_shift, w_pool, pool_scale, w0, w2, a0, a2, g2, k_k, k_a, r_k,
                  lnx_g, lnx_b, w_out, ln2_g, router_w, router_b, ln3_g, ple_gate, ple_proj):
    head = jnp.arange(GROUP_TILE) // HEAD_DIM
    ones_bd = (head[:, None] == head[None, :]).astype(BF16)
    rw_pad = jnp.pad(router_w[i].astype(F32), ((0, 0), (0, ROUTE_LANES - N_EXPERTS)))
    rw_hi = rw_pad.astype(BF16)
    rw_lo = (rw_pad - rw_hi.astype(F32)).astype(BF16)
    rb_pad = jnp.concatenate([router_b[i].astype(F32), jnp.full((ROUTE_LANES - N_EXPERTS,), -jnp.inf, F32)])
    return dict(
        ln1_g=ln1_g[i], w_in_bf=w_in[i].astype(BF16), mu_shift=mu_shift[i], w_pool_bf=w_pool[i].astype(BF16),
        pool_scale=pool_scale[i], w0=w0[i], w2_bf=w2[i].astype(BF16), a0=a0[i], a2_bf=a2[i].astype(BF16),
        g2_bf=g2[i].astype(BF16), k_k=k_k[i], k_a=k_a[i], r_k=r_k[i].reshape(RWKV_W), lnx_g=lnx_g[i],
        lnx_b=lnx_b[i], w_out_bf=w_out[i].astype(BF16), ln2_g=ln2_g[i], rw_hi=rw_hi, rw_lo=rw_lo,
        rb_pad=rb_pad.reshape(1, ROUTE_LANES), ones_bd=ones_bd, ln3_g=ln3_g[i],
        ple_gate_bf=ple_gate[i].astype(BF16), ple_proj_bf=ple_proj[i].astype(BF16))


def _layer_pair(hp, hs, pp, ps, pool_ctx, shift_prev, wkv0, wts, w_gu, b_gu, w_down, b_down, final_g):
    bp, tp, _ = hp.shape
    bs, ts, _ = hs.shape
    np_, ns = bp * tp, bs * ts
    tm_p = _tile(np_, 256)
    tm_s = _tile(ns, 128)
    zero_carry = jnp.zeros((SUBLANES, ROUTE_LANES), F32)
    fp = _front(hp, 0, jnp.zeros((bp, POOL_CTX, POOL_W), F32), jnp.zeros((bp, 1, RWKV_PROJ), F32),
                jnp.zeros((bp, N_HEADS, HEAD_DIM, HEAD_DIM), F32), wts, zero_carry, _tile(np_, 512))
    fs = _front(hs, PAST_LEN, pool_ctx, shift_prev, wkv0, wts, fp["counts"], tm_s)

    counts = fs["counts"][0, :N_EXPERTS].astype(I32)
    padded = (counts + EXPERT_BLOCK - 1) // EXPERT_BLOCK * EXPERT_BLOCK
    pad_end = jnp.cumsum(padded)
    start_pad = pad_end - padded
    n_blocks = -(-(np_ + ns) * TOP_K // EXPERT_BLOCK) + N_EXPERTS
    n_slots = n_blocks * EXPERT_BLOCK
    n_used = (pad_end[-1] // EXPERT_BLOCK).astype(I32)
    block_start = jnp.arange(n_blocks, dtype=I32) * EXPERT_BLOCK
    block_e = jnp.minimum(jnp.sum((block_start[:, None] >= pad_end[None, :]).astype(I32), axis=1), N_EXPERTS - 1)
    last_e = block_e[jnp.maximum(n_used - 1, 0)]
    block_e = jnp.where(jnp.arange(n_blocks) < n_used, block_e, last_e)

    def dest_of(route):
        idx = route[:, :TOP_K]
        rank = route[:, TOP_K:2 * TOP_K]
        start = jnp.sum(jnp.where(idx[..., None] == jnp.arange(N_EXPERTS, dtype=I32), start_pad, 0), axis=-1)
        return ((start + rank) * SUBLANES).reshape(-1).astype(I32)

    dest_p, dest_s = dest_of(fp["route"]), dest_of(fs["route"])
    tm_d = _tile(np_, 1024)
    dest_all = jnp.concatenate([dest_p, dest_s, jnp.zeros(((tm_d - ns) * TOP_K,), I32)])
    tail = jnp.stack([n_used * EXPERT_BLOCK, n_blocks - n_used]).astype(I32)
    xs = _dispatch(start_pad + counts, padded - counts, tail, dest_all, fp["xn"], fs["xn"], n_slots, tm_d)
    ys = _experts(block_e, n_used.reshape(1), xs, w_gu, b_gu, w_down, b_down)

    outs = []
    for f, dest, p, tm in ((fp, dest_p, pp, tm_p), (fs, dest_s, ps, tm_s)):
        n = f["h"].shape[0]
        outs.append(_combine(dest, ys, f["gates"], f["h"], p.reshape(n, PLE_DIM).astype(F32), wts["ln3_g"],
                             wts["ple_gate_bf"], wts["ple_proj_bf"], final_g, tm))
    return outs[0].reshape(hp.shape), outs[1].reshape(hs.shape), fp, fs


def kernel(x_prompt, x_sample, p_prompt, p_sample, cache_pool, state_shift, state_wkv, ln1_g, w_in, mu_shift, w_pool, pool_scale, w0, w2, a0, a2, g2, k_k, k_a, r_k, lnx_g, lnx_b, w_out, ln2_g, router_w, router_b, w_gu, b_gu, w_down, b_down, ln3_g, ple_gate, ple_proj, final_g):
    assert ln1_g.shape[0] == 1, "single-layer kernel"
    i = 0
    wts = _prep_weights(i, ln1_g, w_in, mu_shift, w_pool, pool_scale, w0, w2, a0, a2, g2, k_k, k_a, r_k,
                        lnx_g, lnx_b, w_out, ln2_g, router_w, router_b, ln3_g, ple_gate, ple_proj)
    y_p, y_s, fp, fs = _layer_pair(x_prompt, x_sample, p_prompt[i], p_sample[i], cache_pool[i], state_shift[i],
                                   state_wkv[i], wts, w_gu[i], b_gu[i], w_down[i], b_down[i], final_g)
    stack = lambda a: a[None]
    return (y_p, y_s,
            stack(fp["new_pool"]), stack(fp["new_shift"]), stack(fp["new_wkv"]),
            stack(fs["new_pool"]), stack(fs["new_shift"]), stack(fs["new_wkv"]))
```

```python
import functools

import jax
import jax.numpy as jnp
from jax import lax
from jax.experimental import pallas as pl
from jax.experimental.pallas import tpu as pltpu

F32 = jnp.float32
BF16 = jnp.bfloat16
I32 = jnp.int32

D_MODEL = 1024
POOL_W = 512
POOL_WINDOWS = (2, 4, 8, 16)
POOL_GW = 128
POOL_CTX = 15
POOL_HALO = 16
RWKV_W = 512
HEAD_DIM = 64
N_HEADS = 8
N_PAIRS = N_HEADS // 2
PAIR_W = 2 * HEAD_DIM
DECAY_LORA = 64
ICLR_LORA = 64
GATE_LORA = 128
RWKV_PROJ = 3 * RWKV_W + DECAY_LORA + ICLR_LORA + GATE_LORA
IN_PROJ = POOL_W + RWKV_PROJ
LNX_EPS = 64e-5
N_EXPERTS = 32
TOP_K = 4
D_EXPERT = 1024
SWIGLU_LIMIT = 7.0
SWIGLU_ALPHA = 1.702
PLE_DIM = 256
RMS_EPS = 1e-6
PAST_LEN = 2048

CHUNK = 64
EXPERT_BLOCK = 512
ROUTE_LANES = 128
SUBLANES = 8
LANES = 128
assert D_MODEL == SUBLANES * LANES
PREP_ROWS = 256
ISSUE_UNROLL = 8
GROUP_TILE = 256
EXP_M05 = 0.6065306597126334
V7X_VMEM_BYTES = 64 * 1024 * 1024

NT_DIMS = (((1,), (1,)), ((), ()))
TN_DIMS = (((0,), (0,)), ((), ()))


def _vmem_limit(est_bytes):
    return int(min(est_bytes * 3 // 2 + (4 << 20), V7X_VMEM_BYTES - (8 << 20)))


def _params(n_axes, est_bytes):
    return pltpu.CompilerParams(dimension_semantics=("arbitrary",) * n_axes,
                                vmem_limit_bytes=_vmem_limit(est_bytes))


def _dot(a, b):
    return jnp.dot(a, b, preferred_element_type=F32)


def _split2(x):
    hi = x.astype(BF16)
    lo = (x - hi.astype(F32)).astype(BF16)
    return hi, lo


def _split3(x):
    hi = x.astype(BF16)
    r1 = x - hi.astype(F32)
    mid = r1.astype(BF16)
    lo = (r1 - mid.astype(F32)).astype(BF16)
    return hi, mid, lo


def _group_sum(x, ones_bd):
    xb = x.astype(BF16)
    slabs = [_dot(xb[:, s:s + GROUP_TILE], ones_bd) for s in range(0, x.shape[1], GROUP_TILE)]
    return jnp.concatenate(slabs, axis=1)


def _store_row_tiles(ref, value):
    n = value.shape[0]
    for c in range(SUBLANES):
        ref[pl.ds(c, n, stride=SUBLANES), :] = value[:, c * LANES:(c + 1) * LANES]


def _load_row_tiles(ref, n):
    return jnp.concatenate([ref[pl.ds(c, n, stride=SUBLANES), :] for c in range(SUBLANES)], axis=1)


def _rmsnorm_rows(x, g):
    ms = jnp.mean(x * x, axis=-1, keepdims=True)
    return (x * lax.rsqrt(ms + RMS_EPS)) * g


def _in_proj_pool_kernel(x_ref, ctx_ref, g_ref, w_ref, wp_ref, ps_ref, zr_ref, po_ref, last_ref, buf, *, pos0):
    t = pl.program_id(1)
    bb, tt, _ = x_ref.shape
    xn = _rmsnorm_rows(x_ref[...].reshape(bb * tt, D_MODEL), g_ref[...])
    z = _dot(xn.astype(BF16), w_ref[...])
    zr_ref[...] = z[:, POOL_W:].reshape(bb, tt, RWKV_PROJ)

    @pl.when(t == 0)
    def _():
        buf[:, 0:POOL_HALO, :] = ctx_ref[...]

    buf[:, POOL_HALO:, :] = z[:, :POOL_W].reshape(bb, tt, POOL_W)
    pos = lax.broadcasted_iota(I32, (1, tt, 1), 1) + (t * tt + pos0)
    outs = []
    for gi, w in enumerate(POOL_WINDOWS):
        sl = slice(gi * POOL_GW, (gi + 1) * POOL_GW)
        cur = buf[:, POOL_HALO:POOL_HALO + tt, sl]
        acc = cur
        for j in range(1, w):
            acc = acc + buf[:, POOL_HALO - j:POOL_HALO - j + tt, sl]
        cnt = jnp.minimum(w, pos + 1).astype(F32)
        pooled = (acc / cnt - cur).reshape(bb * tt, POOL_GW)
        outs.append(_dot(pooled.astype(BF16), wp_ref[gi]))
    po = jnp.concatenate(outs, axis=-1) * ps_ref[...]
    po_ref[...] = po.reshape(bb, tt, POOL_W).astype(po_ref.dtype)
    last = buf[:, tt:tt + POOL_HALO, :]
    last_ref[...] = last
    buf[:, 0:POOL_HALO, :] = last


def _in_proj_pool(x, ctx16, ln1_g, w_in_bf, w_pool_bf, pool_scale, pos0, bb, tt):
    b, t, _ = x.shape
    cst = lambda shape: pl.BlockSpec(shape, lambda i, j: (0,) * len(shape))
    seq = lambda w: pl.BlockSpec((bb, tt, w), lambda i, j: (i, j, 0))
    halo = pl.BlockSpec((bb, POOL_HALO, POOL_W), lambda i, j: (i, 0, 0))
    est = (2 * bb * tt * (D_MODEL + RWKV_PROJ) * 4 + 2 * D_MODEL * IN_PROJ * 2 + 3 * bb * tt * IN_PROJ * 4
           + bb * (tt + POOL_HALO) * POOL_W * 4)
    return pl.pallas_call(
        functools.partial(_in_proj_pool_kernel, pos0=pos0),
        grid=(b // bb, t // tt),
        in_specs=[seq(D_MODEL), halo, cst((1, D_MODEL)), cst((D_MODEL, IN_PROJ)),
                  cst((len(POOL_WINDOWS), POOL_GW, POOL_GW)), cst((1, POOL_W))],
        out_specs=[seq(RWKV_PROJ), seq(POOL_W), halo],
        out_shape=[jax.ShapeDtypeStruct((b, t, RWKV_PROJ), F32), jax.ShapeDtypeStruct((b, t, POOL_W), BF16),
                   jax.ShapeDtypeStruct((b, POOL_HALO, POOL_W), F32)],
        scratch_shapes=[pltpu.VMEM((bb, tt + POOL_HALO, POOL_W), F32)],
        compiler_params=_params(2, est),
        name="in_proj_pool",
    )(x, ctx16, ln1_g.reshape(1, D_MODEL), w_in_bf, w_pool_bf, pool_scale.reshape(1, POOL_W))


def _rwkv_prepare(zr, prev, mu, w0, w2, a0, a2, g2, k_k, k_a, r_k, ones_bd):
    zs = zr + (prev - zr) * mu
    o1, o2, o3 = RWKV_W, 2 * RWKV_W, 3 * RWKV_W
    o4, o5 = o3 + DECAY_LORA, o3 + DECAY_LORA + ICLR_LORA
    r, k, v = zs[:, :o1], zs[:, o1:o2], zs[:, o2:o3]
    wd, ad, gd = zs[:, o3:o4], zs[:, o4:o5], zs[:, o5:]
    wz = w0 + _dot(jnp.tanh(wd).astype(BF16), w2)
    lw = -EXP_M05 * jax.nn.sigmoid(wz)
    a = jax.nn.sigmoid(a0 + _dot(ad.astype(BF16), a2))
    g = _dot(jax.nn.sigmoid(gd).astype(BF16), g2)
    kk = k * k_k
    kk = kk / jnp.maximum(jnp.sqrt(_group_sum(kk * kk, ones_bd)), 1e-12)
    kx = k * (1.0 + (a - 1.0) * k_a)
    bonus = _group_sum(r * kx * r_k, ones_bd) * v
    return r, lw, kx, v, -kk, kk * a, g, bonus


def _pair_blockdiag(z, left):
    return jnp.concatenate([jnp.where(left, z, 0.0), jnp.where(left, 0.0, z)], axis=0)


def _wkv_kernel(zr_ref, halo_ref, sh_ref, mu_ref, w0_ref, w2_ref, a0_ref, a2_ref, g2_ref, kk_ref, ka_ref, rk_ref,
                ones_ref, lg_ref, lb_ref, s0_ref, o_ref, sf_ref,
                buf, r_ref, lw_ref, kx_ref, v_ref, na_ref, kb_ref, g_ref, bonus_ref, y_ref, s_scr, *, t_valid):
    t = pl.program_id(1)
    tt = zr_ref.shape[1]
    c = CHUNK
    rows_per_pass = min(tt, PREP_ROWS)

    @pl.when(t == 0)
    def _():
        s_scr[...] = s0_ref[0]

    buf[0:SUBLANES, :] = jnp.where(t == 0, sh_ref[0], halo_ref[0])
    buf[SUBLANES:, :] = zr_ref[0]
    ones_bd = ones_ref[...]
    for r0 in range(0, tt, rows_per_pass):
        rows = slice(r0, r0 + rows_per_pass)
        vals = _rwkv_prepare(zr_ref[0, rows, :], buf[SUBLANES - 1 + r0:SUBLANES - 1 + r0 + rows_per_pass, :],
                             mu_ref[...], w0_ref[...], w2_ref[...], a0_ref[...], a2_ref[...], g2_ref[...],
                             kk_ref[...], ka_ref[...], rk_ref[...], ones_bd)
        if t_valid is not None:
            live = lax.broadcasted_iota(I32, (rows_per_pass, 1), 0) + (t * tt + r0) < t_valid
            vals = tuple(jnp.where(live, x, 0.0) for x in vals)
        for ref, x in zip((r_ref, lw_ref, kx_ref, v_ref, na_ref, kb_ref, g_ref, bonus_ref), vals):
            ref[rows, :] = x

    lane = lax.broadcasted_iota(I32, (c, 2 * c), 1)
    rowi = lax.broadcasted_iota(I32, (c, 2 * c), 0)
    jm = lane & (c - 1)
    strict = jm < rowi
    incl = jm <= rowi
    eye_pair = jnp.where(jm == rowi, 1.0, 0.0).astype(F32)
    left = lax.broadcasted_iota(I32, (c, PAIR_W), 1) < HEAD_DIM
    rr = lax.broadcasted_iota(I32, (c, c), 0)
    cc = lax.broadcasted_iota(I32, (c, c), 1)
    tri_incl_bf = jnp.where(cc <= rr, 1.0, 0.0).astype(BF16)
    br = lax.broadcasted_iota(I32, (PAIR_W, PAIR_W), 0) < HEAD_DIM
    bc = lax.broadcasted_iota(I32, (PAIR_W, PAIR_W), 1) < HEAD_DIM
    bd_mask = br == bc
    bf = lambda x: x.astype(BF16)
    bd = lambda z: _pair_blockdiag(z, left)
    n_chunks = tt // c

    cums = []
    for ci in range(n_chunks):
        h3 = _split3(lw_ref[ci * c:(ci + 1) * c, :])
        cums.append(_dot(tri_incl_bf, h3[0]) + _dot(tri_incl_bf, h3[1]) + _dot(tri_incl_bf, h3[2]))

    chains = []
    for ci in range(n_chunks):
        rows = slice(ci * c, (ci + 1) * c)
        for p in range(N_PAIRS):
            cols = slice(p * PAIR_W, (p + 1) * PAIR_W)
            cum = cums[ci][:, cols]
            lw, kx, kb = lw_ref[rows, cols], kx_ref[rows, cols], kb_ref[rows, cols]
            tot = cum[c - 1:c, :]
            p_end = jnp.exp(tot - cum)
            inv_p = jnp.exp(-cum)
            chains.append(dict(
                ci=ci, p=p, v=v_ref[rows, cols], p_c=jnp.exp(tot),
                a_t=na_ref[rows, cols] * jnp.exp(cum - lw), r_t=r_ref[rows, cols] * jnp.exp(cum),
                b_t=kb * inv_p, k_t=kx * inv_p, b_h=kb * p_end, k_h=kx * p_end))

    for d in chains:
        lhs = bf(jnp.concatenate([d["a_t"], d["r_t"]], axis=0))
        rhs = bf(jnp.concatenate([bd(d["b_t"]), bd(d["k_t"])], axis=0))
        sc = lax.dot_general(lhs, rhs, NT_DIMS, preferred_element_type=F32)
        d["a_ab"] = jnp.where(strict, sc[:c, :2 * c], 0.0)
        d["a_k"] = jnp.concatenate([jnp.where(strict, sc[:c, 2 * c:], 0.0),
                                    jnp.where(incl, sc[c:, 2 * c:], 0.0)], axis=0)
        d["a_rb"] = jnp.where(incl, sc[c:, :2 * c], 0.0)

    for d in chains:
        d["x"] = _dot(bf(d["a_ab"]), bf(bd(d["a_ab"])))
        d["t"] = eye_pair + d["a_ab"]
    for _ in range(c.bit_length() - 3):
        for d in chains:
            st = _dot(bf(jnp.concatenate([d["x"], d["t"]], axis=0)), bf(bd(d["x"])))
            d["x"] = st[:c]
            d["t"] = d["t"] + st[c:]
    for d in chains:
        d["t"] = d["t"] + _dot(bf(d["t"]), bf(bd(d["x"])))

    for d in chains:
        wv = _dot(bf(d["a_k"]), bf(bd(d["v"])))
        d["w1"], d["rkv"] = wv[:c], wv[c:]
    for d in chains:
        tu = _dot(bf(d["t"]), bf(jnp.concatenate([bd(d["w1"]), bd(d["a_t"])], axis=1)))
        d["u_loc"], d["a_tt"] = tu[:, :PAIR_W], tu[:, PAIR_W:]
    for d in chains:
        ar = _dot(bf(d["a_rb"]), bf(jnp.concatenate([bd(d["u_loc"]), bd(d["a_tt"])], axis=1)))
        d["y_loc"] = ar[:, :PAIR_W] + d["rkv"]
        d["r_g"] = d["r_t"] + ar[:, PAIR_W:]
    for d in chains:
        lhs = jnp.concatenate([jnp.concatenate([d["v"], jnp.zeros_like(d["v"])], axis=1),
                               jnp.concatenate([d["u_loc"], d["a_tt"]], axis=1)], axis=0)
        rhs = jnp.concatenate([d["k_h"], d["b_h"]], axis=0)
        dp = lax.dot_general(bf(lhs), bf(rhs), TN_DIMS, preferred_element_type=F32)
        d["d_loc"] = jnp.where(bd_mask, dp[:PAIR_W], 0.0)
        d["phi"] = bf(jnp.where(bd_mask, dp[PAIR_W:], 0.0))

    state = [s_scr[p] for p in range(N_PAIRS)]
    for d in chains:
        p = d["p"]
        s_hi, s_lo = _split2(state[p])
        y = lax.dot_general(bf(d["r_g"]), s_hi, NT_DIMS, preferred_element_type=F32) + d["y_loc"]
        y_ref[d["ci"] * c:(d["ci"] + 1) * c, p * PAIR_W:(p + 1) * PAIR_W] = y
        state[p] = state[p] * d["p_c"] + (_dot(s_hi, d["phi"]) + _dot(s_lo, d["phi"])) + d["d_loc"]
    for p in range(N_PAIRS):
        s_scr[p] = state[p]

    @pl.when(t == pl.num_programs(1) - 1)
    def _():
        sf_ref[0] = s_scr[...]

    inv_n = 1.0 / HEAD_DIM
    for r0 in range(0, tt, rows_per_pass):
        rows = slice(r0, r0 + rows_per_pass)
        y = y_ref[rows, :]
        mu = _group_sum(y, ones_bd) * inv_n
        dlt = y - mu
        var = _group_sum(dlt * dlt, ones_bd) * inv_n
        yn = dlt * lax.rsqrt(var + LNX_EPS) * lg_ref[...] + lb_ref[...]
        o_ref[0, rows, :] = ((yn + bonus_ref[rows, :]) * g_ref[rows, :]).astype(o_ref.dtype)


def _rwkv_mixer(zr, sh8, s0_pair, wts, tt, t_valid):
    b, t, _ = zr.shape
    assert CHUNK == HEAD_DIM and tt % CHUNK == 0
    hb = tt // SUBLANES
    row = lambda x: x.reshape(1, -1)
    cst = lambda shape: pl.BlockSpec(shape, lambda i, j: (0,) * len(shape))
    seq = pl.BlockSpec((1, tt, RWKV_W), lambda i, j: (i, j, 0))
    st = pl.BlockSpec((1, N_PAIRS, PAIR_W, PAIR_W), lambda i, j: (i, 0, 0, 0))
    est = 3 * tt * RWKV_PROJ * 4 + 10 * tt * RWKV_W * 4 + 5 * N_PAIRS * PAIR_W * PAIR_W * 4 + (8 << 20)
    return pl.pallas_call(
        functools.partial(_wkv_kernel, t_valid=t_valid),
        grid=(b, t // tt),
        in_specs=[pl.BlockSpec((1, tt, RWKV_PROJ), lambda i, j: (i, j, 0)),
                  pl.BlockSpec((1, SUBLANES, RWKV_PROJ), lambda i, j: (i, jnp.maximum(j * hb - 1, 0), 0)),
                  pl.BlockSpec((1, SUBLANES, RWKV_PROJ), lambda i, j: (i, 0, 0)),
                  cst((1, RWKV_PROJ)), cst((1, RWKV_W)), cst((DECAY_LORA, RWKV_W)),
                  cst((1, RWKV_W)), cst((ICLR_LORA, RWKV_W)), cst((GATE_LORA, RWKV_W)),
                  cst((1, RWKV_W)), cst((1, RWKV_W)), cst((1, RWKV_W)), cst((GROUP_TILE, GROUP_TILE)),
                  cst((1, RWKV_W)), cst((1, RWKV_W)), st],
        out_specs=[seq, st],
        out_shape=[jax.ShapeDtypeStruct((b, t, RWKV_W), BF16),
                   jax.ShapeDtypeStruct((b, N_PAIRS, PAIR_W, PAIR_W), F32)],
        scratch_shapes=([pltpu.VMEM((tt + SUBLANES, RWKV_PROJ), F32)] + [pltpu.VMEM((tt, RWKV_W), F32)] * 9
                        + [pltpu.VMEM((N_PAIRS, PAIR_W, PAIR_W), F32)]),
        compiler_params=_params(2, est),
        name="rwkv_mixer",
    )(zr, zr, sh8, row(wts["mu_shift"]), row(wts["w0"]), wts["w2_bf"], row(wts["a0"]), wts["a2_bf"], wts["g2_bf"],
      row(wts["k_k"]), row(wts["k_a"]), row(wts["r_k"]), wts["ones_bd"], row(wts["lnx_g"]), row(wts["lnx_b"]),
      s0_pair)


def _state_to_pairs(s):
    b = s.shape[0]
    s = s.astype(F32).reshape(b, N_PAIRS, 2, HEAD_DIM, HEAD_DIM)
    z = jnp.zeros((b, N_PAIRS, HEAD_DIM, HEAD_DIM), F32)
    return jnp.concatenate([jnp.concatenate([s[:, :, 0], z], axis=-1),
                            jnp.concatenate([z, s[:, :, 1]], axis=-1)], axis=-2)


def _pairs_to_state(sp):
    b = sp.shape[0]
    s = jnp.stack([sp[:, :, :HEAD_DIM, :HEAD_DIM], sp[:, :, HEAD_DIM:, HEAD_DIM:]], axis=2)
    return s.reshape(b, N_HEADS, HEAD_DIM, HEAD_DIM)


def _mix_out_kernel(rw_ref, po_ref, x_ref, wo_ref, ln2_ref, rw_hi_ref, rw_lo_ref, rb_ref, cin_ref,
                    h_o, xn_o, route_o, gate_o, cnt_o, carry):
    i = pl.program_id(0)
    tm = x_ref.shape[0]

    @pl.when(i == 0)
    def _():
        carry[...] = cin_ref[...]

    mix = _dot(po_ref[...], wo_ref[:POOL_W, :]) + _dot(rw_ref[...], wo_ref[POOL_W:, :])
    h = x_ref[...] + mix
    h_o[...] = h
    xn = _rmsnorm_rows(h, ln2_ref[...])
    _store_row_tiles(xn_o, xn)

    x_hi, x_lo = _split2(xn)
    logits = (_dot(x_hi, rw_hi_ref[...]) + _dot(x_hi, rw_lo_ref[...]) + _dot(x_lo, rw_hi_ref[...])
              + rb_ref[...])
    lane = lax.broadcasted_iota(I32, (tm, ROUTE_LANES), 1)
    lane_f = lane.astype(F32)
    vals, idxs, hots = [], [], []
    work = logits
    for _ in range(TOP_K):
        m = jnp.max(work, axis=-1, keepdims=True)
        idx = jnp.min(jnp.where(work == m, lane_f, float(ROUTE_LANES)), axis=-1, keepdims=True)
        hit = lane_f == idx
        vals.append(m)
        idxs.append(idx)
        hots.append(jnp.where(hit, 1.0, 0.0).astype(F32))
        work = jnp.where(hit, -jnp.inf, work)
    exps = [jnp.exp(vv - vals[0]) for vv in vals]
    den = exps[0] + exps[1] + exps[2] + exps[3]

    hot_all = hots[0] + hots[1] + hots[2] + hots[3]
    rr = lax.broadcasted_iota(I32, (tm, tm), 0)
    cc = lax.broadcasted_iota(I32, (tm, tm), 1)
    tri_strict = jnp.where(cc < rr, 1.0, 0.0).astype(BF16)
    prefix = _dot(tri_strict, hot_all.astype(BF16)) + carry[0:1, :]
    route = jnp.zeros((tm, ROUTE_LANES), F32)
    gates = jnp.zeros((tm, ROUTE_LANES), F32)
    for j in range(TOP_K):
        rank = jnp.sum(hots[j] * prefix, axis=-1, keepdims=True)
        route = jnp.where(lane == j, idxs[j], route)
        route = jnp.where(lane == TOP_K + j, rank * SUBLANES, route)
        gates = jnp.where(lane == j, exps[j] / den, gates)
    route_o[...] = route.astype(I32)
    gate_o[...] = gates
    new_carry = carry[...] + jnp.sum(hot_all, axis=0, keepdims=True)
    carry[...] = new_carry
    cnt_o[...] = new_carry


def _mix_out(rw2d, po2d, x2d, w_out_bf, ln2_g, rw_hi, rw_lo, rb_pad, carry_in, tm):
    n = x2d.shape[0]
    row = lambda x: x.reshape(1, -1)
    cst = lambda shape: pl.BlockSpec(shape, lambda i: (0,) * len(shape))
    half = pl.BlockSpec((tm, RWKV_W), lambda i: (i, 0))
    full = pl.BlockSpec((tm, D_MODEL), lambda i: (i, 0))
    lanes = pl.BlockSpec((tm, ROUTE_LANES), lambda i: (i, 0))
    est = (2 * (2 * tm * RWKV_W * 2 + 3 * tm * D_MODEL * 4) + 2 * D_MODEL * D_MODEL * 2
           + 8 * tm * D_MODEL * 4 + 4 * tm * tm)
    return pl.pallas_call(
        _mix_out_kernel,
        grid=(n // tm,),
        in_specs=[half, half, full, cst((D_MODEL, D_MODEL)),
                  cst((1, D_MODEL)), cst((D_MODEL, ROUTE_LANES)), cst((D_MODEL, ROUTE_LANES)),
                  cst((1, ROUTE_LANES)), cst((SUBLANES, ROUTE_LANES))],
        out_specs=[full, pl.BlockSpec((tm * SUBLANES, LANES), lambda i: (i, 0)), lanes, lanes,
                   cst((SUBLANES, ROUTE_LANES))],
        out_shape=[jax.ShapeDtypeStruct((n, D_MODEL), F32), jax.ShapeDtypeStruct((n * SUBLANES, LANES), F32),
                   jax.ShapeDtypeStruct((n, ROUTE_LANES), I32), jax.ShapeDtypeStruct((n, ROUTE_LANES), F32),
                   jax.ShapeDtypeStruct((SUBLANES, ROUTE_LANES), F32)],
        scratch_shapes=[pltpu.VMEM((SUBLANES, ROUTE_LANES), F32)],
        compiler_params=_params(1, est),
        name="mix_out_router",
    )(rw2d, po2d, x2d, w_out_bf, row(ln2_g), rw_hi, rw_lo, rb_pad, carry_in)


def _tiles(ref, first_row, n_slots):
    if not isinstance(first_row, int):
        first_row = pl.multiple_of(first_row, SUBLANES)
    return ref.at[pl.ds(first_row, n_slots * SUBLANES), :]


def _dispatch_kernel(pad_start_ref, pad_len_ref, tail_ref, dest_a_ref, dest_b_ref, xa_ref, xb_ref, o_hbm, zbuf, sem, sem_fill,
                     *, n_main, n_last):
    i = pl.program_id(0)
    tm = xa_ref.shape[0] // SUBLANES
    blk = zbuf.shape[0] // SUBLANES

    @pl.when(i == 0)
    def _():
        zbuf[...] = jnp.zeros_like(zbuf)

        def per_expert(e, totals):
            start, n = pad_start_ref[e], pad_len_ref[e]
            n_groups = lax.shift_right_logical(n, 3)
            n_single = n - n_groups * SUBLANES

            def group(g, carry):
                pltpu.make_async_copy(_tiles(zbuf, 0, SUBLANES),
                                      _tiles(o_hbm, (start + g * SUBLANES) * SUBLANES, SUBLANES), sem_fill).start()
                return carry

            def single(r, carry):
                pltpu.make_async_copy(_tiles(zbuf, 0, 1),
                                      _tiles(o_hbm, (start + n_groups * SUBLANES + r) * SUBLANES, 1), sem_fill).start()
                return carry

            lax.fori_loop(0, n_groups, group, 0)
            lax.fori_loop(0, n_single, single, 0)
            return totals[0] + n_groups, totals[1] + n_single

        n_groups_filled, n_single_filled = lax.fori_loop(0, N_EXPERTS, per_expert, (0, 0))

        def wait_group(g, carry):
            pltpu.make_async_copy(_tiles(zbuf, 0, SUBLANES), _tiles(o_hbm, 0, SUBLANES), sem_fill).wait()
            return carry

        def wait_single(r, carry):
            pltpu.make_async_copy(_tiles(zbuf, 0, 1), _tiles(o_hbm, 0, 1), sem_fill).wait()
            return carry

        lax.fori_loop(0, n_groups_filled, wait_group, 0)
        lax.fori_loop(0, n_single_filled, wait_single, 0)

        def fill_block(b, carry):
            pltpu.make_async_copy(zbuf, _tiles(o_hbm, (tail_ref[0] + b * blk) * SUBLANES, blk), sem_fill).start()
            return carry

        def wait_block(b, carry):
            pltpu.make_async_copy(zbuf, _tiles(o_hbm, 0, blk), sem_fill).wait()
            return carry

        lax.fori_loop(0, tail_ref[1], fill_block, 0)
        lax.fori_loop(0, tail_ref[1], wait_block, 0)

    def scatter_rows(d_ref, x_ref, n_rows):
        def issue(n, carry):
            src = _tiles(x_ref, n * SUBLANES, 1)
            for j in range(TOP_K):
                pltpu.make_async_copy(src, _tiles(o_hbm, d_ref[n * TOP_K + j], 1), sem).start(priority=j % 2)
            return carry

        lax.fori_loop(0, n_rows, issue, 0, unroll=ISSUE_UNROLL)
        pltpu.make_async_copy(_tiles(o_hbm, 0, n_rows * TOP_K), _tiles(o_hbm, 0, n_rows * TOP_K), sem).wait()

    @pl.when(i < n_main)
    def _():
        scatter_rows(dest_a_ref, xa_ref, tm)

    @pl.when(i == n_main)
    def _():
        scatter_rows(dest_b_ref, xb_ref, n_last)


def _dispatch(pad_start, pad_len, tail, dest_a, dest_b, xn_a, xn_b, n_slots, tm):
    n_main = xn_a.shape[0] // (tm * SUBLANES)
    n_last = xn_b.shape[0] // SUBLANES
    assert n_slots >= max(tm, n_last) * TOP_K
    grid_spec = pltpu.PrefetchScalarGridSpec(
        num_scalar_prefetch=3,
        grid=(n_main + 1,),
        in_specs=[pl.BlockSpec((tm * TOP_K,), lambda i, *_: (jnp.minimum(i, n_main - 1),),
                               memory_space=pltpu.SMEM),
                  pl.BlockSpec((n_last * TOP_K,), lambda i, *_: (0,), memory_space=pltpu.SMEM),
                  pl.BlockSpec((tm * SUBLANES, LANES), lambda i, *_: (jnp.minimum(i, n_main - 1), 0)),
                  pl.BlockSpec((n_last * SUBLANES, LANES), lambda i, *_: (0, 0))],
        out_specs=pl.BlockSpec(memory_space=pl.ANY),
        scratch_shapes=[pltpu.VMEM((EXPERT_BLOCK * SUBLANES, LANES), F32), pltpu.SemaphoreType.DMA(()),
                        pltpu.SemaphoreType.DMA(())],
    )
    return pl.pallas_call(
        functools.partial(_dispatch_kernel, n_main=n_main, n_last=n_last),
        grid_spec=grid_spec,
        out_shape=jax.ShapeDtypeStruct((n_slots * SUBLANES, LANES), F32),
        compiler_params=_params(1, 2 * (tm + n_last) * D_MODEL * 4 + EXPERT_BLOCK * D_MODEL * 4),
        name="dispatch",
    )(pad_start, pad_len, tail, dest_a, dest_b, xn_a, xn_b)


def _expert_kernel(be_ref, nu_ref, xs_ref, wgu_ref, bgu_ref, wdn_ref, bdn_ref, o_ref, wgu_bf, wdn_bf):
    i = pl.program_id(0)
    n_used = nu_ref[0]
    used = jnp.logical_and(i >= 1, i - 1 < n_used)

    @pl.when(used)
    def _():
        x = _load_row_tiles(xs_ref, xs_ref.shape[0] // SUBLANES)
        gu = _dot(x.astype(BF16), wgu_bf[...]) + bgu_ref[0]
        gate = jnp.minimum(gu[:, :D_EXPERT], SWIGLU_LIMIT)
        up = jnp.clip(gu[:, D_EXPERT:], -SWIGLU_LIMIT, SWIGLU_LIMIT)
        hmid = (up + 1.0) * (gate * jax.nn.sigmoid(SWIGLU_ALPHA * gate))
        _store_row_tiles(o_ref, _dot(hmid.astype(BF16), wdn_bf[...]) + bdn_ref[0])

    @pl.when(jnp.logical_not(used))
    def _():
        o_ref[...] = jnp.zeros_like(o_ref)

    last = be_ref.shape[0] - 1
    changed = jnp.logical_or(i == 0, be_ref[jnp.minimum(i, last)] != be_ref[jnp.maximum(i - 1, 0)])

    @pl.when(jnp.logical_and(i < n_used, changed))
    def _():
        wgu_bf[...] = wgu_ref[0].astype(BF16)
        wdn_bf[...] = wdn_ref[0].astype(BF16)


def _experts(block_e, n_used, xs, w_gu, b_gu, w_down, b_down):
    n_slots = xs.shape[0] // SUBLANES
    nb = n_slots // EXPERT_BLOCK
    blk = EXPERT_BLOCK
    slots = pl.BlockSpec((blk * SUBLANES, LANES), lambda i, be, nu: (jnp.maximum(i - 1, 0), 0))
    w_of = lambda i, be: be[jnp.minimum(i, nb - 1)]
    b_of = lambda i, be: be[jnp.maximum(i - 1, 0)]
    est = (2 * 2 * blk * D_MODEL * 4 + 2 * (D_MODEL * 2 * D_EXPERT + D_EXPERT * D_MODEL) * 4
           + (D_MODEL * 2 * D_EXPERT + D_EXPERT * D_MODEL) * 2 + 4 * blk * 2 * D_EXPERT * 4)
    grid_spec = pltpu.PrefetchScalarGridSpec(
        num_scalar_prefetch=2,
        grid=(nb + 1,),
        in_specs=[slots,
                  pl.BlockSpec((1, D_MODEL, 2 * D_EXPERT), lambda i, be, nu: (w_of(i, be), 0, 0)),
                  pl.BlockSpec((1, 1, 2 * D_EXPERT), lambda i, be, nu: (b_of(i, be), 0, 0)),
                  pl.BlockSpec((1, D_EXPERT, D_MODEL), lambda i, be, nu: (w_of(i, be), 0, 0)),
                  pl.BlockSpec((1, 1, D_MODEL), lambda i, be, nu: (b_of(i, be), 0, 0))],
        out_specs=slots,
        scratch_shapes=[pltpu.VMEM((D_MODEL, 2 * D_EXPERT), BF16), pltpu.VMEM((D_EXPERT, D_MODEL), BF16)],
    )
    return pl.pallas_call(
        _expert_kernel,
        grid_spec=grid_spec,
        out_shape=jax.ShapeDtypeStruct((n_slots * SUBLANES, LANES), F32),
        compiler_params=_params(1, est),
        name="experts",
    )(block_e, n_used, xs, w_gu, b_gu.reshape(N_EXPERTS, 1, 2 * D_EXPERT), w_down,
      b_down.reshape(N_EXPERTS, 1, D_MODEL))


def _combine_kernel(dest_ref, dest_next_ref, ys_hbm, gate_ref, h_ref, p_ref, ln3_ref, pg_ref, pp_ref, fg_ref,
                    o_ref, gbuf, sem):
    i = pl.program_id(0)
    tm = h_ref.shape[0]
    slot = lax.rem(i, 2)

    def issue(d_ref, s):
        def body(n, carry):
            for j in range(TOP_K):
                pltpu.make_async_copy(_tiles(ys_hbm, d_ref[n * TOP_K + j], 1),
                                      _tiles(gbuf.at[s, j], n * SUBLANES, 1), sem.at[s]).start(priority=j % 2)
            return carry
        lax.fori_loop(0, tm, body, 0, unroll=ISSUE_UNROLL)

    @pl.when(i == 0)
    def _():
        issue(dest_ref, 0)

    @pl.when(i + 1 < pl.num_programs(0))
    def _():
        issue(dest_next_ref, 1 - slot)

    for j in range(TOP_K):
        pltpu.make_async_copy(gbuf.at[slot, j], gbuf.at[slot, j], sem.at[slot]).wait()

    gates = gate_ref[...]
    slabs = []
    for c in range(SUBLANES):
        acc = gates[:, 0:1] * gbuf[slot, 0, pl.ds(c, tm, stride=SUBLANES), :]
        for j in range(1, TOP_K):
            acc = acc + gates[:, j:j + 1] * gbuf[slot, j, pl.ds(c, tm, stride=SUBLANES), :]
        slabs.append(acc)
    h = h_ref[...] + jnp.concatenate(slabs, axis=1)
    gate = jax.nn.sigmoid(_dot(_rmsnorm_rows(h, ln3_ref[...]).astype(BF16), pg_ref[...]))
    h = h + gate * _dot(p_ref[...].astype(BF16), pp_ref[...])
    o_ref[...] = _rmsnorm_rows(h, fg_ref[...])


def _combine(dest_flat, ys, gates, h2d, p2d, ln3_g, ple_gate_bf, ple_proj_bf, final_g, tm):
    n = h2d.shape[0]
    nb = n // tm
    row = lambda x: x.reshape(1, -1)
    cst = lambda shape: pl.BlockSpec(shape, lambda i: (0,) * len(shape))
    est = (2 * TOP_K * tm * D_MODEL * 4 + 2 * (2 * tm * D_MODEL * 4 + tm * PLE_DIM * 4 + tm * ROUTE_LANES * 4)
           + 2 * (D_MODEL * D_MODEL * 2 + PLE_DIM * D_MODEL * 2) + 6 * tm * D_MODEL * 4)
    return pl.pallas_call(
        _combine_kernel,
        grid=(nb,),
        in_specs=[pl.BlockSpec((tm * TOP_K,), lambda i: (i,), memory_space=pltpu.SMEM),
                  pl.BlockSpec((tm * TOP_K,), lambda i: (jnp.minimum(i + 1, nb - 1),), memory_space=pltpu.SMEM),
                  pl.BlockSpec(memory_space=pl.ANY),
                  pl.BlockSpec((tm, ROUTE_LANES), lambda i: (i, 0)),
                  pl.BlockSpec((tm, D_MODEL), lambda i: (i, 0)),
                  pl.BlockSpec((tm, PLE_DIM), lambda i: (i, 0)),
                  cst((1, D_MODEL)), cst((D_MODEL, D_MODEL)), cst((PLE_DIM, D_MODEL)), cst((1, D_MODEL))],
        out_specs=pl.BlockSpec((tm, D_MODEL), lambda i: (i, 0)),
        out_shape=jax.ShapeDtypeStruct((n, D_MODEL), F32),
        scratch_shapes=[pltpu.VMEM((2, TOP_K, tm * SUBLANES, LANES), F32), pltpu.SemaphoreType.DMA((2,))],
        compiler_params=_params(1, est),
        name="combine_ple_norm",
    )(dest_flat, dest_flat, ys, gates, h2d, p2d, row(ln3_g), ple_gate_bf, ple_proj_bf, row(final_g))


def _tile(n, pref):
    t = min(n, pref)
    assert n % t == 0, (n, pref)
    return t


def _front(x, pos0, pool_ctx, shift_prev, wkv0, wts, carry_in, tm_mix):
    b, t, _ = x.shape
    n = b * t
    assert t >= POOL_CTX and t % SUBLANES == 0
    x2d = x.reshape(n, D_MODEL)
    ctx16 = jnp.concatenate([jnp.zeros((b, POOL_HALO - POOL_CTX, POOL_W), F32), pool_ctx.astype(F32)], axis=1)
    tt = _tile(t, 512)
    bb = _tile(b, max(1, 128 // tt))
    zr, pool_out, last = _in_proj_pool(x, ctx16, wts["ln1_g"], wts["w_in_bf"], wts["w_pool_bf"],
                                       wts["pool_scale"], pos0, bb, tt)
    new_pool = last[:, POOL_HALO - POOL_CTX:]

    sh8 = jnp.broadcast_to(shift_prev.astype(F32), (b, SUBLANES, RWKV_PROJ))
    new_shift = zr[:, -1:]

    t_pad = -(-t // CHUNK) * CHUNK
    zr_pad = zr if t_pad == t else jnp.pad(zr, ((0, 0), (0, t_pad - t), (0, 0)))
    rw_out, s_fin = _rwkv_mixer(zr_pad, sh8, _state_to_pairs(wkv0), wts, _tile(t_pad, 8 * CHUNK),
                                None if t_pad == t else t)
    rw_out = rw_out[:, :t]
    new_wkv = _pairs_to_state(s_fin)

    h, xn, route, gates, counts = _mix_out(
        rw_out.reshape(n, RWKV_W), pool_out.reshape(n, POOL_W), x2d, wts["w_out_bf"], wts["ln2_g"],
        wts["rw_hi"], wts["rw_lo"], wts["rb_pad"], carry_in, tm_mix)
    return dict(h=h, xn=xn, route=route, gates=gates, counts=counts,
                new_pool=new_pool, new_shift=new_shift, new_wkv=new_wkv)


def _prep_weights(i, ln1_g, w_in, mu_shift, w_pool, pool_scale, w0, w2, a0, a2, g2, k_k, k_a, r_k,
                  lnx_g, lnx_b, w_out, ln2_g, router_w, router_b, ln3_g, ple_gate, ple_proj):
    head = jnp.arange(GROUP_TILE) // HEAD_DIM
    ones_bd = (head[:, None] == head[None, :]).astype(BF16)
    rw_pad = jnp.pad(router_w[i].astype(F32), ((0, 0), (0, ROUTE_LANES - N_EXPERTS)))
    rw_hi = rw_pad.astype(BF16)
    rw_lo = (rw_pad - rw_hi.astype(F32)).astype(BF16)
    rb_pad = jnp.concatenate([router_b[i].astype(F32), jnp.full((ROUTE_LANES - N_EXPERTS,), -jnp.inf, F32)])
    return dict(
        ln1_g=ln1_g[i], w_in_bf=w_in[i].astype(BF16), mu_shift=mu_shift[i], w_pool_bf=w_pool[i].astype(BF16),
        pool_scale=pool_scale[i], w0=w0[i], w2_bf=w2[i].astype(BF16), a0=a0[i], a2_bf=a2[i].astype(BF16),
        g2_bf=g2[i].astype(BF16), k_k=k_k[i], k_a=k_a[i], r_k=r_k[i].reshape(RWKV_W), lnx_g=lnx_g[i],
        lnx_b=lnx_b[i], w_out_bf=w_out[i].astype(BF16), ln2_g=ln2_g[i], rw_hi=rw_hi, rw_lo=rw_lo,
        rb_pad=rb_pad.reshape(1, ROUTE_LANES), ones_bd=ones_bd, ln3_g=ln3_g[i],
        ple_gate_bf=ple_gate[i].astype(BF16), ple_proj_bf=ple_proj[i].astype(BF16))


def _layer_pair(hp, hs, pp, ps, pool_ctx, shift_prev, wkv0, wts, w_gu, b_gu, w_down, b_down, final_g):
    bp, tp, _ = hp.shape
    bs, ts, _ = hs.shape
    np_, ns = bp * tp, bs * ts
    tm_p = _tile(np_, 256)
    tm_s = _tile(ns, 128)
    zero_carry = jnp.zeros((SUBLANES, ROUTE_LANES), F32)
    fp = _front(hp, 0, jnp.zeros((bp, POOL_CTX, POOL_W), F32), jnp.zeros((bp, 1, RWKV_PROJ), F32),
                jnp.zeros((bp, N_HEADS, HEAD_DIM, HEAD_DIM), F32), wts, zero_carry, _tile(np_, 512))
    fs = _front(hs, PAST_LEN, pool_ctx, shift_prev, wkv0, wts, fp["counts"], tm_s)

    counts = fs["counts"][0, :N_EXPERTS].astype(I32)
    padded = (counts + EXPERT_BLOCK - 1) // EXPERT_BLOCK * EXPERT_BLOCK
    pad_end = jnp.cumsum(padded)
    start_pad = pad_end - padded
    n_blocks = -(-(np_ + ns) * TOP_K // EXPERT_BLOCK) + N_EXPERTS
    n_slots = n_blocks * EXPERT_BLOCK
    n_used = (pad_end[-1] // EXPERT_BLOCK).astype(I32)
    block_start = jnp.arange(n_blocks, dtype=I32) * EXPERT_BLOCK
    block_e = jnp.minimum(jnp.sum((block_start[:, None] >= pad_end[None, :]).astype(I32), axis=1), N_EXPERTS - 1)
    last_e = block_e[jnp.maximum(n_used - 1, 0)]
    block_e = jnp.where(jnp.arange(n_blocks) < n_used, block_e, last_e)

    start_row = start_pad * SUBLANES

    def dest_of(route):
        idx = route[:, :TOP_K]
        rank_row = route[:, TOP_K:2 * TOP_K]
        start = jnp.sum(jnp.where(idx[..., None] == jnp.arange(N_EXPERTS, dtype=I32), start_row, 0), axis=-1)
        return (start + rank_row).reshape(-1).astype(I32)

    dest_p, dest_s = dest_of(fp["route"]), dest_of(fs["route"])
    tail = jnp.stack([n_used * EXPERT_BLOCK, n_blocks - n_used]).astype(I32)
    xs = _dispatch(start_pad + counts, padded - counts, tail, dest_p, dest_s, fp["xn"], fs["xn"], n_slots,
                   _tile(np_, 2048))
    ys = _experts(block_e, n_used.reshape(1), xs, w_gu, b_gu, w_down, b_down)

    outs = []
    for f, dest, p, tm in ((fp, dest_p, pp, tm_p), (fs, dest_s, ps, tm_s)):
        n = f["h"].shape[0]
        outs.append(_combine(dest, ys, f["gates"], f["h"], p.reshape(n, PLE_DIM).astype(F32), wts["ln3_g"],
                             wts["ple_gate_bf"], wts["ple_proj_bf"], final_g, tm))
    return outs[0].reshape(hp.shape), outs[1].reshape(hs.shape), fp, fs


def kernel(x_prompt, x_sample, p_prompt, p_sample, cache_pool, state_shift, state_wkv, ln1_g, w_in, mu_shift, w_pool, pool_scale, w0, w2, a0, a2, g2, k_k, k_a, r_k, lnx_g, lnx_b, w_out, ln2_g, router_w, router_b, w_gu, b_gu, w_down, b_down, ln3_g, ple_gate, ple_proj, final_g):
    assert ln1_g.shape[0] == 1, "single-layer kernel"
    i = 0
    wts = _prep_weights(i, ln1_g, w_in, mu_shift, w_pool, pool_scale, w0, w2, a0, a2, g2, k_k, k_a, r_k,
                        lnx_g, lnx_b, w_out, ln2_g, router_w, router_b, ln3_g, ple_gate, ple_proj)
    y_p, y_s, fp, fs = _layer_pair(x_prompt, x_sample, p_prompt[i], p_sample[i], cache_pool[i], state_shift[i],
                                   state_wkv[i], wts, w_gu[i], b_gu[i], w_down[i], b_down[i], final_g)
    stack = lambda a: a[None]
    return (y_p, y_s,
            stack(fp["new_pool"]), stack(fp["new_shift"]), stack(fp["new_wkv"]),
            stack(fs["new_pool"]), stack(fs["new_shift"]), stack(fs["new_wkv"]))
```

```python
import functools

import jax
import jax.numpy as jnp
from jax import lax
from jax.experimental import pallas as pl
from jax.experimental.pallas import tpu as pltpu

F32 = jnp.float32
BF16 = jnp.bfloat16
I32 = jnp.int32

D_MODEL = 1024
POOL_W = 512
POOL_WINDOWS = (2, 4, 8, 16)
POOL_GW = 128
POOL_CTX = 15
POOL_HALO = 16
RWKV_W = 512
HEAD_DIM = 64
N_HEADS = 8
N_PAIRS = N_HEADS // 2
PAIR_W = 2 * HEAD_DIM
DECAY_LORA = 64
ICLR_LORA = 64
GATE_LORA = 128
RWKV_PROJ = 3 * RWKV_W + DECAY_LORA + ICLR_LORA + GATE_LORA
IN_PROJ = POOL_W + RWKV_PROJ
LNX_EPS = 64e-5
N_EXPERTS = 32
TOP_K = 4
D_EXPERT = 1024
SWIGLU_LIMIT = 7.0
SWIGLU_ALPHA = 1.702
PLE_DIM = 256
RMS_EPS = 1e-6
PAST_LEN = 2048

CHUNK = 64
EXPERT_BLOCK = 512
ROUTE_LANES = 128
SUBLANES = 8
LANES = 128
assert D_MODEL == SUBLANES * LANES
PREP_ROWS = 256
ISSUE_UNROLL = 8
GROUP_TILE = 256
EXP_M05 = 0.6065306597126334
V7X_VMEM_BYTES = 64 * 1024 * 1024

NT_DIMS = (((1,), (1,)), ((), ()))
TN_DIMS = (((0,), (0,)), ((), ()))


def _vmem_limit(est_bytes):
    return int(min(est_bytes * 3 // 2 + (4 << 20), V7X_VMEM_BYTES - (8 << 20)))


def _params(n_axes, est_bytes):
    return pltpu.CompilerParams(dimension_semantics=("arbitrary",) * n_axes,
                                vmem_limit_bytes=_vmem_limit(est_bytes))


def _dot(a, b):
    return jnp.dot(a, b, preferred_element_type=F32)


def _split2(x):
    hi = x.astype(BF16)
    lo = (x - hi.astype(F32)).astype(BF16)
    return hi, lo


def _split3(x):
    hi = x.astype(BF16)
    r1 = x - hi.astype(F32)
    mid = r1.astype(BF16)
    lo = (r1 - mid.astype(F32)).astype(BF16)
    return hi, mid, lo


def _group_sum(x, ones_bd):
    xb = x.astype(BF16)
    slabs = [_dot(xb[:, s:s + GROUP_TILE], ones_bd) for s in range(0, x.shape[1], GROUP_TILE)]
    return jnp.concatenate(slabs, axis=1)


def _store_row_tiles(ref, value):
    n = value.shape[0]
    for c in range(SUBLANES):
        ref[pl.ds(c, n, stride=SUBLANES), :] = value[:, c * LANES:(c + 1) * LANES]


def _load_row_tiles(ref, n):
    return jnp.concatenate([ref[pl.ds(c, n, stride=SUBLANES), :] for c in range(SUBLANES)], axis=1)


def _rmsnorm_rows(x, g):
    ms = jnp.mean(x * x, axis=-1, keepdims=True)
    return (x * lax.rsqrt(ms + RMS_EPS)) * g


def _in_proj_pool_kernel(x_ref, ctx_ref, g_ref, w_ref, wp_ref, ps_ref, zr_ref, po_ref, last_ref, buf, *, pos0):
    j = pl.program_id(1)
    bb, tt, _ = x_ref.shape
    cur = lax.rem(j, 2)
    prev = 1 - cur

    @pl.when(j == 0)
    def _():
        buf[1] = jnp.zeros(buf.shape[1:], F32)

    xn = _rmsnorm_rows(x_ref[...].reshape(bb * tt, D_MODEL), g_ref[...])
    z = _dot(xn.astype(BF16), w_ref[...])
    zr_ref[...] = z[:, POOL_W:].reshape(bb, tt, RWKV_PROJ)
    last = buf[prev, :, tt:tt + POOL_HALO, :]
    last_ref[...] = last
    buf[cur, :, 0:POOL_HALO, :] = jnp.where(j == 0, ctx_ref[...], last)
    buf[cur, :, POOL_HALO:, :] = z[:, :POOL_W].reshape(bb, tt, POOL_W)

    pos = lax.broadcasted_iota(I32, (1, tt, 1), 1) + (jnp.maximum(j - 1, 0) * tt + pos0)
    outs = []
    for gi, w in enumerate(POOL_WINDOWS):
        sl = slice(gi * POOL_GW, (gi + 1) * POOL_GW)
        tok = buf[prev, :, POOL_HALO:POOL_HALO + tt, sl]
        acc = tok
        for k in range(1, w):
            acc = acc + buf[prev, :, POOL_HALO - k:POOL_HALO - k + tt, sl]
        cnt = jnp.minimum(w, pos + 1).astype(F32)
        pooled = (acc / cnt - tok).reshape(bb * tt, POOL_GW)
        outs.append(_dot(pooled.astype(BF16), wp_ref[gi]))
    po = jnp.concatenate(outs, axis=-1) * ps_ref[...]
    po_ref[...] = po.reshape(bb, tt, POOL_W).astype(po_ref.dtype)


def _in_proj_pool(x, ctx16, ln1_g, w_in_bf, w_pool_bf, pool_scale, pos0, bb, tt):
    b, t, _ = x.shape
    nt = t // tt
    cst = lambda shape: pl.BlockSpec(shape, lambda i, j: (0,) * len(shape))
    proj = lambda w: pl.BlockSpec((bb, tt, w), lambda i, j: (i, jnp.minimum(j, nt - 1), 0))
    mixed = pl.BlockSpec((bb, tt, POOL_W), lambda i, j: (i, jnp.maximum(j - 1, 0), 0))
    halo = pl.BlockSpec((bb, POOL_HALO, POOL_W), lambda i, j: (i, 0, 0))
    est = (2 * bb * tt * (D_MODEL + RWKV_PROJ) * 4 + 2 * D_MODEL * IN_PROJ * 2 + 3 * bb * tt * IN_PROJ * 4
           + 2 * bb * (tt + POOL_HALO) * POOL_W * 4)
    return pl.pallas_call(
        functools.partial(_in_proj_pool_kernel, pos0=pos0),
        grid=(b // bb, nt + 1),
        in_specs=[proj(D_MODEL), halo, cst((1, D_MODEL)), cst((D_MODEL, IN_PROJ)),
                  cst((len(POOL_WINDOWS), POOL_GW, POOL_GW)), cst((1, POOL_W))],
        out_specs=[proj(RWKV_PROJ), mixed, halo],
        out_shape=[jax.ShapeDtypeStruct((b, t, RWKV_PROJ), F32), jax.ShapeDtypeStruct((b, t, POOL_W), BF16),
                   jax.ShapeDtypeStruct((b, POOL_HALO, POOL_W), F32)],
        scratch_shapes=[pltpu.VMEM((2, bb, tt + POOL_HALO, POOL_W), F32)],
        compiler_params=_params(2, est),
        name="in_proj_pool",
    )(x, ctx16, ln1_g.reshape(1, D_MODEL), w_in_bf, w_pool_bf, pool_scale.reshape(1, POOL_W))


def _rwkv_prepare(zr, prev, mu, w0, w2, a0, a2, g2, k_k, k_a, r_k, ones_bd):
    zs = zr + (prev - zr) * mu
    o1, o2, o3 = RWKV_W, 2 * RWKV_W, 3 * RWKV_W
    o4, o5 = o3 + DECAY_LORA, o3 + DECAY_LORA + ICLR_LORA
    r, k, v = zs[:, :o1], zs[:, o1:o2], zs[:, o2:o3]
    wd, ad, gd = zs[:, o3:o4], zs[:, o4:o5], zs[:, o5:]
    wz = w0 + _dot(jnp.tanh(wd).astype(BF16), w2)
    lw = -EXP_M05 * jax.nn.sigmoid(wz)
    a = jax.nn.sigmoid(a0 + _dot(ad.astype(BF16), a2))
    g = _dot(jax.nn.sigmoid(gd).astype(BF16), g2)
    kk = k * k_k
    kk = kk / jnp.maximum(jnp.sqrt(_group_sum(kk * kk, ones_bd)), 1e-12)
    kx = k * (1.0 + (a - 1.0) * k_a)
    bonus = _group_sum(r * kx * r_k, ones_bd) * v
    return r, lw, kx, v, -kk, kk * a, g, bonus


def _pair_blockdiag(z, left):
    return jnp.concatenate([jnp.where(left, z, 0.0), jnp.where(left, 0.0, z)], axis=0)


def _wkv_kernel(zr_ref, halo_ref, sh_ref, mu_ref, w0_ref, w2_ref, a0_ref, a2_ref, g2_ref, kk_ref, ka_ref, rk_ref,
                ones_ref, lg_ref, lb_ref, s0_ref, o_ref, sf_ref,
                buf, r_ref, lw_ref, kx_ref, v_ref, na_ref, kb_ref, g_ref, bonus_ref, y_ref, s_scr, *, t_valid):
    t = pl.program_id(1)
    tt = zr_ref.shape[1]
    c = CHUNK
    rows_per_pass = min(tt, PREP_ROWS)

    @pl.when(t == 0)
    def _():
        s_scr[...] = s0_ref[0]

    buf[0:SUBLANES, :] = jnp.where(t == 0, sh_ref[0], halo_ref[0])
    buf[SUBLANES:, :] = zr_ref[0]
    ones_bd = ones_ref[...]
    for r0 in range(0, tt, rows_per_pass):
        rows = slice(r0, r0 + rows_per_pass)
        vals = _rwkv_prepare(zr_ref[0, rows, :], buf[SUBLANES - 1 + r0:SUBLANES - 1 + r0 + rows_per_pass, :],
                             mu_ref[...], w0_ref[...], w2_ref[...], a0_ref[...], a2_ref[...], g2_ref[...],
                             kk_ref[...], ka_ref[...], rk_ref[...], ones_bd)
        if t_valid is not None:
            live = lax.broadcasted_iota(I32, (rows_per_pass, 1), 0) + (t * tt + r0) < t_valid
            vals = tuple(jnp.where(live, x, 0.0) for x in vals)
        for ref, x in zip((r_ref, lw_ref, kx_ref, v_ref, na_ref, kb_ref, g_ref, bonus_ref), vals):
            ref[rows, :] = x

    lane = lax.broadcasted_iota(I32, (c, 2 * c), 1)
    rowi = lax.broadcasted_iota(I32, (c, 2 * c), 0)
    jm = lane & (c - 1)
    strict = jm < rowi
    incl = jm <= rowi
    eye_pair = jnp.where(jm == rowi, 1.0, 0.0).astype(F32)
    left = lax.broadcasted_iota(I32, (c, PAIR_W), 1) < HEAD_DIM
    rr = lax.broadcasted_iota(I32, (c, c), 0)
    cc = lax.broadcasted_iota(I32, (c, c), 1)
    tri_incl_bf = jnp.where(cc <= rr, 1.0, 0.0).astype(BF16)
    br = lax.broadcasted_iota(I32, (PAIR_W, PAIR_W), 0) < HEAD_DIM
    bc = lax.broadcasted_iota(I32, (PAIR_W, PAIR_W), 1) < HEAD_DIM
    bd_mask = br == bc
    bf = lambda x: x.astype(BF16)
    bd = lambda z: _pair_blockdiag(z, left)
    n_chunks = tt // c

    cums = []
    for ci in range(n_chunks):
        h3 = _split3(lw_ref[ci * c:(ci + 1) * c, :])
        cums.append(_dot(tri_incl_bf, h3[0]) + _dot(tri_incl_bf, h3[1]) + _dot(tri_incl_bf, h3[2]))

    chains = []
    for ci in range(n_chunks):
        rows = slice(ci * c, (ci + 1) * c)
        for p in range(N_PAIRS):
            cols = slice(p * PAIR_W, (p + 1) * PAIR_W)
            cum = cums[ci][:, cols]
            lw, kx, kb = lw_ref[rows, cols], kx_ref[rows, cols], kb_ref[rows, cols]
            tot = cum[c - 1:c, :]
            p_end = jnp.exp(tot - cum)
            inv_p = jnp.exp(-cum)
            chains.append(dict(
                ci=ci, p=p, v=v_ref[rows, cols], p_c=jnp.exp(tot),
                a_t=na_ref[rows, cols] * jnp.exp(cum - lw), r_t=r_ref[rows, cols] * jnp.exp(cum),
                b_t=kb * inv_p, k_t=kx * inv_p, b_h=kb * p_end, k_h=kx * p_end))

    for d in chains:
        lhs = bf(jnp.concatenate([d["a_t"], d["r_t"]], axis=0))
        rhs = bf(jnp.concatenate([bd(d["b_t"]), bd(d["k_t"])], axis=0))
        sc = lax.dot_general(lhs, rhs, NT_DIMS, preferred_element_type=F32)
        d["a_ab"] = jnp.where(strict, sc[:c, :2 * c], 0.0)
        d["a_k"] = jnp.concatenate([jnp.where(strict, sc[:c, 2 * c:], 0.0),
                                    jnp.where(incl, sc[c:, 2 * c:], 0.0)], axis=0)
        d["a_rb"] = jnp.where(incl, sc[c:, :2 * c], 0.0)

    for d in chains:
        d["x"] = _dot(bf(d["a_ab"]), bf(bd(d["a_ab"])))
        d["t"] = eye_pair + d["a_ab"]
    for _ in range(c.bit_length() - 3):
        for d in chains:
            st = _dot(bf(jnp.concatenate([d["x"], d["t"]], axis=0)), bf(bd(d["x"])))
            d["x"] = st[:c]
            d["t"] = d["t"] + st[c:]
    for d in chains:
        d["t"] = d["t"] + _dot(bf(d["t"]), bf(bd(d["x"])))

    for d in chains:
        wv = _dot(bf(d["a_k"]), bf(bd(d["v"])))
        d["w1"], d["rkv"] = wv[:c], wv[c:]
    for d in chains:
        tu = _dot(bf(d["t"]), bf(jnp.concatenate([bd(d["w1"]), bd(d["a_t"])], axis=1)))
        d["u_loc"], d["a_tt"] = tu[:, :PAIR_W], tu[:, PAIR_W:]
    for d in chains:
        ar = _dot(bf(d["a_rb"]), bf(jnp.concatenate([bd(d["u_loc"]), bd(d["a_tt"])], axis=1)))
        d["y_loc"] = ar[:, :PAIR_W] + d["rkv"]
        d["r_g"] = d["r_t"] + ar[:, PAIR_W:]
    for d in chains:
        lhs = jnp.concatenate([jnp.concatenate([d["v"], jnp.zeros_like(d["v"])], axis=1),
                               jnp.concatenate([d["u_loc"], d["a_tt"]], axis=1)], axis=0)
        rhs = jnp.concatenate([d["k_h"], d["b_h"]], axis=0)
        dp = lax.dot_general(bf(lhs), bf(rhs), TN_DIMS, preferred_element_type=F32)
        d["d_loc"] = jnp.where(bd_mask, dp[:PAIR_W], 0.0)
        d["phi"] = bf(jnp.where(bd_mask, dp[PAIR_W:], 0.0))

    state = [s_scr[p] for p in range(N_PAIRS)]
    for d in chains:
        p = d["p"]
        s_hi, s_lo = _split2(state[p])
        y = lax.dot_general(bf(d["r_g"]), s_hi, NT_DIMS, preferred_element_type=F32) + d["y_loc"]
        y_ref[d["ci"] * c:(d["ci"] + 1) * c, p * PAIR_W:(p + 1) * PAIR_W] = y
        state[p] = state[p] * d["p_c"] + (_dot(s_hi, d["phi"]) + _dot(s_lo, d["phi"])) + d["d_loc"]
    for p in range(N_PAIRS):
        s_scr[p] = state[p]

    @pl.when(t == pl.num_programs(1) - 1)
    def _():
        sf_ref[0] = s_scr[...]

    inv_n = 1.0 / HEAD_DIM
    for r0 in range(0, tt, rows_per_pass):
        rows = slice(r0, r0 + rows_per_pass)
        y = y_ref[rows, :]
        mu = _group_sum(y, ones_bd) * inv_n
        dlt = y - mu
        var = _group_sum(dlt * dlt, ones_bd) * inv_n
        yn = dlt * lax.rsqrt(var + LNX_EPS) * lg_ref[...] + lb_ref[...]
        o_ref[0, rows, :] = ((yn + bonus_ref[rows, :]) * g_ref[rows, :]).astype(o_ref.dtype)


def _rwkv_mixer(zr, sh8, s0_pair, wts, tt, t_valid):
    b, t, _ = zr.shape
    assert CHUNK == HEAD_DIM and tt % CHUNK == 0
    hb = tt // SUBLANES
    row = lambda x: x.reshape(1, -1)
    cst = lambda shape: pl.BlockSpec(shape, lambda i, j: (0,) * len(shape))
    seq = pl.BlockSpec((1, tt, RWKV_W), lambda i, j: (i, j, 0))
    st = pl.BlockSpec((1, N_PAIRS, PAIR_W, PAIR_W), lambda i, j: (i, 0, 0, 0))
    est = 3 * tt * RWKV_PROJ * 4 + 10 * tt * RWKV_W * 4 + 5 * N_PAIRS * PAIR_W * PAIR_W * 4 + (8 << 20)
    return pl.pallas_call(
        functools.partial(_wkv_kernel, t_valid=t_valid),
        grid=(b, t // tt),
        in_specs=[pl.BlockSpec((1, tt, RWKV_PROJ), lambda i, j: (i, j, 0)),
                  pl.BlockSpec((1, SUBLANES, RWKV_PROJ), lambda i, j: (i, jnp.maximum(j * hb - 1, 0), 0)),
                  pl.BlockSpec((1, SUBLANES, RWKV_PROJ), lambda i, j: (i, 0, 0)),
                  cst((1, RWKV_PROJ)), cst((1, RWKV_W)), cst((DECAY_LORA, RWKV_W)),
                  cst((1, RWKV_W)), cst((ICLR_LORA, RWKV_W)), cst((GATE_LORA, RWKV_W)),
                  cst((1, RWKV_W)), cst((1, RWKV_W)), cst((1, RWKV_W)), cst((GROUP_TILE, GROUP_TILE)),
                  cst((1, RWKV_W)), cst((1, RWKV_W)), st],
        out_specs=[seq, st],
        out_shape=[jax.ShapeDtypeStruct((b, t, RWKV_W), BF16),
                   jax.ShapeDtypeStruct((b, N_PAIRS, PAIR_W, PAIR_W), F32)],
        scratch_shapes=([pltpu.VMEM((tt + SUBLANES, RWKV_PROJ), F32)] + [pltpu.VMEM((tt, RWKV_W), F32)] * 9
                        + [pltpu.VMEM((N_PAIRS, PAIR_W, PAIR_W), F32)]),
        compiler_params=_params(2, est),
        name="rwkv_mixer",
    )(zr, zr, sh8, row(wts["mu_shift"]), row(wts["w0"]), wts["w2_bf"], row(wts["a0"]), wts["a2_bf"], wts["g2_bf"],
      row(wts["k_k"]), row(wts["k_a"]), row(wts["r_k"]), wts["ones_bd"], row(wts["lnx_g"]), row(wts["lnx_b"]),
      s0_pair)


def _state_to_pairs(s):
    b = s.shape[0]
    s = s.astype(F32).reshape(b, N_PAIRS, 2, HEAD_DIM, HEAD_DIM)
    z = jnp.zeros((b, N_PAIRS, HEAD_DIM, HEAD_DIM), F32)
    return jnp.concatenate([jnp.concatenate([s[:, :, 0], z], axis=-1),
                            jnp.concatenate([z, s[:, :, 1]], axis=-1)], axis=-2)


def _pairs_to_state(sp):
    b = sp.shape[0]
    s = jnp.stack([sp[:, :, :HEAD_DIM, :HEAD_DIM], sp[:, :, HEAD_DIM:, HEAD_DIM:]], axis=2)
    return s.reshape(b, N_HEADS, HEAD_DIM, HEAD_DIM)


def _mix_out_kernel(rw_ref, po_ref, x_ref, wo_ref, ln2_ref, rw_hi_ref, rw_lo_ref, rb_ref, cin_ref,
                    h_o, xn_o, route_o, gate_o, cnt_o, carry):
    i = pl.program_id(0)
    tm = x_ref.shape[0]

    @pl.when(i == 0)
    def _():
        carry[...] = cin_ref[...]

    mix = _dot(po_ref[...], wo_ref[:POOL_W, :]) + _dot(rw_ref[...], wo_ref[POOL_W:, :])
    h = x_ref[...] + mix
    h_o[...] = h
    xn = _rmsnorm_rows(h, ln2_ref[...])
    _store_row_tiles(xn_o, xn)

    x_hi, x_lo = _split2(xn)
    logits = (_dot(x_hi, rw_hi_ref[...]) + _dot(x_hi, rw_lo_ref[...]) + _dot(x_lo, rw_hi_ref[...])
              + rb_ref[...])
    lane = lax.broadcasted_iota(I32, (tm, ROUTE_LANES), 1)
    lane_f = lane.astype(F32)
    vals, idxs, hots = [], [], []
    work = logits
    for _ in range(TOP_K):
        m = jnp.max(work, axis=-1, keepdims=True)
        idx = jnp.min(jnp.where(work == m, lane_f, float(ROUTE_LANES)), axis=-1, keepdims=True)
        hit = lane_f == idx
        vals.append(m)
        idxs.append(idx)
        hots.append(jnp.where(hit, 1.0, 0.0).astype(F32))
        work = jnp.where(hit, -jnp.inf, work)
    exps = [jnp.exp(vv - vals[0]) for vv in vals]
    den = exps[0] + exps[1] + exps[2] + exps[3]

    hot_all = hots[0] + hots[1] + hots[2] + hots[3]
    rr = lax.broadcasted_iota(I32, (tm, tm), 0)
    cc = lax.broadcasted_iota(I32, (tm, tm), 1)
    tri_strict = jnp.where(cc < rr, 1.0, 0.0).astype(BF16)
    prefix = _dot(tri_strict, hot_all.astype(BF16)) + carry[0:1, :]
    route = jnp.zeros((tm, ROUTE_LANES), F32)
    gates = jnp.zeros((tm, ROUTE_LANES), F32)
    for j in range(TOP_K):
        rank = jnp.sum(hots[j] * prefix, axis=-1, keepdims=True)
        route = jnp.where(lane == j, idxs[j], route)
        route = jnp.where(lane == TOP_K + j, rank * SUBLANES, route)
        gates = jnp.where(lane == j, exps[j] / den, gates)
    route_o[...] = route.astype(I32)
    gate_o[...] = gates
    new_carry = carry[...] + jnp.sum(hot_all, axis=0, keepdims=True)
    carry[...] = new_carry
    cnt_o[...] = new_carry


def _mix_out(rw2d, po2d, x2d, w_out_bf, ln2_g, rw_hi, rw_lo, rb_pad, carry_in, tm):
    n = x2d.shape[0]
    row = lambda x: x.reshape(1, -1)
    cst = lambda shape: pl.BlockSpec(shape, lambda i: (0,) * len(shape))
    half = pl.BlockSpec((tm, RWKV_W), lambda i: (i, 0))
    full = pl.BlockSpec((tm, D_MODEL), lambda i: (i, 0))
    lanes = pl.BlockSpec((tm, ROUTE_LANES), lambda i: (i, 0))
    est = (2 * (2 * tm * RWKV_W * 2 + 3 * tm * D_MODEL * 4) + 2 * D_MODEL * D_MODEL * 2
           + 8 * tm * D_MODEL * 4 + 4 * tm * tm)
    return pl.pallas_call(
        _mix_out_kernel,
        grid=(n // tm,),
        in_specs=[half, half, full, cst((D_MODEL, D_MODEL)),
                  cst((1, D_MODEL)), cst((D_MODEL, ROUTE_LANES)), cst((D_MODEL, ROUTE_LANES)),
                  cst((1, ROUTE_LANES)), cst((SUBLANES, ROUTE_LANES))],
        out_specs=[full, pl.BlockSpec((tm * SUBLANES, LANES), lambda i: (i, 0)), lanes, lanes,
                   cst((SUBLANES, ROUTE_LANES))],
        out_shape=[jax.ShapeDtypeStruct((n, D_MODEL), F32), jax.ShapeDtypeStruct((n * SUBLANES, LANES), F32),
                   jax.ShapeDtypeStruct((n, ROUTE_LANES), I32), jax.ShapeDtypeStruct((n, ROUTE_LANES), F32),
                   jax.ShapeDtypeStruct((SUBLANES, ROUTE_LANES), F32)],
        scratch_shapes=[pltpu.VMEM((SUBLANES, ROUTE_LANES), F32)],
        compiler_params=_params(1, est),
        name="mix_out_router",
    )(rw2d, po2d, x2d, w_out_bf, row(ln2_g), rw_hi, rw_lo, rb_pad, carry_in)


def _tiles(ref, first_row, n_slots):
    if not isinstance(first_row, int):
        first_row = pl.multiple_of(first_row, SUBLANES)
    return ref.at[pl.ds(first_row, n_slots * SUBLANES), :]


def _dispatch_kernel(pad_start_ref, pad_len_ref, tail_ref, dest_a_ref, dest_b_ref, xa_ref, xb_ref, o_hbm, zbuf, sem, sem_fill,
                     *, n_main, n_last):
    i = pl.program_id(0)
    tm = xa_ref.shape[0] // SUBLANES
    blk = zbuf.shape[0] // SUBLANES

    @pl.when(i == 0)
    def _():
        zbuf[...] = jnp.zeros_like(zbuf)

        def per_expert(e, totals):
            start, n = pad_start_ref[e], pad_len_ref[e]
            n_groups = lax.shift_right_logical(n, 3)
            n_single = n - n_groups * SUBLANES

            def group(g, carry):
                pltpu.make_async_copy(_tiles(zbuf, 0, SUBLANES),
                                      _tiles(o_hbm, (start + g * SUBLANES) * SUBLANES, SUBLANES), sem_fill).start()
                return carry

            def single(r, carry):
                pltpu.make_async_copy(_tiles(zbuf, 0, 1),
                                      _tiles(o_hbm, (start + n_groups * SUBLANES + r) * SUBLANES, 1), sem_fill).start()
                return carry

            lax.fori_loop(0, n_groups, group, 0)
            lax.fori_loop(0, n_single, single, 0)
            return totals[0] + n_groups, totals[1] + n_single

        n_groups_filled, n_single_filled = lax.fori_loop(0, N_EXPERTS, per_expert, (0, 0))

        def wait_group(g, carry):
            pltpu.make_async_copy(_tiles(zbuf, 0, SUBLANES), _tiles(o_hbm, 0, SUBLANES), sem_fill).wait()
            return carry

        def wait_single(r, carry):
            pltpu.make_async_copy(_tiles(zbuf, 0, 1), _tiles(o_hbm, 0, 1), sem_fill).wait()
            return carry

        lax.fori_loop(0, n_groups_filled, wait_group, 0)
        lax.fori_loop(0, n_single_filled, wait_single, 0)

        def fill_block(b, carry):
            pltpu.make_async_copy(zbuf, _tiles(o_hbm, (tail_ref[0] + b * blk) * SUBLANES, blk), sem_fill).start()
            return carry

        def wait_block(b, carry):
            pltpu.make_async_copy(zbuf, _tiles(o_hbm, 0, blk), sem_fill).wait()
            return carry

        lax.fori_loop(0, tail_ref[1], fill_block, 0)
        lax.fori_loop(0, tail_ref[1], wait_block, 0)

    def scatter_rows(d_ref, x_ref, n_rows):
        def issue(n, carry):
            src = _tiles(x_ref, n * SUBLANES, 1)
            for j in range(TOP_K):
                pltpu.make_async_copy(src, _tiles(o_hbm, d_ref[n * TOP_K + j], 1), sem).start(priority=j % 2)
            return carry

        lax.fori_loop(0, n_rows, issue, 0, unroll=ISSUE_UNROLL)
        pltpu.make_async_copy(_tiles(o_hbm, 0, n_rows * TOP_K), _tiles(o_hbm, 0, n_rows * TOP_K), sem).wait()

    @pl.when(i < n_main)
    def _():
        scatter_rows(dest_a_ref, xa_ref, tm)

    @pl.when(i == n_main)
    def _():
        scatter_rows(dest_b_ref, xb_ref, n_last)


def _dispatch(pad_start, pad_len, tail, dest_a, dest_b, xn_a, xn_b, n_slots, tm):
    n_main = xn_a.shape[0] // (tm * SUBLANES)
    n_last = xn_b.shape[0] // SUBLANES
    assert n_slots >= max(tm, n_last) * TOP_K
    grid_spec = pltpu.PrefetchScalarGridSpec(
        num_scalar_prefetch=3,
        grid=(n_main + 1,),
        in_specs=[pl.BlockSpec((tm * TOP_K,), lambda i, *_: (jnp.minimum(i, n_main - 1),),
                               memory_space=pltpu.SMEM),
                  pl.BlockSpec((n_last * TOP_K,), lambda i, *_: (0,), memory_space=pltpu.SMEM),
                  pl.BlockSpec((tm * SUBLANES, LANES), lambda i, *_: (jnp.minimum(i, n_main - 1), 0)),
                  pl.BlockSpec((n_last * SUBLANES, LANES), lambda i, *_: (0, 0))],
        out_specs=pl.BlockSpec(memory_space=pl.ANY),
        scratch_shapes=[pltpu.VMEM((EXPERT_BLOCK * SUBLANES, LANES), F32), pltpu.SemaphoreType.DMA(()),
                        pltpu.SemaphoreType.DMA(())],
    )
    return pl.pallas_call(
        functools.partial(_dispatch_kernel, n_main=n_main, n_last=n_last),
        grid_spec=grid_spec,
        out_shape=jax.ShapeDtypeStruct((n_slots * SUBLANES, LANES), F32),
        compiler_params=_params(1, 2 * (tm + n_last) * D_MODEL * 4 + EXPERT_BLOCK * D_MODEL * 4),
        name="dispatch",
    )(pad_start, pad_len, tail, dest_a, dest_b, xn_a, xn_b)


def _expert_kernel(be_ref, nu_ref, xs_ref, wgu_ref, bgu_ref, wdn_ref, bdn_ref, o_ref, wgu_bf, wdn_bf):
    i = pl.program_id(0)
    used = i < nu_ref[0]
    changed = jnp.logical_or(i == 0, be_ref[i] != be_ref[jnp.maximum(i - 1, 0)])

    @pl.when(jnp.logical_and(used, changed))
    def _():
        wgu_bf[...] = wgu_ref[0].astype(BF16)
        wdn_bf[...] = wdn_ref[0].astype(BF16)

    @pl.when(used)
    def _():
        x = _load_row_tiles(xs_ref, xs_ref.shape[0] // SUBLANES)
        gu = _dot(x.astype(BF16), wgu_bf[...]) + bgu_ref[0]
        gate = jnp.minimum(gu[:, :D_EXPERT], SWIGLU_LIMIT)
        up = jnp.clip(gu[:, D_EXPERT:], -SWIGLU_LIMIT, SWIGLU_LIMIT)
        hmid = (up + 1.0) * (gate * jax.nn.sigmoid(SWIGLU_ALPHA * gate))
        _store_row_tiles(o_ref, _dot(hmid.astype(BF16), wdn_bf[...]) + bdn_ref[0])

    @pl.when(jnp.logical_not(used))
    def _():
        o_ref[...] = jnp.zeros_like(o_ref)


def _experts(block_e, n_used, xs, w_gu, b_gu, w_down, b_down):
    n_slots = xs.shape[0] // SUBLANES
    nb = n_slots // EXPERT_BLOCK
    blk = EXPERT_BLOCK
    slots = pl.BlockSpec((blk * SUBLANES, LANES), lambda i, be, nu: (i, 0))
    est = (2 * 2 * blk * D_MODEL * 4 + 2 * (D_MODEL * 2 * D_EXPERT + D_EXPERT * D_MODEL) * 4
           + (D_MODEL * 2 * D_EXPERT + D_EXPERT * D_MODEL) * 2 + 4 * blk * 2 * D_EXPERT * 4)
    grid_spec = pltpu.PrefetchScalarGridSpec(
        num_scalar_prefetch=2,
        grid=(nb,),
        in_specs=[slots,
                  pl.BlockSpec((1, D_MODEL, 2 * D_EXPERT), lambda i, be, nu: (be[i], 0, 0)),
                  pl.BlockSpec((1, 1, 2 * D_EXPERT), lambda i, be, nu: (be[i], 0, 0)),
                  pl.BlockSpec((1, D_EXPERT, D_MODEL), lambda i, be, nu: (be[i], 0, 0)),
                  pl.BlockSpec((1, 1, D_MODEL), lambda i, be, nu: (be[i], 0, 0))],
        out_specs=slots,
        scratch_shapes=[pltpu.VMEM((D_MODEL, 2 * D_EXPERT), BF16), pltpu.VMEM((D_EXPERT, D_MODEL), BF16)],
    )
    return pl.pallas_call(
        _expert_kernel,
        grid_spec=grid_spec,
        out_shape=jax.ShapeDtypeStruct((n_slots * SUBLANES, LANES), F32),
        compiler_params=_params(1, est),
        name="experts",
    )(block_e, n_used, xs, w_gu, b_gu.reshape(N_EXPERTS, 1, 2 * D_EXPERT), w_down,
      b_down.reshape(N_EXPERTS, 1, D_MODEL))


def _combine_kernel(dest_ref, dest_next_ref, ys_hbm, gate_ref, h_ref, p_ref, ln3_ref, pg_ref, pp_ref, fg_ref,
                    o_ref, gbuf, sem):
    i = pl.program_id(0)
    tm = h_ref.shape[0]
    slot = lax.rem(i, 2)

    def issue(d_ref, s):
        def body(n, carry):
            for j in range(TOP_K):
                pltpu.make_async_copy(_tiles(ys_hbm, d_ref[n * TOP_K + j], 1),
                                      _tiles(gbuf.at[s, j], n * SUBLANES, 1), sem.at[s]).start(priority=j % 2)
            return carry
        lax.fori_loop(0, tm, body, 0, unroll=ISSUE_UNROLL)

    @pl.when(i == 0)
    def _():
        issue(dest_ref, 0)

    @pl.when(i + 1 < pl.num_programs(0))
    def _():
        issue(dest_next_ref, 1 - slot)

    for j in range(TOP_K):
        pltpu.make_async_copy(gbuf.at[slot, j], gbuf.at[slot, j], sem.at[slot]).wait()

    gates = gate_ref[...]
    slabs = []
    for c in range(SUBLANES):
        acc = gates[:, 0:1] * gbuf[slot, 0, pl.ds(c, tm, stride=SUBLANES), :]
        for j in range(1, TOP_K):
            acc = acc + gates[:, j:j + 1] * gbuf[slot, j, pl.ds(c, tm, stride=SUBLANES), :]
        slabs.append(acc)
    h = h_ref[...] + jnp.concatenate(slabs, axis=1)
    gate = jax.nn.sigmoid(_dot(_rmsnorm_rows(h, ln3_ref[...]).astype(BF16), pg_ref[...]))
    h = h + gate * _dot(p_ref[...].astype(BF16), pp_ref[...])
    o_ref[...] = _rmsnorm_rows(h, fg_ref[...])


def _combine(dest_flat, ys, gates, h2d, p2d, ln3_g, ple_gate_bf, ple_proj_bf, final_g, tm):
    n = h2d.shape[0]
    nb = n // tm
    row = lambda x: x.reshape(1, -1)
    cst = lambda shape: pl.BlockSpec(shape, lambda i: (0,) * len(shape))
    est = (2 * TOP_K * tm * D_MODEL * 4 + 2 * (2 * tm * D_MODEL * 4 + tm * PLE_DIM * 4 + tm * ROUTE_LANES * 4)
           + 2 * (D_MODEL * D_MODEL * 2 + PLE_DIM * D_MODEL * 2) + 6 * tm * D_MODEL * 4)
    return pl.pallas_call(
        _combine_kernel,
        grid=(nb,),
        in_specs=[pl.BlockSpec((tm * TOP_K,), lambda i: (i,), memory_space=pltpu.SMEM),
                  pl.BlockSpec((tm * TOP_K,), lambda i: (jnp.minimum(i + 1, nb - 1),), memory_space=pltpu.SMEM),
                  pl.BlockSpec(memory_space=pl.ANY),
                  pl.BlockSpec((tm, ROUTE_LANES), lambda i: (i, 0)),
                  pl.BlockSpec((tm, D_MODEL), lambda i: (i, 0)),
                  pl.BlockSpec((tm, PLE_DIM), lambda i: (i, 0)),
                  cst((1, D_MODEL)), cst((D_MODEL, D_MODEL)), cst((PLE_DIM, D_MODEL)), cst((1, D_MODEL))],
        out_specs=pl.BlockSpec((tm, D_MODEL), lambda i: (i, 0)),
        out_shape=jax.ShapeDtypeStruct((n, D_MODEL), F32),
        scratch_shapes=[pltpu.VMEM((2, TOP_K, tm * SUBLANES, LANES), F32), pltpu.SemaphoreType.DMA((2,))],
        compiler_params=_params(1, est),
        name="combine_ple_norm",
    )(dest_flat, dest_flat, ys, gates, h2d, p2d, row(ln3_g), ple_gate_bf, ple_proj_bf, row(final_g))


def _tile(n, pref):
    t = min(n, pref)
    assert n % t == 0, (n, pref)
    return t


def _front(x, pos0, pool_ctx, shift_prev, wkv0, wts, carry_in, tm_mix):
    b, t, _ = x.shape
    n = b * t
    assert t >= POOL_CTX and t % SUBLANES == 0
    x2d = x.reshape(n, D_MODEL)
    ctx16 = jnp.concatenate([jnp.zeros((b, POOL_HALO - POOL_CTX, POOL_W), F32), pool_ctx.astype(F32)], axis=1)
    tt = _tile(t, 512)
    bb = _tile(b, max(1, 128 // tt))
    zr, pool_out, last = _in_proj_pool(x, ctx16, wts["ln1_g"], wts["w_in_bf"], wts["w_pool_bf"],
                                       wts["pool_scale"], pos0, bb, tt)
    new_pool = last[:, POOL_HALO - POOL_CTX:]

    sh8 = jnp.broadcast_to(shift_prev.astype(F32), (b, SUBLANES, RWKV_PROJ))
    new_shift = zr[:, -1:]

    t_pad = -(-t // CHUNK) * CHUNK
    zr_pad = zr if t_pad == t else jnp.pad(zr, ((0, 0), (0, t_pad - t), (0, 0)))
    rw_out, s_fin = _rwkv_mixer(zr_pad, sh8, _state_to_pairs(wkv0), wts, _tile(t_pad, 8 * CHUNK),
                                None if t_pad == t else t)
    rw_out = rw_out[:, :t]
    new_wkv = _pairs_to_state(s_fin)

    h, xn, route, gates, counts = _mix_out(
        rw_out.reshape(n, RWKV_W), pool_out.reshape(n, POOL_W), x2d, wts["w_out_bf"], wts["ln2_g"],
        wts["rw_hi"], wts["rw_lo"], wts["rb_pad"], carry_in, tm_mix)
    return dict(h=h, xn=xn, route=route, gates=gates, counts=counts,
                new_pool=new_pool, new_shift=new_shift, new_wkv=new_wkv)


def _prep_weights(i, ln1_g, w_in, mu_shift, w_pool, pool_scale, w0, w2, a0, a2, g2, k_k, k_a, r_k,
                  lnx_g, lnx_b, w_out, ln2_g, router_w, router_b, ln3_g, ple_gate, ple_proj):
    head = jnp.arange(GROUP_TILE) // HEAD_DIM
    ones_bd = (head[:, None] == head[None, :]).astype(BF16)
    rw_pad = jnp.pad(router_w[i].astype(F32), ((0, 0), (0, ROUTE_LANES - N_EXPERTS)))
    rw_hi = rw_pad.astype(BF16)
    rw_lo = (rw_pad - rw_hi.astype(F32)).astype(BF16)
    rb_pad = jnp.concatenate([router_b[i].astype(F32), jnp.full((ROUTE_LANES - N_EXPERTS,), -jnp.inf, F32)])
    return dict(
        ln1_g=ln1_g[i], w_in_bf=w_in[i].astype(BF16), mu_shift=mu_shift[i], w_pool_bf=w_pool[i].astype(BF16),
        pool_scale=pool_scale[i], w0=w0[i], w2_bf=w2[i].astype(BF16), a0=a0[i], a2_bf=a2[i].astype(BF16),
        g2_bf=g2[i].astype(BF16), k_k=k_k[i], k_a=k_a[i], r_k=r_k[i].reshape(RWKV_W), lnx_g=lnx_g[i],
        lnx_b=lnx_b[i], w_out_bf=w_out[i].astype(BF16), ln2_g=ln2_g[i], rw_hi=rw_hi, rw_lo=rw_lo,
        rb_pad=rb_pad.reshape(1, ROUTE_LANES), ones_bd=ones_bd, ln3_g=ln3_g[i],
        ple_gate_bf=ple_gate[i].astype(BF16), ple_proj_bf=ple_proj[i].astype(BF16))


def _layer_pair(hp, hs, pp, ps, pool_ctx, shift_prev, wkv0, wts, w_gu, b_gu, w_down, b_down, final_g):
    bp, tp, _ = hp.shape
    bs, ts, _ = hs.shape
    np_, ns = bp * tp, bs * ts
    tm_p = _tile(np_, 256)
    tm_s = _tile(ns, 128)
    zero_carry = jnp.zeros((SUBLANES, ROUTE_LANES), F32)
    fp = _front(hp, 0, jnp.zeros((bp, POOL_CTX, POOL_W), F32), jnp.zeros((bp, 1, RWKV_PROJ), F32),
                jnp.zeros((bp, N_HEADS, HEAD_DIM, HEAD_DIM), F32), wts, zero_carry, _tile(np_, 512))
    fs = _front(hs, PAST_LEN, pool_ctx, shift_prev, wkv0, wts, fp["counts"], tm_s)

    counts = fs["counts"][0, :N_EXPERTS].astype(I32)
    padded = (counts + EXPERT_BLOCK - 1) // EXPERT_BLOCK * EXPERT_BLOCK
    pad_end = jnp.cumsum(padded)
    start_pad = pad_end - padded
    n_blocks = -(-(np_ + ns) * TOP_K // EXPERT_BLOCK) + N_EXPERTS
    n_slots = n_blocks * EXPERT_BLOCK
    n_used = (pad_end[-1] // EXPERT_BLOCK).astype(I32)
    block_start = jnp.arange(n_blocks, dtype=I32) * EXPERT_BLOCK
    block_e = jnp.minimum(jnp.sum((block_start[:, None] >= pad_end[None, :]).astype(I32), axis=1), N_EXPERTS - 1)
    last_e = block_e[jnp.maximum(n_used - 1, 0)]
    block_e = jnp.where(jnp.arange(n_blocks) < n_used, block_e, last_e)

    start_row = start_pad * SUBLANES

    def dest_of(route):
        idx = route[:, :TOP_K]
        rank_row = route[:, TOP_K:2 * TOP_K]
        start = jnp.sum(jnp.where(idx[..., None] == jnp.arange(N_EXPERTS, dtype=I32), start_row, 0), axis=-1)
        return (start + rank_row).reshape(-1).astype(I32)

    dest_p, dest_s = dest_of(fp["route"]), dest_of(fs["route"])
    tail = jnp.stack([n_used * EXPERT_BLOCK, n_blocks - n_used]).astype(I32)
    xs = _dispatch(start_pad + counts, padded - counts, tail, dest_p, dest_s, fp["xn"], fs["xn"], n_slots,
                   _tile(np_, 2048))
    ys = _experts(block_e, n_used.reshape(1), xs, w_gu, b_gu, w_down, b_down)

    outs = []
    for f, dest, p, tm in ((fp, dest_p, pp, tm_p), (fs, dest_s, ps, tm_s)):
        n = f["h"].shape[0]
        outs.append(_combine(dest, ys, f["gates"], f["h"], p.reshape(n, PLE_DIM).astype(F32), wts["ln3_g"],
                             wts["ple_gate_bf"], wts["ple_proj_bf"], final_g, tm))
    return outs[0].reshape(hp.shape), outs[1].reshape(hs.shape), fp, fs


def kernel(x_prompt, x_sample, p_prompt, p_sample, cache_pool, state_shift, state_wkv, ln1_g, w_in, mu_shift, w_pool, pool_scale, w0, w2, a0, a2, g2, k_k, k_a, r_k, lnx_g, lnx_b, w_out, ln2_g, router_w, router_b, w_gu, b_gu, w_down, b_down, ln3_g, ple_gate, ple_proj, final_g):
    assert ln1_g.shape[0] == 1, "single-layer kernel"
    i = 0
    wts = _prep_weights(i, ln1_g, w_in, mu_shift, w_pool, pool_scale, w0, w2, a0, a2, g2, k_k, k_a, r_k,
                        lnx_g, lnx_b, w_out, ln2_g, router_w, router_b, ln3_g, ple_gate, ple_proj)
    y_p, y_s, fp, fs = _layer_pair(x_prompt, x_sample, p_prompt[i], p_sample[i], cache_pool[i], state_shift[i],
                                   state_wkv[i], wts, w_gu[i], b_gu[i], w_down[i], b_down[i], final_g)
    stack = lambda a: a[None]
    return (y_p, y_s,
            stack(fp["new_pool"]), stack(fp["new_shift"]), stack(fp["new_wkv"]),
            stack(fs["new_pool"]), stack(fs["new_shift"]), stack(fs["new_wkv"]))
```

```python
import functools

import jax
import jax.numpy as jnp
from jax import lax
from jax.experimental import pallas as pl
from jax.experimental.pallas import tpu as pltpu

F32 = jnp.float32
BF16 = jnp.bfloat16
I32 = jnp.int32

D_MODEL = 1024
POOL_W = 512
POOL_WINDOWS = (2, 4, 8, 16)
POOL_GW = 128
POOL_CTX = 15
POOL_HALO = 16
RWKV_W = 512
HEAD_DIM = 64
N_HEADS = 8
N_PAIRS = N_HEADS // 2
PAIR_W = 2 * HEAD_DIM
DECAY_LORA = 64
ICLR_LORA = 64
GATE_LORA = 128
RWKV_PROJ = 3 * RWKV_W + DECAY_LORA + ICLR_LORA + GATE_LORA
IN_PROJ = POOL_W + RWKV_PROJ
LNX_EPS = 64e-5
N_EXPERTS = 32
TOP_K = 4
D_EXPERT = 1024
SWIGLU_LIMIT = 7.0
SWIGLU_ALPHA = 1.702
PLE_DIM = 256
RMS_EPS = 1e-6
PAST_LEN = 2048

CHUNK = 64
EXPERT_BLOCK = 512
ROUTE_LANES = 128
SUBLANES = 8
LANES = 128
assert D_MODEL == SUBLANES * LANES
PREP_ROWS = 256
ISSUE_UNROLL = 8
GROUP_TILE = 256
EXP_M05 = 0.6065306597126334
V7X_VMEM_BYTES = 64 * 1024 * 1024

NT_DIMS = (((1,), (1,)), ((), ()))
TN_DIMS = (((0,), (0,)), ((), ()))


def _vmem_limit(est_bytes):
    return int(min(est_bytes * 3 // 2 + (4 << 20), V7X_VMEM_BYTES - (8 << 20)))


def _params(n_axes, est_bytes):
    return pltpu.CompilerParams(dimension_semantics=("arbitrary",) * n_axes,
                                vmem_limit_bytes=_vmem_limit(est_bytes))


def _dot(a, b):
    return jnp.dot(a, b, preferred_element_type=F32)


def _split2(x):
    hi = x.astype(BF16)
    lo = (x - hi.astype(F32)).astype(BF16)
    return hi, lo


def _split3(x):
    hi = x.astype(BF16)
    r1 = x - hi.astype(F32)
    mid = r1.astype(BF16)
    lo = (r1 - mid.astype(F32)).astype(BF16)
    return hi, mid, lo


def _group_sum(x, ones_bd):
    xb = x.astype(BF16)
    slabs = [_dot(xb[:, s:s + GROUP_TILE], ones_bd) for s in range(0, x.shape[1], GROUP_TILE)]
    return jnp.concatenate(slabs, axis=1)


def _store_row_tiles(ref, value):
    n = value.shape[0]
    for c in range(SUBLANES):
        ref[pl.ds(c, n, stride=SUBLANES), :] = value[:, c * LANES:(c + 1) * LANES]


def _load_row_tiles(ref, n):
    return jnp.concatenate([ref[pl.ds(c, n, stride=SUBLANES), :] for c in range(SUBLANES)], axis=1)


def _rmsnorm_rows(x, g):
    ms = jnp.mean(x * x, axis=-1, keepdims=True)
    return (x * lax.rsqrt(ms + RMS_EPS)) * g


def _in_proj_pool_kernel(x_ref, ctx_ref, g_ref, w_ref, wp_ref, ps_ref, zr_ref, po_ref, last_ref, buf, *, pos0):
    j = pl.program_id(1)
    bb, tt, _ = x_ref.shape
    cur = lax.rem(j, 2)
    prev = 1 - cur

    @pl.when(j == 0)
    def _():
        buf[1] = jnp.zeros(buf.shape[1:], F32)

    xn = _rmsnorm_rows(x_ref[...].reshape(bb * tt, D_MODEL), g_ref[...])
    z = _dot(xn.astype(BF16), w_ref[...])
    zr_ref[...] = z[:, POOL_W:].reshape(bb, tt, RWKV_PROJ)
    last = buf[prev, :, tt:tt + POOL_HALO, :]
    last_ref[...] = last
    buf[cur, :, 0:POOL_HALO, :] = jnp.where(j == 0, ctx_ref[...], last)
    buf[cur, :, POOL_HALO:, :] = z[:, :POOL_W].reshape(bb, tt, POOL_W)

    pos = lax.broadcasted_iota(I32, (1, tt, 1), 1) + (jnp.maximum(j - 1, 0) * tt + pos0)
    outs = []
    for gi, w in enumerate(POOL_WINDOWS):
        sl = slice(gi * POOL_GW, (gi + 1) * POOL_GW)
        tok = buf[prev, :, POOL_HALO:POOL_HALO + tt, sl]
        acc = tok
        for k in range(1, w):
            acc = acc + buf[prev, :, POOL_HALO - k:POOL_HALO - k + tt, sl]
        cnt = jnp.minimum(w, pos + 1).astype(F32)
        pooled = (acc / cnt - tok).reshape(bb * tt, POOL_GW)
        outs.append(_dot(pooled.astype(BF16), wp_ref[gi]))
    po = jnp.concatenate(outs, axis=-1) * ps_ref[...]
    po_ref[...] = po.reshape(bb, tt, POOL_W).astype(po_ref.dtype)


def _in_proj_pool(x, ctx16, ln1_g, w_in_bf, w_pool_bf, pool_scale, pos0, bb, tt):
    b, t, _ = x.shape
    nt = t // tt
    cst = lambda shape: pl.BlockSpec(shape, lambda i, j: (0,) * len(shape))
    proj = lambda w: pl.BlockSpec((bb, tt, w), lambda i, j: (i, jnp.minimum(j, nt - 1), 0))
    mixed = pl.BlockSpec((bb, tt, POOL_W), lambda i, j: (i, jnp.maximum(j - 1, 0), 0))
    halo = pl.BlockSpec((bb, POOL_HALO, POOL_W), lambda i, j: (i, 0, 0))
    est = (2 * bb * tt * (D_MODEL + RWKV_PROJ) * 4 + 2 * D_MODEL * IN_PROJ * 2 + 3 * bb * tt * IN_PROJ * 4
           + 2 * bb * (tt + POOL_HALO) * POOL_W * 4)
    return pl.pallas_call(
        functools.partial(_in_proj_pool_kernel, pos0=pos0),
        grid=(b // bb, nt + 1),
        in_specs=[proj(D_MODEL), halo, cst((1, D_MODEL)), cst((D_MODEL, IN_PROJ)),
                  cst((len(POOL_WINDOWS), POOL_GW, POOL_GW)), cst((1, POOL_W))],
        out_specs=[proj(RWKV_PROJ), mixed, halo],
        out_shape=[jax.ShapeDtypeStruct((b, t, RWKV_PROJ), F32), jax.ShapeDtypeStruct((b, t, POOL_W), BF16),
                   jax.ShapeDtypeStruct((b, POOL_HALO, POOL_W), F32)],
        scratch_shapes=[pltpu.VMEM((2, bb, tt + POOL_HALO, POOL_W), F32)],
        compiler_params=_params(2, est),
        name="in_proj_pool",
    )(x, ctx16, ln1_g.reshape(1, D_MODEL), w_in_bf, w_pool_bf, pool_scale.reshape(1, POOL_W))


def _rwkv_prepare(zr, prev, mu, w0, w2, a0, a2, g2, k_k, k_a, r_k, ones_bd):
    zs = zr + (prev - zr) * mu
    o1, o2, o3 = RWKV_W, 2 * RWKV_W, 3 * RWKV_W
    o4, o5 = o3 + DECAY_LORA, o3 + DECAY_LORA + ICLR_LORA
    r, k, v = zs[:, :o1], zs[:, o1:o2], zs[:, o2:o3]
    wd, ad, gd = zs[:, o3:o4], zs[:, o4:o5], zs[:, o5:]
    wz = w0 + _dot(jnp.tanh(wd).astype(BF16), w2)
    lw = -EXP_M05 * jax.nn.sigmoid(wz)
    a = jax.nn.sigmoid(a0 + _dot(ad.astype(BF16), a2))
    g = _dot(jax.nn.sigmoid(gd).astype(BF16), g2)
    kk = k * k_k
    kk = kk / jnp.maximum(jnp.sqrt(_group_sum(kk * kk, ones_bd)), 1e-12)
    kx = k * (1.0 + (a - 1.0) * k_a)
    bonus = _group_sum(r * kx * r_k, ones_bd) * v
    return r, lw, kx, v, -kk, kk * a, g, bonus


def _pair_blockdiag(z, left):
    return jnp.concatenate([jnp.where(left, z, 0.0), jnp.where(left, 0.0, z)], axis=0)


def _wkv_kernel(zr_ref, halo_ref, sh_ref, mu_ref, w0_ref, w2_ref, a0_ref, a2_ref, g2_ref, kk_ref, ka_ref, rk_ref,
                ones_ref, lg_ref, lb_ref, s0_ref, o_ref, sf_ref,
                buf, r_ref, lw_ref, kx_ref, v_ref, na_ref, kb_ref, g_ref, bonus_ref, y_ref, s_scr, *, t_valid):
    t = pl.program_id(1)
    tt = zr_ref.shape[1]
    c = CHUNK
    rows_per_pass = min(tt, PREP_ROWS)

    @pl.when(t == 0)
    def _():
        s_scr[...] = s0_ref[0]

    buf[0:SUBLANES, :] = jnp.where(t == 0, sh_ref[0], halo_ref[0])
    buf[SUBLANES:, :] = zr_ref[0]
    ones_bd = ones_ref[...]
    for r0 in range(0, tt, rows_per_pass):
        rows = slice(r0, r0 + rows_per_pass)
        vals = _rwkv_prepare(zr_ref[0, rows, :], buf[SUBLANES - 1 + r0:SUBLANES - 1 + r0 + rows_per_pass, :],
                             mu_ref[...], w0_ref[...], w2_ref[...], a0_ref[...], a2_ref[...], g2_ref[...],
                             kk_ref[...], ka_ref[...], rk_ref[...], ones_bd)
        if t_valid is not None:
            live = lax.broadcasted_iota(I32, (rows_per_pass, 1), 0) + (t * tt + r0) < t_valid
            vals = tuple(jnp.where(live, x, 0.0) for x in vals)
        for ref, x in zip((r_ref, lw_ref, kx_ref, v_ref, na_ref, kb_ref, g_ref, bonus_ref), vals):
            ref[rows, :] = x

    lane = lax.broadcasted_iota(I32, (c, 2 * c), 1)
    rowi = lax.broadcasted_iota(I32, (c, 2 * c), 0)
    jm = lane & (c - 1)
    strict = jm < rowi
    incl = jm <= rowi
    eye_pair = jnp.where(jm == rowi, 1.0, 0.0).astype(F32)
    left = lax.broadcasted_iota(I32, (c, PAIR_W), 1) < HEAD_DIM
    rr = lax.broadcasted_iota(I32, (c, c), 0)
    cc = lax.broadcasted_iota(I32, (c, c), 1)
    tri_incl_bf = jnp.where(cc <= rr, 1.0, 0.0).astype(BF16)
    br = lax.broadcasted_iota(I32, (PAIR_W, PAIR_W), 0) < HEAD_DIM
    bc = lax.broadcasted_iota(I32, (PAIR_W, PAIR_W), 1) < HEAD_DIM
    bd_mask = br == bc
    bf = lambda x: x.astype(BF16)
    bd = lambda z: _pair_blockdiag(z, left)
    n_chunks = tt // c

    cums = []
    for ci in range(n_chunks):
        h3 = _split3(lw_ref[ci * c:(ci + 1) * c, :])
        cums.append(_dot(tri_incl_bf, h3[0]) + _dot(tri_incl_bf, h3[1]) + _dot(tri_incl_bf, h3[2]))

    chains = []
    for ci in range(n_chunks):
        rows = slice(ci * c, (ci + 1) * c)
        for p in range(N_PAIRS):
            cols = slice(p * PAIR_W, (p + 1) * PAIR_W)
            cum = cums[ci][:, cols]
            lw, kx, kb = lw_ref[rows, cols], kx_ref[rows, cols], kb_ref[rows, cols]
            tot = cum[c - 1:c, :]
            p_end = jnp.exp(tot - cum)
            inv_p = jnp.exp(-cum)
            chains.append(dict(
                ci=ci, p=p, v=v_ref[rows, cols], p_c=jnp.exp(tot),
                a_t=na_ref[rows, cols] * jnp.exp(cum - lw), r_t=r_ref[rows, cols] * jnp.exp(cum),
                b_t=kb * inv_p, k_t=kx * inv_p, b_h=kb * p_end, k_h=kx * p_end))

    for d in chains:
        lhs = bf(jnp.concatenate([d["a_t"], d["r_t"]], axis=0))
        rhs = bf(jnp.concatenate([bd(d["b_t"]), bd(d["k_t"])], axis=0))
        sc = lax.dot_general(lhs, rhs, NT_DIMS, preferred_element_type=F32)
        d["a_ab"] = jnp.where(strict, sc[:c, :2 * c], 0.0)
        d["a_k"] = jnp.concatenate([jnp.where(strict, sc[:c, 2 * c:], 0.0),
                                    jnp.where(incl, sc[c:, 2 * c:], 0.0)], axis=0)
        d["a_rb"] = jnp.where(incl, sc[c:, :2 * c], 0.0)

    for d in chains:
        d["x"] = _dot(bf(d["a_ab"]), bf(bd(d["a_ab"])))
        d["t"] = eye_pair + d["a_ab"]
    for _ in range(c.bit_length() - 3):
        for d in chains:
            st = _dot(bf(jnp.concatenate([d["x"], d["t"]], axis=0)), bf(bd(d["x"])))
            d["x"] = st[:c]
            d["t"] = d["t"] + st[c:]
    for d in chains:
        d["t"] = d["t"] + _dot(bf(d["t"]), bf(bd(d["x"])))

    for d in chains:
        wv = _dot(bf(d["a_k"]), bf(bd(d["v"])))
        d["w1"], d["rkv"] = wv[:c], wv[c:]
    for d in chains:
        tu = _dot(bf(d["t"]), bf(jnp.concatenate([bd(d["w1"]), bd(d["a_t"])], axis=1)))
        d["u_loc"], d["a_tt"] = tu[:, :PAIR_W], tu[:, PAIR_W:]
    for d in chains:
        ar = _dot(bf(d["a_rb"]), bf(jnp.concatenate([bd(d["u_loc"]), bd(d["a_tt"])], axis=1)))
        d["y_loc"] = ar[:, :PAIR_W] + d["rkv"]
        d["r_g"] = d["r_t"] + ar[:, PAIR_W:]
    for d in chains:
        lhs = jnp.concatenate([jnp.concatenate([d["v"], jnp.zeros_like(d["v"])], axis=1),
                               jnp.concatenate([d["u_loc"], d["a_tt"]], axis=1)], axis=0)
        rhs = jnp.concatenate([d["k_h"], d["b_h"]], axis=0)
        dp = lax.dot_general(bf(lhs), bf(rhs), TN_DIMS, preferred_element_type=F32)
        d["d_loc"] = jnp.where(bd_mask, dp[:PAIR_W], 0.0)
        d["phi"] = bf(jnp.where(bd_mask, dp[PAIR_W:], 0.0))

    state = [s_scr[p] for p in range(N_PAIRS)]
    for d in chains:
        p = d["p"]
        s_hi, s_lo = _split2(state[p])
        y = lax.dot_general(bf(d["r_g"]), s_hi, NT_DIMS, preferred_element_type=F32) + d["y_loc"]
        y_ref[d["ci"] * c:(d["ci"] + 1) * c, p * PAIR_W:(p + 1) * PAIR_W] = y
        state[p] = state[p] * d["p_c"] + (_dot(s_hi, d["phi"]) + _dot(s_lo, d["phi"])) + d["d_loc"]
    for p in range(N_PAIRS):
        s_scr[p] = state[p]

    @pl.when(t == pl.num_programs(1) - 1)
    def _():
        sf_ref[0] = s_scr[...]

    inv_n = 1.0 / HEAD_DIM
    for r0 in range(0, tt, rows_per_pass):
        rows = slice(r0, r0 + rows_per_pass)
        y = y_ref[rows, :]
        mu = _group_sum(y, ones_bd) * inv_n
        dlt = y - mu
        var = _group_sum(dlt * dlt, ones_bd) * inv_n
        yn = dlt * lax.rsqrt(var + LNX_EPS) * lg_ref[...] + lb_ref[...]
        o_ref[0, rows, :] = ((yn + bonus_ref[rows, :]) * g_ref[rows, :]).astype(o_ref.dtype)


def _rwkv_mixer(zr, sh8, s0_pair, wts, tt, t_valid):
    b, t, _ = zr.shape
    assert CHUNK == HEAD_DIM and tt % CHUNK == 0
    hb = tt // SUBLANES
    row = lambda x: x.reshape(1, -1)
    cst = lambda shape: pl.BlockSpec(shape, lambda i, j: (0,) * len(shape))
    seq = pl.BlockSpec((1, tt, RWKV_W), lambda i, j: (i, j, 0))
    st = pl.BlockSpec((1, N_PAIRS, PAIR_W, PAIR_W), lambda i, j: (i, 0, 0, 0))
    est = 3 * tt * RWKV_PROJ * 4 + 10 * tt * RWKV_W * 4 + 5 * N_PAIRS * PAIR_W * PAIR_W * 4 + (8 << 20)
    return pl.pallas_call(
        functools.partial(_wkv_kernel, t_valid=t_valid),
        grid=(b, t // tt),
        in_specs=[pl.BlockSpec((1, tt, RWKV_PROJ), lambda i, j: (i, j, 0)),
                  pl.BlockSpec((1, SUBLANES, RWKV_PROJ), lambda i, j: (i, jnp.maximum(j * hb - 1, 0), 0)),
                  pl.BlockSpec((1, SUBLANES, RWKV_PROJ), lambda i, j: (i, 0, 0)),
                  cst((1, RWKV_PROJ)), cst((1, RWKV_W)), cst((DECAY_LORA, RWKV_W)),
                  cst((1, RWKV_W)), cst((ICLR_LORA, RWKV_W)), cst((GATE_LORA, RWKV_W)),
                  cst((1, RWKV_W)), cst((1, RWKV_W)), cst((1, RWKV_W)), cst((GROUP_TILE, GROUP_TILE)),
                  cst((1, RWKV_W)), cst((1, RWKV_W)), st],
        out_specs=[seq, st],
        out_shape=[jax.ShapeDtypeStruct((b, t, RWKV_W), BF16),
                   jax.ShapeDtypeStruct((b, N_PAIRS, PAIR_W, PAIR_W), F32)],
        scratch_shapes=([pltpu.VMEM((tt + SUBLANES, RWKV_PROJ), F32)] + [pltpu.VMEM((tt, RWKV_W), F32)] * 9
                        + [pltpu.VMEM((N_PAIRS, PAIR_W, PAIR_W), F32)]),
        compiler_params=_params(2, est),
        name="rwkv_mixer",
    )(zr, zr, sh8, row(wts["mu_shift"]), row(wts["w0"]), wts["w2_bf"], row(wts["a0"]), wts["a2_bf"], wts["g2_bf"],
      row(wts["k_k"]), row(wts["k_a"]), row(wts["r_k"]), wts["ones_bd"], row(wts["lnx_g"]), row(wts["lnx_b"]),
      s0_pair)


def _state_to_pairs(s):
    b = s.shape[0]
    s = s.astype(F32).reshape(b, N_PAIRS, 2, HEAD_DIM, HEAD_DIM)
    z = jnp.zeros((b, N_PAIRS, HEAD_DIM, HEAD_DIM), F32)
    return jnp.concatenate([jnp.concatenate([s[:, :, 0], z], axis=-1),
                            jnp.concatenate([z, s[:, :, 1]], axis=-1)], axis=-2)


def _pairs_to_state(sp):
    b = sp.shape[0]
    s = jnp.stack([sp[:, :, :HEAD_DIM, :HEAD_DIM], sp[:, :, HEAD_DIM:, HEAD_DIM:]], axis=2)
    return s.reshape(b, N_HEADS, HEAD_DIM, HEAD_DIM)


def _mix_out_kernel(rw_ref, po_ref, x_ref, wo_ref, ln2_ref, rw_hi_ref, rw_lo_ref, rb_ref, cin_ref,
                    h_o, xn_o, route_o, gate_o, cnt_o, carry, lbuf):
    i = pl.program_id(0)
    tm = x_ref.shape[0]
    slot = lax.rem(i, 2)

    @pl.when(i == 0)
    def _():
        carry[...] = cin_ref[...]
        lbuf[1] = jnp.zeros(lbuf.shape[1:], F32)

    logits = lbuf[1 - slot]
    mix = _dot(po_ref[...], wo_ref[:POOL_W, :]) + _dot(rw_ref[...], wo_ref[POOL_W:, :])
    h = x_ref[...] + mix
    h_o[...] = h
    xn = _rmsnorm_rows(h, ln2_ref[...])
    _store_row_tiles(xn_o, xn)
    x_hi, x_lo = _split2(xn)
    lbuf[slot] = (_dot(x_hi, rw_hi_ref[...]) + _dot(x_hi, rw_lo_ref[...]) + _dot(x_lo, rw_hi_ref[...])
                  + rb_ref[...])

    lane = lax.broadcasted_iota(I32, (tm, ROUTE_LANES), 1)
    lane_f = lane.astype(F32)
    vals, idxs, hots = [], [], []
    work = logits
    for _ in range(TOP_K):
        m = jnp.max(work, axis=-1, keepdims=True)
        idx = jnp.min(jnp.where(work == m, lane_f, float(ROUTE_LANES)), axis=-1, keepdims=True)
        hit = lane_f == idx
        vals.append(m)
        idxs.append(idx)
        hots.append(jnp.where(hit, 1.0, 0.0).astype(F32))
        work = jnp.where(hit, -jnp.inf, work)
    exps = [jnp.exp(vv - vals[0]) for vv in vals]
    den = exps[0] + exps[1] + exps[2] + exps[3]

    hot_all = hots[0] + hots[1] + hots[2] + hots[3]
    rr = lax.broadcasted_iota(I32, (tm, tm), 0)
    cc = lax.broadcasted_iota(I32, (tm, tm), 1)
    tri_strict = jnp.where(cc < rr, 1.0, 0.0).astype(BF16)
    prefix = _dot(tri_strict, hot_all.astype(BF16)) + carry[0:1, :]
    route = jnp.zeros((tm, ROUTE_LANES), F32)
    gates = jnp.zeros((tm, ROUTE_LANES), F32)
    for j in range(TOP_K):
        rank = jnp.sum(hots[j] * prefix, axis=-1, keepdims=True)
        route = jnp.where(lane == j, idxs[j], route)
        route = jnp.where(lane == TOP_K + j, rank * SUBLANES, route)
        gates = jnp.where(lane == j, exps[j] / den, gates)
    route_o[...] = route.astype(I32)
    gate_o[...] = gates
    counted = jnp.where(i >= 1, jnp.sum(hot_all, axis=0, keepdims=True), 0.0)
    new_carry = carry[...] + counted
    carry[...] = new_carry
    cnt_o[...] = new_carry


def _mix_out(rw2d, po2d, x2d, w_out_bf, ln2_g, rw_hi, rw_lo, rb_pad, carry_in, tm):
    n = x2d.shape[0]
    nb = n // tm
    row = lambda x: x.reshape(1, -1)
    cst = lambda shape: pl.BlockSpec(shape, lambda i: (0,) * len(shape))
    cur = lambda i: jnp.minimum(i, nb - 1)
    done = lambda i: jnp.maximum(i - 1, 0)
    half = pl.BlockSpec((tm, RWKV_W), lambda i: (cur(i), 0))
    full = pl.BlockSpec((tm, D_MODEL), lambda i: (cur(i), 0))
    lanes = pl.BlockSpec((tm, ROUTE_LANES), lambda i: (done(i), 0))
    est = (2 * (2 * tm * RWKV_W * 2 + 3 * tm * D_MODEL * 4) + 2 * D_MODEL * D_MODEL * 2
           + 8 * tm * D_MODEL * 4 + 4 * tm * tm)
    return pl.pallas_call(
        _mix_out_kernel,
        grid=(nb + 1,),
        in_specs=[half, half, full, cst((D_MODEL, D_MODEL)),
                  cst((1, D_MODEL)), cst((D_MODEL, ROUTE_LANES)), cst((D_MODEL, ROUTE_LANES)),
                  cst((1, ROUTE_LANES)), cst((SUBLANES, ROUTE_LANES))],
        out_specs=[full, pl.BlockSpec((tm * SUBLANES, LANES), lambda i: (cur(i), 0)), lanes, lanes,
                   cst((SUBLANES, ROUTE_LANES))],
        out_shape=[jax.ShapeDtypeStruct((n, D_MODEL), F32), jax.ShapeDtypeStruct((n * SUBLANES, LANES), F32),
                   jax.ShapeDtypeStruct((n, ROUTE_LANES), I32), jax.ShapeDtypeStruct((n, ROUTE_LANES), F32),
                   jax.ShapeDtypeStruct((SUBLANES, ROUTE_LANES), F32)],
        scratch_shapes=[pltpu.VMEM((SUBLANES, ROUTE_LANES), F32), pltpu.VMEM((2, tm, ROUTE_LANES), F32)],
        compiler_params=_params(1, est),
        name="mix_out_router",
    )(rw2d, po2d, x2d, w_out_bf, row(ln2_g), rw_hi, rw_lo, rb_pad, carry_in)


def _tiles(ref, first_row, n_slots):
    if not isinstance(first_row, int):
        first_row = pl.multiple_of(first_row, SUBLANES)
    return ref.at[pl.ds(first_row, n_slots * SUBLANES), :]


def _dispatch_kernel(pad_start_ref, pad_len_ref, tail_ref, dest_a_ref, dest_b_ref, xa_ref, xb_ref, o_hbm, zbuf, sem, sem_fill,
                     *, n_main, n_last):
    i = pl.program_id(0)
    tm = xa_ref.shape[0] // SUBLANES
    blk = zbuf.shape[0] // SUBLANES

    @pl.when(i == 0)
    def _():
        zbuf[...] = jnp.zeros_like(zbuf)

        def per_expert(e, totals):
            start, n = pad_start_ref[e], pad_len_ref[e]
            n_groups = lax.shift_right_logical(n, 3)
            n_single = n - n_groups * SUBLANES

            def group(g, carry):
                pltpu.make_async_copy(_tiles(zbuf, 0, SUBLANES),
                                      _tiles(o_hbm, (start + g * SUBLANES) * SUBLANES, SUBLANES), sem_fill).start()
                return carry

            def single(r, carry):
                pltpu.make_async_copy(_tiles(zbuf, 0, 1),
                                      _tiles(o_hbm, (start + n_groups * SUBLANES + r) * SUBLANES, 1), sem_fill).start()
                return carry

            lax.fori_loop(0, n_groups, group, 0)
            lax.fori_loop(0, n_single, single, 0)
            return totals[0] + n_groups, totals[1] + n_single

        n_groups_filled, n_single_filled = lax.fori_loop(0, N_EXPERTS, per_expert, (0, 0))

        def wait_group(g, carry):
            pltpu.make_async_copy(_tiles(zbuf, 0, SUBLANES), _tiles(o_hbm, 0, SUBLANES), sem_fill).wait()
            return carry

        def wait_single(r, carry):
            pltpu.make_async_copy(_tiles(zbuf, 0, 1), _tiles(o_hbm, 0, 1), sem_fill).wait()
            return carry

        lax.fori_loop(0, n_groups_filled, wait_group, 0)
        lax.fori_loop(0, n_single_filled, wait_single, 0)

        def fill_block(b, carry):
            pltpu.make_async_copy(zbuf, _tiles(o_hbm, (tail_ref[0] + b * blk) * SUBLANES, blk), sem_fill).start()
            return carry

        def wait_block(b, carry):
            pltpu.make_async_copy(zbuf, _tiles(o_hbm, 0, blk), sem_fill).wait()
            return carry

        lax.fori_loop(0, tail_ref[1], fill_block, 0)
        lax.fori_loop(0, tail_ref[1], wait_block, 0)

    def scatter_rows(d_ref, x_ref, n_rows):
        def issue(n, carry):
            src = _tiles(x_ref, n * SUBLANES, 1)
            for j in range(TOP_K):
                pltpu.make_async_copy(src, _tiles(o_hbm, d_ref[n * TOP_K + j], 1), sem).start()
            return carry

        lax.fori_loop(0, n_rows, issue, 0, unroll=ISSUE_UNROLL)
        pltpu.make_async_copy(_tiles(o_hbm, 0, n_rows * TOP_K), _tiles(o_hbm, 0, n_rows * TOP_K), sem).wait()

    @pl.when(i < n_main)
    def _():
        scatter_rows(dest_a_ref, xa_ref, tm)

    @pl.when(i == n_main)
    def _():
        scatter_rows(dest_b_ref, xb_ref, n_last)


def _dispatch(pad_start, pad_len, tail, dest_a, dest_b, xn_a, xn_b, n_slots, tm):
    n_main = xn_a.shape[0] // (tm * SUBLANES)
    n_last = xn_b.shape[0] // SUBLANES
    assert n_slots >= max(tm, n_last) * TOP_K
    grid_spec = pltpu.PrefetchScalarGridSpec(
        num_scalar_prefetch=3,
        grid=(n_main + 1,),
        in_specs=[pl.BlockSpec((tm * TOP_K,), lambda i, *_: (jnp.minimum(i, n_main - 1),),
                               memory_space=pltpu.SMEM),
                  pl.BlockSpec((n_last * TOP_K,), lambda i, *_: (0,), memory_space=pltpu.SMEM),
                  pl.BlockSpec((tm * SUBLANES, LANES), lambda i, *_: (jnp.minimum(i, n_main - 1), 0)),
                  pl.BlockSpec((n_last * SUBLANES, LANES), lambda i, *_: (0, 0))],
        out_specs=pl.BlockSpec(memory_space=pl.ANY),
        scratch_shapes=[pltpu.VMEM((EXPERT_BLOCK * SUBLANES, LANES), F32), pltpu.SemaphoreType.DMA(()),
                        pltpu.SemaphoreType.DMA(())],
    )
    return pl.pallas_call(
        functools.partial(_dispatch_kernel, n_main=n_main, n_last=n_last),
        grid_spec=grid_spec,
        out_shape=jax.ShapeDtypeStruct((n_slots * SUBLANES, LANES), F32),
        compiler_params=_params(1, 2 * (tm + n_last) * D_MODEL * 4 + EXPERT_BLOCK * D_MODEL * 4),
        name="dispatch",
    )(pad_start, pad_len, tail, dest_a, dest_b, xn_a, xn_b)


def _expert_kernel(be_ref, nu_ref, xs_ref, wgu_ref, bgu_ref, wdn_ref, bdn_ref, o_ref, wgu_bf, wdn_bf):
    i = pl.program_id(0)
    used = i < nu_ref[0]
    changed = jnp.logical_or(i == 0, be_ref[i] != be_ref[jnp.maximum(i - 1, 0)])

    @pl.when(jnp.logical_and(used, changed))
    def _():
        wgu_bf[...] = wgu_ref[0].astype(BF16)
        wdn_bf[...] = wdn_ref[0].astype(BF16)

    @pl.when(used)
    def _():
        x = _load_row_tiles(xs_ref, xs_ref.shape[0] // SUBLANES)
        e = be_ref[i]
        gu = _dot(x.astype(BF16), wgu_bf[...]) + bgu_ref[e]
        gate = jnp.minimum(gu[:, :D_EXPERT], SWIGLU_LIMIT)
        up = jnp.clip(gu[:, D_EXPERT:], -SWIGLU_LIMIT, SWIGLU_LIMIT)
        hmid = (up + 1.0) * (gate * jax.nn.sigmoid(SWIGLU_ALPHA * gate))
        _store_row_tiles(o_ref, _dot(hmid.astype(BF16), wdn_bf[...]) + bdn_ref[e])

    @pl.when(jnp.logical_not(used))
    def _():
        o_ref[...] = jnp.zeros_like(o_ref)


def _experts(block_e, n_used, xs, w_gu, b_gu, w_down, b_down):
    n_slots = xs.shape[0] // SUBLANES
    nb = n_slots // EXPERT_BLOCK
    blk = EXPERT_BLOCK
    slots = pl.BlockSpec((blk * SUBLANES, LANES), lambda i, be, nu: (i, 0))
    est = (2 * 2 * blk * D_MODEL * 4 + 2 * (D_MODEL * 2 * D_EXPERT + D_EXPERT * D_MODEL) * 4
           + (D_MODEL * 2 * D_EXPERT + D_EXPERT * D_MODEL) * 2 + 4 * blk * 2 * D_EXPERT * 4)
    grid_spec = pltpu.PrefetchScalarGridSpec(
        num_scalar_prefetch=2,
        grid=(nb,),
        in_specs=[slots,
                  pl.BlockSpec((1, D_MODEL, 2 * D_EXPERT), lambda i, be, nu: (be[i], 0, 0)),
                  pl.BlockSpec((N_EXPERTS, 1, 2 * D_EXPERT), lambda i, be, nu: (0, 0, 0)),
                  pl.BlockSpec((1, D_EXPERT, D_MODEL), lambda i, be, nu: (be[i], 0, 0)),
                  pl.BlockSpec((N_EXPERTS, 1, D_MODEL), lambda i, be, nu: (0, 0, 0))],
        out_specs=slots,
        scratch_shapes=[pltpu.VMEM((D_MODEL, 2 * D_EXPERT), BF16), pltpu.VMEM((D_EXPERT, D_MODEL), BF16)],
    )
    return pl.pallas_call(
        _expert_kernel,
        grid_spec=grid_spec,
        out_shape=jax.ShapeDtypeStruct((n_slots * SUBLANES, LANES), F32),
        compiler_params=_params(1, est),
        name="experts",
    )(block_e, n_used, xs, w_gu, b_gu.reshape(N_EXPERTS, 1, 2 * D_EXPERT), w_down,
      b_down.reshape(N_EXPERTS, 1, D_MODEL))


def _combine_kernel(dest_ref, dest_next_ref, ys_hbm, gate_ref, h_ref, p_ref, ln3_ref, pg_ref, pp_ref, fg_ref,
                    o_ref, gbuf, hbuf, sem):
    i = pl.program_id(0)
    tm = h_ref.shape[0]
    slot = lax.rem(i, 2)

    def issue(d_ref, s):
        def body(n, carry):
            for j in range(TOP_K):
                pltpu.make_async_copy(_tiles(ys_hbm, d_ref[n * TOP_K + j], 1),
                                      _tiles(gbuf.at[s, j], n * SUBLANES, 1), sem.at[s]).start(priority=j % 2)
            return carry
        lax.fori_loop(0, tm, body, 0, unroll=ISSUE_UNROLL)

    other = 1 - slot
    n_tiles = pl.num_programs(0) - 1

    @pl.when(i == 0)
    def _():
        issue(dest_ref, 0)
        hbuf[1] = jnp.zeros(hbuf.shape[1:], F32)
        gbuf[1] = jnp.zeros(gbuf.shape[1:], F32)

    @pl.when(i + 1 < n_tiles)
    def _():
        issue(dest_next_ref, other)

    @pl.when(i < n_tiles)
    def _():
        for j in range(TOP_K):
            pltpu.make_async_copy(gbuf.at[slot, j], gbuf.at[slot, j], sem.at[slot]).wait()

    h = hbuf[other]
    gate = jax.nn.sigmoid(_dot(_rmsnorm_rows(h, ln3_ref[...]).astype(BF16), pg_ref[...]))
    h = h + gate * _dot(p_ref[...].astype(BF16), pp_ref[...])
    o_ref[...] = _rmsnorm_rows(h, fg_ref[...])

    gates = gate_ref[...]
    slabs = []
    for c in range(SUBLANES):
        acc = gates[:, 0:1] * gbuf[slot, 0, pl.ds(c, tm, stride=SUBLANES), :]
        for j in range(1, TOP_K):
            acc = acc + gates[:, j:j + 1] * gbuf[slot, j, pl.ds(c, tm, stride=SUBLANES), :]
        slabs.append(acc)
    hbuf[slot] = h_ref[...] + jnp.concatenate(slabs, axis=1)


def _combine(dest_flat, ys, gates, h2d, p2d, ln3_g, ple_gate_bf, ple_proj_bf, final_g, tm):
    n = h2d.shape[0]
    nb = n // tm
    row = lambda x: x.reshape(1, -1)
    cst = lambda shape: pl.BlockSpec(shape, lambda i: (0,) * len(shape))
    cur = lambda i: jnp.minimum(i, nb - 1)
    done = lambda i: jnp.maximum(i - 1, 0)
    est = (2 * TOP_K * tm * D_MODEL * 4 + 2 * (2 * tm * D_MODEL * 4 + tm * PLE_DIM * 4 + tm * ROUTE_LANES * 4)
           + 2 * (D_MODEL * D_MODEL * 2 + PLE_DIM * D_MODEL * 2) + 8 * tm * D_MODEL * 4)
    return pl.pallas_call(
        _combine_kernel,
        grid=(nb + 1,),
        in_specs=[pl.BlockSpec((tm * TOP_K,), lambda i: (cur(i),), memory_space=pltpu.SMEM),
                  pl.BlockSpec((tm * TOP_K,), lambda i: (cur(i + 1),), memory_space=pltpu.SMEM),
                  pl.BlockSpec(memory_space=pl.ANY),
                  pl.BlockSpec((tm, ROUTE_LANES), lambda i: (cur(i), 0)),
                  pl.BlockSpec((tm, D_MODEL), lambda i: (cur(i), 0)),
                  pl.BlockSpec((tm, PLE_DIM), lambda i: (done(i), 0)),
                  cst((1, D_MODEL)), cst((D_MODEL, D_MODEL)), cst((PLE_DIM, D_MODEL)), cst((1, D_MODEL))],
        out_specs=pl.BlockSpec((tm, D_MODEL), lambda i: (done(i), 0)),
        out_shape=jax.ShapeDtypeStruct((n, D_MODEL), F32),
        scratch_shapes=[pltpu.VMEM((2, TOP_K, tm * SUBLANES, LANES), F32), pltpu.VMEM((2, tm, D_MODEL), F32),
                        pltpu.SemaphoreType.DMA((2,))],
        compiler_params=_params(1, est),
        name="combine_ple_norm",
    )(dest_flat, dest_flat, ys, gates, h2d, p2d, row(ln3_g), ple_gate_bf, ple_proj_bf, row(final_g))


def _tile(n, pref):
    t = min(n, pref)
    assert n % t == 0, (n, pref)
    return t


def _front(x, pos0, pool_ctx, shift_prev, wkv0, wts, carry_in, tm_mix):
    b, t, _ = x.shape
    n = b * t
    assert t >= POOL_CTX and t % SUBLANES == 0
    x2d = x.reshape(n, D_MODEL)
    ctx16 = jnp.concatenate([jnp.zeros((b, POOL_HALO - POOL_CTX, POOL_W), F32), pool_ctx.astype(F32)], axis=1)
    tt = _tile(t, 512)
    bb = _tile(b, max(1, 128 // tt))
    zr, pool_out, last = _in_proj_pool(x, ctx16, wts["ln1_g"], wts["w_in_bf"], wts["w_pool_bf"],
                                       wts["pool_scale"], pos0, bb, tt)
    new_pool = last[:, POOL_HALO - POOL_CTX:]

    sh8 = jnp.broadcast_to(shift_prev.astype(F32), (b, SUBLANES, RWKV_PROJ))
    new_shift = zr[:, -1:]

    t_pad = -(-t // CHUNK) * CHUNK
    zr_pad = zr if t_pad == t else jnp.pad(zr, ((0, 0), (0, t_pad - t), (0, 0)))
    rw_out, s_fin = _rwkv_mixer(zr_pad, sh8, _state_to_pairs(wkv0), wts, _tile(t_pad, 16 * CHUNK),
                                None if t_pad == t else t)
    rw_out = rw_out[:, :t]
    new_wkv = _pairs_to_state(s_fin)

    h, xn, route, gates, counts = _mix_out(
        rw_out.reshape(n, RWKV_W), pool_out.reshape(n, POOL_W), x2d, wts["w_out_bf"], wts["ln2_g"],
        wts["rw_hi"], wts["rw_lo"], wts["rb_pad"], carry_in, tm_mix)
    return dict(h=h, xn=xn, route=route, gates=gates, counts=counts,
                new_pool=new_pool, new_shift=new_shift, new_wkv=new_wkv)


def _prep_weights(i, ln1_g, w_in, mu_shift, w_pool, pool_scale, w0, w2, a0, a2, g2, k_k, k_a, r_k,
                  lnx_g, lnx_b, w_out, ln2_g, router_w, router_b, ln3_g, ple_gate, ple_proj):
    head = jnp.arange(GROUP_TILE) // HEAD_DIM
    ones_bd = (head[:, None] == head[None, :]).astype(BF16)
    rw_pad = jnp.pad(router_w[i].astype(F32), ((0, 0), (0, ROUTE_LANES - N_EXPERTS)))
    rw_hi = rw_pad.astype(BF16)
    rw_lo = (rw_pad - rw_hi.astype(F32)).astype(BF16)
    rb_pad = jnp.concatenate([router_b[i].astype(F32), jnp.full((ROUTE_LANES - N_EXPERTS,), -jnp.inf, F32)])
    return dict(
        ln1_g=ln1_g[i], w_in_bf=w_in[i].astype(BF16), mu_shift=mu_shift[i], w_pool_bf=w_pool[i].astype(BF16),
        pool_scale=pool_scale[i], w0=w0[i], w2_bf=w2[i].astype(BF16), a0=a0[i], a2_bf=a2[i].astype(BF16),
        g2_bf=g2[i].astype(BF16), k_k=k_k[i], k_a=k_a[i], r_k=r_k[i].reshape(RWKV_W), lnx_g=lnx_g[i],
        lnx_b=lnx_b[i], w_out_bf=w_out[i].astype(BF16), ln2_g=ln2_g[i], rw_hi=rw_hi, rw_lo=rw_lo,
        rb_pad=rb_pad.reshape(1, ROUTE_LANES), ones_bd=ones_bd, ln3_g=ln3_g[i],
        ple_gate_bf=ple_gate[i].astype(BF16), ple_proj_bf=ple_proj[i].astype(BF16))


def _layer_pair(hp, hs, pp, ps, pool_ctx, shift_prev, wkv0, wts, w_gu, b_gu, w_down, b_down, final_g):
    bp, tp, _ = hp.shape
    bs, ts, _ = hs.shape
    np_, ns = bp * tp, bs * ts
    tm_p = _tile(np_, 256)
    tm_s = _tile(ns, 128)
    zero_carry = jnp.zeros((SUBLANES, ROUTE_LANES), F32)
    fp = _front(hp, 0, jnp.zeros((bp, POOL_CTX, POOL_W), F32), jnp.zeros((bp, 1, RWKV_PROJ), F32),
                jnp.zeros((bp, N_HEADS, HEAD_DIM, HEAD_DIM), F32), wts, zero_carry, _tile(np_, 512))
    fs = _front(hs, PAST_LEN, pool_ctx, shift_prev, wkv0, wts, fp["counts"], tm_s)

    counts = fs["counts"][0, :N_EXPERTS].astype(I32)
    padded = (counts + EXPERT_BLOCK - 1) // EXPERT_BLOCK * EXPERT_BLOCK
    pad_end = jnp.cumsum(padded)
    start_pad = pad_end - padded
    n_blocks = -(-(np_ + ns) * TOP_K // EXPERT_BLOCK) + N_EXPERTS
    n_slots = n_blocks * EXPERT_BLOCK
    n_used = (pad_end[-1] // EXPERT_BLOCK).astype(I32)
    block_start = jnp.arange(n_blocks, dtype=I32) * EXPERT_BLOCK
    block_e = jnp.minimum(jnp.sum((block_start[:, None] >= pad_end[None, :]).astype(I32), axis=1), N_EXPERTS - 1)
    last_e = block_e[jnp.maximum(n_used - 1, 0)]
    block_e = jnp.where(jnp.arange(n_blocks) < n_used, block_e, last_e)

    start_row = start_pad * SUBLANES

    def dest_of(route):
        idx = route[:, :TOP_K]
        rank_row = route[:, TOP_K:2 * TOP_K]
        start = jnp.sum(jnp.where(idx[..., None] == jnp.arange(N_EXPERTS, dtype=I32), start_row, 0), axis=-1)
        return (start + rank_row).reshape(-1).astype(I32)

    dest_p, dest_s = dest_of(fp["route"]), dest_of(fs["route"])
    tail = jnp.stack([n_used * EXPERT_BLOCK, n_blocks - n_used]).astype(I32)
    xs = _dispatch(start_pad + counts, padded - counts, tail, dest_p, dest_s, fp["xn"], fs["xn"], n_slots,
                   _tile(np_, 2048))
    ys = _experts(block_e, n_used.reshape(1), xs, w_gu, b_gu, w_down, b_down)

    outs = []
    for f, dest, p, tm in ((fp, dest_p, pp, tm_p), (fs, dest_s, ps, tm_s)):
        n = f["h"].shape[0]
        outs.append(_combine(dest, ys, f["gates"], f["h"], p.reshape(n, PLE_DIM).astype(F32), wts["ln3_g"],
                             wts["ple_gate_bf"], wts["ple_proj_bf"], final_g, tm))
    return outs[0].reshape(hp.shape), outs[1].reshape(hs.shape), fp, fs


def kernel(x_prompt, x_sample, p_prompt, p_sample, cache_pool, state_shift, state_wkv, ln1_g, w_in, mu_shift, w_pool, pool_scale, w0, w2, a0, a2, g2, k_k, k_a, r_k, lnx_g, lnx_b, w_out, ln2_g, router_w, router_b, w_gu, b_gu, w_down, b_down, ln3_g, ple_gate, ple_proj, final_g):
    assert ln1_g.shape[0] == 1, "single-layer kernel"
    i = 0
    wts = _prep_weights(i, ln1_g, w_in, mu_shift, w_pool, pool_scale, w0, w2, a0, a2, g2, k_k, k_a, r_k,
                        lnx_g, lnx_b, w_out, ln2_g, router_w, router_b, ln3_g, ple_gate, ple_proj)
    y_p, y_s, fp, fs = _layer_pair(x_prompt, x_sample, p_prompt[i], p_sample[i], cache_pool[i], state_shift[i],
                                   state_wkv[i], wts, w_gu[i], b_gu[i], w_down[i], b_down[i], final_g)
    stack = lambda a: a[None]
    return (y_p, y_s,
            stack(fp["new_pool"]), stack(fp["new_shift"]), stack(fp["new_wkv"]),
            stack(fs["new_pool"]), stack(fs["new_shift"]), stack(fs["new_wkv"]))
```

```python
import functools

import jax
import jax.numpy as jnp
from jax import lax
from jax.experimental import pallas as pl
from jax.experimental.pallas import tpu as pltpu

F32 = jnp.float32
BF16 = jnp.bfloat16
I32 = jnp.int32

D_MODEL = 1024
POOL_W = 512
POOL_WINDOWS = (2, 4, 8, 16)
POOL_GW = 128
POOL_CTX = 15
POOL_HALO = 16
RWKV_W = 512
HEAD_DIM = 64
N_HEADS = 8
N_PAIRS = N_HEADS // 2
PAIR_W = 2 * HEAD_DIM
DECAY_LORA = 64
ICLR_LORA = 64
GATE_LORA = 128
RWKV_PROJ = 3 * RWKV_W + DECAY_LORA + ICLR_LORA + GATE_LORA
IN_PROJ = POOL_W + RWKV_PROJ
LNX_EPS = 64e-5
N_EXPERTS = 32
TOP_K = 4
D_EXPERT = 1024
SWIGLU_LIMIT = 7.0
SWIGLU_ALPHA = 1.702
PLE_DIM = 256
RMS_EPS = 1e-6
PAST_LEN = 2048

CHUNK = 64
EXPERT_BLOCK = 512
ROUTE_LANES = 128
SUBLANES = 8
LANES = 128
assert D_MODEL == SUBLANES * LANES
PREP_ROWS = 256
ISSUE_UNROLL = 8
GROUP_TILE = 256
EXP_M05 = 0.6065306597126334
V7X_VMEM_BYTES = 64 * 1024 * 1024

NT_DIMS = (((1,), (1,)), ((), ()))
TN_DIMS = (((0,), (0,)), ((), ()))


def _vmem_limit(est_bytes):
    return int(min(est_bytes * 3 // 2 + (4 << 20), V7X_VMEM_BYTES - (8 << 20)))


def _params(n_axes, est_bytes):
    return pltpu.CompilerParams(dimension_semantics=("arbitrary",) * n_axes,
                                vmem_limit_bytes=_vmem_limit(est_bytes))


def _dot(a, b):
    return jnp.dot(a, b, preferred_element_type=F32)


def _split2(x):
    hi = x.astype(BF16)
    lo = (x - hi.astype(F32)).astype(BF16)
    return hi, lo


def _split3(x):
    hi = x.astype(BF16)
    r1 = x - hi.astype(F32)
    mid = r1.astype(BF16)
    lo = (r1 - mid.astype(F32)).astype(BF16)
    return hi, mid, lo


def _group_sum(x, ones_bd):
    xb = x.astype(BF16)
    slabs = [_dot(xb[:, s:s + GROUP_TILE], ones_bd) for s in range(0, x.shape[1], GROUP_TILE)]
    return jnp.concatenate(slabs, axis=1)


def _store_row_tiles(ref, value):
    n = value.shape[0]
    for c in range(SUBLANES):
        ref[pl.ds(c, n, stride=SUBLANES), :] = value[:, c * LANES:(c + 1) * LANES]


def _load_row_tiles(ref, n):
    return jnp.concatenate([ref[pl.ds(c, n, stride=SUBLANES), :] for c in range(SUBLANES)], axis=1)


def _rmsnorm_rows(x, g):
    ms = jnp.mean(x * x, axis=-1, keepdims=True)
    return (x * lax.rsqrt(ms + RMS_EPS)) * g


def _in_proj_pool_kernel(x_ref, ctx_ref, g_ref, w_ref, wp_ref, ps_ref, zr_ref, po_ref, last_ref, buf, *, pos0):
    j = pl.program_id(1)
    bb, tt, _ = x_ref.shape
    cur = lax.rem(j, 2)
    prev = 1 - cur

    @pl.when(j == 0)
    def _():
        buf[1] = jnp.zeros(buf.shape[1:], F32)

    xn = _rmsnorm_rows(x_ref[...].reshape(bb * tt, D_MODEL), g_ref[...])
    z = _dot(xn.astype(BF16), w_ref[...])
    zr_ref[...] = z[:, POOL_W:].reshape(bb, tt, RWKV_PROJ)
    last = buf[prev, :, tt:tt + POOL_HALO, :]
    last_ref[...] = last
    buf[cur, :, 0:POOL_HALO, :] = jnp.where(j == 0, ctx_ref[...], last)
    buf[cur, :, POOL_HALO:, :] = z[:, :POOL_W].reshape(bb, tt, POOL_W)

    pos = lax.broadcasted_iota(I32, (1, tt, 1), 1) + (jnp.maximum(j - 1, 0) * tt + pos0)
    outs = []
    for gi, w in enumerate(POOL_WINDOWS):
        sl = slice(gi * POOL_GW, (gi + 1) * POOL_GW)
        tok = buf[prev, :, POOL_HALO:POOL_HALO + tt, sl]
        acc = tok
        for k in range(1, w):
            acc = acc + buf[prev, :, POOL_HALO - k:POOL_HALO - k + tt, sl]
        cnt = jnp.minimum(w, pos + 1).astype(F32)
        pooled = (acc / cnt - tok).reshape(bb * tt, POOL_GW)
        outs.append(_dot(pooled.astype(BF16), wp_ref[gi]))
    po = jnp.concatenate(outs, axis=-1) * ps_ref[...]
    po_ref[...] = po.reshape(bb, tt, POOL_W).astype(po_ref.dtype)


def _in_proj_pool(x, ctx16, ln1_g, w_in_bf, w_pool_bf, pool_scale, pos0, bb, tt):
    b, t, _ = x.shape
    nt = t // tt
    cst = lambda shape: pl.BlockSpec(shape, lambda i, j: (0,) * len(shape))
    proj = lambda w: pl.BlockSpec((bb, tt, w), lambda i, j: (i, jnp.minimum(j, nt - 1), 0))
    mixed = pl.BlockSpec((bb, tt, POOL_W), lambda i, j: (i, jnp.maximum(j - 1, 0), 0))
    halo = pl.BlockSpec((bb, POOL_HALO, POOL_W), lambda i, j: (i, 0, 0))
    est = (2 * bb * tt * (D_MODEL + RWKV_PROJ) * 4 + 2 * D_MODEL * IN_PROJ * 2 + 3 * bb * tt * IN_PROJ * 4
           + 2 * bb * (tt + POOL_HALO) * POOL_W * 4)
    return pl.pallas_call(
        functools.partial(_in_proj_pool_kernel, pos0=pos0),
        grid=(b // bb, nt + 1),
        in_specs=[proj(D_MODEL), halo, cst((1, D_MODEL)), cst((D_MODEL, IN_PROJ)),
                  cst((len(POOL_WINDOWS), POOL_GW, POOL_GW)), cst((1, POOL_W))],
        out_specs=[proj(RWKV_PROJ), mixed, halo],
        out_shape=[jax.ShapeDtypeStruct((b, t, RWKV_PROJ), F32), jax.ShapeDtypeStruct((b, t, POOL_W), BF16),
                   jax.ShapeDtypeStruct((b, POOL_HALO, POOL_W), F32)],
        scratch_shapes=[pltpu.VMEM((2, bb, tt + POOL_HALO, POOL_W), F32)],
        compiler_params=_params(2, est),
        name="in_proj_pool",
    )(x, ctx16, ln1_g.reshape(1, D_MODEL), w_in_bf, w_pool_bf, pool_scale.reshape(1, POOL_W))


def _rwkv_prepare(zr, prev, mu, w0, w2, a0, a2, g2, k_k, k_a, r_k, ones_bd):
    zs = zr + (prev - zr) * mu
    o1, o2, o3 = RWKV_W, 2 * RWKV_W, 3 * RWKV_W
    o4, o5 = o3 + DECAY_LORA, o3 + DECAY_LORA + ICLR_LORA
    r, k, v = zs[:, :o1], zs[:, o1:o2], zs[:, o2:o3]
    wd, ad, gd = zs[:, o3:o4], zs[:, o4:o5], zs[:, o5:]
    wz = w0 + _dot(jnp.tanh(wd).astype(BF16), w2)
    lw = -EXP_M05 * jax.nn.sigmoid(wz)
    a = jax.nn.sigmoid(a0 + _dot(ad.astype(BF16), a2))
    g = _dot(jax.nn.sigmoid(gd).astype(BF16), g2)
    kk = k * k_k
    kk = kk / jnp.maximum(jnp.sqrt(_group_sum(kk * kk, ones_bd)), 1e-12)
    kx = k * (1.0 + (a - 1.0) * k_a)
    bonus = _group_sum(r * kx * r_k, ones_bd) * v
    return r, lw, kx, v, -kk, kk * a, g, bonus


def _pair_blockdiag(z, left):
    return jnp.concatenate([jnp.where(left, z, 0.0), jnp.where(left, 0.0, z)], axis=0)


def _wkv_kernel(zr_ref, halo_ref, sh_ref, mu_ref, w0_ref, w2_ref, a0_ref, a2_ref, g2_ref, kk_ref, ka_ref, rk_ref,
                ones_ref, lg_ref, lb_ref, s0_ref, o_ref, sf_ref,
                buf, r_ref, lw_ref, kx_ref, v_ref, na_ref, kb_ref, g_ref, bonus_ref, y_ref, s_scr, *, t_valid):
    t = pl.program_id(1)
    tt = zr_ref.shape[1]
    c = CHUNK
    rows_per_pass = min(tt, PREP_ROWS)

    @pl.when(t == 0)
    def _():
        s_scr[...] = s0_ref[0]

    buf[0:SUBLANES, :] = jnp.where(t == 0, sh_ref[0], halo_ref[0])
    buf[SUBLANES:, :] = zr_ref[0]
    ones_bd = ones_ref[...]
    for r0 in range(0, tt, rows_per_pass):
        rows = slice(r0, r0 + rows_per_pass)
        vals = _rwkv_prepare(zr_ref[0, rows, :], buf[SUBLANES - 1 + r0:SUBLANES - 1 + r0 + rows_per_pass, :],
                             mu_ref[...], w0_ref[...], w2_ref[...], a0_ref[...], a2_ref[...], g2_ref[...],
                             kk_ref[...], ka_ref[...], rk_ref[...], ones_bd)
        if t_valid is not None:
            live = lax.broadcasted_iota(I32, (rows_per_pass, 1), 0) + (t * tt + r0) < t_valid
            vals = tuple(jnp.where(live, x, 0.0) for x in vals)
        for ref, x in zip((r_ref, lw_ref, kx_ref, v_ref, na_ref, kb_ref, g_ref, bonus_ref), vals):
            ref[rows, :] = x

    lane = lax.broadcasted_iota(I32, (c, 2 * c), 1)
    rowi = lax.broadcasted_iota(I32, (c, 2 * c), 0)
    jm = lane & (c - 1)
    strict = jm < rowi
    incl = jm <= rowi
    eye_pair = jnp.where(jm == rowi, 1.0, 0.0).astype(F32)
    left = lax.broadcasted_iota(I32, (c, PAIR_W), 1) < HEAD_DIM
    rr = lax.broadcasted_iota(I32, (c, c), 0)
    cc = lax.broadcasted_iota(I32, (c, c), 1)
    tri_incl_bf = jnp.where(cc <= rr, 1.0, 0.0).astype(BF16)
    br = lax.broadcasted_iota(I32, (PAIR_W, PAIR_W), 0) < HEAD_DIM
    bc = lax.broadcasted_iota(I32, (PAIR_W, PAIR_W), 1) < HEAD_DIM
    bd_mask = br == bc
    bf = lambda x: x.astype(BF16)
    bd = lambda z: _pair_blockdiag(z, left)
    n_chunks = tt // c

    cums = []
    for ci in range(n_chunks):
        h3 = _split3(lw_ref[ci * c:(ci + 1) * c, :])
        cums.append(_dot(tri_incl_bf, h3[0]) + _dot(tri_incl_bf, h3[1]) + _dot(tri_incl_bf, h3[2]))

    chains = []
    for ci in range(n_chunks):
        rows = slice(ci * c, (ci + 1) * c)
        for p in range(N_PAIRS):
            cols = slice(p * PAIR_W, (p + 1) * PAIR_W)
            cum = cums[ci][:, cols]
            lw, kx, kb = lw_ref[rows, cols], kx_ref[rows, cols], kb_ref[rows, cols]
            tot = cum[c - 1:c, :]
            p_end = jnp.exp(tot - cum)
            inv_p = jnp.exp(-cum)
            chains.append(dict(
                ci=ci, p=p, v=v_ref[rows, cols], p_c=jnp.exp(tot),
                a_t=na_ref[rows, cols] * jnp.exp(cum - lw), r_t=r_ref[rows, cols] * jnp.exp(cum),
                b_t=kb * inv_p, k_t=kx * inv_p, b_h=kb * p_end, k_h=kx * p_end))

    for d in chains:
        lhs = bf(jnp.concatenate([d["a_t"], d["r_t"]], axis=0))
        rhs = bf(jnp.concatenate([bd(d["b_t"]), bd(d["k_t"])], axis=0))
        sc = lax.dot_general(lhs, rhs, NT_DIMS, preferred_element_type=F32)
        d["a_ab"] = jnp.where(strict, sc[:c, :2 * c], 0.0)
        d["a_k"] = jnp.concatenate([jnp.where(strict, sc[:c, 2 * c:], 0.0),
                                    jnp.where(incl, sc[c:, 2 * c:], 0.0)], axis=0)
        d["a_rb"] = jnp.where(incl, sc[c:, :2 * c], 0.0)

    for d in chains:
        d["x"] = _dot(bf(d["a_ab"]), bf(bd(d["a_ab"])))
        d["t"] = eye_pair + d["a_ab"]
    for _ in range(c.bit_length() - 3):
        for d in chains:
            st = _dot(bf(jnp.concatenate([d["x"], d["t"]], axis=0)), bf(bd(d["x"])))
            d["x"] = st[:c]
            d["t"] = d["t"] + st[c:]
    for d in chains:
        d["t"] = d["t"] + _dot(bf(d["t"]), bf(bd(d["x"])))

    for d in chains:
        wv = _dot(bf(d["a_k"]), bf(bd(d["v"])))
        d["w1"], d["rkv"] = wv[:c], wv[c:]
    for d in chains:
        tu = _dot(bf(d["t"]), bf(jnp.concatenate([bd(d["w1"]), bd(d["a_t"])], axis=1)))
        d["u_loc"], d["a_tt"] = tu[:, :PAIR_W], tu[:, PAIR_W:]
    for d in chains:
        ar = _dot(bf(d["a_rb"]), bf(jnp.concatenate([bd(d["u_loc"]), bd(d["a_tt"])], axis=1)))
        d["y_loc"] = ar[:, :PAIR_W] + d["rkv"]
        d["r_g"] = d["r_t"] + ar[:, PAIR_W:]
    for d in chains:
        lhs = jnp.concatenate([jnp.concatenate([d["v"], jnp.zeros_like(d["v"])], axis=1),
                               jnp.concatenate([d["u_loc"], d["a_tt"]], axis=1)], axis=0)
        rhs = jnp.concatenate([d["k_h"], d["b_h"]], axis=0)
        dp = lax.dot_general(bf(lhs), bf(rhs), TN_DIMS, preferred_element_type=F32)
        d["d_loc"] = jnp.where(bd_mask, dp[:PAIR_W], 0.0)
        d["phi"] = bf(jnp.where(bd_mask, dp[PAIR_W:], 0.0))

    state = [s_scr[p] for p in range(N_PAIRS)]
    for d in chains:
        p = d["p"]
        s_hi, s_lo = _split2(state[p])
        y = lax.dot_general(bf(d["r_g"]), s_hi, NT_DIMS, preferred_element_type=F32) + d["y_loc"]
        y_ref[d["ci"] * c:(d["ci"] + 1) * c, p * PAIR_W:(p + 1) * PAIR_W] = y
        state[p] = state[p] * d["p_c"] + (_dot(s_hi, d["phi"]) + _dot(s_lo, d["phi"])) + d["d_loc"]
    for p in range(N_PAIRS):
        s_scr[p] = state[p]

    @pl.when(t == pl.num_programs(1) - 1)
    def _():
        sf_ref[0] = s_scr[...]

    inv_n = 1.0 / HEAD_DIM
    for r0 in range(0, tt, rows_per_pass):
        rows = slice(r0, r0 + rows_per_pass)
        y = y_ref[rows, :]
        mu = _group_sum(y, ones_bd) * inv_n
        dlt = y - mu
        var = _group_sum(dlt * dlt, ones_bd) * inv_n
        yn = dlt * lax.rsqrt(var + LNX_EPS) * lg_ref[...] + lb_ref[...]
        o_ref[0, rows, :] = ((yn + bonus_ref[rows, :]) * g_ref[rows, :]).astype(o_ref.dtype)


def _rwkv_mixer(zr, sh8, s0_pair, wts, tt, t_valid):
    b, t, _ = zr.shape
    assert CHUNK == HEAD_DIM and tt % CHUNK == 0
    hb = tt // SUBLANES
    row = lambda x: x.reshape(1, -1)
    cst = lambda shape: pl.BlockSpec(shape, lambda i, j: (0,) * len(shape))
    seq = pl.BlockSpec((1, tt, RWKV_W), lambda i, j: (i, j, 0))
    st = pl.BlockSpec((1, N_PAIRS, PAIR_W, PAIR_W), lambda i, j: (i, 0, 0, 0))
    est = 3 * tt * RWKV_PROJ * 4 + 10 * tt * RWKV_W * 4 + 5 * N_PAIRS * PAIR_W * PAIR_W * 4 + (8 << 20)
    return pl.pallas_call(
        functools.partial(_wkv_kernel, t_valid=t_valid),
        grid=(b, t // tt),
        in_specs=[pl.BlockSpec((1, tt, RWKV_PROJ), lambda i, j: (i, j, 0)),
                  pl.BlockSpec((1, SUBLANES, RWKV_PROJ), lambda i, j: (i, jnp.maximum(j * hb - 1, 0), 0)),
                  pl.BlockSpec((1, SUBLANES, RWKV_PROJ), lambda i, j: (i, 0, 0)),
                  cst((1, RWKV_PROJ)), cst((1, RWKV_W)), cst((DECAY_LORA, RWKV_W)),
                  cst((1, RWKV_W)), cst((ICLR_LORA, RWKV_W)), cst((GATE_LORA, RWKV_W)),
                  cst((1, RWKV_W)), cst((1, RWKV_W)), cst((1, RWKV_W)), cst((GROUP_TILE, GROUP_TILE)),
                  cst((1, RWKV_W)), cst((1, RWKV_W)), st],
        out_specs=[seq, st],
        out_shape=[jax.ShapeDtypeStruct((b, t, RWKV_W), BF16),
                   jax.ShapeDtypeStruct((b, N_PAIRS, PAIR_W, PAIR_W), F32)],
        scratch_shapes=([pltpu.VMEM((tt + SUBLANES, RWKV_PROJ), F32)] + [pltpu.VMEM((tt, RWKV_W), F32)] * 9
                        + [pltpu.VMEM((N_PAIRS, PAIR_W, PAIR_W), F32)]),
        compiler_params=_params(2, est),
        name="rwkv_mixer",
    )(zr, zr, sh8, row(wts["mu_shift"]), row(wts["w0"]), wts["w2_bf"], row(wts["a0"]), wts["a2_bf"], wts["g2_bf"],
      row(wts["k_k"]), row(wts["k_a"]), row(wts["r_k"]), wts["ones_bd"], row(wts["lnx_g"]), row(wts["lnx_b"]),
      s0_pair)


def _state_to_pairs(s):
    b = s.shape[0]
    s = s.astype(F32).reshape(b, N_PAIRS, 2, HEAD_DIM, HEAD_DIM)
    z = jnp.zeros((b, N_PAIRS, HEAD_DIM, HEAD_DIM), F32)
    return jnp.concatenate([jnp.concatenate([s[:, :, 0], z], axis=-1),
                            jnp.concatenate([z, s[:, :, 1]], axis=-1)], axis=-2)


def _pairs_to_state(sp):
    b = sp.shape[0]
    s = jnp.stack([sp[:, :, :HEAD_DIM, :HEAD_DIM], sp[:, :, HEAD_DIM:, HEAD_DIM:]], axis=2)
    return s.reshape(b, N_HEADS, HEAD_DIM, HEAD_DIM)


def _mix_out_kernel(rw_ref, po_ref, x_ref, wo_ref, ln2_ref, rw_hi_ref, rw_lo_ref, rb_ref, cin_ref,
                    h_o, xn_o, route_o, gate_o, cnt_o, carry, lbuf):
    i = pl.program_id(0)
    tm = x_ref.shape[0]
    slot = lax.rem(i, 2)

    @pl.when(i == 0)
    def _():
        carry[...] = cin_ref[...]
        lbuf[1] = jnp.zeros(lbuf.shape[1:], F32)

    logits = lbuf[1 - slot]
    mix = _dot(po_ref[...], wo_ref[:POOL_W, :]) + _dot(rw_ref[...], wo_ref[POOL_W:, :])
    h = x_ref[...] + mix
    h_o[...] = h
    xn = _rmsnorm_rows(h, ln2_ref[...])
    _store_row_tiles(xn_o, xn)
    x_hi, x_lo = _split2(xn)
    lbuf[slot] = (_dot(x_hi, rw_hi_ref[...]) + _dot(x_hi, rw_lo_ref[...]) + _dot(x_lo, rw_hi_ref[...])
                  + rb_ref[...])

    lane = lax.broadcasted_iota(I32, (tm, ROUTE_LANES), 1)
    lane_f = lane.astype(F32)
    vals, idxs, hots = [], [], []
    work = logits
    for _ in range(TOP_K):
        m = jnp.max(work, axis=-1, keepdims=True)
        idx = jnp.min(jnp.where(work == m, lane_f, float(ROUTE_LANES)), axis=-1, keepdims=True)
        hit = lane_f == idx
        vals.append(m)
        idxs.append(idx)
        hots.append(jnp.where(hit, 1.0, 0.0).astype(F32))
        work = jnp.where(hit, -jnp.inf, work)
    exps = [jnp.exp(vv - vals[0]) for vv in vals]
    den = exps[0] + exps[1] + exps[2] + exps[3]

    hot_all = hots[0] + hots[1] + hots[2] + hots[3]
    rr = lax.broadcasted_iota(I32, (tm, tm), 0)
    cc = lax.broadcasted_iota(I32, (tm, tm), 1)
    tri_strict = jnp.where(cc < rr, 1.0, 0.0).astype(BF16)
    prefix = _dot(tri_strict, hot_all.astype(BF16)) + carry[0:1, :]
    route = jnp.zeros((tm, ROUTE_LANES), F32)
    gates = jnp.zeros((tm, ROUTE_LANES), F32)
    for j in range(TOP_K):
        rank = jnp.sum(hots[j] * prefix, axis=-1, keepdims=True)
        route = jnp.where(lane == j, idxs[j], route)
        route = jnp.where(lane == TOP_K + j, rank * SUBLANES, route)
        gates = jnp.where(lane == j, exps[j] / den, gates)
    route_o[...] = route.astype(I32)
    gate_o[...] = gates
    counted = jnp.where(i >= 1, jnp.sum(hot_all, axis=0, keepdims=True), 0.0)
    new_carry = carry[...] + counted
    carry[...] = new_carry
    cnt_o[...] = new_carry


def _mix_out(rw2d, po2d, x2d, w_out_bf, ln2_g, rw_hi, rw_lo, rb_pad, carry_in, tm):
    n = x2d.shape[0]
    nb = n // tm
    row = lambda x: x.reshape(1, -1)
    cst = lambda shape: pl.BlockSpec(shape, lambda i: (0,) * len(shape))
    cur = lambda i: jnp.minimum(i, nb - 1)
    done = lambda i: jnp.maximum(i - 1, 0)
    half = pl.BlockSpec((tm, RWKV_W), lambda i: (cur(i), 0))
    full = pl.BlockSpec((tm, D_MODEL), lambda i: (cur(i), 0))
    lanes = pl.BlockSpec((tm, ROUTE_LANES), lambda i: (done(i), 0))
    est = (2 * (2 * tm * RWKV_W * 2 + 3 * tm * D_MODEL * 4) + 2 * D_MODEL * D_MODEL * 2
           + 8 * tm * D_MODEL * 4 + 4 * tm * tm)
    return pl.pallas_call(
        _mix_out_kernel,
        grid=(nb + 1,),
        in_specs=[half, half, full, cst((D_MODEL, D_MODEL)),
                  cst((1, D_MODEL)), cst((D_MODEL, ROUTE_LANES)), cst((D_MODEL, ROUTE_LANES)),
                  cst((1, ROUTE_LANES)), cst((SUBLANES, ROUTE_LANES))],
        out_specs=[full, pl.BlockSpec((tm * SUBLANES, LANES), lambda i: (cur(i), 0)), lanes, lanes,
                   cst((SUBLANES, ROUTE_LANES))],
        out_shape=[jax.ShapeDtypeStruct((n, D_MODEL), F32), jax.ShapeDtypeStruct((n * SUBLANES, LANES), F32),
                   jax.ShapeDtypeStruct((n, ROUTE_LANES), I32), jax.ShapeDtypeStruct((n, ROUTE_LANES), F32),
                   jax.ShapeDtypeStruct((SUBLANES, ROUTE_LANES), F32)],
        scratch_shapes=[pltpu.VMEM((SUBLANES, ROUTE_LANES), F32), pltpu.VMEM((2, tm, ROUTE_LANES), F32)],
        compiler_params=_params(1, est),
        name="mix_out_router",
    )(rw2d, po2d, x2d, w_out_bf, row(ln2_g), rw_hi, rw_lo, rb_pad, carry_in)


def _tiles(ref, first_row, n_slots):
    if not isinstance(first_row, int):
        first_row = pl.multiple_of(first_row, SUBLANES)
    return ref.at[pl.ds(first_row, n_slots * SUBLANES), :]


def _dispatch_kernel(pad_start_ref, pad_len_ref, tail_ref, dest_a_ref, dest_b_ref, xa_ref, xb_ref, o_hbm, zbuf, sem, sem_fill,
                     *, n_main, n_last):
    i = pl.program_id(0)
    tm = xa_ref.shape[0] // SUBLANES
    blk = zbuf.shape[0] // SUBLANES

    @pl.when(i == 0)
    def _():
        zbuf[...] = jnp.zeros_like(zbuf)

        def per_expert(e, totals):
            start, n = pad_start_ref[e], pad_len_ref[e]
            n_groups = lax.shift_right_logical(n, 3)
            n_single = n - n_groups * SUBLANES

            def group(g, carry):
                pltpu.make_async_copy(_tiles(zbuf, 0, SUBLANES),
                                      _tiles(o_hbm, (start + g * SUBLANES) * SUBLANES, SUBLANES), sem_fill).start()
                return carry

            def single(r, carry):
                pltpu.make_async_copy(_tiles(zbuf, 0, 1),
                                      _tiles(o_hbm, (start + n_groups * SUBLANES + r) * SUBLANES, 1), sem_fill).start()
                return carry

            lax.fori_loop(0, n_groups, group, 0)
            lax.fori_loop(0, n_single, single, 0)
            return totals[0] + n_groups, totals[1] + n_single

        n_groups_filled, n_single_filled = lax.fori_loop(0, N_EXPERTS, per_expert, (0, 0))

        def wait_group(g, carry):
            pltpu.make_async_copy(_tiles(zbuf, 0, SUBLANES), _tiles(o_hbm, 0, SUBLANES), sem_fill).wait()
            return carry

        def wait_single(r, carry):
            pltpu.make_async_copy(_tiles(zbuf, 0, 1), _tiles(o_hbm, 0, 1), sem_fill).wait()
            return carry

        lax.fori_loop(0, n_groups_filled, wait_group, 0)
        lax.fori_loop(0, n_single_filled, wait_single, 0)

        def fill_block(b, carry):
            pltpu.make_async_copy(zbuf, _tiles(o_hbm, (tail_ref[0] + b * blk) * SUBLANES, blk), sem_fill).start()
            return carry

        def wait_block(b, carry):
            pltpu.make_async_copy(zbuf, _tiles(o_hbm, 0, blk), sem_fill).wait()
            return carry

        lax.fori_loop(0, tail_ref[1], fill_block, 0)
        lax.fori_loop(0, tail_ref[1], wait_block, 0)

    def scatter_rows(d_ref, x_ref, n_rows):
        def issue(n, carry):
            src = _tiles(x_ref, n * SUBLANES, 1)
            for j in range(TOP_K):
                pltpu.make_async_copy(src, _tiles(o_hbm, d_ref[n * TOP_K + j], 1), sem).start(priority=j % 2)
            return carry

        lax.fori_loop(0, n_rows, issue, 0, unroll=ISSUE_UNROLL)
        pltpu.make_async_copy(_tiles(o_hbm, 0, n_rows * TOP_K), _tiles(o_hbm, 0, n_rows * TOP_K), sem).wait()

    @pl.when(i < n_main)
    def _():
        scatter_rows(dest_a_ref, xa_ref, tm)

    @pl.when(i == n_main)
    def _():
        scatter_rows(dest_b_ref, xb_ref, n_last)


def _dispatch(pad_start, pad_len, tail, dest_a, dest_b, xn_a, xn_b, n_slots, tm):
    n_main = xn_a.shape[0] // (tm * SUBLANES)
    n_last = xn_b.shape[0] // SUBLANES
    assert n_slots >= max(tm, n_last) * TOP_K
    grid_spec = pltpu.PrefetchScalarGridSpec(
        num_scalar_prefetch=3,
        grid=(n_main + 1,),
        in_specs=[pl.BlockSpec((tm * TOP_K,), lambda i, *_: (jnp.minimum(i, n_main - 1),),
                               memory_space=pltpu.SMEM),
                  pl.BlockSpec((n_last * TOP_K,), lambda i, *_: (0,), memory_space=pltpu.SMEM),
                  pl.BlockSpec((tm * SUBLANES, LANES), lambda i, *_: (jnp.minimum(i, n_main - 1), 0)),
                  pl.BlockSpec((n_last * SUBLANES, LANES), lambda i, *_: (0, 0))],
        out_specs=pl.BlockSpec(memory_space=pl.ANY),
        scratch_shapes=[pltpu.VMEM((EXPERT_BLOCK * SUBLANES, LANES), F32), pltpu.SemaphoreType.DMA(()),
                        pltpu.SemaphoreType.DMA(())],
    )
    return pl.pallas_call(
        functools.partial(_dispatch_kernel, n_main=n_main, n_last=n_last),
        grid_spec=grid_spec,
        out_shape=jax.ShapeDtypeStruct((n_slots * SUBLANES, LANES), F32),
        compiler_params=_params(1, 2 * (tm + n_last) * D_MODEL * 4 + EXPERT_BLOCK * D_MODEL * 4),
        name="dispatch",
    )(pad_start, pad_len, tail, dest_a, dest_b, xn_a, xn_b)


def _expert_kernel(be_ref, nu_ref, xs_ref, wgu_ref, bgu_ref, wdn_ref, bdn_ref, o_ref, wgu_bf, wdn_bf):
    i = pl.program_id(0)
    used = i < nu_ref[0]
    changed = jnp.logical_or(i == 0, be_ref[i] != be_ref[jnp.maximum(i - 1, 0)])

    @pl.when(jnp.logical_and(used, changed))
    def _():
        wgu_bf[...] = wgu_ref[0].astype(BF16)
        wdn_bf[...] = wdn_ref[0].astype(BF16)

    @pl.when(used)
    def _():
        x = _load_row_tiles(xs_ref, xs_ref.shape[0] // SUBLANES)
        e = be_ref[i]
        gu = _dot(x.astype(BF16), wgu_bf[...]) + bgu_ref[e]
        gate = jnp.minimum(gu[:, :D_EXPERT], SWIGLU_LIMIT)
        up = jnp.clip(gu[:, D_EXPERT:], -SWIGLU_LIMIT, SWIGLU_LIMIT)
        hmid = (up + 1.0) * (gate * jax.nn.sigmoid(SWIGLU_ALPHA * gate))
        _store_row_tiles(o_ref, _dot(hmid.astype(BF16), wdn_bf[...]) + bdn_ref[e])

    @pl.when(jnp.logical_not(used))
    def _():
        o_ref[...] = jnp.zeros_like(o_ref)


def _experts(block_e, n_used, xs, w_gu, b_gu, w_down, b_down):
    n_slots = xs.shape[0] // SUBLANES
    nb = n_slots // EXPERT_BLOCK
    blk = EXPERT_BLOCK
    slots = pl.BlockSpec((blk * SUBLANES, LANES), lambda i, be, nu: (i, 0))
    est = (2 * 2 * blk * D_MODEL * 4 + 2 * (D_MODEL * 2 * D_EXPERT + D_EXPERT * D_MODEL) * 4
           + (D_MODEL * 2 * D_EXPERT + D_EXPERT * D_MODEL) * 2 + 4 * blk * 2 * D_EXPERT * 4)
    grid_spec = pltpu.PrefetchScalarGridSpec(
        num_scalar_prefetch=2,
        grid=(nb,),
        in_specs=[slots,
                  pl.BlockSpec((1, D_MODEL, 2 * D_EXPERT), lambda i, be, nu: (be[i], 0, 0)),
                  pl.BlockSpec((N_EXPERTS, 1, 2 * D_EXPERT), lambda i, be, nu: (0, 0, 0)),
                  pl.BlockSpec((1, D_EXPERT, D_MODEL), lambda i, be, nu: (be[i], 0, 0)),
                  pl.BlockSpec((N_EXPERTS, 1, D_MODEL), lambda i, be, nu: (0, 0, 0))],
        out_specs=slots,
        scratch_shapes=[pltpu.VMEM((D_MODEL, 2 * D_EXPERT), BF16), pltpu.VMEM((D_EXPERT, D_MODEL), BF16)],
    )
    return pl.pallas_call(
        _expert_kernel,
        grid_spec=grid_spec,
        out_shape=jax.ShapeDtypeStruct((n_slots * SUBLANES, LANES), F32),
        compiler_params=_params(1, est),
        name="experts",
    )(block_e, n_used, xs, w_gu, b_gu.reshape(N_EXPERTS, 1, 2 * D_EXPERT), w_down,
      b_down.reshape(N_EXPERTS, 1, D_MODEL))


def _combine_kernel(dest_ref, dest_next_ref, ys_hbm, gate_ref, h_ref, p_ref, ln3_ref, pg_ref, pp_ref, fg_ref,
                    o_ref, gbuf, hbuf, sem):
    i = pl.program_id(0)
    tm = h_ref.shape[0]
    slot = lax.rem(i, 2)

    def issue(d_ref, s):
        def body(n, carry):
            for j in range(TOP_K):
                pltpu.make_async_copy(_tiles(ys_hbm, d_ref[n * TOP_K + j], 1),
                                      _tiles(gbuf.at[s, j], n * SUBLANES, 1), sem.at[s]).start(priority=j % 2)
            return carry
        lax.fori_loop(0, tm, body, 0, unroll=ISSUE_UNROLL)

    other = 1 - slot
    n_tiles = pl.num_programs(0) - 1

    @pl.when(i == 0)
    def _():
        issue(dest_ref, 0)
        hbuf[1] = jnp.zeros(hbuf.shape[1:], F32)
        gbuf[1] = jnp.zeros(gbuf.shape[1:], F32)

    @pl.when(i + 1 < n_tiles)
    def _():
        issue(dest_next_ref, other)

    @pl.when(i < n_tiles)
    def _():
        for j in range(TOP_K):
            pltpu.make_async_copy(gbuf.at[slot, j], gbuf.at[slot, j], sem.at[slot]).wait()

    h = hbuf[other]
    gate = jax.nn.sigmoid(_dot(_rmsnorm_rows(h, ln3_ref[...]).astype(BF16), pg_ref[...]))
    h = h + gate * _dot(p_ref[...].astype(BF16), pp_ref[...])
    o_ref[...] = _rmsnorm_rows(h, fg_ref[...])

    gates = gate_ref[...]
    slabs = []
    for c in range(SUBLANES):
        acc = gates[:, 0:1] * gbuf[slot, 0, pl.ds(c, tm, stride=SUBLANES), :]
        for j in range(1, TOP_K):
            acc = acc + gates[:, j:j + 1] * gbuf[slot, j, pl.ds(c, tm, stride=SUBLANES), :]
        slabs.append(acc)
    hbuf[slot] = h_ref[...] + jnp.concatenate(slabs, axis=1)


def _combine(dest_flat, ys, gates, h2d, p2d, ln3_g, ple_gate_bf, ple_proj_bf, final_g, tm):
    n = h2d.shape[0]
    nb = n // tm
    row = lambda x: x.reshape(1, -1)
    cst = lambda shape: pl.BlockSpec(shape, lambda i: (0,) * len(shape))
    cur = lambda i: jnp.minimum(i, nb - 1)
    done = lambda i: jnp.maximum(i - 1, 0)
    est = (2 * TOP_K * tm * D_MODEL * 4 + 2 * (2 * tm * D_MODEL * 4 + tm * PLE_DIM * 4 + tm * ROUTE_LANES * 4)
           + 2 * (D_MODEL * D_MODEL * 2 + PLE_DIM * D_MODEL * 2) + 8 * tm * D_MODEL * 4)
    return pl.pallas_call(
        _combine_kernel,
        grid=(nb + 1,),
        in_specs=[pl.BlockSpec((tm * TOP_K,), lambda i: (cur(i),), memory_space=pltpu.SMEM),
                  pl.BlockSpec((tm * TOP_K,), lambda i: (cur(i + 1),), memory_space=pltpu.SMEM),
                  pl.BlockSpec(memory_space=pl.ANY),
                  pl.BlockSpec((tm, ROUTE_LANES), lambda i: (cur(i), 0)),
                  pl.BlockSpec((tm, D_MODEL), lambda i: (cur(i), 0)),
                  pl.BlockSpec((tm, PLE_DIM), lambda i: (done(i), 0)),
                  cst((1, D_MODEL)), cst((D_MODEL, D_MODEL)), cst((PLE_DIM, D_MODEL)), cst((1, D_MODEL))],
        out_specs=pl.BlockSpec((tm, D_MODEL), lambda i: (done(i), 0)),
        out_shape=jax.ShapeDtypeStruct((n, D_MODEL), F32),
        scratch_shapes=[pltpu.VMEM((2, TOP_K, tm * SUBLANES, LANES), F32), pltpu.VMEM((2, tm, D_MODEL), F32),
                        pltpu.SemaphoreType.DMA((2,))],
        compiler_params=_params(1, est),
        name="combine_ple_norm",
    )(dest_flat, dest_flat, ys, gates, h2d, p2d, row(ln3_g), ple_gate_bf, ple_proj_bf, row(final_g))


def _tile(n, pref):
    t = min(n, pref)
    assert n % t == 0, (n, pref)
    return t


def _front(x, pos0, pool_ctx, shift_prev, wkv0, wts, carry_in, tm_mix):
    b, t, _ = x.shape
    n = b * t
    assert t >= POOL_CTX and t % SUBLANES == 0
    x2d = x.reshape(n, D_MODEL)
    ctx16 = jnp.concatenate([jnp.zeros((b, POOL_HALO - POOL_CTX, POOL_W), F32), pool_ctx.astype(F32)], axis=1)
    tt = _tile(t, 512)
    bb = _tile(b, max(1, 128 // tt))
    zr, pool_out, last = _in_proj_pool(x, ctx16, wts["ln1_g"], wts["w_in_bf"], wts["w_pool_bf"],
                                       wts["pool_scale"], pos0, bb, tt)
    new_pool = last[:, POOL_HALO - POOL_CTX:]

    sh8 = jnp.broadcast_to(shift_prev.astype(F32), (b, SUBLANES, RWKV_PROJ))
    new_shift = zr[:, -1:]

    t_pad = -(-t // CHUNK) * CHUNK
    zr_pad = zr if t_pad == t else jnp.pad(zr, ((0, 0), (0, t_pad - t), (0, 0)))
    rw_out, s_fin = _rwkv_mixer(zr_pad, sh8, _state_to_pairs(wkv0), wts, _tile(t_pad, 16 * CHUNK),
                                None if t_pad == t else t)
    rw_out = rw_out[:, :t]
    new_wkv = _pairs_to_state(s_fin)

    h, xn, route, gates, counts = _mix_out(
        rw_out.reshape(n, RWKV_W), pool_out.reshape(n, POOL_W), x2d, wts["w_out_bf"], wts["ln2_g"],
        wts["rw_hi"], wts["rw_lo"], wts["rb_pad"], carry_in, tm_mix)
    return dict(h=h, xn=xn, route=route, gates=gates, counts=counts,
                new_pool=new_pool, new_shift=new_shift, new_wkv=new_wkv)


def _prep_weights(i, ln1_g, w_in, mu_shift, w_pool, pool_scale, w0, w2, a0, a2, g2, k_k, k_a, r_k,
                  lnx_g, lnx_b, w_out, ln2_g, router_w, router_b, ln3_g, ple_gate, ple_proj):
    head = jnp.arange(GROUP_TILE) // HEAD_DIM
    ones_bd = (head[:, None] == head[None, :]).astype(BF16)
    rw_pad = jnp.pad(router_w[i].astype(F32), ((0, 0), (0, ROUTE_LANES - N_EXPERTS)))
    rw_hi = rw_pad.astype(BF16)
    rw_lo = (rw_pad - rw_hi.astype(F32)).astype(BF16)
    rb_pad = jnp.concatenate([router_b[i].astype(F32), jnp.full((ROUTE_LANES - N_EXPERTS,), -jnp.inf, F32)])
    return dict(
        ln1_g=ln1_g[i], w_in_bf=w_in[i].astype(BF16), mu_shift=mu_shift[i], w_pool_bf=w_pool[i].astype(BF16),
        pool_scale=pool_scale[i], w0=w0[i], w2_bf=w2[i].astype(BF16), a0=a0[i], a2_bf=a2[i].astype(BF16),
        g2_bf=g2[i].astype(BF16), k_k=k_k[i], k_a=k_a[i], r_k=r_k[i].reshape(RWKV_W), lnx_g=lnx_g[i],
        lnx_b=lnx_b[i], w_out_bf=w_out[i].astype(BF16), ln2_g=ln2_g[i], rw_hi=rw_hi, rw_lo=rw_lo,
        rb_pad=rb_pad.reshape(1, ROUTE_LANES), ones_bd=ones_bd, ln3_g=ln3_g[i],
        ple_gate_bf=ple_gate[i].astype(BF16), ple_proj_bf=ple_proj[i].astype(BF16))


def _layer_pair(hp, hs, pp, ps, pool_ctx, shift_prev, wkv0, wts, w_gu, b_gu, w_down, b_down, final_g):
    bp, tp, _ = hp.shape
    bs, ts, _ = hs.shape
    np_, ns = bp * tp, bs * ts
    tm_p = _tile(np_, 256)
    tm_s = _tile(ns, 128)
    zero_carry = jnp.zeros((SUBLANES, ROUTE_LANES), F32)
    fp = _front(hp, 0, jnp.zeros((bp, POOL_CTX, POOL_W), F32), jnp.zeros((bp, 1, RWKV_PROJ), F32),
                jnp.zeros((bp, N_HEADS, HEAD_DIM, HEAD_DIM), F32), wts, zero_carry, _tile(np_, 512))
    fs = _front(hs, PAST_LEN, pool_ctx, shift_prev, wkv0, wts, fp["counts"], tm_s)

    counts = fs["counts"][0, :N_EXPERTS].astype(I32)
    padded = (counts + EXPERT_BLOCK - 1) // EXPERT_BLOCK * EXPERT_BLOCK
    pad_end = jnp.cumsum(padded)
    start_pad = pad_end - padded
    n_blocks = -(-(np_ + ns) * TOP_K // EXPERT_BLOCK) + N_EXPERTS
    n_slots = n_blocks * EXPERT_BLOCK
    n_used = (pad_end[-1] // EXPERT_BLOCK).astype(I32)
    block_start = jnp.arange(n_blocks, dtype=I32) * EXPERT_BLOCK
    block_e = jnp.minimum(jnp.sum((block_start[:, None] >= pad_end[None, :]).astype(I32), axis=1), N_EXPERTS - 1)
    last_e = block_e[jnp.maximum(n_used - 1, 0)]
    block_e = jnp.where(jnp.arange(n_blocks) < n_used, block_e, last_e)

    start_row = start_pad * SUBLANES

    def dest_of(route):
        idx = route[:, :TOP_K]
        rank_row = route[:, TOP_K:2 * TOP_K]
        start = jnp.sum(jnp.where(idx[..., None] == jnp.arange(N_EXPERTS, dtype=I32), start_row, 0), axis=-1)
        return (start + rank_row).reshape(-1).astype(I32)

    dest_p, dest_s = dest_of(fp["route"]), dest_of(fs["route"])
    tail = jnp.stack([n_used * EXPERT_BLOCK, n_blocks - n_used]).astype(I32)
    xs = _dispatch(start_pad + counts, padded - counts, tail, dest_p, dest_s, fp["xn"], fs["xn"], n_slots,
                   _tile(np_, 2048))
    ys = _experts(block_e, n_used.reshape(1), xs, w_gu, b_gu, w_down, b_down)

    outs = []
    for f, dest, p, tm in ((fp, dest_p, pp, tm_p), (fs, dest_s, ps, tm_s)):
        n = f["h"].shape[0]
        outs.append(_combine(dest, ys, f["gates"], f["h"], p.reshape(n, PLE_DIM).astype(F32), wts["ln3_g"],
                             wts["ple_gate_bf"], wts["ple_proj_bf"], final_g, tm))
    return outs[0].reshape(hp.shape), outs[1].reshape(hs.shape), fp, fs


def kernel(x_prompt, x_sample, p_prompt, p_sample, cache_pool, state_shift, state_wkv, ln1_g, w_in, mu_shift, w_pool, pool_scale, w0, w2, a0, a2, g2, k_k, k_a, r_k, lnx_g, lnx_b, w_out, ln2_g, router_w, router_b, w_gu, b_gu, w_down, b_down, ln3_g, ple_gate, ple_proj, final_g):
    assert ln1_g.shape[0] == 1, "single-layer kernel"
    i = 0
    wts = _prep_weights(i, ln1_g, w_in, mu_shift, w_pool, pool_scale, w0, w2, a0, a2, g2, k_k, k_a, r_k,
                        lnx_g, lnx_b, w_out, ln2_g, router_w, router_b, ln3_g, ple_gate, ple_proj)
    y_p, y_s, fp, fs = _layer_pair(x_prompt, x_sample, p_prompt[i], p_sample[i], cache_pool[i], state_shift[i],
                                   state_wkv[i], wts, w_gu[i], b_gu[i], w_down[i], b_down[i], final_g)
    stack = lambda a: a[None]
    return (y_p, y_s,
            stack(fp["new_pool"]), stack(fp["new_shift"]), stack(fp["new_wkv"]),
            stack(fs["new_pool"]), stack(fs["new_shift"]), stack(fs["new_wkv"]))
```
